```python
import math
import jax, jax.numpy as jnp
from jax import lax
import numpy as np

D_MODEL = 2048
BATCH = 4
SEQ = 4096
DEPTH = 1

DN_HEADS = 8
DN_HEAD_DIM = 128
DN_WIDTH = DN_HEADS * DN_HEAD_DIM
CONV_WIDTH = 4
CHUNK = 64
DA_HEADS = 4
DA_HEAD_DIM = 128
DA_V_DIM = 2 * DA_HEAD_DIM
DA_QK_WIDTH = DA_HEADS * 2 * DA_HEAD_DIM
DA_WIDTH = DA_HEADS * DA_V_DIM
Q_BLOCK = 128
D_MIX = DN_WIDTH + DA_WIDTH
OFF_DN_QKV = 0
OFF_DN_Z = OFF_DN_QKV + 3 * DN_WIDTH
OFF_DN_B = OFF_DN_Z + DN_WIDTH
OFF_DN_A = OFF_DN_B + DN_HEADS
OFF_DA_Q = OFF_DN_A + DN_HEADS
OFF_DA_K = OFF_DA_Q + DA_QK_WIDTH
OFF_DA_V = OFF_DA_K + DA_QK_WIDTH
D_IN = OFF_DA_V + DA_WIDTH
D_FF = (((8 * D_MODEL + 2) // 3 + 255) // 256) * 256
EPS = 1e-6

kernel_name = "hymba_gdn_diffattn_alibi_swiglu"


def rmsnorm(x, w):
    xf = x.astype(jnp.float32)
    y = xf * lax.rsqrt(jnp.mean(xf * xf, axis=-1, keepdims=True) + EPS)
    return (y * w.astype(jnp.float32)).astype(x.dtype)


def l2norm(x):
    return x * lax.rsqrt(jnp.sum(x * x, axis=-1, keepdims=True) + EPS)


def causal_depthwise_conv(x, w):
    K = w.shape[0]
    T = x.shape[1]
    xp = jnp.pad(x, ((0, 0), (K - 1, 0), (0, 0)))
    return sum(xp[:, i:i + T] * w[i] for i in range(K))


def gated_delta_rule(q, k, v, g, beta):
    B, T, H, dk = q.shape
    dv = v.shape[-1]
    N = T // CHUNK

    def chunks(a):
        return a.reshape(B, N, CHUNK, H, a.shape[-1]).transpose(0, 3, 1, 2, 4)

    qc, kc, vc = chunks(q), chunks(k), chunks(v)
    gc = jnp.cumsum(g.reshape(B, N, CHUNK, H).transpose(0, 3, 1, 2), axis=-1)
    bc = beta.reshape(B, N, CHUNK, H).transpose(0, 3, 1, 2)[..., None]

    tri_incl = jnp.tril(jnp.ones((CHUNK, CHUNK), dtype=bool))
    tri_strict = jnp.tril(jnp.ones((CHUNK, CHUNK), dtype=bool), k=-1)
    decay = jnp.exp(jnp.where(tri_incl, gc[..., :, None] - gc[..., None, :], -jnp.inf))

    kb = kc * bc
    m = jnp.where(tri_strict, jnp.einsum('bhnid,bhnjd->bhnij', kb, kc) * decay, 0.0)
    eye_m = m + jnp.eye(CHUNK, dtype=m.dtype)
    u = lax.linalg.triangular_solve(eye_m, vc * bc, left_side=True, lower=True, unit_diagonal=True)
    w = lax.linalg.triangular_solve(eye_m, kb * jnp.exp(gc)[..., None], left_side=True, lower=True,
                                    unit_diagonal=True)
    a_qk = jnp.einsum('bhnid,bhnjd->bhnij', qc, kc) * decay
    glast = gc[..., -1]
    q_dec = qc * jnp.exp(gc)[..., None]
    k_dec = kc * jnp.exp(glast[..., None] - gc)[..., None]

    def step(S, inp):
        q_i, k_i, u_i, w_i, a_i, gl_i = inp
        v_new = u_i - jnp.einsum('bhcd,bhde->bhce', w_i, S)
        o_i = jnp.einsum('bhcd,bhde->bhce', q_i, S) + jnp.einsum('bhij,bhje->bhie', a_i, v_new)
        S = S * jnp.exp(gl_i)[..., None, None] + jnp.einsum('bhcd,bhce->bhde', k_i, v_new)
        return S, o_i

    xs = tuple(jnp.moveaxis(a, 2, 0) for a in (q_dec, k_dec, u, w, a_qk, glast))
    S0 = jnp.zeros((B, H, dk, dv), dtype=jnp.float32)
    _, o = lax.scan(step, S0, xs)
    return o.transpose(1, 0, 3, 2, 4).reshape(B, T, H, dv)


def differential_attention(q, k, v, lam):
    B, T, H, _, d = q.shape
    NB = T // Q_BLOCK
    scale = d ** -0.5
    slopes = 2.0 ** (-8.0 * jnp.arange(1, H + 1, dtype=jnp.float32) / H)
    kf = k.astype(jnp.float32)
    vf = v.astype(jnp.float32)
    kpos = jnp.arange(T)
    q_blocks = q.astype(jnp.float32).reshape(B, NB, Q_BLOCK, H, 2, d).transpose(1, 0, 2, 3, 4, 5)

    def block(args):
        qb, bi = args
        s = jnp.einsum('bqhcd,bkhcd->bhcqk', qb, kf) * scale
        qpos = bi * Q_BLOCK + jnp.arange(Q_BLOCK)
        dist = (qpos[:, None] - kpos[None, :]).astype(jnp.float32)
        s = s - slopes[None, :, None, None, None] * dist
        s = jnp.where(dist >= 0, s, -jnp.inf)
        p = jax.nn.softmax(s, axis=-1)
        p = p[:, :, 0] - lam * p[:, :, 1]
        return jnp.einsum('bhqk,bkhe->bqhe', p, vf)

    out = lax.map(block, (q_blocks, jnp.arange(NB)))
    return out.transpose(1, 0, 2, 3, 4).reshape(B, T, H, v.shape[-1])


def setup_inputs(seed: int = 0) -> dict:
    key = jax.random.key(seed)
    ks = jax.random.split(key, 20)
    f32 = jnp.float32
    L = DEPTH

    def nrm(k, shape, scale):
        return jax.random.normal(k, shape, f32) * scale

    dt = jnp.exp(jax.random.uniform(ks[5], (L, DN_HEADS), f32, math.log(1e-3), math.log(1e-1)))
    return {
        "x": nrm(ks[0], (BATCH, SEQ, D_MODEL), 1.0),
        "attn_norm_w": 1.0 + nrm(ks[1], (L, D_MODEL), 0.02),
        "w_in": nrm(ks[2], (L, D_MODEL, D_IN), D_MODEL ** -0.5),
        "conv_w": nrm(ks[3], (L, CONV_WIDTH, 3 * DN_WIDTH), CONV_WIDTH ** -0.5),
        "a_log": jnp.log(jax.random.uniform(ks[4], (L, DN_HEADS), f32, 1.0, 16.0)),
        "dt_bias": dt + jnp.log(-jnp.expm1(-dt)),
        "dn_norm_w": 1.0 + nrm(ks[6], (L, DN_HEAD_DIM), 0.02),
        "lam_q1": nrm(ks[7], (L, DA_HEAD_DIM), 0.1),
        "lam_k1": nrm(ks[8], (L, DA_HEAD_DIM), 0.1),
        "lam_q2": nrm(ks[9], (L, DA_HEAD_DIM), 0.1),
        "lam_k2": nrm(ks[10], (L, DA_HEAD_DIM), 0.1),
        "da_norm_w": 1.0 + nrm(ks[11], (L, DA_V_DIM), 0.02),
        "w_out": nrm(ks[12], (L, D_MIX, D_MODEL), D_MIX ** -0.5),
        "ffn_norm_w": 1.0 + nrm(ks[13], (L, D_MODEL), 0.02),
        "w_gate": nrm(ks[14], (L, D_MODEL, D_FF), D_MODEL ** -0.5),
        "w_up": nrm(ks[15], (L, D_MODEL, D_FF), D_MODEL ** -0.5),
        "w_down": nrm(ks[16], (L, D_FF, D_MODEL), D_FF ** -0.5),
        "final_norm_w": 1.0 + nrm(ks[17], (D_MODEL,), 0.02),
    }


def reference(x, attn_norm_w, w_in, conv_w, a_log, dt_bias, dn_norm_w, lam_q1, lam_k1, lam_q2, lam_k2,
              da_norm_w, w_out, ffn_norm_w, w_gate, w_up, w_down, final_norm_w):
    B, T, _ = x.shape
    f32 = jnp.float32
    for l in range(DEPTH):
        lam_init = 0.8 - 0.6 * math.exp(-0.3 * l)
        h = rmsnorm(x, attn_norm_w[l])
        proj = h @ w_in[l]

        qkv = jax.nn.silu(causal_depthwise_conv(proj[..., OFF_DN_QKV:OFF_DN_Z], conv_w[l])).astype(f32)
        dq, dk_, dv_ = jnp.split(qkv, 3, axis=-1)
        dq = l2norm(dq.reshape(B, T, DN_HEADS, DN_HEAD_DIM)) * (DN_HEAD_DIM ** -0.5)
        dk_ = l2norm(dk_.reshape(B, T, DN_HEADS, DN_HEAD_DIM))
        dv_ = dv_.reshape(B, T, DN_HEADS, DN_HEAD_DIM)
        z = proj[..., OFF_DN_Z:OFF_DN_B].reshape(B, T, DN_HEADS, DN_HEAD_DIM).astype(f32)
        beta = jax.nn.sigmoid(proj[..., OFF_DN_B:OFF_DN_A].astype(f32))
        g = -jnp.exp(a_log[l].astype(f32)) * jax.nn.softplus(
            proj[..., OFF_DN_A:OFF_DA_Q].astype(f32) + dt_bias[l].astype(f32))
        o_dn = gated_delta_rule(dq, dk_, dv_, g, beta)
        o_dn = rmsnorm(o_dn, dn_norm_w[l]) * jax.nn.silu(z)
        o_dn = o_dn.reshape(B, T, DN_WIDTH).astype(x.dtype)

        qa = proj[..., OFF_DA_Q:OFF_DA_K].reshape(B, T, DA_HEADS, 2, DA_HEAD_DIM)
        ka = proj[..., OFF_DA_K:OFF_DA_V].reshape(B, T, DA_HEADS, 2, DA_HEAD_DIM)
        va = proj[..., OFF_DA_V:D_IN].reshape(B, T, DA_HEADS, DA_V_DIM)
        lam = (jnp.exp(jnp.sum(lam_q1[l].astype(f32) * lam_k1[l].astype(f32)))
               - jnp.exp(jnp.sum(lam_q2[l].astype(f32) * lam_k2[l].astype(f32))) + lam_init)
        o_da = differential_attention(qa, ka, va, lam)
        o_da = rmsnorm(o_da, da_norm_w[l]) * (1.0 - lam_init)
        o_da = o_da.reshape(B, T, DA_WIDTH).astype(x.dtype)

        x = x + jnp.concatenate([o_dn, o_da], axis=-1) @ w_out[l]

        h2 = rmsnorm(x, ffn_norm_w[l])
        x = x + (jax.nn.silu(h2 @ w_gate[l]) * (h2 @ w_up[l])) @ w_down[l]
    return rmsnorm(x, final_norm_w)
```

```python
import functools
import math

import jax
import jax.numpy as jnp
from jax import lax
from jax.experimental import pallas as pl
from jax.experimental.pallas import tpu as pltpu

F32 = jnp.float32
BF16 = jnp.bfloat16
EPS = 1e-6
NEG_BIG = -1e30

DN_HEADS = 8
DN_DIM = 128
DN_WIDTH = DN_HEADS * DN_DIM
CONV_WIDTH = 4
CHUNK = 64
DA_HEADS = 4
DA_DIM = 128
DA_VDIM = 2 * DA_DIM
DA_WIDTH = DA_HEADS * DA_VDIM
LANES = 128
SUBLANES = 8
GROUP = 256
VMEM_LIMIT = 56 * 1024 * 1024

_NT = (((1,), (1,)), ((), ()))
_TN = (((0,), (0,)), ((), ()))


def _params(*sem):
    return pltpu.CompilerParams(dimension_semantics=sem, vmem_limit_bytes=VMEM_LIMIT)


def _sigmoid(x):
    return 1.0 / (1.0 + jnp.exp(-x))


def _rmsnorm_rows(x, w):
    return x * lax.rsqrt(jnp.mean(x * x, axis=-1, keepdims=True) + EPS) * w


def _norm_matmul_kernel(x_ref, nw_ref, w_ref, cs_ref, o_ref, h_ref):
    @pl.when(pl.program_id(1) == 0)
    def _():
        h_ref[...] = _rmsnorm_rows(x_ref[...], nw_ref[...]).astype(BF16)

    acc = jnp.dot(h_ref[...], w_ref[...], preferred_element_type=F32)
    o_ref[...] = (acc * cs_ref[...]).astype(o_ref.dtype)


def _norm_matmul(x2d, norm_w, w_bf16, col_scale, out_dtype, tm, tn, name):
    m, d = x2d.shape
    n = w_bf16.shape[1]
    return pl.pallas_call(
        _norm_matmul_kernel,
        grid=(m // tm, n // tn),
        in_specs=[
            pl.BlockSpec((tm, d), lambda i, j: (i, 0)),
            pl.BlockSpec((1, d), lambda i, j: (0, 0)),
            pl.BlockSpec((d, tn), lambda i, j: (0, j)),
            pl.BlockSpec((1, tn), lambda i, j: (0, j)),
        ],
        out_specs=pl.BlockSpec((tm, tn), lambda i, j: (i, j)),
        out_shape=jax.ShapeDtypeStruct((m, n), out_dtype),
        scratch_shapes=[pltpu.VMEM((tm, d), BF16)],
        compiler_params=_params("parallel", "arbitrary"),
        name=name,
    )(x2d, norm_w, w_bf16, col_scale)


def _gates_kernel(x_ref, nw_ref, wt_ref, alog_ref, dtb_ref, row_ref, col_ref):
    h = _rmsnorm_rows(x_ref[...], nw_ref[...]).astype(BF16)
    r = lax.dot_general(wt_ref[...], h, _NT, preferred_element_type=F32)
    beta = _sigmoid(r[0:DN_HEADS])
    a = r[DN_HEADS:2 * DN_HEADS] + dtb_ref[...]
    softplus = jnp.maximum(a, 0.0) + jnp.log1p(jnp.exp(-jnp.abs(a)))
    g = -jnp.exp(alog_ref[...]) * softplus
    tm = g.shape[1]
    ii = lax.broadcasted_iota(jnp.int32, (tm, tm), 0)
    jj = lax.broadcasted_iota(jnp.int32, (tm, tm), 1)
    same = (ii // CHUNK) == (jj // CHUNK)
    cum = (same & (ii <= jj)).astype(F32)
    tot = same.astype(F32)
    gc = jnp.dot(g, cum, precision=lax.Precision.HIGHEST, preferred_element_type=F32)
    gl = jnp.dot(g, tot, precision=lax.Precision.HIGHEST, preferred_element_type=F32)
    rows = jnp.concatenate([beta, gc, gl], axis=0)
    row_ref[...] = rows
    padded = jnp.concatenate([rows, jnp.zeros((LANES - 3 * DN_HEADS, tm), F32)], axis=0)
    col_ref[...] = padded.T


def _gates(x3d, norm_w, w_gate_t, a_log_col, dt_bias_col, tm):
    b, t, d = x3d.shape
    return pl.pallas_call(
        _gates_kernel,
        grid=(b, t // tm),
        in_specs=[
            pl.BlockSpec((None, tm, d), lambda i, j: (i, j, 0)),
            pl.BlockSpec((1, d), lambda i, j: (0, 0)),
            pl.BlockSpec((2 * DN_HEADS, d), lambda i, j: (0, 0)),
            pl.BlockSpec((DN_HEADS, 1), lambda i, j: (0, 0)),
            pl.BlockSpec((DN_HEADS, 1), lambda i, j: (0, 0)),
        ],
        out_specs=[
            pl.BlockSpec((None, 3 * DN_HEADS, tm), lambda i, j: (i, 0, j)),
            pl.BlockSpec((None, tm, LANES), lambda i, j: (i, j, 0)),
        ],
        out_shape=[
            jax.ShapeDtypeStruct((b, 3 * DN_HEADS, t), F32),
            jax.ShapeDtypeStruct((b, t, LANES), F32),
        ],
        compiler_params=_params("parallel", "arbitrary"),
        name="dn_gates",
    )(x3d, norm_w, w_gate_t, a_log_col, dt_bias_col)


def _dn_intra_kernel(q_ref, k_ref, v_ref, qh_ref, kh_ref, vh_ref, cwq_ref, cwk_ref, cwv_ref,
                     gcol_ref, grow_ref, u_ref, w_ref, qd_ref, kd_ref, a_ref, xs_ref):
    head = pl.program_id(1)
    first = pl.program_id(2) == 0
    tb = q_ref.shape[0]
    halo = SUBLANES
    for idx, (x_ref, xh_ref) in enumerate(((q_ref, qh_ref), (k_ref, kh_ref), (v_ref, vh_ref))):
        xs_ref[idx, 0:halo, :] = jnp.where(first, 0.0, xh_ref[...])
        xs_ref[idx, halo:halo + tb, :] = x_ref[...]

    lane = lax.broadcasted_iota(jnp.int32, (GROUP, LANES), 1)
    sel_beta = (lane == head).astype(F32)
    sel_gc = (lane == head + DN_HEADS).astype(F32)
    sel_gl = (lane == head + 2 * DN_HEADS).astype(F32)
    ri = lax.broadcasted_iota(jnp.int32, (GROUP, GROUP), 0)
    ci = lax.broadcasted_iota(jnp.int32, (GROUP, GROUP), 1)
    same = (ri // CHUNK) == (ci // CHUNK)
    incl = same & (ri >= ci)
    strict = same & (ri > ci)

    def conv_silu(idx, cw_ref, r0):
        base = r0 + halo - (CONV_WIDTH - 1)
        acc = xs_ref[idx, base:base + GROUP, :] * cw_ref[0:1, :]
        for i in range(1, CONV_WIDTH):
            acc = acc + xs_ref[idx, base + i:base + i + GROUP, :] * cw_ref[i:i + 1, :]
        return acc * _sigmoid(acc)

    for gi in range(tb // GROUP):
        r0 = gi * GROUP
        q = conv_silu(0, cwq_ref, r0)
        k = conv_silu(1, cwk_ref, r0)
        v = conv_silu(2, cwv_ref, r0)
        q = q * lax.rsqrt(jnp.sum(q * q, axis=-1, keepdims=True) + EPS) * (DN_DIM ** -0.5)
        k = k * lax.rsqrt(jnp.sum(k * k, axis=-1, keepdims=True) + EPS)

        gates = gcol_ref[r0:r0 + GROUP, :]
        beta = jnp.sum(gates * sel_beta, axis=-1, keepdims=True)
        gc = jnp.sum(gates * sel_gc, axis=-1, keepdims=True)
        gl = jnp.sum(gates * sel_gl, axis=-1, keepdims=True)
        gc_row = grow_ref[pl.ds(head + DN_HEADS, 1), r0:r0 + GROUP]

        decay = jnp.exp(jnp.where(incl, gc - gc_row, NEG_BIG))
        kb = k * beta
        k16 = k.astype(BF16)
        gram = lax.dot_general(jnp.concatenate([kb, q], axis=0).astype(BF16), k16, _NT,
                               preferred_element_type=F32)
        n_pow = jnp.where(strict, -(gram[:GROUP] * decay), 0.0)
        a_qk = gram[GROUP:] * decay

        egc = jnp.exp(gc)
        x = jnp.concatenate([v * beta, kb * egc], axis=1)
        levels = int(math.log2(CHUNK))
        for lvl in range(levels):
            n16 = n_pow.astype(BF16)
            x = x + jnp.dot(n16, x.astype(BF16), preferred_element_type=F32)
            if lvl + 1 < levels:
                n_pow = jnp.dot(n16, n16, preferred_element_type=F32)

        u_ref[r0:r0 + GROUP, :] = x[:, :DN_DIM]
        w_ref[r0:r0 + GROUP, :] = x[:, DN_DIM:].astype(BF16)
        qd_ref[r0:r0 + GROUP, :] = (q * egc).astype(BF16)
        kd_ref[r0:r0 + GROUP, :] = (k * jnp.exp(gl - gc)).astype(BF16)
        for c in range(GROUP // CHUNK):
            lo = c * CHUNK
            a_ref[r0 + lo:r0 + lo + CHUNK, :] = a_qk[lo:lo + CHUNK, lo:lo + CHUNK].astype(BF16)


def _dn_intra(qkvz, conv_w, gcol, grow, tb):
    b, t, _ = qkvz.shape
    hb = tb // SUBLANES
    nh = DN_HEADS

    def tok(off):
        return pl.BlockSpec((None, tb, DN_DIM), lambda i, h, j: (i, j, off + h))

    def halo(off):
        return pl.BlockSpec((None, SUBLANES, DN_DIM),
                            lambda i, h, j: (i, jnp.maximum(j * hb - 1, 0), off + h))

    def cw(off):
        return pl.BlockSpec((CONV_WIDTH, DN_DIM), lambda i, h, j: (0, off + h))

    out_tok = pl.BlockSpec((None, tb, DN_DIM), lambda i, h, j: (i, j, h))
    return pl.pallas_call(
        _dn_intra_kernel,
        grid=(b, nh, t // tb),
        in_specs=[tok(0), tok(nh), tok(2 * nh), halo(0), halo(nh), halo(2 * nh),
                  cw(0), cw(nh), cw(2 * nh),
                  pl.BlockSpec((None, tb, LANES), lambda i, h, j: (i, j, 0)),
                  pl.BlockSpec((None, 3 * nh, tb), lambda i, h, j: (i, 0, j))],
        out_specs=[out_tok, out_tok, out_tok, out_tok,
                   pl.BlockSpec((None, None, tb, CHUNK), lambda i, h, j: (i, h, j, 0))],
        out_shape=[jax.ShapeDtypeStruct((b, t, DN_WIDTH), F32),
                   jax.ShapeDtypeStruct((b, t, DN_WIDTH), BF16),
                   jax.ShapeDtypeStruct((b, t, DN_WIDTH), BF16),
                   jax.ShapeDtypeStruct((b, t, DN_WIDTH), BF16),
                   jax.ShapeDtypeStruct((b, nh, t, CHUNK), BF16)],
        scratch_shapes=[pltpu.VMEM((3, tb + SUBLANES, DN_DIM), F32)],
        compiler_params=_params("parallel", "parallel", "arbitrary"),
        name="dn_intra",
    )(qkvz, qkvz, qkvz, qkvz, qkvz, qkvz, conv_w, conv_w, conv_w, gcol, grow)


def _dn_scan_kernel(u_ref, w_ref, qd_ref, kd_ref, a_ref, z_ref, gcol_ref, nw_ref, o_ref, s_ref):
    @pl.when(pl.program_id(1) == 0)
    def _():
        s_ref[...] = jnp.zeros_like(s_ref)

    tb = u_ref.shape[0]
    nw = nw_ref[...]

    def chunk(c, carry):
        rows = pl.ds(pl.multiple_of(c * CHUNK, CHUNK), CHUNK)
        gl_row = gcol_ref[pl.ds(pl.multiple_of(c * CHUNK, CHUNK), 1), :]
        for h in range(DN_HEADS):
            cols = slice(h * DN_DIM, (h + 1) * DN_DIM)
            s = s_ref[h]
            wq = jnp.concatenate([w_ref[rows, cols], qd_ref[rows, cols]], axis=0)
            proj = jnp.dot(wq, s.astype(BF16), preferred_element_type=F32)
            v_new = u_ref[rows, cols] - proj[:CHUNK]
            v16 = v_new.astype(BF16)
            o = proj[CHUNK:] + jnp.dot(a_ref[h, rows, :], v16, preferred_element_type=F32)
            decay = jnp.exp(gl_row[:, 2 * DN_HEADS + h:2 * DN_HEADS + h + 1])
            s_ref[h] = s * decay + lax.dot_general(kd_ref[rows, cols], v16, _TN,
                                                   preferred_element_type=F32)
            z = z_ref[rows, cols]
            o_ref[rows, cols] = (_rmsnorm_rows(o, nw) * (z * _sigmoid(z))).astype(o_ref.dtype)
        return carry

    lax.fori_loop(0, tb // CHUNK, chunk, 0)


def _dn_scan(u, w, qd, kd, a, qkvz, gcol, dn_norm_w, tb):
    b, t, _ = u.shape
    tok = pl.BlockSpec((None, tb, DN_WIDTH), lambda i, j: (i, j, 0))
    z_block = 3 * DN_WIDTH // DN_WIDTH
    return pl.pallas_call(
        _dn_scan_kernel,
        grid=(b, t // tb),
        in_specs=[tok, tok, tok, tok,
                  pl.BlockSpec((None, DN_HEADS, tb, CHUNK), lambda i, j: (i, 0, j, 0)),
                  pl.BlockSpec((None, tb, DN_WIDTH), lambda i, j: (i, j, z_block)),
                  pl.BlockSpec((None, tb, LANES), lambda i, j: (i, j, 0)),
                  pl.BlockSpec((1, DN_DIM), lambda i, j: (0, 0))],
        out_specs=tok,
        out_shape=jax.ShapeDtypeStruct((b, t, DN_WIDTH), BF16),
        scratch_shapes=[pltpu.VMEM((DN_HEADS, DN_DIM, DN_DIM), F32)],
        compiler_params=_params("parallel", "arbitrary"),
        name="dn_scan",
    )(u, w, qd, kd, a, qkvz, gcol, dn_norm_w)


def _attn_kernel(q_ref, k_ref, v_ref, lq1_ref, lk1_ref, lq2_ref, lk2_ref, nw_ref, o_ref,
                 m_ref, l_ref, acc_ref, *, lam_init):
    head = pl.program_id(1)
    qi = pl.program_id(2)
    tq = q_ref.shape[0]
    tk = tq
    slope = jnp.float32(0.0)
    for hh in range(DA_HEADS):
        slope = jnp.where(head == hh, jnp.float32(2.0 ** (-8.0 * (hh + 1) / DA_HEADS)), slope)

    m_ref[...] = jnp.full_like(m_ref, NEG_BIG)
    l_ref[...] = jnp.zeros_like(l_ref)
    acc_ref[...] = jnp.zeros_like(acc_ref)
    kpos = lax.broadcasted_iota(jnp.int32, (1, tk), 1)

    def tile(j, masked):
        k0 = pl.multiple_of(j * tk, tk)
        bias = slope * (kpos + (j - qi) * tk).astype(F32)
        v = v_ref[pl.ds(k0, tk), :]
        for c in range(2):
            cols = slice(c * DA_DIM, (c + 1) * DA_DIM)
            s = lax.dot_general(q_ref[:, cols], k_ref[pl.ds(k0, tk), cols], _NT,
                                preferred_element_type=F32) + bias
            if masked:
                rr = lax.broadcasted_iota(jnp.int32, (tq, tk), 0)
                cc = lax.broadcasted_iota(jnp.int32, (tq, tk), 1)
                s = jnp.where(cc <= rr, s, NEG_BIG)
            m_prev = m_ref[c]
            m_new = jnp.maximum(m_prev, jnp.max(s, axis=-1, keepdims=True))
            alpha = jnp.exp(m_prev - m_new)
            p = jnp.exp(s - m_new)
            l_ref[c] = alpha * l_ref[c] + jnp.sum(p, axis=-1, keepdims=True)
            acc_ref[c] = alpha * acc_ref[c] + jnp.dot(p.astype(BF16), v,
                                                      preferred_element_type=F32)
            m_ref[c] = m_new

    def body(j, carry):
        tile(j, False)
        return carry

    lax.fori_loop(0, qi, body, 0)
    tile(qi, True)

    lam = (jnp.exp(jnp.sum(lq1_ref[...] * lk1_ref[...], axis=-1, keepdims=True))
           - jnp.exp(jnp.sum(lq2_ref[...] * lk2_ref[...], axis=-1, keepdims=True)) + lam_init)
    o = acc_ref[0] / l_ref[0] - lam * (acc_ref[1] / l_ref[1])
    o_ref[...] = (_rmsnorm_rows(o, nw_ref[...]) * (1.0 - lam_init)).astype(o_ref.dtype)


def _attention(da, lam_q1, lam_k1, lam_q2, lam_k2, da_norm_w, lam_init, tq):
    b, t, _ = da.shape
    nh = DA_HEADS
    vec = pl.BlockSpec((1, DA_DIM), lambda i, h, j: (0, 0))
    return pl.pallas_call(
        functools.partial(_attn_kernel, lam_init=lam_init),
        grid=(b, nh, t // tq),
        in_specs=[pl.BlockSpec((None, tq, DA_VDIM), lambda i, h, j: (i, j, h)),
                  pl.BlockSpec((None, t, DA_VDIM), lambda i, h, j: (i, 0, nh + h)),
                  pl.BlockSpec((None, t, DA_VDIM), lambda i, h, j: (i, 0, 2 * nh + h)),
                  vec, vec, vec, vec,
                  pl.BlockSpec((1, DA_VDIM), lambda i, h, j: (0, 0))],
        out_specs=pl.BlockSpec((None, tq, DA_VDIM), lambda i, h, j: (i, j, h)),
        out_shape=jax.ShapeDtypeStruct((b, t, DA_WIDTH), BF16),
        scratch_shapes=[pltpu.VMEM((2, tq, 1), F32), pltpu.VMEM((2, tq, 1), F32),
                        pltpu.VMEM((2, tq, DA_VDIM), F32)],
        compiler_params=_params("parallel", "parallel", "arbitrary"),
        name="diff_attn",
    )(da, da, da, lam_q1, lam_k1, lam_q2, lam_k2, da_norm_w)


def _out_proj_kernel(a_ref, b_ref, wa_ref, wb_ref, x_ref, o_ref):
    acc = jnp.dot(a_ref[...], wa_ref[...], preferred_element_type=F32)
    acc = acc + jnp.dot(b_ref[...], wb_ref[...], preferred_element_type=F32)
    o_ref[...] = x_ref[...] + acc


def _out_proj(o_dn, o_da, w_a, w_b, x2d, tm, tn):
    m, d = x2d.shape
    ka, kb = o_dn.shape[1], o_da.shape[1]
    return pl.pallas_call(
        _out_proj_kernel,
        grid=(m // tm, d // tn),
        in_specs=[pl.BlockSpec((tm, ka), lambda i, j: (i, 0)),
                  pl.BlockSpec((tm, kb), lambda i, j: (i, 0)),
                  pl.BlockSpec((ka, tn), lambda i, j: (0, j)),
                  pl.BlockSpec((kb, tn), lambda i, j: (0, j)),
                  pl.BlockSpec((tm, tn), lambda i, j: (i, j))],
        out_specs=pl.BlockSpec((tm, tn), lambda i, j: (i, j)),
        out_shape=jax.ShapeDtypeStruct((m, d), F32),
        compiler_params=_params("parallel", "arbitrary"),
        name="out_proj",
    )(o_dn, o_da, w_a, w_b, x2d)


def _ffn_kernel(x_ref, nw_ref, wg_ref, wu_ref, wd_ref, fw_ref, o_ref, h_ref):
    f = pl.program_id(1)

    @pl.when(f == 0)
    def _():
        h_ref[...] = _rmsnorm_rows(x_ref[...], nw_ref[...]).astype(BF16)

    h = h_ref[...]
    g = jnp.dot(h, wg_ref[...], preferred_element_type=F32)
    u = jnp.dot(h, wu_ref[...], preferred_element_type=F32)
    act = (g * _sigmoid(g) * u).astype(BF16)
    part = jnp.dot(act, wd_ref[...], preferred_element_type=F32)

    @pl.when(f == 0)
    def _():
        o_ref[...] = part

    @pl.when(f > 0)
    def _():
        o_ref[...] += part

    @pl.when(f == pl.num_programs(1) - 1)
    def _():
        o_ref[...] = _rmsnorm_rows(x_ref[...] + o_ref[...], fw_ref[...])


def _ffn(x2d, ffn_norm_w, w_gate, w_up, w_down, final_norm_w, tm, tf):
    m, d = x2d.shape
    dff = w_gate.shape[1]
    return pl.pallas_call(
        _ffn_kernel,
        grid=(m // tm, dff // tf),
        in_specs=[pl.BlockSpec((tm, d), lambda i, f: (i, 0)),
                  pl.BlockSpec((1, d), lambda i, f: (0, 0)),
                  pl.BlockSpec((d, tf), lambda i, f: (0, f)),
                  pl.BlockSpec((d, tf), lambda i, f: (0, f)),
                  pl.BlockSpec((tf, d), lambda i, f: (f, 0)),
                  pl.BlockSpec((1, d), lambda i, f: (0, 0))],
        out_specs=pl.BlockSpec((tm, d), lambda i, f: (i, 0)),
        out_shape=jax.ShapeDtypeStruct((m, d), F32),
        scratch_shapes=[pltpu.VMEM((tm, d), BF16)],
        compiler_params=_params("parallel", "arbitrary"),
        name="ffn",
    )(x2d, ffn_norm_w, w_gate, w_up, w_down, final_norm_w)


def _tile(n, pref):
    return pref if n % pref == 0 else n


def _layer(x, lam_init, attn_norm_w, w_in, conv_w, a_log, dt_bias, dn_norm_w, lam_q1, lam_k1,
           lam_q2, lam_k2, da_norm_w, w_out, ffn_norm_w, w_gate, w_up, w_down, final_norm_w):
    b, t, d = x.shape
    m = b * t
    x2d = x.reshape(m, d)
    off_z = 3 * DN_WIDTH
    off_b = off_z + DN_WIDTH
    off_q = off_b + 2 * DN_HEADS
    nw = attn_norm_w.reshape(1, d)

    w_dn = w_in[:, :off_b].astype(BF16)
    w_gt = w_in[:, off_b:off_q].T.astype(BF16)
    w_da = w_in[:, off_q:].astype(BF16)
    n_da = w_da.shape[1]
    da_scale = jnp.concatenate([jnp.full((1, DA_HEADS * 2 * DA_DIM), DA_DIM ** -0.5, F32),
                                jnp.ones((1, n_da - DA_HEADS * 2 * DA_DIM), F32)], axis=1)

    tm = _tile(m, 1024)
    qkvz = _norm_matmul(x2d, nw, w_dn, jnp.ones((1, off_b), F32), F32, tm, 1024, "in_proj_dn")
    da = _norm_matmul(x2d, nw, w_da, da_scale, BF16, tm, 1024, "in_proj_da")
    grow, gcol = _gates(x, nw, w_gt, a_log.reshape(DN_HEADS, 1), dt_bias.reshape(DN_HEADS, 1),
                        _tile(t, 256))

    qkvz = qkvz.reshape(b, t, off_b)
    u, w, qd, kd, a = _dn_intra(qkvz, conv_w, gcol, grow, _tile(t, 1024))
    o_dn = _dn_scan(u, w, qd, kd, a, qkvz, gcol, dn_norm_w.reshape(1, DN_DIM), _tile(t, 512))

    o_da = _attention(da.reshape(b, t, n_da), lam_q1.reshape(1, DA_DIM), lam_k1.reshape(1, DA_DIM),
                      lam_q2.reshape(1, DA_DIM), lam_k2.reshape(1, DA_DIM),
                      da_norm_w.reshape(1, DA_VDIM), lam_init, _tile(t, 512))

    w_out16 = w_out.astype(BF16)
    x2d = _out_proj(o_dn.reshape(m, DN_WIDTH), o_da.reshape(m, DA_WIDTH),
                    w_out16[:DN_WIDTH], w_out16[DN_WIDTH:], x2d, tm, _tile(d, 1024))
    return x2d


def kernel(x, attn_norm_w, w_in, conv_w, a_log, dt_bias, dn_norm_w, lam_q1, lam_k1, lam_q2, lam_k2,
           da_norm_w, w_out, ffn_norm_w, w_gate, w_up, w_down, final_norm_w):
    b, t, d = x.shape
    depth = w_in.shape[0]
    assert depth == 1, "the final rmsnorm is fused into the last layer's FFN kernel"
    lam_init = 0.8 - 0.6 * math.exp(-0.3 * 0)
    x2d = _layer(x, lam_init, attn_norm_w[0], w_in[0], conv_w[0], a_log[0], dt_bias[0],
                 dn_norm_w[0], lam_q1[0], lam_k1[0], lam_q2[0], lam_k2[0], da_norm_w[0],
                 w_out[0], ffn_norm_w[0], w_gate[0], w_up[0], w_down[0], final_norm_w)
    dff = w_gate.shape[2]
    out = _ffn(x2d, ffn_norm_w[0].reshape(1, d), w_gate[0].astype(BF16), w_up[0].astype(BF16),
               w_down[0].astype(BF16), final_norm_w.reshape(1, d), _tile(b * t, 512),
               _tile(dff, 512))
    return out.reshape(b, t, d)
```

```python
import functools
import math

import jax
import jax.numpy as jnp
from jax import lax
from jax.experimental import pallas as pl
from jax.experimental.pallas import tpu as pltpu

F32 = jnp.float32
BF16 = jnp.bfloat16
EPS = 1e-6
NEG_BIG = -1e30
LOG2E = math.log2(math.e)

DN_HEADS = 8
DN_DIM = 128
DN_WIDTH = DN_HEADS * DN_DIM
CONV_WIDTH = 4
CHUNK = 64
DA_HEADS = 4
DA_DIM = 128
DA_VDIM = 2 * DA_DIM
DA_WIDTH = DA_HEADS * DA_VDIM
LANES = 128
SUBLANES = 8
GROUP = 256
VMEM_LIMIT = 56 * 1024 * 1024

_NT = (((1,), (1,)), ((), ()))
_TN = (((0,), (0,)), ((), ()))


def _params(*sem):
    return pltpu.CompilerParams(dimension_semantics=sem, vmem_limit_bytes=VMEM_LIMIT)


def _sigmoid(x):
    return 1.0 / (1.0 + jnp.exp(-x))


def _rmsnorm_rows(x, w):
    return x * lax.rsqrt(jnp.mean(x * x, axis=-1, keepdims=True) + EPS) * w


def _norm_matmul_kernel(x_ref, nw_ref, w_ref, cs_ref, o_ref, h_ref):
    @pl.when(pl.program_id(1) == 0)
    def _():
        h_ref[...] = _rmsnorm_rows(x_ref[...], nw_ref[...]).astype(BF16)

    acc = jnp.dot(h_ref[...], w_ref[...], preferred_element_type=F32)
    o_ref[...] = (acc * cs_ref[...]).astype(o_ref.dtype)


def _norm_matmul(x2d, norm_w, w_bf16, col_scale, out_dtype, tm, tn, name):
    m, d = x2d.shape
    n = w_bf16.shape[1]
    return pl.pallas_call(
        _norm_matmul_kernel,
        grid=(m // tm, n // tn),
        in_specs=[
            pl.BlockSpec((tm, d), lambda i, j: (i, 0)),
            pl.BlockSpec((1, d), lambda i, j: (0, 0)),
            pl.BlockSpec((d, tn), lambda i, j: (0, j)),
            pl.BlockSpec((1, tn), lambda i, j: (0, j)),
        ],
        out_specs=pl.BlockSpec((tm, tn), lambda i, j: (i, j)),
        out_shape=jax.ShapeDtypeStruct((m, n), out_dtype),
        scratch_shapes=[pltpu.VMEM((tm, d), BF16)],
        compiler_params=_params("parallel", "arbitrary"),
        name=name,
    )(x2d, norm_w, w_bf16, col_scale)


def _gates_kernel(x_ref, nw_ref, wt_ref, alog_ref, dtb_ref, row_ref, col_ref):
    h = _rmsnorm_rows(x_ref[...], nw_ref[...]).astype(BF16)
    r = lax.dot_general(wt_ref[...], h, _NT, preferred_element_type=F32)
    beta = _sigmoid(r[0:DN_HEADS])
    a = r[DN_HEADS:2 * DN_HEADS] + dtb_ref[...]
    softplus = jnp.maximum(a, 0.0) + jnp.log1p(jnp.exp(-jnp.abs(a)))
    g = -jnp.exp(alog_ref[...]) * softplus
    tm = g.shape[1]
    ii = lax.broadcasted_iota(jnp.int32, (tm, tm), 0)
    jj = lax.broadcasted_iota(jnp.int32, (tm, tm), 1)
    same = (ii // CHUNK) == (jj // CHUNK)
    cum = (same & (ii <= jj)).astype(F32)
    tot = same.astype(F32)
    gc = jnp.dot(g, cum, precision=lax.Precision.HIGHEST, preferred_element_type=F32)
    gl = jnp.dot(g, tot, precision=lax.Precision.HIGHEST, preferred_element_type=F32)
    rows = jnp.concatenate([beta, gc, gl], axis=0)
    row_ref[...] = rows
    padded = jnp.concatenate([rows, jnp.zeros((LANES - 3 * DN_HEADS, tm), F32)], axis=0)
    col_ref[...] = padded.T


def _gates(x3d, norm_w, w_gate_t, a_log_col, dt_bias_col, tm):
    b, t, d = x3d.shape
    return pl.pallas_call(
        _gates_kernel,
        grid=(b, t // tm),
        in_specs=[
            pl.BlockSpec((None, tm, d), lambda i, j: (i, j, 0)),
            pl.BlockSpec((1, d), lambda i, j: (0, 0)),
            pl.BlockSpec((2 * DN_HEADS, d), lambda i, j: (0, 0)),
            pl.BlockSpec((DN_HEADS, 1), lambda i, j: (0, 0)),
            pl.BlockSpec((DN_HEADS, 1), lambda i, j: (0, 0)),
        ],
        out_specs=[
            pl.BlockSpec((None, 3 * DN_HEADS, tm), lambda i, j: (i, 0, j)),
            pl.BlockSpec((None, tm, LANES), lambda i, j: (i, j, 0)),
        ],
        out_shape=[
            jax.ShapeDtypeStruct((b, 3 * DN_HEADS, t), F32),
            jax.ShapeDtypeStruct((b, t, LANES), F32),
        ],
        compiler_params=_params("parallel", "arbitrary"),
        name="dn_gates",
    )(x3d, norm_w, w_gate_t, a_log_col, dt_bias_col)


def _dn_intra_kernel(q_ref, k_ref, v_ref, qh_ref, kh_ref, vh_ref, cwq_ref, cwk_ref, cwv_ref,
                     gcol_ref, grow_ref, u_ref, w_ref, qd_ref, kd_ref, a_ref, xs_ref):
    head = pl.program_id(1)
    first = pl.program_id(2) == 0
    tb = q_ref.shape[0]
    halo = SUBLANES
    for idx, (x_ref, xh_ref) in enumerate(((q_ref, qh_ref), (k_ref, kh_ref), (v_ref, vh_ref))):
        xs_ref[idx, 0:halo, :] = jnp.where(first, 0.0, xh_ref[...])
        xs_ref[idx, halo:halo + tb, :] = x_ref[...]

    lane = lax.broadcasted_iota(jnp.int32, (GROUP, LANES), 1)
    sel_beta = (lane == head).astype(F32)
    sel_gc = (lane == head + DN_HEADS).astype(F32)
    sel_gl = (lane == head + 2 * DN_HEADS).astype(F32)
    ri = lax.broadcasted_iota(jnp.int32, (GROUP, GROUP), 0)
    ci = lax.broadcasted_iota(jnp.int32, (GROUP, GROUP), 1)
    same = (ri // CHUNK) == (ci // CHUNK)
    incl = same & (ri >= ci)
    strict = same & (ri > ci)

    def conv_silu(idx, cw_ref, r0):
        base = r0 + halo - (CONV_WIDTH - 1)
        acc = xs_ref[idx, base:base + GROUP, :] * cw_ref[0:1, :]
        for i in range(1, CONV_WIDTH):
            acc = acc + xs_ref[idx, base + i:base + i + GROUP, :] * cw_ref[i:i + 1, :]
        return acc * _sigmoid(acc)

    for gi in range(tb // GROUP):
        r0 = gi * GROUP
        q = conv_silu(0, cwq_ref, r0)
        k = conv_silu(1, cwk_ref, r0)
        v = conv_silu(2, cwv_ref, r0)
        q = q * lax.rsqrt(jnp.sum(q * q, axis=-1, keepdims=True) + EPS) * (DN_DIM ** -0.5)
        k = k * lax.rsqrt(jnp.sum(k * k, axis=-1, keepdims=True) + EPS)

        gates = gcol_ref[r0:r0 + GROUP, :]
        beta = jnp.sum(gates * sel_beta, axis=-1, keepdims=True)
        gc = jnp.sum(gates * sel_gc, axis=-1, keepdims=True)
        gl = jnp.sum(gates * sel_gl, axis=-1, keepdims=True)
        gc_row = grow_ref[pl.ds(head + DN_HEADS, 1), r0:r0 + GROUP]

        decay = jnp.exp(jnp.where(incl, gc - gc_row, NEG_BIG))
        kb = k * beta
        k16 = k.astype(BF16)
        gram = lax.dot_general(jnp.concatenate([kb, q], axis=0).astype(BF16), k16, _NT,
                               preferred_element_type=F32)
        n_pow = jnp.where(strict, -(gram[:GROUP] * decay), 0.0)
        a_qk = gram[GROUP:] * decay

        egc = jnp.exp(gc)
        x = jnp.concatenate([v * beta, kb * egc], axis=1)
        levels = int(math.log2(CHUNK))
        for lvl in range(levels):
            n16 = n_pow.astype(BF16)
            x = x + jnp.dot(n16, x.astype(BF16), preferred_element_type=F32)
            if lvl + 1 < levels:
                n_pow = jnp.dot(n16, n16, preferred_element_type=F32)

        u_ref[r0:r0 + GROUP, :] = x[:, :DN_DIM]
        w_ref[r0:r0 + GROUP, :] = x[:, DN_DIM:].astype(BF16)
        qd_ref[r0:r0 + GROUP, :] = (q * egc).astype(BF16)
        kd_ref[r0:r0 + GROUP, :] = (k * jnp.exp(gl - gc)).astype(BF16)
        for c in range(GROUP // CHUNK):
            lo = c * CHUNK
            a_ref[r0 + lo:r0 + lo + CHUNK, :] = a_qk[lo:lo + CHUNK, lo:lo + CHUNK].astype(BF16)


def _dn_intra(qkvz, conv_w, gcol, grow, tb):
    b, t, _ = qkvz.shape
    hb = tb // SUBLANES
    nh = DN_HEADS

    def tok(off):
        return pl.BlockSpec((None, tb, DN_DIM), lambda i, h, j: (i, j, off + h))

    def halo(off):
        return pl.BlockSpec((None, SUBLANES, DN_DIM),
                            lambda i, h, j: (i, jnp.maximum(j * hb - 1, 0), off + h))

    def cw(off):
        return pl.BlockSpec((CONV_WIDTH, DN_DIM), lambda i, h, j: (0, off + h))

    out_tok = pl.BlockSpec((None, tb, DN_DIM), lambda i, h, j: (i, j, h))
    return pl.pallas_call(
        _dn_intra_kernel,
        grid=(b, nh, t // tb),
        in_specs=[tok(0), tok(nh), tok(2 * nh), halo(0), halo(nh), halo(2 * nh),
                  cw(0), cw(nh), cw(2 * nh),
                  pl.BlockSpec((None, tb, LANES), lambda i, h, j: (i, j, 0)),
                  pl.BlockSpec((None, 3 * nh, tb), lambda i, h, j: (i, 0, j))],
        out_specs=[out_tok, out_tok, out_tok, out_tok,
                   pl.BlockSpec((None, None, tb, CHUNK), lambda i, h, j: (i, h, j, 0))],
        out_shape=[jax.ShapeDtypeStruct((b, t, DN_WIDTH), F32),
                   jax.ShapeDtypeStruct((b, t, DN_WIDTH), BF16),
                   jax.ShapeDtypeStruct((b, t, DN_WIDTH), BF16),
                   jax.ShapeDtypeStruct((b, t, DN_WIDTH), BF16),
                   jax.ShapeDtypeStruct((b, nh, t, CHUNK), BF16)],
        scratch_shapes=[pltpu.VMEM((3, tb + SUBLANES, DN_DIM), F32)],
        compiler_params=_params("parallel", "parallel", "arbitrary"),
        name="dn_intra",
    )(qkvz, qkvz, qkvz, qkvz, qkvz, qkvz, conv_w, conv_w, conv_w, gcol, grow)


def _dn_scan_kernel(u_ref, w_ref, qd_ref, kd_ref, a_ref, z_ref, gcol_ref, nw_ref, o_ref, s_ref):
    @pl.when(pl.program_id(1) == 0)
    def _():
        s_ref[...] = jnp.zeros_like(s_ref)

    tb = u_ref.shape[0]
    nw = nw_ref[...]

    def chunk(c, carry):
        rows = pl.ds(pl.multiple_of(c * CHUNK, CHUNK), CHUNK)
        gl_row = gcol_ref[pl.ds(pl.multiple_of(c * CHUNK, CHUNK), 1), :]
        for h in range(DN_HEADS):
            cols = slice(h * DN_DIM, (h + 1) * DN_DIM)
            s = s_ref[h]
            wq = jnp.concatenate([w_ref[rows, cols], qd_ref[rows, cols]], axis=0)
            proj = jnp.dot(wq, s.astype(BF16), preferred_element_type=F32)
            v_new = u_ref[rows, cols] - proj[:CHUNK]
            v16 = v_new.astype(BF16)
            o = proj[CHUNK:] + jnp.dot(a_ref[h, rows, :], v16, preferred_element_type=F32)
            decay = jnp.exp(gl_row[:, 2 * DN_HEADS + h:2 * DN_HEADS + h + 1])
            s_ref[h] = s * decay + lax.dot_general(kd_ref[rows, cols], v16, _TN,
                                                   preferred_element_type=F32)
            z = z_ref[rows, cols]
            o_ref[rows, cols] = (_rmsnorm_rows(o, nw) * (z * _sigmoid(z))).astype(o_ref.dtype)
        return carry

    lax.fori_loop(0, tb // CHUNK, chunk, 0)


def _dn_scan(u, w, qd, kd, a, qkvz, gcol, dn_norm_w, tb):
    b, t, _ = u.shape
    tok = pl.BlockSpec((None, tb, DN_WIDTH), lambda i, j: (i, j, 0))
    z_block = 3 * DN_WIDTH // DN_WIDTH
    return pl.pallas_call(
        _dn_scan_kernel,
        grid=(b, t // tb),
        in_specs=[tok, tok, tok, tok,
                  pl.BlockSpec((None, DN_HEADS, tb, CHUNK), lambda i, j: (i, 0, j, 0)),
                  pl.BlockSpec((None, tb, DN_WIDTH), lambda i, j: (i, j, z_block)),
                  pl.BlockSpec((None, tb, LANES), lambda i, j: (i, j, 0)),
                  pl.BlockSpec((1, DN_DIM), lambda i, j: (0, 0))],
        out_specs=tok,
        out_shape=jax.ShapeDtypeStruct((b, t, DN_WIDTH), BF16),
        scratch_shapes=[pltpu.VMEM((DN_HEADS, DN_DIM, DN_DIM), F32)],
        compiler_params=_params("parallel", "arbitrary"),
        name="dn_scan",
    )(u, w, qd, kd, a, qkvz, gcol, dn_norm_w)


def _attn_kernel(q_ref, k_ref, v_ref, lq1_ref, lk1_ref, lq2_ref, lk2_ref, nw_ref, o_ref,
                 m_ref, l_ref, acc_ref, *, lam_init):
    head = pl.program_id(1)
    qi = pl.program_id(2)
    tq = q_ref.shape[0]
    tk = tq
    slope = jnp.float32(0.0)
    for hh in range(DA_HEADS):
        slope = jnp.where(head == hh, jnp.float32(2.0 ** (-8.0 * (hh + 1) / DA_HEADS)), slope)

    m_ref[...] = jnp.full_like(m_ref, NEG_BIG)
    l_ref[...] = jnp.zeros_like(l_ref)
    acc_ref[...] = jnp.zeros_like(acc_ref)
    kpos = lax.broadcasted_iota(jnp.int32, (1, tk), 1)
    slope2 = slope * LOG2E

    def tile(j, masked):
        k0 = pl.multiple_of(j * tk, tk)
        bias = slope2 * (kpos + (j - qi) * tk).astype(F32)
        v = v_ref[pl.ds(k0, tk), :]
        for c in range(2):
            cols = slice(c * DA_DIM, (c + 1) * DA_DIM)
            s = lax.dot_general(q_ref[:, cols], k_ref[pl.ds(k0, tk), cols], _NT,
                                preferred_element_type=F32) + bias
            if masked:
                rr = lax.broadcasted_iota(jnp.int32, (tq, tk), 0)
                cc = lax.broadcasted_iota(jnp.int32, (tq, tk), 1)
                s = jnp.where(cc <= rr, s, NEG_BIG)
            m_prev = m_ref[c]
            m_new = jnp.maximum(m_prev, jnp.max(s, axis=-1, keepdims=True))
            alpha = jnp.exp2(m_prev - m_new)
            p = jnp.exp2(s - pltpu.repeat(m_new, tk // LANES, axis=1))
            l_ref[c] = alpha * l_ref[c] + jnp.sum(p, axis=-1, keepdims=True)
            acc_ref[c] = (pltpu.repeat(alpha, DA_VDIM // LANES, axis=1) * acc_ref[c]
                          + jnp.dot(p.astype(BF16), v, preferred_element_type=F32))
            m_ref[c] = m_new

    def body(j, carry):
        tile(j, False)
        return carry

    lax.fori_loop(0, qi, body, 0)
    tile(qi, True)

    lam = (jnp.exp(jnp.sum(lq1_ref[...] * lk1_ref[...], axis=-1, keepdims=True))
           - jnp.exp(jnp.sum(lq2_ref[...] * lk2_ref[...], axis=-1, keepdims=True)) + lam_init)
    rep = DA_VDIM // LANES
    o = (acc_ref[0] / pltpu.repeat(l_ref[0], rep, axis=1)
         - lam * (acc_ref[1] / pltpu.repeat(l_ref[1], rep, axis=1)))
    o_ref[...] = (_rmsnorm_rows(o, nw_ref[...]) * (1.0 - lam_init)).astype(o_ref.dtype)


def _attention(da, lam_q1, lam_k1, lam_q2, lam_k2, da_norm_w, lam_init, tq):
    b, t, _ = da.shape
    nh = DA_HEADS
    vec = pl.BlockSpec((1, DA_DIM), lambda i, h, j: (0, 0))
    return pl.pallas_call(
        functools.partial(_attn_kernel, lam_init=lam_init),
        grid=(b, nh, t // tq),
        in_specs=[pl.BlockSpec((None, tq, DA_VDIM), lambda i, h, j: (i, j, h)),
                  pl.BlockSpec((None, t, DA_VDIM), lambda i, h, j: (i, 0, nh + h)),
                  pl.BlockSpec((None, t, DA_VDIM), lambda i, h, j: (i, 0, 2 * nh + h)),
                  vec, vec, vec, vec,
                  pl.BlockSpec((1, DA_VDIM), lambda i, h, j: (0, 0))],
        out_specs=pl.BlockSpec((None, tq, DA_VDIM), lambda i, h, j: (i, j, h)),
        out_shape=jax.ShapeDtypeStruct((b, t, DA_WIDTH), BF16),
        scratch_shapes=[pltpu.VMEM((2, tq, LANES), F32), pltpu.VMEM((2, tq, LANES), F32),
                        pltpu.VMEM((2, tq, DA_VDIM), F32)],
        compiler_params=_params("parallel", "parallel", "arbitrary"),
        name="diff_attn",
    )(da, da, da, lam_q1, lam_k1, lam_q2, lam_k2, da_norm_w)


def _out_proj_kernel(a_ref, b_ref, wa_ref, wb_ref, x_ref, o_ref):
    acc = jnp.dot(a_ref[...], wa_ref[...], preferred_element_type=F32)
    acc = acc + jnp.dot(b_ref[...], wb_ref[...], preferred_element_type=F32)
    o_ref[...] = x_ref[...] + acc


def _out_proj(o_dn, o_da, w_a, w_b, x2d, tm, tn):
    m, d = x2d.shape
    ka, kb = o_dn.shape[1], o_da.shape[1]
    return pl.pallas_call(
        _out_proj_kernel,
        grid=(m // tm, d // tn),
        in_specs=[pl.BlockSpec((tm, ka), lambda i, j: (i, 0)),
                  pl.BlockSpec((tm, kb), lambda i, j: (i, 0)),
                  pl.BlockSpec((ka, tn), lambda i, j: (0, j)),
                  pl.BlockSpec((kb, tn), lambda i, j: (0, j)),
                  pl.BlockSpec((tm, tn), lambda i, j: (i, j))],
        out_specs=pl.BlockSpec((tm, tn), lambda i, j: (i, j)),
        out_shape=jax.ShapeDtypeStruct((m, d), F32),
        compiler_params=_params("parallel", "arbitrary"),
        name="out_proj",
    )(o_dn, o_da, w_a, w_b, x2d)


def _ffn_kernel(x_ref, nw_ref, wg_ref, wu_ref, wd_ref, fw_ref, o_ref, h_ref):
    f = pl.program_id(1)

    @pl.when(f == 0)
    def _():
        h_ref[...] = _rmsnorm_rows(x_ref[...], nw_ref[...]).astype(BF16)

    h = h_ref[...]
    g = jnp.dot(h, wg_ref[...], preferred_element_type=F32)
    u = jnp.dot(h, wu_ref[...], preferred_element_type=F32)
    act = (g * _sigmoid(g) * u).astype(BF16)
    part = jnp.dot(act, wd_ref[...], preferred_element_type=F32)

    @pl.when(f == 0)
    def _():
        o_ref[...] = part

    @pl.when(f > 0)
    def _():
        o_ref[...] += part

    @pl.when(f == pl.num_programs(1) - 1)
    def _():
        o_ref[...] = _rmsnorm_rows(x_ref[...] + o_ref[...], fw_ref[...])


def _ffn(x2d, ffn_norm_w, w_gate, w_up, w_down, final_norm_w, tm, tf):
    m, d = x2d.shape
    dff = w_gate.shape[1]
    return pl.pallas_call(
        _ffn_kernel,
        grid=(m // tm, dff // tf),
        in_specs=[pl.BlockSpec((tm, d), lambda i, f: (i, 0)),
                  pl.BlockSpec((1, d), lambda i, f: (0, 0)),
                  pl.BlockSpec((d, tf), lambda i, f: (0, f)),
                  pl.BlockSpec((d, tf), lambda i, f: (0, f)),
                  pl.BlockSpec((tf, d), lambda i, f: (f, 0)),
                  pl.BlockSpec((1, d), lambda i, f: (0, 0))],
        out_specs=pl.BlockSpec((tm, d), lambda i, f: (i, 0)),
        out_shape=jax.ShapeDtypeStruct((m, d), F32),
        scratch_shapes=[pltpu.VMEM((tm, d), BF16)],
        compiler_params=_params("parallel", "arbitrary"),
        name="ffn",
    )(x2d, ffn_norm_w, w_gate, w_up, w_down, final_norm_w)


def _tile(n, pref):
    return pref if n % pref == 0 else n


def _layer(x, lam_init, attn_norm_w, w_in, conv_w, a_log, dt_bias, dn_norm_w, lam_q1, lam_k1,
           lam_q2, lam_k2, da_norm_w, w_out, ffn_norm_w, w_gate, w_up, w_down, final_norm_w):
    b, t, d = x.shape
    m = b * t
    x2d = x.reshape(m, d)
    off_z = 3 * DN_WIDTH
    off_b = off_z + DN_WIDTH
    off_q = off_b + 2 * DN_HEADS
    nw = attn_norm_w.reshape(1, d)

    w_dn = w_in[:, :off_b].astype(BF16)
    w_gt = w_in[:, off_b:off_q].T.astype(BF16)
    w_da = w_in[:, off_q:].astype(BF16)
    n_da = w_da.shape[1]
    da_scale = jnp.concatenate([jnp.full((1, DA_HEADS * 2 * DA_DIM), LOG2E * DA_DIM ** -0.5, F32),
                                jnp.ones((1, n_da - DA_HEADS * 2 * DA_DIM), F32)], axis=1)

    tm = _tile(m, 1024)
    qkvz = _norm_matmul(x2d, nw, w_dn, jnp.ones((1, off_b), F32), F32, tm, 1024, "in_proj_dn")
    da = _norm_matmul(x2d, nw, w_da, da_scale, BF16, tm, 1024, "in_proj_da")
    grow, gcol = _gates(x, nw, w_gt, a_log.reshape(DN_HEADS, 1), dt_bias.reshape(DN_HEADS, 1),
                        _tile(t, 256))

    qkvz = qkvz.reshape(b, t, off_b)
    u, w, qd, kd, a = _dn_intra(qkvz, conv_w, gcol, grow, _tile(t, 1024))
    o_dn = _dn_scan(u, w, qd, kd, a, qkvz, gcol, dn_norm_w.reshape(1, DN_DIM), _tile(t, 512))

    o_da = _attention(da.reshape(b, t, n_da), lam_q1.reshape(1, DA_DIM), lam_k1.reshape(1, DA_DIM),
                      lam_q2.reshape(1, DA_DIM), lam_k2.reshape(1, DA_DIM),
                      da_norm_w.reshape(1, DA_VDIM), lam_init, _tile(t, 512))

    w_out16 = w_out.astype(BF16)
    x2d = _out_proj(o_dn.reshape(m, DN_WIDTH), o_da.reshape(m, DA_WIDTH),
                    w_out16[:DN_WIDTH], w_out16[DN_WIDTH:], x2d, tm, _tile(d, 1024))
    return x2d


def kernel(x, attn_norm_w, w_in, conv_w, a_log, dt_bias, dn_norm_w, lam_q1, lam_k1, lam_q2, lam_k2,
           da_norm_w, w_out, ffn_norm_w, w_gate, w_up, w_down, final_norm_w):
    b, t, d = x.shape
    depth = w_in.shape[0]
    assert depth == 1, "the final rmsnorm is fused into the last layer's FFN kernel"
    lam_init = 0.8 - 0.6 * math.exp(-0.3 * 0)
    x2d = _layer(x, lam_init, attn_norm_w[0], w_in[0], conv_w[0], a_log[0], dt_bias[0],
                 dn_norm_w[0], lam_q1[0], lam_k1[0], lam_q2[0], lam_k2[0], da_norm_w[0],
                 w_out[0], ffn_norm_w[0], w_gate[0], w_up[0], w_down[0], final_norm_w)
    dff = w_gate.shape[2]
    out = _ffn(x2d, ffn_norm_w[0].reshape(1, d), w_gate[0].astype(BF16), w_up[0].astype(BF16),
               w_down[0].astype(BF16), final_norm_w.reshape(1, d), _tile(b * t, 512),
               _tile(dff, 512))
    return out.reshape(b, t, d)
```

```python
import functools
import math

import jax
import jax.numpy as jnp
from jax import lax
from jax.experimental import pallas as pl
from jax.experimental.pallas import tpu as pltpu

F32 = jnp.float32
BF16 = jnp.bfloat16
EPS = 1e-6
NEG_BIG = -1e30
LOG2E = math.log2(math.e)

DN_HEADS = 8
DN_DIM = 128
DN_WIDTH = DN_HEADS * DN_DIM
CONV_WIDTH = 4
CHUNK = 64
DA_HEADS = 4
DA_DIM = 128
DA_VDIM = 2 * DA_DIM
DA_WIDTH = DA_HEADS * DA_VDIM
LANES = 128
SUBLANES = 8
GROUP = 256
VMEM_LIMIT = 56 * 1024 * 1024

_NT = (((1,), (1,)), ((), ()))
_TN = (((0,), (0,)), ((), ()))


def _params(*sem):
    return pltpu.CompilerParams(dimension_semantics=sem, vmem_limit_bytes=VMEM_LIMIT)


def _sigmoid(x):
    return 1.0 / (1.0 + jnp.exp(-x))


def _lane_tile(x, n):
    return jnp.concatenate([x] * n, axis=1)


def _rmsnorm_rows(x, w):
    return x * lax.rsqrt(jnp.mean(x * x, axis=-1, keepdims=True) + EPS) * w


def _norm_matmul_kernel(x_ref, nw_ref, w_ref, cs_ref, o_ref, h_ref):
    @pl.when(pl.program_id(1) == 0)
    def _():
        h_ref[...] = _rmsnorm_rows(x_ref[...], nw_ref[...]).astype(BF16)

    acc = jnp.dot(h_ref[...], w_ref[...], preferred_element_type=F32)
    o_ref[...] = (acc * cs_ref[...]).astype(o_ref.dtype)


def _norm_matmul(x2d, norm_w, w_bf16, col_scale, out_dtype, tm, tn, name):
    m, d = x2d.shape
    n = w_bf16.shape[1]
    return pl.pallas_call(
        _norm_matmul_kernel,
        grid=(m // tm, n // tn),
        in_specs=[
            pl.BlockSpec((tm, d), lambda i, j: (i, 0)),
            pl.BlockSpec((1, d), lambda i, j: (0, 0)),
            pl.BlockSpec((d, tn), lambda i, j: (0, j)),
            pl.BlockSpec((1, tn), lambda i, j: (0, j)),
        ],
        out_specs=pl.BlockSpec((tm, tn), lambda i, j: (i, j)),
        out_shape=jax.ShapeDtypeStruct((m, n), out_dtype),
        scratch_shapes=[pltpu.VMEM((tm, d), BF16)],
        compiler_params=_params("parallel", "arbitrary"),
        name=name,
    )(x2d, norm_w, w_bf16, col_scale)


def _gates_kernel(x_ref, nw_ref, wt_ref, alog_ref, dtb_ref, row_ref, col_ref):
    h = _rmsnorm_rows(x_ref[...], nw_ref[...]).astype(BF16)
    r = lax.dot_general(wt_ref[...], h, _NT, preferred_element_type=F32)
    beta = _sigmoid(r[0:DN_HEADS])
    a = r[DN_HEADS:2 * DN_HEADS] + dtb_ref[...]
    softplus = jnp.maximum(a, 0.0) + jnp.log1p(jnp.exp(-jnp.abs(a)))
    g = -jnp.exp(alog_ref[...]) * softplus
    tm = g.shape[1]
    ii = lax.broadcasted_iota(jnp.int32, (tm, tm), 0)
    jj = lax.broadcasted_iota(jnp.int32, (tm, tm), 1)
    same = (ii // CHUNK) == (jj // CHUNK)
    cum = (same & (ii <= jj)).astype(F32)
    tot = same.astype(F32)
    gc = jnp.dot(g, cum, precision=lax.Precision.HIGHEST, preferred_element_type=F32)
    gl = jnp.dot(g, tot, precision=lax.Precision.HIGHEST, preferred_element_type=F32)
    rows = jnp.concatenate([beta, gc, gl], axis=0)
    row_ref[...] = rows
    padded = jnp.concatenate([rows, jnp.zeros((LANES - 3 * DN_HEADS, tm), F32)], axis=0)
    col_ref[...] = padded.T


def _gates(x3d, norm_w, w_gate_t, a_log_col, dt_bias_col, tm):
    b, t, d = x3d.shape
    return pl.pallas_call(
        _gates_kernel,
        grid=(b, t // tm),
        in_specs=[
            pl.BlockSpec((None, tm, d), lambda i, j: (i, j, 0)),
            pl.BlockSpec((1, d), lambda i, j: (0, 0)),
            pl.BlockSpec((2 * DN_HEADS, d), lambda i, j: (0, 0)),
            pl.BlockSpec((DN_HEADS, 1), lambda i, j: (0, 0)),
            pl.BlockSpec((DN_HEADS, 1), lambda i, j: (0, 0)),
        ],
        out_specs=[
            pl.BlockSpec((None, 3 * DN_HEADS, tm), lambda i, j: (i, 0, j)),
            pl.BlockSpec((None, tm, LANES), lambda i, j: (i, j, 0)),
        ],
        out_shape=[
            jax.ShapeDtypeStruct((b, 3 * DN_HEADS, t), F32),
            jax.ShapeDtypeStruct((b, t, LANES), F32),
        ],
        compiler_params=_params("parallel", "arbitrary"),
        name="dn_gates",
    )(x3d, norm_w, w_gate_t, a_log_col, dt_bias_col)


def _dn_intra_kernel(q_ref, k_ref, v_ref, qh_ref, kh_ref, vh_ref, cwq_ref, cwk_ref, cwv_ref,
                     gcol_ref, grow_ref, u_ref, w_ref, qd_ref, kd_ref, a_ref, xs_ref):
    head = pl.program_id(1)
    first = pl.program_id(2) == 0
    tb = q_ref.shape[0]
    halo = SUBLANES
    for idx, (x_ref, xh_ref) in enumerate(((q_ref, qh_ref), (k_ref, kh_ref), (v_ref, vh_ref))):
        xs_ref[idx, 0:halo, :] = jnp.where(first, 0.0, xh_ref[...])
        xs_ref[idx, halo:halo + tb, :] = x_ref[...]

    lane = lax.broadcasted_iota(jnp.int32, (GROUP, LANES), 1)
    sel_beta = (lane == head).astype(F32)
    sel_gc = (lane == head + DN_HEADS).astype(F32)
    sel_gl = (lane == head + 2 * DN_HEADS).astype(F32)
    ri = lax.broadcasted_iota(jnp.int32, (GROUP, GROUP), 0)
    ci = lax.broadcasted_iota(jnp.int32, (GROUP, GROUP), 1)
    same = (ri // CHUNK) == (ci // CHUNK)
    incl = same & (ri >= ci)
    strict = same & (ri > ci)

    def conv_silu(idx, cw_ref, r0):
        base = r0 + halo - (CONV_WIDTH - 1)
        acc = xs_ref[idx, base:base + GROUP, :] * cw_ref[0:1, :]
        for i in range(1, CONV_WIDTH):
            acc = acc + xs_ref[idx, base + i:base + i + GROUP, :] * cw_ref[i:i + 1, :]
        return acc * _sigmoid(acc)

    n_groups = tb // GROUP
    levels = int(math.log2(CHUNK))
    n_pows = [None] * n_groups
    xs = [None] * n_groups

    def prepare(gi):
        r0 = gi * GROUP
        q = conv_silu(0, cwq_ref, r0)
        k = conv_silu(1, cwk_ref, r0)
        v = conv_silu(2, cwv_ref, r0)
        q = q * lax.rsqrt(jnp.sum(q * q, axis=-1, keepdims=True) + EPS) * (DN_DIM ** -0.5)
        k = k * lax.rsqrt(jnp.sum(k * k, axis=-1, keepdims=True) + EPS)

        gates = gcol_ref[r0:r0 + GROUP, :]
        beta = jnp.sum(gates * sel_beta, axis=-1, keepdims=True)
        gc = jnp.sum(gates * sel_gc, axis=-1, keepdims=True)
        gl = jnp.sum(gates * sel_gl, axis=-1, keepdims=True)
        gc_row = grow_ref[pl.ds(head + DN_HEADS, 1), r0:r0 + GROUP]

        decay = jnp.exp(jnp.where(incl, gc - gc_row, NEG_BIG))
        kb = k * beta
        k16 = k.astype(BF16)
        gram = lax.dot_general(jnp.concatenate([kb, q], axis=0).astype(BF16), k16, _NT,
                               preferred_element_type=F32)
        n_pows[gi] = jnp.where(strict, -(gram[:GROUP] * decay), 0.0)
        a_qk = gram[GROUP:] * decay

        egc = jnp.exp(gc)
        xs[gi] = jnp.concatenate([v * beta, kb * egc], axis=1)
        qd_ref[r0:r0 + GROUP, :] = (q * egc).astype(BF16)
        kd_ref[r0:r0 + GROUP, :] = (k * jnp.exp(gl - gc)).astype(BF16)
        for c in range(GROUP // CHUNK):
            lo = c * CHUNK
            a_ref[r0 + lo:r0 + lo + CHUNK, :] = a_qk[lo:lo + CHUNK, lo:lo + CHUNK].astype(BF16)

    def level(gi, lvl):
        n16 = n_pows[gi].astype(BF16)
        xs[gi] = xs[gi] + jnp.dot(n16, xs[gi].astype(BF16), preferred_element_type=F32)
        if lvl + 1 < levels:
            n_pows[gi] = jnp.dot(n16, n16, preferred_element_type=F32)

    def finish(gi):
        r0 = gi * GROUP
        u_ref[r0:r0 + GROUP, :] = xs[gi][:, :DN_DIM]
        w_ref[r0:r0 + GROUP, :] = xs[gi][:, DN_DIM:].astype(BF16)

    for t in range(n_groups + levels + 1):
        for gi in range(n_groups):
            stage = t - gi
            if stage == 0:
                prepare(gi)
            elif 1 <= stage <= levels:
                level(gi, stage - 1)
            elif stage == levels + 1:
                finish(gi)


def _dn_intra(qkvz, conv_w, gcol, grow, tb):
    b, t, _ = qkvz.shape
    hb = tb // SUBLANES
    nh = DN_HEADS

    def tok(off):
        return pl.BlockSpec((None, tb, DN_DIM), lambda i, h, j: (i, j, off + h))

    def halo(off):
        return pl.BlockSpec((None, SUBLANES, DN_DIM),
                            lambda i, h, j: (i, jnp.maximum(j * hb - 1, 0), off + h))

    def cw(off):
        return pl.BlockSpec((CONV_WIDTH, DN_DIM), lambda i, h, j: (0, off + h))

    out_tok = pl.BlockSpec((None, tb, DN_DIM), lambda i, h, j: (i, j, h))
    return pl.pallas_call(
        _dn_intra_kernel,
        grid=(b, nh, t // tb),
        in_specs=[tok(0), tok(nh), tok(2 * nh), halo(0), halo(nh), halo(2 * nh),
                  cw(0), cw(nh), cw(2 * nh),
                  pl.BlockSpec((None, tb, LANES), lambda i, h, j: (i, j, 0)),
                  pl.BlockSpec((None, 3 * nh, tb), lambda i, h, j: (i, 0, j))],
        out_specs=[out_tok, out_tok, out_tok, out_tok,
                   pl.BlockSpec((None, None, tb, CHUNK), lambda i, h, j: (i, h, j, 0))],
        out_shape=[jax.ShapeDtypeStruct((b, t, DN_WIDTH), F32),
                   jax.ShapeDtypeStruct((b, t, DN_WIDTH), BF16),
                   jax.ShapeDtypeStruct((b, t, DN_WIDTH), BF16),
                   jax.ShapeDtypeStruct((b, t, DN_WIDTH), BF16),
                   jax.ShapeDtypeStruct((b, nh, t, CHUNK), BF16)],
        scratch_shapes=[pltpu.VMEM((3, tb + SUBLANES, DN_DIM), F32)],
        compiler_params=_params("parallel", "parallel", "arbitrary"),
        name="dn_intra",
    )(qkvz, qkvz, qkvz, qkvz, qkvz, qkvz, conv_w, conv_w, conv_w, gcol, grow)


def _dn_scan_kernel(u_ref, w_ref, qd_ref, kd_ref, a_ref, z_ref, gcol_ref, nw_ref, o_ref, s_ref):
    @pl.when(pl.program_id(1) == 0)
    def _():
        s_ref[...] = jnp.zeros_like(s_ref)

    tb = u_ref.shape[0]
    nw = nw_ref[...]

    def chunk(c, carry):
        rows = pl.ds(pl.multiple_of(c * CHUNK, CHUNK), CHUNK)
        gl_row = gcol_ref[pl.ds(pl.multiple_of(c * CHUNK, CHUNK), 1), :]
        heads = range(DN_HEADS)
        cols = [slice(h * DN_DIM, (h + 1) * DN_DIM) for h in heads]
        s = [s_ref[h] for h in heads]
        proj = [jnp.dot(jnp.concatenate([w_ref[rows, cols[h]], qd_ref[rows, cols[h]]], axis=0),
                        s[h].astype(BF16), preferred_element_type=F32) for h in heads]
        v16 = [(u_ref[rows, cols[h]] - proj[h][:CHUNK]).astype(BF16) for h in heads]
        upd = [lax.dot_general(kd_ref[rows, cols[h]], v16[h], _TN, preferred_element_type=F32)
               for h in heads]
        for h in heads:
            decay = jnp.exp(gl_row[:, 2 * DN_HEADS + h:2 * DN_HEADS + h + 1])
            s_ref[h] = s[h] * decay + upd[h]
        o = [proj[h][CHUNK:] + jnp.dot(a_ref[h, rows, :], v16[h], preferred_element_type=F32)
             for h in heads]
        for h in heads:
            z = z_ref[rows, cols[h]]
            o_ref[rows, cols[h]] = (_rmsnorm_rows(o[h], nw) * (z * _sigmoid(z))).astype(o_ref.dtype)
        return carry

    lax.fori_loop(0, tb // CHUNK, chunk, 0)


def _dn_scan(u, w, qd, kd, a, qkvz, gcol, dn_norm_w, tb):
    b, t, _ = u.shape
    tok = pl.BlockSpec((None, tb, DN_WIDTH), lambda i, j: (i, j, 0))
    z_block = 3 * DN_WIDTH // DN_WIDTH
    return pl.pallas_call(
        _dn_scan_kernel,
        grid=(b, t // tb),
        in_specs=[tok, tok, tok, tok,
                  pl.BlockSpec((None, DN_HEADS, tb, CHUNK), lambda i, j: (i, 0, j, 0)),
                  pl.BlockSpec((None, tb, DN_WIDTH), lambda i, j: (i, j, z_block)),
                  pl.BlockSpec((None, tb, LANES), lambda i, j: (i, j, 0)),
                  pl.BlockSpec((1, DN_DIM), lambda i, j: (0, 0))],
        out_specs=tok,
        out_shape=jax.ShapeDtypeStruct((b, t, DN_WIDTH), BF16),
        scratch_shapes=[pltpu.VMEM((DN_HEADS, DN_DIM, DN_DIM), F32)],
        compiler_params=_params("parallel", "arbitrary"),
        name="dn_scan",
    )(u, w, qd, kd, a, qkvz, gcol, dn_norm_w)


def _attn_kernel(q_ref, k_ref, v_ref, lq1_ref, lk1_ref, lq2_ref, lk2_ref, nw_ref, o_ref,
                 m_ref, l_ref, acc_ref, *, lam_init):
    head = pl.program_id(1)
    qi = pl.program_id(2)
    tq = q_ref.shape[0]
    tk = tq
    slope = jnp.float32(0.0)
    for hh in range(DA_HEADS):
        slope = jnp.where(head == hh, jnp.float32(2.0 ** (-8.0 * (hh + 1) / DA_HEADS)), slope)

    m_ref[...] = jnp.full_like(m_ref, NEG_BIG)
    l_ref[...] = jnp.zeros_like(l_ref)
    acc_ref[...] = jnp.zeros_like(acc_ref)
    kpos = lax.broadcasted_iota(jnp.int32, (1, tk), 1)
    slope2 = slope * LOG2E

    def tile(j, masked):
        k0 = pl.multiple_of(j * tk, tk)
        bias = slope2 * (kpos + (j - qi) * tk).astype(F32)
        v = v_ref[pl.ds(k0, tk), :]
        maps = range(2)
        s = [lax.dot_general(q_ref[:, c * DA_DIM:(c + 1) * DA_DIM],
                             k_ref[pl.ds(k0, tk), c * DA_DIM:(c + 1) * DA_DIM], _NT,
                             preferred_element_type=F32) + bias for c in maps]
        if masked:
            rr = lax.broadcasted_iota(jnp.int32, (tq, tk), 0)
            cc = lax.broadcasted_iota(jnp.int32, (tq, tk), 1)
            s = [jnp.where(cc <= rr, s[c], NEG_BIG) for c in maps]
        m_prev = [m_ref[c] for c in maps]
        m_new = [jnp.maximum(m_prev[c], jnp.max(s[c], axis=-1, keepdims=True)) for c in maps]
        p = [jnp.exp2(s[c] - _lane_tile(m_new[c], tk // LANES)) for c in maps]
        pv = [jnp.dot(p[c].astype(BF16), v, preferred_element_type=F32) for c in maps]
        for c in maps:
            alpha = jnp.exp2(m_prev[c] - m_new[c])
            l_ref[c] = alpha * l_ref[c] + jnp.sum(p[c], axis=-1, keepdims=True)
            acc_ref[c] = _lane_tile(alpha, DA_VDIM // LANES) * acc_ref[c] + pv[c]
            m_ref[c] = m_new[c]

    def body(j, carry):
        tile(j, False)
        return carry

    lax.fori_loop(0, qi, body, 0)
    tile(qi, True)

    lam = (jnp.exp(jnp.sum(lq1_ref[...] * lk1_ref[...], axis=-1, keepdims=True))
           - jnp.exp(jnp.sum(lq2_ref[...] * lk2_ref[...], axis=-1, keepdims=True)) + lam_init)
    rep = DA_VDIM // LANES
    o = (acc_ref[0] / _lane_tile(l_ref[0], rep)
         - lam * (acc_ref[1] / _lane_tile(l_ref[1], rep)))
    o_ref[...] = (_rmsnorm_rows(o, nw_ref[...]) * (1.0 - lam_init)).astype(o_ref.dtype)


def _attention(da, lam_q1, lam_k1, lam_q2, lam_k2, da_norm_w, lam_init, tq):
    b, t, _ = da.shape
    nh = DA_HEADS
    vec = pl.BlockSpec((1, DA_DIM), lambda i, h, j: (0, 0))
    return pl.pallas_call(
        functools.partial(_attn_kernel, lam_init=lam_init),
        grid=(b, nh, t // tq),
        in_specs=[pl.BlockSpec((None, tq, DA_VDIM), lambda i, h, j: (i, j, h)),
                  pl.BlockSpec((None, t, DA_VDIM), lambda i, h, j: (i, 0, nh + h)),
                  pl.BlockSpec((None, t, DA_VDIM), lambda i, h, j: (i, 0, 2 * nh + h)),
                  vec, vec, vec, vec,
                  pl.BlockSpec((1, DA_VDIM), lambda i, h, j: (0, 0))],
        out_specs=pl.BlockSpec((None, tq, DA_VDIM), lambda i, h, j: (i, j, h)),
        out_shape=jax.ShapeDtypeStruct((b, t, DA_WIDTH), BF16),
        scratch_shapes=[pltpu.VMEM((2, tq, LANES), F32), pltpu.VMEM((2, tq, LANES), F32),
                        pltpu.VMEM((2, tq, DA_VDIM), F32)],
        compiler_params=_params("parallel", "parallel", "arbitrary"),
        name="diff_attn",
    )(da, da, da, lam_q1, lam_k1, lam_q2, lam_k2, da_norm_w)


def _out_proj_kernel(a_ref, b_ref, wa_ref, wb_ref, x_ref, o_ref):
    acc = jnp.dot(a_ref[...], wa_ref[...], preferred_element_type=F32)
    acc = acc + jnp.dot(b_ref[...], wb_ref[...], preferred_element_type=F32)
    o_ref[...] = x_ref[...] + acc


def _out_proj(o_dn, o_da, w_a, w_b, x2d, tm, tn):
    m, d = x2d.shape
    ka, kb = o_dn.shape[1], o_da.shape[1]
    return pl.pallas_call(
        _out_proj_kernel,
        grid=(m // tm, d // tn),
        in_specs=[pl.BlockSpec((tm, ka), lambda i, j: (i, 0)),
                  pl.BlockSpec((tm, kb), lambda i, j: (i, 0)),
                  pl.BlockSpec((ka, tn), lambda i, j: (0, j)),
                  pl.BlockSpec((kb, tn), lambda i, j: (0, j)),
                  pl.BlockSpec((tm, tn), lambda i, j: (i, j))],
        out_specs=pl.BlockSpec((tm, tn), lambda i, j: (i, j)),
        out_shape=jax.ShapeDtypeStruct((m, d), F32),
        compiler_params=_params("parallel", "arbitrary"),
        name="out_proj",
    )(o_dn, o_da, w_a, w_b, x2d)


def _ffn_kernel(x_ref, nw_ref, wg_ref, wu_ref, wd_ref, fw_ref, o_ref, h_ref):
    f = pl.program_id(1)

    @pl.when(f == 0)
    def _():
        h_ref[...] = _rmsnorm_rows(x_ref[...], nw_ref[...]).astype(BF16)

    h = h_ref[...]
    g = jnp.dot(h, wg_ref[...], preferred_element_type=F32)
    u = jnp.dot(h, wu_ref[...], preferred_element_type=F32)
    act = (g * _sigmoid(g) * u).astype(BF16)
    part = jnp.dot(act, wd_ref[...], preferred_element_type=F32)

    @pl.when(f == 0)
    def _():
        o_ref[...] = part

    @pl.when(f > 0)
    def _():
        o_ref[...] += part

    @pl.when(f == pl.num_programs(1) - 1)
    def _():
        o_ref[...] = _rmsnorm_rows(x_ref[...] + o_ref[...], fw_ref[...])


def _ffn(x2d, ffn_norm_w, w_gate, w_up, w_down, final_norm_w, tm, tf):
    m, d = x2d.shape
    dff = w_gate.shape[1]
    return pl.pallas_call(
        _ffn_kernel,
        grid=(m // tm, dff // tf),
        in_specs=[pl.BlockSpec((tm, d), lambda i, f: (i, 0)),
                  pl.BlockSpec((1, d), lambda i, f: (0, 0)),
                  pl.BlockSpec((d, tf), lambda i, f: (0, f)),
                  pl.BlockSpec((d, tf), lambda i, f: (0, f)),
                  pl.BlockSpec((tf, d), lambda i, f: (f, 0)),
                  pl.BlockSpec((1, d), lambda i, f: (0, 0))],
        out_specs=pl.BlockSpec((tm, d), lambda i, f: (i, 0)),
        out_shape=jax.ShapeDtypeStruct((m, d), F32),
        scratch_shapes=[pltpu.VMEM((tm, d), BF16)],
        compiler_params=_params("parallel", "arbitrary"),
        name="ffn",
    )(x2d, ffn_norm_w, w_gate, w_up, w_down, final_norm_w)


def _tile(n, pref):
    return pref if n % pref == 0 else n


def _layer(x, lam_init, attn_norm_w, w_in, conv_w, a_log, dt_bias, dn_norm_w, lam_q1, lam_k1,
           lam_q2, lam_k2, da_norm_w, w_out, ffn_norm_w, w_gate, w_up, w_down, final_norm_w):
    b, t, d = x.shape
    m = b * t
    x2d = x.reshape(m, d)
    off_z = 3 * DN_WIDTH
    off_b = off_z + DN_WIDTH
    off_q = off_b + 2 * DN_HEADS
    nw = attn_norm_w.reshape(1, d)

    w_dn = w_in[:, :off_b].astype(BF16)
    w_gt = w_in[:, off_b:off_q].T.astype(BF16)
    w_da = w_in[:, off_q:].astype(BF16)
    n_da = w_da.shape[1]
    da_scale = jnp.concatenate([jnp.full((1, DA_HEADS * 2 * DA_DIM), LOG2E * DA_DIM ** -0.5, F32),
                                jnp.ones((1, n_da - DA_HEADS * 2 * DA_DIM), F32)], axis=1)

    tm = _tile(m, 1024)
    qkvz = _norm_matmul(x2d, nw, w_dn, jnp.ones((1, off_b), F32), F32, tm, 1024, "in_proj_dn")
    da = _norm_matmul(x2d, nw, w_da, da_scale, BF16, tm, 1024, "in_proj_da")
    grow, gcol = _gates(x, nw, w_gt, a_log.reshape(DN_HEADS, 1), dt_bias.reshape(DN_HEADS, 1),
                        _tile(t, 256))

    qkvz = qkvz.reshape(b, t, off_b)
    u, w, qd, kd, a = _dn_intra(qkvz, conv_w, gcol, grow, _tile(t, 2048))
    o_dn = _dn_scan(u, w, qd, kd, a, qkvz, gcol, dn_norm_w.reshape(1, DN_DIM), _tile(t, 512))

    o_da = _attention(da.reshape(b, t, n_da), lam_q1.reshape(1, DA_DIM), lam_k1.reshape(1, DA_DIM),
                      lam_q2.reshape(1, DA_DIM), lam_k2.reshape(1, DA_DIM),
                      da_norm_w.reshape(1, DA_VDIM), lam_init, _tile(t, 512))

    w_out16 = w_out.astype(BF16)
    x2d = _out_proj(o_dn.reshape(m, DN_WIDTH), o_da.reshape(m, DA_WIDTH),
                    w_out16[:DN_WIDTH], w_out16[DN_WIDTH:], x2d, tm, _tile(d, 1024))
    return x2d


def kernel(x, attn_norm_w, w_in, conv_w, a_log, dt_bias, dn_norm_w, lam_q1, lam_k1, lam_q2, lam_k2,
           da_norm_w, w_out, ffn_norm_w, w_gate, w_up, w_down, final_norm_w):
    b, t, d = x.shape
    depth = w_in.shape[0]
    assert depth == 1, "the final rmsnorm is fused into the last layer's FFN kernel"
    lam_init = 0.8 - 0.6 * math.exp(-0.3 * 0)
    x2d = _layer(x, lam_init, attn_norm_w[0], w_in[0], conv_w[0], a_log[0], dt_bias[0],
                 dn_norm_w[0], lam_q1[0], lam_k1[0], lam_q2[0], lam_k2[0], da_norm_w[0],
                 w_out[0], ffn_norm_w[0], w_gate[0], w_up[0], w_down[0], final_norm_w)
    dff = w_gate.shape[2]
    out = _ffn(x2d, ffn_norm_w[0].reshape(1, d), w_gate[0].astype(BF16), w_up[0].astype(BF16),
               w_down[0].astype(BF16), final_norm_w.reshape(1, d), _tile(b * t, 512),
               _tile(dff, 512))
    return out.reshape(b, t, d)
```

```python
import functools
import math

import jax
import jax.numpy as jnp
from jax import lax
from jax.experimental import pallas as pl
from jax.experimental.pallas import tpu as pltpu

F32 = jnp.float32
BF16 = jnp.bfloat16
EPS = 1e-6
NEG_BIG = -1e30
LOG2E = math.log2(math.e)

DN_HEADS = 8
DN_DIM = 128
DN_WIDTH = DN_HEADS * DN_DIM
CONV_WIDTH = 4
CHUNK = 64
DA_HEADS = 4
DA_DIM = 128
DA_VDIM = 2 * DA_DIM
DA_WIDTH = DA_HEADS * DA_VDIM
LANES = 128
SUBLANES = 8
GROUP = 256
VMEM_LIMIT = 56 * 1024 * 1024

_NT = (((1,), (1,)), ((), ()))
_TN = (((0,), (0,)), ((), ()))


def _params(*sem):
    return pltpu.CompilerParams(dimension_semantics=sem, vmem_limit_bytes=VMEM_LIMIT)


def _sigmoid(x):
    return 1.0 / (1.0 + jnp.exp(-x))


def _lane_tile(x, n):
    return jnp.concatenate([x] * n, axis=1)


def _rmsnorm_rows(x, w):
    return x * lax.rsqrt(jnp.mean(x * x, axis=-1, keepdims=True) + EPS) * w


def _norm_matmul_kernel(x_ref, nw_ref, w_ref, cs_ref, o_ref, h_ref):
    @pl.when(pl.program_id(1) == 0)
    def _():
        h_ref[...] = _rmsnorm_rows(x_ref[...], nw_ref[...]).astype(BF16)

    acc = jnp.dot(h_ref[...], w_ref[...], preferred_element_type=F32)
    o_ref[...] = (acc * cs_ref[...]).astype(o_ref.dtype)


def _norm_matmul(x2d, norm_w, w_bf16, col_scale, out_dtype, tm, tn, name):
    m, d = x2d.shape
    n = w_bf16.shape[1]
    return pl.pallas_call(
        _norm_matmul_kernel,
        grid=(m // tm, n // tn),
        in_specs=[
            pl.BlockSpec((tm, d), lambda i, j: (i, 0)),
            pl.BlockSpec((1, d), lambda i, j: (0, 0)),
            pl.BlockSpec((d, tn), lambda i, j: (0, j)),
            pl.BlockSpec((1, tn), lambda i, j: (0, j)),
        ],
        out_specs=pl.BlockSpec((tm, tn), lambda i, j: (i, j)),
        out_shape=jax.ShapeDtypeStruct((m, n), out_dtype),
        scratch_shapes=[pltpu.VMEM((tm, d), BF16)],
        compiler_params=_params("parallel", "arbitrary"),
        name=name,
    )(x2d, norm_w, w_bf16, col_scale)


def _gates_kernel(x_ref, nw_ref, wt_ref, alog_ref, dtb_ref, row_ref, col_ref):
    h = _rmsnorm_rows(x_ref[...], nw_ref[...]).astype(BF16)
    r = lax.dot_general(wt_ref[...], h, _NT, preferred_element_type=F32)
    beta = _sigmoid(r[0:DN_HEADS])
    a = r[DN_HEADS:2 * DN_HEADS] + dtb_ref[...]
    softplus = jnp.maximum(a, 0.0) + jnp.log1p(jnp.exp(-jnp.abs(a)))
    g = -jnp.exp(alog_ref[...]) * softplus
    tm = g.shape[1]
    ii = lax.broadcasted_iota(jnp.int32, (tm, tm), 0)
    jj = lax.broadcasted_iota(jnp.int32, (tm, tm), 1)
    same = (ii // CHUNK) == (jj // CHUNK)
    cum = (same & (ii <= jj)).astype(F32)
    tot = same.astype(F32)
    gc = jnp.dot(g, cum, precision=lax.Precision.HIGHEST, preferred_element_type=F32)
    gl = jnp.dot(g, tot, precision=lax.Precision.HIGHEST, preferred_element_type=F32)
    rows = jnp.concatenate([beta, gc, gl], axis=0)
    row_ref[...] = rows
    padded = jnp.concatenate([rows, jnp.zeros((LANES - 3 * DN_HEADS, tm), F32)], axis=0)
    col_ref[...] = padded.T


def _gates(x3d, norm_w, w_gate_t, a_log_col, dt_bias_col, tm):
    b, t, d = x3d.shape
    return pl.pallas_call(
        _gates_kernel,
        grid=(b, t // tm),
        in_specs=[
            pl.BlockSpec((None, tm, d), lambda i, j: (i, j, 0)),
            pl.BlockSpec((1, d), lambda i, j: (0, 0)),
            pl.BlockSpec((2 * DN_HEADS, d), lambda i, j: (0, 0)),
            pl.BlockSpec((DN_HEADS, 1), lambda i, j: (0, 0)),
            pl.BlockSpec((DN_HEADS, 1), lambda i, j: (0, 0)),
        ],
        out_specs=[
            pl.BlockSpec((None, 3 * DN_HEADS, tm), lambda i, j: (i, 0, j)),
            pl.BlockSpec((None, tm, LANES), lambda i, j: (i, j, 0)),
        ],
        out_shape=[
            jax.ShapeDtypeStruct((b, 3 * DN_HEADS, t), F32),
            jax.ShapeDtypeStruct((b, t, LANES), F32),
        ],
        compiler_params=_params("parallel", "arbitrary"),
        name="dn_gates",
    )(x3d, norm_w, w_gate_t, a_log_col, dt_bias_col)


def _dn_intra_kernel(q_ref, k_ref, v_ref, qh_ref, kh_ref, vh_ref, cwq_ref, cwk_ref, cwv_ref,
                     gcol_ref, grow_ref, u_ref, w_ref, qd_ref, kd_ref, a_ref, xs_ref):
    head = pl.program_id(1)
    first = pl.program_id(2) == 0
    tb = q_ref.shape[0]
    halo = SUBLANES
    for idx, (x_ref, xh_ref) in enumerate(((q_ref, qh_ref), (k_ref, kh_ref), (v_ref, vh_ref))):
        xs_ref[idx, 0:halo, :] = jnp.where(first, 0.0, xh_ref[...])
        xs_ref[idx, halo:halo + tb, :] = x_ref[...]

    lane = lax.broadcasted_iota(jnp.int32, (GROUP, LANES), 1)
    sel_beta = (lane == head).astype(F32)
    sel_gc = (lane == head + DN_HEADS).astype(F32)
    sel_gl = (lane == head + 2 * DN_HEADS).astype(F32)
    ri = lax.broadcasted_iota(jnp.int32, (GROUP, GROUP), 0)
    ci = lax.broadcasted_iota(jnp.int32, (GROUP, GROUP), 1)
    same = (ri // CHUNK) == (ci // CHUNK)
    incl = same & (ri >= ci)
    strict = same & (ri > ci)

    def conv_silu(idx, cw_ref, r0):
        base = r0 + halo - (CONV_WIDTH - 1)
        acc = xs_ref[idx, base:base + GROUP, :] * cw_ref[0:1, :]
        for i in range(1, CONV_WIDTH):
            acc = acc + xs_ref[idx, base + i:base + i + GROUP, :] * cw_ref[i:i + 1, :]
        return acc * _sigmoid(acc)

    n_groups = tb // GROUP
    levels = int(math.log2(CHUNK))
    n_pows = [None] * n_groups
    xs = [None] * n_groups

    def prepare(gi):
        r0 = gi * GROUP
        q = conv_silu(0, cwq_ref, r0)
        k = conv_silu(1, cwk_ref, r0)
        v = conv_silu(2, cwv_ref, r0)
        q = q * lax.rsqrt(jnp.sum(q * q, axis=-1, keepdims=True) + EPS) * (DN_DIM ** -0.5)
        k = k * lax.rsqrt(jnp.sum(k * k, axis=-1, keepdims=True) + EPS)

        gates = gcol_ref[r0:r0 + GROUP, :]
        beta = jnp.sum(gates * sel_beta, axis=-1, keepdims=True)
        gc = jnp.sum(gates * sel_gc, axis=-1, keepdims=True)
        gl = jnp.sum(gates * sel_gl, axis=-1, keepdims=True)
        gc_row = grow_ref[pl.ds(head + DN_HEADS, 1), r0:r0 + GROUP]

        decay = jnp.exp(jnp.where(incl, gc - gc_row, NEG_BIG))
        kb = k * beta
        k16 = k.astype(BF16)
        gram = lax.dot_general(jnp.concatenate([kb, q], axis=0).astype(BF16), k16, _NT,
                               preferred_element_type=F32)
        n_pows[gi] = jnp.where(strict, -(gram[:GROUP] * decay), 0.0)
        a_qk = gram[GROUP:] * decay

        egc = jnp.exp(gc)
        xs[gi] = jnp.concatenate([v * beta, kb * egc], axis=1)
        qd_ref[r0:r0 + GROUP, :] = (q * egc).astype(BF16)
        kd_ref[r0:r0 + GROUP, :] = (k * jnp.exp(gl - gc)).astype(BF16)
        for c in range(GROUP // CHUNK):
            lo = c * CHUNK
            a_ref[r0 + lo:r0 + lo + CHUNK, :] = a_qk[lo:lo + CHUNK, lo:lo + CHUNK].astype(BF16)

    def level(gi, lvl):
        n16 = n_pows[gi].astype(BF16)
        xs[gi] = xs[gi] + jnp.dot(n16, xs[gi].astype(BF16), preferred_element_type=F32)
        if lvl + 1 < levels:
            n_pows[gi] = jnp.dot(n16, n16, preferred_element_type=F32)

    def finish(gi):
        r0 = gi * GROUP
        u_ref[r0:r0 + GROUP, :] = xs[gi][:, :DN_DIM]
        w_ref[r0:r0 + GROUP, :] = xs[gi][:, DN_DIM:].astype(BF16)

    for t in range(n_groups + levels + 1):
        for gi in range(n_groups):
            stage = t - gi
            if stage == 0:
                prepare(gi)
            elif 1 <= stage <= levels:
                level(gi, stage - 1)
            elif stage == levels + 1:
                finish(gi)


def _dn_intra(qkvz, conv_w, gcol, grow, tb):
    b, t, _ = qkvz.shape
    hb = tb // SUBLANES
    nh = DN_HEADS

    def tok(off):
        return pl.BlockSpec((None, tb, DN_DIM), lambda i, h, j: (i, j, off + h))

    def halo(off):
        return pl.BlockSpec((None, SUBLANES, DN_DIM),
                            lambda i, h, j: (i, jnp.maximum(j * hb - 1, 0), off + h))

    def cw(off):
        return pl.BlockSpec((CONV_WIDTH, DN_DIM), lambda i, h, j: (0, off + h))

    out_tok = pl.BlockSpec((None, tb, DN_DIM), lambda i, h, j: (i, j, h))
    return pl.pallas_call(
        _dn_intra_kernel,
        grid=(b, nh, t // tb),
        in_specs=[tok(0), tok(nh), tok(2 * nh), halo(0), halo(nh), halo(2 * nh),
                  cw(0), cw(nh), cw(2 * nh),
                  pl.BlockSpec((None, tb, LANES), lambda i, h, j: (i, j, 0)),
                  pl.BlockSpec((None, 3 * nh, tb), lambda i, h, j: (i, 0, j))],
        out_specs=[out_tok, out_tok, out_tok, out_tok,
                   pl.BlockSpec((None, None, tb, CHUNK), lambda i, h, j: (i, h, j, 0))],
        out_shape=[jax.ShapeDtypeStruct((b, t, DN_WIDTH), F32),
                   jax.ShapeDtypeStruct((b, t, DN_WIDTH), BF16),
                   jax.ShapeDtypeStruct((b, t, DN_WIDTH), BF16),
                   jax.ShapeDtypeStruct((b, t, DN_WIDTH), BF16),
                   jax.ShapeDtypeStruct((b, nh, t, CHUNK), BF16)],
        scratch_shapes=[pltpu.VMEM((3, tb + SUBLANES, DN_DIM), F32)],
        compiler_params=_params("parallel", "parallel", "arbitrary"),
        name="dn_intra",
    )(qkvz, qkvz, qkvz, qkvz, qkvz, qkvz, conv_w, conv_w, conv_w, gcol, grow)


def _dn_scan_kernel(u_ref, w_ref, qd_ref, kd_ref, a_ref, z_ref, gcol_ref, nw_ref, o_ref, s_ref):
    @pl.when(pl.program_id(1) == 0)
    def _():
        s_ref[...] = jnp.zeros_like(s_ref)

    tb = u_ref.shape[0]
    nw = nw_ref[...]

    def chunk(c, carry):
        rows = pl.ds(pl.multiple_of(c * CHUNK, CHUNK), CHUNK)
        gl_row = gcol_ref[pl.ds(pl.multiple_of(c * CHUNK, CHUNK), 1), :]
        heads = range(DN_HEADS)
        cols = [slice(h * DN_DIM, (h + 1) * DN_DIM) for h in heads]
        s = [s_ref[h] for h in heads]
        proj = [jnp.dot(jnp.concatenate([w_ref[rows, cols[h]], qd_ref[rows, cols[h]]], axis=0),
                        s[h].astype(BF16), preferred_element_type=F32) for h in heads]
        v16 = [(u_ref[rows, cols[h]] - proj[h][:CHUNK]).astype(BF16) for h in heads]
        upd = [lax.dot_general(kd_ref[rows, cols[h]], v16[h], _TN, preferred_element_type=F32)
               for h in heads]
        for h in heads:
            decay = jnp.exp(gl_row[:, 2 * DN_HEADS + h:2 * DN_HEADS + h + 1])
            s_ref[h] = s[h] * decay + upd[h]
        o = [proj[h][CHUNK:] + jnp.dot(a_ref[h, rows, :], v16[h], preferred_element_type=F32)
             for h in heads]
        for h in heads:
            z = z_ref[rows, cols[h]]
            o_ref[rows, cols[h]] = (_rmsnorm_rows(o[h], nw) * (z * _sigmoid(z))).astype(o_ref.dtype)
        return carry

    lax.fori_loop(0, tb // CHUNK, chunk, 0)


def _dn_scan(u, w, qd, kd, a, qkvz, gcol, dn_norm_w, tb):
    b, t, _ = u.shape
    tok = pl.BlockSpec((None, tb, DN_WIDTH), lambda i, j: (i, j, 0))
    z_block = 3 * DN_WIDTH // DN_WIDTH
    return pl.pallas_call(
        _dn_scan_kernel,
        grid=(b, t // tb),
        in_specs=[tok, tok, tok, tok,
                  pl.BlockSpec((None, DN_HEADS, tb, CHUNK), lambda i, j: (i, 0, j, 0)),
                  pl.BlockSpec((None, tb, DN_WIDTH), lambda i, j: (i, j, z_block)),
                  pl.BlockSpec((None, tb, LANES), lambda i, j: (i, j, 0)),
                  pl.BlockSpec((1, DN_DIM), lambda i, j: (0, 0))],
        out_specs=tok,
        out_shape=jax.ShapeDtypeStruct((b, t, DN_WIDTH), BF16),
        scratch_shapes=[pltpu.VMEM((DN_HEADS, DN_DIM, DN_DIM), F32)],
        compiler_params=_params("parallel", "arbitrary"),
        name="dn_scan",
    )(u, w, qd, kd, a, qkvz, gcol, dn_norm_w)


def _attn_kernel(q_ref, k_ref, v_ref, lq1_ref, lk1_ref, lq2_ref, lk2_ref, nw_ref, o_ref,
                 m_ref, l_ref, acc_ref, *, lam_init):
    head = pl.program_id(1)
    qi = pl.program_id(2)
    tq = q_ref.shape[0]
    tk = tq
    slope = jnp.float32(0.0)
    for hh in range(DA_HEADS):
        slope = jnp.where(head == hh, jnp.float32(2.0 ** (-8.0 * (hh + 1) / DA_HEADS)), slope)

    m_ref[...] = jnp.full_like(m_ref, NEG_BIG)
    l_ref[...] = jnp.zeros_like(l_ref)
    acc_ref[...] = jnp.zeros_like(acc_ref)
    kpos = lax.broadcasted_iota(jnp.int32, (1, tk), 1)
    slope2 = slope * LOG2E

    def tile(j, masked):
        k0 = pl.multiple_of(j * tk, tk)
        bias = slope2 * (kpos + (j - qi) * tk).astype(F32)
        v = v_ref[pl.ds(k0, tk), :]
        maps = range(2)
        s = [lax.dot_general(q_ref[:, c * DA_DIM:(c + 1) * DA_DIM],
                             k_ref[pl.ds(k0, tk), c * DA_DIM:(c + 1) * DA_DIM], _NT,
                             preferred_element_type=F32) + bias for c in maps]
        if masked:
            rr = lax.broadcasted_iota(jnp.int32, (tq, tk), 0)
            cc = lax.broadcasted_iota(jnp.int32, (tq, tk), 1)
            s = [jnp.where(cc <= rr, s[c], NEG_BIG) for c in maps]
        m_prev = [m_ref[c] for c in maps]
        m_new = [jnp.maximum(m_prev[c], jnp.max(s[c], axis=-1, keepdims=True)) for c in maps]
        p = [jnp.exp2(s[c] - _lane_tile(m_new[c], tk // LANES)) for c in maps]
        pv = [jnp.dot(p[c].astype(BF16), v, preferred_element_type=F32) for c in maps]
        for c in maps:
            alpha = jnp.exp2(m_prev[c] - m_new[c])
            l_ref[c] = alpha * l_ref[c] + jnp.sum(p[c], axis=-1, keepdims=True)
            acc_ref[c] = _lane_tile(alpha, DA_VDIM // LANES) * acc_ref[c] + pv[c]
            m_ref[c] = m_new[c]

    def body(j, carry):
        tile(j, False)
        return carry

    lax.fori_loop(0, qi, body, 0)
    tile(qi, True)

    lam = (jnp.exp(jnp.sum(lq1_ref[...] * lk1_ref[...], axis=-1, keepdims=True))
           - jnp.exp(jnp.sum(lq2_ref[...] * lk2_ref[...], axis=-1, keepdims=True)) + lam_init)
    rep = DA_VDIM // LANES
    o = (acc_ref[0] / _lane_tile(l_ref[0], rep)
         - lam * (acc_ref[1] / _lane_tile(l_ref[1], rep)))
    o_ref[...] = (_rmsnorm_rows(o, nw_ref[...]) * (1.0 - lam_init)).astype(o_ref.dtype)


def _attention(da, lam_q1, lam_k1, lam_q2, lam_k2, da_norm_w, lam_init, tq):
    b, t, _ = da.shape
    nh = DA_HEADS
    vec = pl.BlockSpec((1, DA_DIM), lambda i, h, j: (0, 0))
    return pl.pallas_call(
        functools.partial(_attn_kernel, lam_init=lam_init),
        grid=(b, nh, t // tq),
        in_specs=[pl.BlockSpec((None, tq, DA_VDIM), lambda i, h, j: (i, j, h)),
                  pl.BlockSpec((None, t, DA_VDIM), lambda i, h, j: (i, 0, nh + h)),
                  pl.BlockSpec((None, t, DA_VDIM), lambda i, h, j: (i, 0, 2 * nh + h)),
                  vec, vec, vec, vec,
                  pl.BlockSpec((1, DA_VDIM), lambda i, h, j: (0, 0))],
        out_specs=pl.BlockSpec((None, tq, DA_VDIM), lambda i, h, j: (i, j, h)),
        out_shape=jax.ShapeDtypeStruct((b, t, DA_WIDTH), BF16),
        scratch_shapes=[pltpu.VMEM((2, tq, LANES), F32), pltpu.VMEM((2, tq, LANES), F32),
                        pltpu.VMEM((2, tq, DA_VDIM), F32)],
        compiler_params=_params("parallel", "parallel", "arbitrary"),
        name="diff_attn",
    )(da, da, da, lam_q1, lam_k1, lam_q2, lam_k2, da_norm_w)


def _out_proj_kernel(a_ref, b_ref, wa_ref, wb_ref, x_ref, o_ref):
    acc = jnp.dot(a_ref[...], wa_ref[...], preferred_element_type=F32)
    acc = acc + jnp.dot(b_ref[...], wb_ref[...], preferred_element_type=F32)
    o_ref[...] = x_ref[...] + acc


def _out_proj(o_dn, o_da, w_a, w_b, x2d, tm, tn):
    m, d = x2d.shape
    ka, kb = o_dn.shape[1], o_da.shape[1]
    return pl.pallas_call(
        _out_proj_kernel,
        grid=(m // tm, d // tn),
        in_specs=[pl.BlockSpec((tm, ka), lambda i, j: (i, 0)),
                  pl.BlockSpec((tm, kb), lambda i, j: (i, 0)),
                  pl.BlockSpec((ka, tn), lambda i, j: (0, j)),
                  pl.BlockSpec((kb, tn), lambda i, j: (0, j)),
                  pl.BlockSpec((tm, tn), lambda i, j: (i, j))],
        out_specs=pl.BlockSpec((tm, tn), lambda i, j: (i, j)),
        out_shape=jax.ShapeDtypeStruct((m, d), F32),
        compiler_params=_params("parallel", "arbitrary"),
        name="out_proj",
    )(o_dn, o_da, w_a, w_b, x2d)


def _ffn_kernel(x_ref, nw_ref, wg_ref, wu_ref, wd_ref, fw_ref, o_ref, h_ref):
    f = pl.program_id(1)

    @pl.when(f == 0)
    def _():
        x = x_ref[...]
        h_ref[...] = _rmsnorm_rows(x, nw_ref[...]).astype(BF16)
        o_ref[...] = x

    h = h_ref[...]
    g = jnp.dot(h, wg_ref[...], preferred_element_type=F32)
    u = jnp.dot(h, wu_ref[...], preferred_element_type=F32)
    act = (g * _sigmoid(g) * u).astype(BF16)
    o_ref[...] += jnp.dot(act, wd_ref[...], preferred_element_type=F32)

    @pl.when(f == pl.num_programs(1) - 1)
    def _():
        o_ref[...] = _rmsnorm_rows(o_ref[...], fw_ref[...])


def _ffn(x2d, ffn_norm_w, w_gate, w_up, w_down, final_norm_w, tm, tf):
    m, d = x2d.shape
    dff = w_gate.shape[1]
    return pl.pallas_call(
        _ffn_kernel,
        grid=(m // tm, dff // tf),
        in_specs=[pl.BlockSpec((tm, d), lambda i, f: (i, 0)),
                  pl.BlockSpec((1, d), lambda i, f: (0, 0)),
                  pl.BlockSpec((d, tf), lambda i, f: (0, f)),
                  pl.BlockSpec((d, tf), lambda i, f: (0, f)),
                  pl.BlockSpec((tf, d), lambda i, f: (f, 0)),
                  pl.BlockSpec((1, d), lambda i, f: (0, 0))],
        out_specs=pl.BlockSpec((tm, d), lambda i, f: (i, 0)),
        out_shape=jax.ShapeDtypeStruct((m, d), F32),
        scratch_shapes=[pltpu.VMEM((tm, d), BF16)],
        compiler_params=_params("parallel", "arbitrary"),
        name="ffn",
    )(x2d, ffn_norm_w, w_gate, w_up, w_down, final_norm_w)


def _tile(n, pref):
    return pref if n % pref == 0 else n


def _layer(x, lam_init, attn_norm_w, w_in, conv_w, a_log, dt_bias, dn_norm_w, lam_q1, lam_k1,
           lam_q2, lam_k2, da_norm_w, w_out, ffn_norm_w, w_gate, w_up, w_down, final_norm_w):
    b, t, d = x.shape
    m = b * t
    x2d = x.reshape(m, d)
    off_z = 3 * DN_WIDTH
    off_b = off_z + DN_WIDTH
    off_q = off_b + 2 * DN_HEADS
    nw = attn_norm_w.reshape(1, d)

    w_dn = w_in[:, :off_b].astype(BF16)
    w_gt = w_in[:, off_b:off_q].T.astype(BF16)
    w_da = w_in[:, off_q:].astype(BF16)
    n_da = w_da.shape[1]
    da_scale = jnp.concatenate([jnp.full((1, DA_HEADS * 2 * DA_DIM), LOG2E * DA_DIM ** -0.5, F32),
                                jnp.ones((1, n_da - DA_HEADS * 2 * DA_DIM), F32)], axis=1)

    tm = _tile(m, 1024)
    qkvz = _norm_matmul(x2d, nw, w_dn, jnp.ones((1, off_b), F32), F32, tm, 1024, "in_proj_dn")
    da = _norm_matmul(x2d, nw, w_da, da_scale, BF16, tm, 1024, "in_proj_da")
    grow, gcol = _gates(x, nw, w_gt, a_log.reshape(DN_HEADS, 1), dt_bias.reshape(DN_HEADS, 1),
                        _tile(t, 256))

    qkvz = qkvz.reshape(b, t, off_b)
    u, w, qd, kd, a = _dn_intra(qkvz, conv_w, gcol, grow, _tile(t, 2048))
    o_dn = _dn_scan(u, w, qd, kd, a, qkvz, gcol, dn_norm_w.reshape(1, DN_DIM), _tile(t, 512))

    o_da = _attention(da.reshape(b, t, n_da), lam_q1.reshape(1, DA_DIM), lam_k1.reshape(1, DA_DIM),
                      lam_q2.reshape(1, DA_DIM), lam_k2.reshape(1, DA_DIM),
                      da_norm_w.reshape(1, DA_VDIM), lam_init, _tile(t, 512))

    w_out16 = w_out.astype(BF16)
    x2d = _out_proj(o_dn.reshape(m, DN_WIDTH), o_da.reshape(m, DA_WIDTH),
                    w_out16[:DN_WIDTH], w_out16[DN_WIDTH:], x2d, tm, _tile(d, 1024))
    return x2d


def kernel(x, attn_norm_w, w_in, conv_w, a_log, dt_bias, dn_norm_w, lam_q1, lam_k1, lam_q2, lam_k2,
           da_norm_w, w_out, ffn_norm_w, w_gate, w_up, w_down, final_norm_w):
    b, t, d = x.shape
    depth = w_in.shape[0]
    assert depth == 1, "the final rmsnorm is fused into the last layer's FFN kernel"
    lam_init = 0.8 - 0.6 * math.exp(-0.3 * 0)
    x2d = _layer(x, lam_init, attn_norm_w[0], w_in[0], conv_w[0], a_log[0], dt_bias[0],
                 dn_norm_w[0], lam_q1[0], lam_k1[0], lam_q2[0], lam_k2[0], da_norm_w[0],
                 w_out[0], ffn_norm_w[0], w_gate[0], w_up[0], w_down[0], final_norm_w)
    dff = w_gate.shape[2]
    out = _ffn(x2d, ffn_norm_w[0].reshape(1, d), w_gate[0].astype(BF16), w_up[0].astype(BF16),
               w_down[0].astype(BF16), final_norm_w.reshape(1, d), _tile(b * t, 1024),
               _tile(dff, 512))
    return out.reshape(b, t, d)
```

```python
import functools
import math

import jax
import jax.numpy as jnp
from jax import lax
from jax.experimental import pallas as pl
from jax.experimental.pallas import tpu as pltpu

F32 = jnp.float32
BF16 = jnp.bfloat16
EPS = 1e-6
NEG_BIG = -1e30
LOG2E = math.log2(math.e)

DN_HEADS = 8
DN_DIM = 128
DN_WIDTH = DN_HEADS * DN_DIM
CONV_WIDTH = 4
CHUNK = 64
DA_HEADS = 4
DA_DIM = 128
DA_VDIM = 2 * DA_DIM
DA_WIDTH = DA_HEADS * DA_VDIM
LANES = 128
SUBLANES = 8
GROUP = 256
VMEM_LIMIT = 56 * 1024 * 1024

_NT = (((1,), (1,)), ((), ()))
_TN = (((0,), (0,)), ((), ()))


def _params(*sem):
    return pltpu.CompilerParams(dimension_semantics=sem, vmem_limit_bytes=VMEM_LIMIT)


def _sigmoid(x):
    return 1.0 / (1.0 + jnp.exp(-x))


def _lane_tile(x, n):
    return jnp.concatenate([x] * n, axis=1)


def _rmsnorm_rows(x, w):
    return x * lax.rsqrt(jnp.mean(x * x, axis=-1, keepdims=True) + EPS) * w


def _norm_matmul_kernel(x_ref, nw_ref, w_ref, cs_ref, o_ref, h_ref):
    @pl.when(pl.program_id(1) == 0)
    def _():
        h_ref[...] = _rmsnorm_rows(x_ref[...], nw_ref[...]).astype(BF16)

    acc = jnp.dot(h_ref[...], w_ref[...], preferred_element_type=F32)
    o_ref[...] = (acc * cs_ref[...]).astype(o_ref.dtype)


def _norm_matmul(x2d, norm_w, w_bf16, col_scale, out_dtype, tm, tn, name):
    m, d = x2d.shape
    n = w_bf16.shape[1]
    return pl.pallas_call(
        _norm_matmul_kernel,
        grid=(m // tm, n // tn),
        in_specs=[
            pl.BlockSpec((tm, d), lambda i, j: (i, 0)),
            pl.BlockSpec((1, d), lambda i, j: (0, 0)),
            pl.BlockSpec((d, tn), lambda i, j: (0, j)),
            pl.BlockSpec((1, tn), lambda i, j: (0, j)),
        ],
        out_specs=pl.BlockSpec((tm, tn), lambda i, j: (i, j)),
        out_shape=jax.ShapeDtypeStruct((m, n), out_dtype),
        scratch_shapes=[pltpu.VMEM((tm, d), BF16)],
        compiler_params=_params("parallel", "arbitrary"),
        name=name,
    )(x2d, norm_w, w_bf16, col_scale)


def _gates_kernel(x_ref, nw_ref, wt_ref, alog_ref, dtb_ref, row_ref, col_ref):
    h = _rmsnorm_rows(x_ref[...], nw_ref[...]).astype(BF16)
    r = lax.dot_general(wt_ref[...], h, _NT, preferred_element_type=F32)
    beta = _sigmoid(r[0:DN_HEADS])
    a = r[DN_HEADS:2 * DN_HEADS] + dtb_ref[...]
    softplus = jnp.maximum(a, 0.0) + jnp.log1p(jnp.exp(-jnp.abs(a)))
    g = -jnp.exp(alog_ref[...]) * softplus
    tm = g.shape[1]
    ii = lax.broadcasted_iota(jnp.int32, (tm, tm), 0)
    jj = lax.broadcasted_iota(jnp.int32, (tm, tm), 1)
    same = (ii // CHUNK) == (jj // CHUNK)
    sums = jnp.concatenate([same & (ii <= jj), same], axis=1).astype(BF16)
    g1 = g.astype(BF16)
    r1 = g - g1.astype(F32)
    g2 = r1.astype(BF16)
    g3 = (r1 - g2.astype(F32)).astype(BF16)
    pieces = jnp.concatenate([g1, g2, g3, jnp.zeros_like(g1)], axis=0)
    acc = jnp.dot(pieces, sums, preferred_element_type=F32)
    nh = DN_HEADS
    acc = acc[0:nh] + acc[nh:2 * nh] + acc[2 * nh:3 * nh]
    gc, gl = acc[:, :tm], acc[:, tm:]
    rows = jnp.concatenate([beta, gc, gl], axis=0)
    row_ref[...] = rows
    padded = jnp.concatenate([rows, jnp.zeros((LANES - 3 * DN_HEADS, tm), F32)], axis=0)
    col_ref[...] = padded.T


def _gates(x3d, norm_w, w_gate_t, a_log_col, dt_bias_col, tm):
    b, t, d = x3d.shape
    return pl.pallas_call(
        _gates_kernel,
        grid=(b, t // tm),
        in_specs=[
            pl.BlockSpec((None, tm, d), lambda i, j: (i, j, 0)),
            pl.BlockSpec((1, d), lambda i, j: (0, 0)),
            pl.BlockSpec((2 * DN_HEADS, d), lambda i, j: (0, 0)),
            pl.BlockSpec((DN_HEADS, 1), lambda i, j: (0, 0)),
            pl.BlockSpec((DN_HEADS, 1), lambda i, j: (0, 0)),
        ],
        out_specs=[
            pl.BlockSpec((None, 3 * DN_HEADS, tm), lambda i, j: (i, 0, j)),
            pl.BlockSpec((None, tm, LANES), lambda i, j: (i, j, 0)),
        ],
        out_shape=[
            jax.ShapeDtypeStruct((b, 3 * DN_HEADS, t), F32),
            jax.ShapeDtypeStruct((b, t, LANES), F32),
        ],
        compiler_params=_params("parallel", "arbitrary"),
        name="dn_gates",
    )(x3d, norm_w, w_gate_t, a_log_col, dt_bias_col)


def _dn_intra_kernel(q_ref, k_ref, v_ref, qh_ref, kh_ref, vh_ref, cwq_ref, cwk_ref, cwv_ref,
                     gcol_ref, grow_ref, u_ref, w_ref, qd_ref, kd_ref, a_ref, xs_ref):
    head = pl.program_id(1)
    first = pl.program_id(2) == 0
    tb = q_ref.shape[0]
    halo = SUBLANES
    for idx, (x_ref, xh_ref) in enumerate(((q_ref, qh_ref), (k_ref, kh_ref), (v_ref, vh_ref))):
        xs_ref[idx, 0:halo, :] = jnp.where(first, 0.0, xh_ref[...])
        xs_ref[idx, halo:halo + tb, :] = x_ref[...]

    lane = lax.broadcasted_iota(jnp.int32, (GROUP, LANES), 1)
    sel_beta = (lane == head).astype(F32)
    sel_gc = (lane == head + DN_HEADS).astype(F32)
    sel_gl = (lane == head + 2 * DN_HEADS).astype(F32)
    ri = lax.broadcasted_iota(jnp.int32, (GROUP, GROUP), 0)
    ci = lax.broadcasted_iota(jnp.int32, (GROUP, GROUP), 1)
    same = (ri // CHUNK) == (ci // CHUNK)
    incl = same & (ri >= ci)
    strict = same & (ri > ci)

    def conv_silu(idx, cw_ref, r0):
        base = r0 + halo - (CONV_WIDTH - 1)
        acc = xs_ref[idx, base:base + GROUP, :] * cw_ref[0:1, :]
        for i in range(1, CONV_WIDTH):
            acc = acc + xs_ref[idx, base + i:base + i + GROUP, :] * cw_ref[i:i + 1, :]
        return acc * _sigmoid(acc)

    n_groups = tb // GROUP
    levels = int(math.log2(CHUNK))
    n_pows = [None] * n_groups
    xs = [None] * n_groups

    def prepare(gi):
        r0 = gi * GROUP
        q = conv_silu(0, cwq_ref, r0)
        k = conv_silu(1, cwk_ref, r0)
        v = conv_silu(2, cwv_ref, r0)
        q = q * lax.rsqrt(jnp.sum(q * q, axis=-1, keepdims=True) + EPS) * (DN_DIM ** -0.5)
        k = k * lax.rsqrt(jnp.sum(k * k, axis=-1, keepdims=True) + EPS)

        gates = gcol_ref[r0:r0 + GROUP, :]
        beta = jnp.sum(gates * sel_beta, axis=-1, keepdims=True)
        gc = jnp.sum(gates * sel_gc, axis=-1, keepdims=True)
        gl = jnp.sum(gates * sel_gl, axis=-1, keepdims=True)
        gc_row = grow_ref[pl.ds(head + DN_HEADS, 1), r0:r0 + GROUP]

        decay = jnp.exp(jnp.where(incl, gc - gc_row, NEG_BIG))
        kb = k * beta
        k16 = k.astype(BF16)
        gram = lax.dot_general(jnp.concatenate([kb, q], axis=0).astype(BF16), k16, _NT,
                               preferred_element_type=F32)
        n_pows[gi] = jnp.where(strict, -(gram[:GROUP] * decay), 0.0)
        a_qk = gram[GROUP:] * decay

        egc = jnp.exp(gc)
        xs[gi] = jnp.concatenate([v * beta, kb * egc], axis=1)
        qd_ref[r0:r0 + GROUP, :] = (q * egc).astype(BF16)
        kd_ref[r0:r0 + GROUP, :] = (k * jnp.exp(gl - gc)).astype(BF16)
        for c in range(GROUP // CHUNK):
            lo = c * CHUNK
            a_ref[r0 + lo:r0 + lo + CHUNK, :] = a_qk[lo:lo + CHUNK, lo:lo + CHUNK].astype(BF16)

    def level(gi, lvl):
        n16 = n_pows[gi].astype(BF16)
        xs[gi] = xs[gi] + jnp.dot(n16, xs[gi].astype(BF16), preferred_element_type=F32)
        if lvl + 1 < levels:
            n_pows[gi] = jnp.dot(n16, n16, preferred_element_type=F32)

    def finish(gi):
        r0 = gi * GROUP
        u_ref[r0:r0 + GROUP, :] = xs[gi][:, :DN_DIM]
        w_ref[r0:r0 + GROUP, :] = xs[gi][:, DN_DIM:].astype(BF16)

    for t in range(n_groups + levels + 1):
        for gi in range(n_groups):
            stage = t - gi
            if stage == 0:
                prepare(gi)
            elif 1 <= stage <= levels:
                level(gi, stage - 1)
            elif stage == levels + 1:
                finish(gi)


def _dn_intra(qkvz, conv_w, gcol, grow, tb):
    b, t, _ = qkvz.shape
    hb = tb // SUBLANES
    nh = DN_HEADS

    def tok(off):
        return pl.BlockSpec((None, tb, DN_DIM), lambda i, h, j: (i, j, off + h))

    def halo(off):
        return pl.BlockSpec((None, SUBLANES, DN_DIM),
                            lambda i, h, j: (i, jnp.maximum(j * hb - 1, 0), off + h))

    def cw(off):
        return pl.BlockSpec((CONV_WIDTH, DN_DIM), lambda i, h, j: (0, off + h))

    out_tok = pl.BlockSpec((None, tb, DN_DIM), lambda i, h, j: (i, j, h))
    return pl.pallas_call(
        _dn_intra_kernel,
        grid=(b, nh, t // tb),
        in_specs=[tok(0), tok(nh), tok(2 * nh), halo(0), halo(nh), halo(2 * nh),
                  cw(0), cw(nh), cw(2 * nh),
                  pl.BlockSpec((None, tb, LANES), lambda i, h, j: (i, j, 0)),
                  pl.BlockSpec((None, 3 * nh, tb), lambda i, h, j: (i, 0, j))],
        out_specs=[out_tok, out_tok, out_tok, out_tok,
                   pl.BlockSpec((None, None, tb, CHUNK), lambda i, h, j: (i, h, j, 0))],
        out_shape=[jax.ShapeDtypeStruct((b, t, DN_WIDTH), F32),
                   jax.ShapeDtypeStruct((b, t, DN_WIDTH), BF16),
                   jax.ShapeDtypeStruct((b, t, DN_WIDTH), BF16),
                   jax.ShapeDtypeStruct((b, t, DN_WIDTH), BF16),
                   jax.ShapeDtypeStruct((b, nh, t, CHUNK), BF16)],
        scratch_shapes=[pltpu.VMEM((3, tb + SUBLANES, DN_DIM), F32)],
        compiler_params=_params("parallel", "parallel", "arbitrary"),
        name="dn_intra",
    )(qkvz, qkvz, qkvz, qkvz, qkvz, qkvz, conv_w, conv_w, conv_w, gcol, grow)


def _dn_scan_kernel(u_ref, w_ref, qd_ref, kd_ref, a_ref, z_ref, gcol_ref, nw_ref, o_ref, s_ref):
    @pl.when(pl.program_id(1) == 0)
    def _():
        s_ref[...] = jnp.zeros_like(s_ref)

    bb, tb = u_ref.shape[0], u_ref.shape[1]
    nw = nw_ref[...]
    units = [(i, h) for i in range(bb) for h in range(DN_HEADS)]

    def chunk(c, carry):
        rows = pl.ds(pl.multiple_of(c * CHUNK, CHUNK), CHUNK)
        cols = [slice(h * DN_DIM, (h + 1) * DN_DIM) for h in range(DN_HEADS)]
        s = [s_ref[i, h] for i, h in units]
        proj = [jnp.dot(jnp.concatenate([w_ref[i, rows, cols[h]], qd_ref[i, rows, cols[h]]], axis=0),
                        s[n].astype(BF16), preferred_element_type=F32)
                for n, (i, h) in enumerate(units)]
        v16 = [(u_ref[i, rows, cols[h]] - proj[n][:CHUNK]).astype(BF16)
               for n, (i, h) in enumerate(units)]
        upd = [lax.dot_general(kd_ref[i, rows, cols[h]], v16[n], _TN, preferred_element_type=F32)
               for n, (i, h) in enumerate(units)]
        for n, (i, h) in enumerate(units):
            gl = gcol_ref[i, pl.ds(pl.multiple_of(c * CHUNK, CHUNK), 1),
                          2 * DN_HEADS + h:2 * DN_HEADS + h + 1]
            s_ref[i, h] = s[n] * jnp.exp(gl) + upd[n]
        o = [proj[n][CHUNK:] + jnp.dot(a_ref[i, h, rows, :], v16[n], preferred_element_type=F32)
             for n, (i, h) in enumerate(units)]
        for n, (i, h) in enumerate(units):
            z = z_ref[i, rows, cols[h]]
            o_ref[i, rows, cols[h]] = (_rmsnorm_rows(o[n], nw) * (z * _sigmoid(z))).astype(o_ref.dtype)
        return carry

    lax.fori_loop(0, tb // CHUNK, chunk, 0)


def _dn_scan(u, w, qd, kd, a, qkvz, gcol, dn_norm_w, bb, tb):
    b, t, _ = u.shape
    tok = pl.BlockSpec((bb, tb, DN_WIDTH), lambda i, j: (i, j, 0))
    z_block = 3 * DN_WIDTH // DN_WIDTH
    return pl.pallas_call(
        _dn_scan_kernel,
        grid=(b // bb, t // tb),
        in_specs=[tok, tok, tok, tok,
                  pl.BlockSpec((bb, DN_HEADS, tb, CHUNK), lambda i, j: (i, 0, j, 0)),
                  pl.BlockSpec((bb, tb, DN_WIDTH), lambda i, j: (i, j, z_block)),
                  pl.BlockSpec((bb, tb, LANES), lambda i, j: (i, j, 0)),
                  pl.BlockSpec((1, DN_DIM), lambda i, j: (0, 0))],
        out_specs=tok,
        out_shape=jax.ShapeDtypeStruct((b, t, DN_WIDTH), BF16),
        scratch_shapes=[pltpu.VMEM((bb, DN_HEADS, DN_DIM, DN_DIM), F32)],
        compiler_params=_params("parallel", "arbitrary"),
        name="dn_scan",
    )(u, w, qd, kd, a, qkvz, gcol, dn_norm_w)


def _attn_kernel(q_ref, k_ref, v_ref, lq1_ref, lk1_ref, lq2_ref, lk2_ref, nw_ref, o_ref,
                 m_ref, l_ref, acc_ref, sa_ref, sb_ref, *, lam_init):
    head = pl.program_id(1)
    qi = pl.program_id(2)
    tq = q_ref.shape[0]
    tk = tq
    slope = jnp.float32(0.0)
    for hh in range(DA_HEADS):
        slope = jnp.where(head == hh, jnp.float32(2.0 ** (-8.0 * (hh + 1) / DA_HEADS)), slope)

    m_ref[...] = jnp.full_like(m_ref, NEG_BIG)
    l_ref[...] = jnp.zeros_like(l_ref)
    acc_ref[...] = jnp.zeros_like(acc_ref)
    kpos = lax.broadcasted_iota(jnp.int32, (1, tk), 1)
    slope2 = slope * LOG2E

    maps = range(2)

    def scores(j, s_ref):
        k0 = pl.multiple_of(j * tk, tk)
        bias = slope2 * (kpos + (j - qi) * tk).astype(F32)
        for c in maps:
            cols = slice(c * DA_DIM, (c + 1) * DA_DIM)
            s_ref[c] = lax.dot_general(q_ref[:, cols], k_ref[pl.ds(k0, tk), cols], _NT,
                                       preferred_element_type=F32) + bias

    def accumulate(j, s_ref, masked):
        k0 = pl.multiple_of(j * tk, tk)
        v = v_ref[pl.ds(k0, tk), :]
        s = [s_ref[c] for c in maps]
        if masked:
            rr = lax.broadcasted_iota(jnp.int32, (tq, tk), 0)
            cc = lax.broadcasted_iota(jnp.int32, (tq, tk), 1)
            s = [jnp.where(cc <= rr, s[c], NEG_BIG) for c in maps]
        m_prev = [m_ref[c] for c in maps]
        m_new = [jnp.maximum(m_prev[c], jnp.max(s[c], axis=-1, keepdims=True)) for c in maps]
        p = [jnp.exp2(s[c] - _lane_tile(m_new[c], tk // LANES)) for c in maps]
        pv = [jnp.dot(p[c].astype(BF16), v, preferred_element_type=F32) for c in maps]
        for c in maps:
            alpha = jnp.exp2(m_prev[c] - m_new[c])
            l_ref[c] = alpha * l_ref[c] + jnp.sum(p[c], axis=-1, keepdims=True)
            acc_ref[c] = _lane_tile(alpha, DA_VDIM // LANES) * acc_ref[c] + pv[c]
            m_ref[c] = m_new[c]

    scores(0, sa_ref)

    def pair(jj, carry):
        j = 2 * jj
        scores(j + 1, sb_ref)
        accumulate(j, sa_ref, False)
        scores(j + 2, sa_ref)
        accumulate(j + 1, sb_ref, False)
        return carry

    lax.fori_loop(0, qi // 2, pair, 0)

    @pl.when(qi % 2 == 1)
    def _():
        scores(qi, sb_ref)
        accumulate(qi - 1, sa_ref, False)
        accumulate(qi, sb_ref, True)

    @pl.when(qi % 2 == 0)
    def _():
        accumulate(qi, sa_ref, True)

    lam = (jnp.exp(jnp.sum(lq1_ref[...] * lk1_ref[...], axis=-1, keepdims=True))
           - jnp.exp(jnp.sum(lq2_ref[...] * lk2_ref[...], axis=-1, keepdims=True)) + lam_init)
    rep = DA_VDIM // LANES
    o = (acc_ref[0] / _lane_tile(l_ref[0], rep)
         - lam * (acc_ref[1] / _lane_tile(l_ref[1], rep)))
    o_ref[...] = (_rmsnorm_rows(o, nw_ref[...]) * (1.0 - lam_init)).astype(o_ref.dtype)


def _attention(da, lam_q1, lam_k1, lam_q2, lam_k2, da_norm_w, lam_init, tq):
    b, t, _ = da.shape
    nh = DA_HEADS
    vec = pl.BlockSpec((1, DA_DIM), lambda i, h, j: (0, 0))
    return pl.pallas_call(
        functools.partial(_attn_kernel, lam_init=lam_init),
        grid=(b, nh, t // tq),
        in_specs=[pl.BlockSpec((None, tq, DA_VDIM), lambda i, h, j: (i, j, h)),
                  pl.BlockSpec((None, t, DA_VDIM), lambda i, h, j: (i, 0, nh + h)),
                  pl.BlockSpec((None, t, DA_VDIM), lambda i, h, j: (i, 0, 2 * nh + h)),
                  vec, vec, vec, vec,
                  pl.BlockSpec((1, DA_VDIM), lambda i, h, j: (0, 0))],
        out_specs=pl.BlockSpec((None, tq, DA_VDIM), lambda i, h, j: (i, j, h)),
        out_shape=jax.ShapeDtypeStruct((b, t, DA_WIDTH), BF16),
        scratch_shapes=[pltpu.VMEM((2, tq, LANES), F32), pltpu.VMEM((2, tq, LANES), F32),
                        pltpu.VMEM((2, tq, DA_VDIM), F32),
                        pltpu.VMEM((2, tq, tq), F32), pltpu.VMEM((2, tq, tq), F32)],
        compiler_params=_params("parallel", "parallel", "arbitrary"),
        name="diff_attn",
    )(da, da, da, lam_q1, lam_k1, lam_q2, lam_k2, da_norm_w)


def _out_proj_kernel(a_ref, b_ref, wa_ref, wb_ref, x_ref, o_ref):
    acc = jnp.dot(a_ref[...], wa_ref[...], preferred_element_type=F32)
    acc = acc + jnp.dot(b_ref[...], wb_ref[...], preferred_element_type=F32)
    o_ref[...] = x_ref[...] + acc


def _out_proj(o_dn, o_da, w_a, w_b, x2d, tm, tn):
    m, d = x2d.shape
    ka, kb = o_dn.shape[1], o_da.shape[1]
    return pl.pallas_call(
        _out_proj_kernel,
        grid=(m // tm, d // tn),
        in_specs=[pl.BlockSpec((tm, ka), lambda i, j: (i, 0)),
                  pl.BlockSpec((tm, kb), lambda i, j: (i, 0)),
                  pl.BlockSpec((ka, tn), lambda i, j: (0, j)),
                  pl.BlockSpec((kb, tn), lambda i, j: (0, j)),
                  pl.BlockSpec((tm, tn), lambda i, j: (i, j))],
        out_specs=pl.BlockSpec((tm, tn), lambda i, j: (i, j)),
        out_shape=jax.ShapeDtypeStruct((m, d), F32),
        compiler_params=_params("parallel", "arbitrary"),
        name="out_proj",
    )(o_dn, o_da, w_a, w_b, x2d)


def _ffn_kernel(x_ref, nw_ref, wg_ref, wu_ref, wd_ref, fw_ref, o_ref, h_ref):
    f = pl.program_id(1)

    @pl.when(f == 0)
    def _():
        x = x_ref[...]
        h_ref[...] = _rmsnorm_rows(x, nw_ref[...]).astype(BF16)
        o_ref[...] = x

    h = h_ref[...]
    g = jnp.dot(h, wg_ref[...], preferred_element_type=F32)
    u = jnp.dot(h, wu_ref[...], preferred_element_type=F32)
    act = (g * _sigmoid(g) * u).astype(BF16)
    o_ref[...] += jnp.dot(act, wd_ref[...], preferred_element_type=F32)

    @pl.when(f == pl.num_programs(1) - 1)
    def _():
        o_ref[...] = _rmsnorm_rows(o_ref[...], fw_ref[...])


def _ffn(x2d, ffn_norm_w, w_gate, w_up, w_down, final_norm_w, tm, tf):
    m, d = x2d.shape
    dff = w_gate.shape[1]
    return pl.pallas_call(
        _ffn_kernel,
        grid=(m // tm, dff // tf),
        in_specs=[pl.BlockSpec((tm, d), lambda i, f: (i, 0)),
                  pl.BlockSpec((1, d), lambda i, f: (0, 0)),
                  pl.BlockSpec((d, tf), lambda i, f: (0, f)),
                  pl.BlockSpec((d, tf), lambda i, f: (0, f)),
                  pl.BlockSpec((tf, d), lambda i, f: (f, 0)),
                  pl.BlockSpec((1, d), lambda i, f: (0, 0))],
        out_specs=pl.BlockSpec((tm, d), lambda i, f: (i, 0)),
        out_shape=jax.ShapeDtypeStruct((m, d), F32),
        scratch_shapes=[pltpu.VMEM((tm, d), BF16)],
        compiler_params=_params("parallel", "arbitrary"),
        name="ffn",
    )(x2d, ffn_norm_w, w_gate, w_up, w_down, final_norm_w)


def _tile(n, pref):
    return pref if n % pref == 0 else n


def _layer(x, lam_init, attn_norm_w, w_in, conv_w, a_log, dt_bias, dn_norm_w, lam_q1, lam_k1,
           lam_q2, lam_k2, da_norm_w, w_out, ffn_norm_w, w_gate, w_up, w_down, final_norm_w):
    b, t, d = x.shape
    m = b * t
    x2d = x.reshape(m, d)
    off_z = 3 * DN_WIDTH
    off_b = off_z + DN_WIDTH
    off_q = off_b + 2 * DN_HEADS
    nw = attn_norm_w.reshape(1, d)

    w_dn = w_in[:, :off_b].astype(BF16)
    w_gt = w_in[:, off_b:off_q].T.astype(BF16)
    w_da = w_in[:, off_q:].astype(BF16)
    n_da = w_da.shape[1]
    da_scale = jnp.concatenate([jnp.full((1, DA_HEADS * 2 * DA_DIM), LOG2E * DA_DIM ** -0.5, F32),
                                jnp.ones((1, n_da - DA_HEADS * 2 * DA_DIM), F32)], axis=1)

    tm = _tile(m, 1024)
    qkvz = _norm_matmul(x2d, nw, w_dn, jnp.ones((1, off_b), F32), F32, tm, 1024, "in_proj_dn")
    da = _norm_matmul(x2d, nw, w_da, da_scale, BF16, tm, 1024, "in_proj_da")
    grow, gcol = _gates(x, nw, w_gt, a_log.reshape(DN_HEADS, 1), dt_bias.reshape(DN_HEADS, 1),
                        _tile(t, 256))

    qkvz = qkvz.reshape(b, t, off_b)
    u, w, qd, kd, a = _dn_intra(qkvz, conv_w, gcol, grow, _tile(t, 2048))
    o_dn = _dn_scan(u, w, qd, kd, a, qkvz, gcol, dn_norm_w.reshape(1, DN_DIM),
                    4 if b % 4 == 0 else 1, _tile(t, 128))

    o_da = _attention(da.reshape(b, t, n_da), lam_q1.reshape(1, DA_DIM), lam_k1.reshape(1, DA_DIM),
                      lam_q2.reshape(1, DA_DIM), lam_k2.reshape(1, DA_DIM),
                      da_norm_w.reshape(1, DA_VDIM), lam_init, _tile(t, 512))

    w_out16 = w_out.astype(BF16)
    x2d = _out_proj(o_dn.reshape(m, DN_WIDTH), o_da.reshape(m, DA_WIDTH),
                    w_out16[:DN_WIDTH], w_out16[DN_WIDTH:], x2d, tm, _tile(d, 1024))
    return x2d


def kernel(x, attn_norm_w, w_in, conv_w, a_log, dt_bias, dn_norm_w, lam_q1, lam_k1, lam_q2, lam_k2,
           da_norm_w, w_out, ffn_norm_w, w_gate, w_up, w_down, final_norm_w):
    b, t, d = x.shape
    depth = w_in.shape[0]
    assert depth == 1, "the final rmsnorm is fused into the last layer's FFN kernel"
    lam_init = 0.8 - 0.6 * math.exp(-0.3 * 0)
    x2d = _layer(x, lam_init, attn_norm_w[0], w_in[0], conv_w[0], a_log[0], dt_bias[0],
                 dn_norm_w[0], lam_q1[0], lam_k1[0], lam_q2[0], lam_k2[0], da_norm_w[0],
                 w_out[0], ffn_norm_w[0], w_gate[0], w_up[0], w_down[0], final_norm_w)
    dff = w_gate.shape[2]
    out = _ffn(x2d, ffn_norm_w[0].reshape(1, d), w_gate[0].astype(BF16), w_up[0].astype(BF16),
               w_down[0].astype(BF16), final_norm_w.reshape(1, d), _tile(b * t, 1024),
               _tile(dff, 512))
    return out.reshape(b, t, d)
```

```python
import functools
import math

import jax
import jax.numpy as jnp
from jax import lax
from jax.experimental import pallas as pl
from jax.experimental.pallas import tpu as pltpu

F32 = jnp.float32
BF16 = jnp.bfloat16
EPS = 1e-6
NEG_BIG = -1e30
LOG2E = math.log2(math.e)

DN_HEADS = 8
DN_DIM = 128
DN_WIDTH = DN_HEADS * DN_DIM
CONV_WIDTH = 4
CHUNK = 64
DA_HEADS = 4
DA_DIM = 128
DA_VDIM = 2 * DA_DIM
DA_WIDTH = DA_HEADS * DA_VDIM
LANES = 128
SUBLANES = 8
GROUP = 256
VMEM_LIMIT = 56 * 1024 * 1024

_NT = (((1,), (1,)), ((), ()))
_TN = (((0,), (0,)), ((), ()))


def _params(*sem):
    return pltpu.CompilerParams(dimension_semantics=sem, vmem_limit_bytes=VMEM_LIMIT)


def _sigmoid(x):
    return 1.0 / (1.0 + jnp.exp(-x))


def _lane_tile(x, n):
    return jnp.concatenate([x] * n, axis=1)


def _rmsnorm_rows(x, w):
    return x * lax.rsqrt(jnp.mean(x * x, axis=-1, keepdims=True) + EPS) * w


def _norm_matmul_kernel(x_ref, nw_ref, w_ref, cs_ref, o_ref, h_ref):
    @pl.when(pl.program_id(1) == 0)
    def _():
        h_ref[...] = _rmsnorm_rows(x_ref[...], nw_ref[...]).astype(BF16)

    acc = jnp.dot(h_ref[...], w_ref[...], preferred_element_type=F32)
    o_ref[...] = (acc * cs_ref[...]).astype(o_ref.dtype)


def _norm_matmul(x2d, norm_w, w_bf16, col_scale, out_dtype, tm, tn, name):
    m, d = x2d.shape
    n = w_bf16.shape[1]
    return pl.pallas_call(
        _norm_matmul_kernel,
        grid=(m // tm, n // tn),
        in_specs=[
            pl.BlockSpec((tm, d), lambda i, j: (i, 0)),
            pl.BlockSpec((1, d), lambda i, j: (0, 0)),
            pl.BlockSpec((d, tn), lambda i, j: (0, j)),
            pl.BlockSpec((1, tn), lambda i, j: (0, j)),
        ],
        out_specs=pl.BlockSpec((tm, tn), lambda i, j: (i, j)),
        out_shape=jax.ShapeDtypeStruct((m, n), out_dtype),
        scratch_shapes=[pltpu.VMEM((tm, d), BF16)],
        compiler_params=_params("parallel", "arbitrary"),
        name=name,
    )(x2d, norm_w, w_bf16, col_scale)


def _gates_kernel(x_ref, nw_ref, wt_ref, alog_ref, dtb_ref, win_ref, row_ref, col_ref, wdn_ref, wda_ref):
    w_rows = win_ref[...]
    wdn_ref[...] = w_rows[:, :wdn_ref.shape[1]].astype(BF16)
    wda_ref[...] = w_rows[:, w_rows.shape[1] - wda_ref.shape[1]:].astype(BF16)

    h = _rmsnorm_rows(x_ref[...], nw_ref[...]).astype(BF16)
    r = lax.dot_general(wt_ref[...], h, _NT, preferred_element_type=F32)
    beta = _sigmoid(r[0:DN_HEADS])
    a = r[DN_HEADS:2 * DN_HEADS] + dtb_ref[...]
    softplus = jnp.maximum(a, 0.0) + jnp.log1p(jnp.exp(-jnp.abs(a)))
    g = -jnp.exp(alog_ref[...]) * softplus
    tm = g.shape[1]
    ii = lax.broadcasted_iota(jnp.int32, (tm, tm), 0)
    jj = lax.broadcasted_iota(jnp.int32, (tm, tm), 1)
    same = (ii // CHUNK) == (jj // CHUNK)
    sums = jnp.concatenate([same & (ii <= jj), same], axis=1).astype(BF16)
    g1 = g.astype(BF16)
    r1 = g - g1.astype(F32)
    g2 = r1.astype(BF16)
    g3 = (r1 - g2.astype(F32)).astype(BF16)
    pieces = jnp.concatenate([g1, g2, g3, jnp.zeros_like(g1)], axis=0)
    acc = jnp.dot(pieces, sums, preferred_element_type=F32)
    nh = DN_HEADS
    acc = acc[0:nh] + acc[nh:2 * nh] + acc[2 * nh:3 * nh]
    gc, gl = acc[:, :tm], acc[:, tm:]
    rows = jnp.concatenate([beta, gc, gl], axis=0)
    row_ref[...] = rows
    padded = jnp.concatenate([rows, jnp.zeros((LANES - 3 * DN_HEADS, tm), F32)], axis=0)
    col_ref[...] = padded.T


def _gates(x3d, norm_w, w_gate_t, a_log_col, dt_bias_col, w_in, n_dn, n_da, tm):
    b, t, d = x3d.shape
    d_in = w_in.shape[1]
    nj = t // tm
    wr = d // (b * nj)
    return pl.pallas_call(
        _gates_kernel,
        grid=(b, nj),
        in_specs=[
            pl.BlockSpec((None, tm, d), lambda i, j: (i, j, 0)),
            pl.BlockSpec((1, d), lambda i, j: (0, 0)),
            pl.BlockSpec((2 * DN_HEADS, d), lambda i, j: (0, 0)),
            pl.BlockSpec((DN_HEADS, 1), lambda i, j: (0, 0)),
            pl.BlockSpec((DN_HEADS, 1), lambda i, j: (0, 0)),
            pl.BlockSpec((wr, d_in), lambda i, j: (i * nj + j, 0)),
        ],
        out_specs=[
            pl.BlockSpec((None, 3 * DN_HEADS, tm), lambda i, j: (i, 0, j)),
            pl.BlockSpec((None, tm, LANES), lambda i, j: (i, j, 0)),
            pl.BlockSpec((wr, n_dn), lambda i, j: (i * nj + j, 0)),
            pl.BlockSpec((wr, n_da), lambda i, j: (i * nj + j, 0)),
        ],
        out_shape=[
            jax.ShapeDtypeStruct((b, 3 * DN_HEADS, t), F32),
            jax.ShapeDtypeStruct((b, t, LANES), F32),
            jax.ShapeDtypeStruct((d, n_dn), BF16),
            jax.ShapeDtypeStruct((d, n_da), BF16),
        ],
        compiler_params=_params("parallel", "arbitrary"),
        name="dn_gates",
    )(x3d, norm_w, w_gate_t, a_log_col, dt_bias_col, w_in)


def _dn_intra_kernel(q_ref, k_ref, v_ref, qh_ref, kh_ref, vh_ref, cwq_ref, cwk_ref, cwv_ref,
                     gcol_ref, grow_ref, wo_ref, u_ref, w_ref, qd_ref, kd_ref, a_ref, wo16_ref, xs_ref):
    wo16_ref[...] = wo_ref[...].astype(BF16)
    head = pl.program_id(1)
    first = pl.program_id(2) == 0
    tb = q_ref.shape[0]
    halo = SUBLANES
    for idx, (x_ref, xh_ref) in enumerate(((q_ref, qh_ref), (k_ref, kh_ref), (v_ref, vh_ref))):
        xs_ref[idx, 0:halo, :] = jnp.where(first, 0.0, xh_ref[...])
        xs_ref[idx, halo:halo + tb, :] = x_ref[...]

    lane = lax.broadcasted_iota(jnp.int32, (GROUP, LANES), 1)
    sel_beta = (lane == head).astype(F32)
    sel_gc = (lane == head + DN_HEADS).astype(F32)
    sel_gl = (lane == head + 2 * DN_HEADS).astype(F32)
    ri = lax.broadcasted_iota(jnp.int32, (GROUP, GROUP), 0)
    ci = lax.broadcasted_iota(jnp.int32, (GROUP, GROUP), 1)
    same = (ri // CHUNK) == (ci // CHUNK)
    incl = same & (ri >= ci)
    strict = same & (ri > ci)

    def conv_silu(idx, cw_ref, r0):
        base = r0 + halo - (CONV_WIDTH - 1)
        acc = xs_ref[idx, base:base + GROUP, :] * cw_ref[0:1, :]
        for i in range(1, CONV_WIDTH):
            acc = acc + xs_ref[idx, base + i:base + i + GROUP, :] * cw_ref[i:i + 1, :]
        return acc * _sigmoid(acc)

    n_groups = tb // GROUP
    levels = int(math.log2(CHUNK))
    n_pows = [None] * n_groups
    xs = [None] * n_groups

    def prepare(gi):
        r0 = gi * GROUP
        q = conv_silu(0, cwq_ref, r0)
        k = conv_silu(1, cwk_ref, r0)
        v = conv_silu(2, cwv_ref, r0)
        q = q * lax.rsqrt(jnp.sum(q * q, axis=-1, keepdims=True) + EPS) * (DN_DIM ** -0.5)
        k = k * lax.rsqrt(jnp.sum(k * k, axis=-1, keepdims=True) + EPS)

        gates = gcol_ref[r0:r0 + GROUP, :]
        beta = jnp.sum(gates * sel_beta, axis=-1, keepdims=True)
        gc = jnp.sum(gates * sel_gc, axis=-1, keepdims=True)
        gl = jnp.sum(gates * sel_gl, axis=-1, keepdims=True)
        gc_row = grow_ref[pl.ds(head + DN_HEADS, 1), r0:r0 + GROUP]

        decay = jnp.exp(jnp.where(incl, gc - gc_row, NEG_BIG))
        kb = k * beta
        k16 = k.astype(BF16)
        gram = lax.dot_general(jnp.concatenate([kb, q], axis=0).astype(BF16), k16, _NT,
                               preferred_element_type=F32)
        n_pows[gi] = jnp.where(strict, -(gram[:GROUP] * decay), 0.0)
        a_qk = gram[GROUP:] * decay

        egc = jnp.exp(gc)
        xs[gi] = jnp.concatenate([v * beta, kb * egc], axis=1)
        qd_ref[r0:r0 + GROUP, :] = (q * egc).astype(BF16)
        kd_ref[r0:r0 + GROUP, :] = (k * jnp.exp(gl - gc)).astype(BF16)
        for c in range(GROUP // CHUNK):
            lo = c * CHUNK
            a_ref[r0 + lo:r0 + lo + CHUNK, :] = a_qk[lo:lo + CHUNK, lo:lo + CHUNK].astype(BF16)

    def level(gi, lvl):
        n16 = n_pows[gi].astype(BF16)
        xs[gi] = xs[gi] + jnp.dot(n16, xs[gi].astype(BF16), preferred_element_type=F32)
        if lvl + 1 < levels:
            n_pows[gi] = jnp.dot(n16, n16, preferred_element_type=F32)

    def finish(gi):
        r0 = gi * GROUP
        u_ref[r0:r0 + GROUP, :] = xs[gi][:, :DN_DIM]
        w_ref[r0:r0 + GROUP, :] = xs[gi][:, DN_DIM:].astype(BF16)

    for t in range(n_groups + levels + 1):
        for gi in range(n_groups):
            stage = t - gi
            if stage == 0:
                prepare(gi)
            elif 1 <= stage <= levels:
                level(gi, stage - 1)
            elif stage == levels + 1:
                finish(gi)


def _dn_intra(qkvz, conv_w, gcol, grow, w_out, tb):
    b, t, _ = qkvz.shape
    hb = tb // SUBLANES
    nh = DN_HEADS
    nj = t // tb
    wo_rows, wo_share = _slab_rows(w_out.shape[0], b * nh * nj)
    assert wo_share == 1, "every grid step converts its own slab of w_out"
    wo_spec = pl.BlockSpec((wo_rows, w_out.shape[1]), lambda i, h, j: ((i * nh + h) * nj + j, 0))

    def tok(off):
        return pl.BlockSpec((None, tb, DN_DIM), lambda i, h, j: (i, j, off + h))

    def halo(off):
        return pl.BlockSpec((None, SUBLANES, DN_DIM),
                            lambda i, h, j: (i, jnp.maximum(j * hb - 1, 0), off + h))

    def cw(off):
        return pl.BlockSpec((CONV_WIDTH, DN_DIM), lambda i, h, j: (0, off + h))

    out_tok = pl.BlockSpec((None, tb, DN_DIM), lambda i, h, j: (i, j, h))
    return pl.pallas_call(
        _dn_intra_kernel,
        grid=(b, nh, nj),
        in_specs=[tok(0), tok(nh), tok(2 * nh), halo(0), halo(nh), halo(2 * nh),
                  cw(0), cw(nh), cw(2 * nh),
                  pl.BlockSpec((None, tb, LANES), lambda i, h, j: (i, j, 0)),
                  pl.BlockSpec((None, 3 * nh, tb), lambda i, h, j: (i, 0, j)),
                  wo_spec],
        out_specs=[out_tok, out_tok, out_tok, out_tok,
                   pl.BlockSpec((None, None, tb, CHUNK), lambda i, h, j: (i, h, j, 0)),
                   wo_spec],
        out_shape=[jax.ShapeDtypeStruct((b, t, DN_WIDTH), F32),
                   jax.ShapeDtypeStruct((b, t, DN_WIDTH), BF16),
                   jax.ShapeDtypeStruct((b, t, DN_WIDTH), BF16),
                   jax.ShapeDtypeStruct((b, t, DN_WIDTH), BF16),
                   jax.ShapeDtypeStruct((b, nh, t, CHUNK), BF16),
                   jax.ShapeDtypeStruct(w_out.shape, BF16)],
        scratch_shapes=[pltpu.VMEM((3, tb + SUBLANES, DN_DIM), F32)],
        compiler_params=_params("parallel", "parallel", "arbitrary"),
        name="dn_intra",
    )(qkvz, qkvz, qkvz, qkvz, qkvz, qkvz, conv_w, conv_w, conv_w, gcol, grow, w_out)


def _dn_scan_kernel(u_ref, w_ref, qd_ref, kd_ref, a_ref, z_ref, gcol_ref, nw_ref, o_ref, s_ref):
    @pl.when(pl.program_id(1) == 0)
    def _():
        s_ref[...] = jnp.zeros_like(s_ref)

    bb, tb = u_ref.shape[0], u_ref.shape[1]
    nw = nw_ref[...]
    units = [(i, h) for i in range(bb) for h in range(DN_HEADS)]

    def chunk(c, carry):
        rows = pl.ds(pl.multiple_of(c * CHUNK, CHUNK), CHUNK)
        cols = [slice(h * DN_DIM, (h + 1) * DN_DIM) for h in range(DN_HEADS)]
        s = [s_ref[i, h] for i, h in units]
        proj = [jnp.dot(jnp.concatenate([w_ref[i, rows, cols[h]], qd_ref[i, rows, cols[h]]], axis=0),
                        s[n].astype(BF16), preferred_element_type=F32)
                for n, (i, h) in enumerate(units)]
        v16 = [(u_ref[i, rows, cols[h]] - proj[n][:CHUNK]).astype(BF16)
               for n, (i, h) in enumerate(units)]
        upd = [lax.dot_general(kd_ref[i, rows, cols[h]], v16[n], _TN, preferred_element_type=F32)
               for n, (i, h) in enumerate(units)]
        for n, (i, h) in enumerate(units):
            gl = gcol_ref[i, pl.ds(pl.multiple_of(c * CHUNK, CHUNK), 1),
                          2 * DN_HEADS + h:2 * DN_HEADS + h + 1]
            s_ref[i, h] = s[n] * jnp.exp(gl) + upd[n]
        o = [proj[n][CHUNK:] + jnp.dot(a_ref[i, h, rows, :], v16[n], preferred_element_type=F32)
             for n, (i, h) in enumerate(units)]
        for n, (i, h) in enumerate(units):
            z = z_ref[i, rows, cols[h]]
            o_ref[i, rows, cols[h]] = (_rmsnorm_rows(o[n], nw) * (z * _sigmoid(z))).astype(o_ref.dtype)
        return carry

    lax.fori_loop(0, tb // CHUNK, chunk, 0)


def _dn_scan(u, w, qd, kd, a, qkvz, gcol, dn_norm_w, bb, tb):
    b, t, _ = u.shape
    tok = pl.BlockSpec((bb, tb, DN_WIDTH), lambda i, j: (i, j, 0))
    z_block = 3 * DN_WIDTH // DN_WIDTH
    return pl.pallas_call(
        _dn_scan_kernel,
        grid=(b // bb, t // tb),
        in_specs=[tok, tok, tok, tok,
                  pl.BlockSpec((bb, DN_HEADS, tb, CHUNK), lambda i, j: (i, 0, j, 0)),
                  pl.BlockSpec((bb, tb, DN_WIDTH), lambda i, j: (i, j, z_block)),
                  pl.BlockSpec((bb, tb, LANES), lambda i, j: (i, j, 0)),
                  pl.BlockSpec((1, DN_DIM), lambda i, j: (0, 0))],
        out_specs=tok,
        out_shape=jax.ShapeDtypeStruct((b, t, DN_WIDTH), BF16),
        scratch_shapes=[pltpu.VMEM((bb, DN_HEADS, DN_DIM, DN_DIM), F32)],
        compiler_params=_params("parallel", "arbitrary"),
        name="dn_scan",
    )(u, w, qd, kd, a, qkvz, gcol, dn_norm_w)


def _attn_kernel(q_ref, k_ref, v_ref, lq1_ref, lk1_ref, lq2_ref, lk2_ref, nw_ref,
                 wg_ref, wu_ref, wd_ref, o_ref, wg16_ref, wu16_ref, wd16_ref,
                 m_ref, l_ref, acc_ref, sa_ref, sb_ref, *, lam_init, down_share):
    wg16_ref[...] = wg_ref[...].astype(BF16)
    wu16_ref[...] = wu_ref[...].astype(BF16)
    grid_step = ((pl.program_id(0) * pl.num_programs(1) + pl.program_id(1)) * pl.num_programs(2)
                 + pl.program_id(2))

    @pl.when(grid_step % down_share == 0)
    def _():
        wd16_ref[...] = wd_ref[...].astype(BF16)

    head = pl.program_id(1)
    qi = pl.program_id(2)
    tq = q_ref.shape[0]
    tk = tq
    slope = jnp.float32(0.0)
    for hh in range(DA_HEADS):
        slope = jnp.where(head == hh, jnp.float32(2.0 ** (-8.0 * (hh + 1) / DA_HEADS)), slope)

    m_ref[...] = jnp.full_like(m_ref, NEG_BIG)
    l_ref[...] = jnp.zeros_like(l_ref)
    acc_ref[...] = jnp.zeros_like(acc_ref)
    kpos = lax.broadcasted_iota(jnp.int32, (1, tk), 1)
    slope2 = slope * LOG2E

    maps = range(2)

    def scores(j, s_ref):
        k0 = pl.multiple_of(j * tk, tk)
        bias = slope2 * (kpos + (j - qi) * tk).astype(F32)
        for c in maps:
            cols = slice(c * DA_DIM, (c + 1) * DA_DIM)
            s_ref[c] = lax.dot_general(q_ref[:, cols], k_ref[pl.ds(k0, tk), cols], _NT,
                                       preferred_element_type=F32) + bias

    def accumulate(j, s_ref, masked):
        k0 = pl.multiple_of(j * tk, tk)
        v = v_ref[pl.ds(k0, tk), :]
        s = [s_ref[c] for c in maps]
        if masked:
            rr = lax.broadcasted_iota(jnp.int32, (tq, tk), 0)
            cc = lax.broadcasted_iota(jnp.int32, (tq, tk), 1)
            s = [jnp.where(cc <= rr, s[c], NEG_BIG) for c in maps]
        m_prev = [m_ref[c] for c in maps]
        m_new = [jnp.maximum(m_prev[c], jnp.max(s[c], axis=-1, keepdims=True)) for c in maps]
        p = [jnp.exp2(s[c] - _lane_tile(m_new[c], tk // LANES)) for c in maps]
        p_sum = [jnp.sum(p[c], axis=-1, keepdims=True) for c in maps]
        p16 = [p[c].astype(BF16) for c in maps]
        pv = [jnp.dot(p16[c], v, preferred_element_type=F32) for c in maps]
        for c in maps:
            alpha = jnp.exp2(m_prev[c] - m_new[c])
            l_ref[c] = alpha * l_ref[c] + p_sum[c]
            acc_ref[c] = _lane_tile(alpha, DA_VDIM // LANES) * acc_ref[c] + pv[c]
            m_ref[c] = m_new[c]

    scores(0, sa_ref)

    def pair(jj, carry):
        j = 2 * jj
        scores(j + 1, sb_ref)
        accumulate(j, sa_ref, False)
        scores(j + 2, sa_ref)
        accumulate(j + 1, sb_ref, False)
        return carry

    lax.fori_loop(0, qi // 2, pair, 0)

    @pl.when(qi % 2 == 1)
    def _():
        scores(qi, sb_ref)
        accumulate(qi - 1, sa_ref, False)
        accumulate(qi, sb_ref, True)

    @pl.when(qi % 2 == 0)
    def _():
        accumulate(qi, sa_ref, True)

    lam = (jnp.exp(jnp.sum(lq1_ref[...] * lk1_ref[...], axis=-1, keepdims=True))
           - jnp.exp(jnp.sum(lq2_ref[...] * lk2_ref[...], axis=-1, keepdims=True)) + lam_init)
    rep = DA_VDIM // LANES
    o = (acc_ref[0] / _lane_tile(l_ref[0], rep)
         - lam * (acc_ref[1] / _lane_tile(l_ref[1], rep)))
    o_ref[...] = (_rmsnorm_rows(o, nw_ref[...]) * (1.0 - lam_init)).astype(o_ref.dtype)


def _slab_rows(n_rows, n_steps):
    for share in (1, 2, 4, 8, 16):
        if n_steps % share == 0 and n_rows % (n_steps // share) == 0:
            rows = n_rows // (n_steps // share)
            if rows % 16 == 0:
                return rows, share
    raise ValueError(f"cannot split {n_rows} weight rows over {n_steps} grid steps")


def _attention(da, lam_q1, lam_k1, lam_q2, lam_k2, da_norm_w, w_gate, w_up, w_down, lam_init, tq):
    b, t, _ = da.shape
    nh = DA_HEADS
    nq = t // tq
    d, dff = w_gate.shape
    steps = b * nh * nq
    rg, sg = _slab_rows(d, steps)
    rd, sd = _slab_rows(dff, steps)

    def step(i, h, j):
        return (i * nh + h) * nq + j

    vec = pl.BlockSpec((1, DA_DIM), lambda i, h, j: (0, 0))
    up_spec = pl.BlockSpec((rg, dff), lambda i, h, j: (step(i, h, j) // sg, 0))
    down_spec = pl.BlockSpec((rd, d), lambda i, h, j: (step(i, h, j) // sd, 0))
    return pl.pallas_call(
        functools.partial(_attn_kernel, lam_init=lam_init, down_share=sd),
        grid=(b, nh, nq),
        in_specs=[pl.BlockSpec((None, tq, DA_VDIM), lambda i, h, j: (i, j, h)),
                  pl.BlockSpec((None, t, DA_VDIM), lambda i, h, j: (i, 0, nh + h)),
                  pl.BlockSpec((None, t, DA_VDIM), lambda i, h, j: (i, 0, 2 * nh + h)),
                  vec, vec, vec, vec,
                  pl.BlockSpec((1, DA_VDIM), lambda i, h, j: (0, 0)),
                  up_spec, up_spec, down_spec],
        out_specs=[pl.BlockSpec((None, tq, DA_VDIM), lambda i, h, j: (i, j, h)),
                   up_spec, up_spec, down_spec],
        out_shape=[jax.ShapeDtypeStruct((b, t, DA_WIDTH), BF16),
                   jax.ShapeDtypeStruct((d, dff), BF16),
                   jax.ShapeDtypeStruct((d, dff), BF16),
                   jax.ShapeDtypeStruct((dff, d), BF16)],
        scratch_shapes=[pltpu.VMEM((2, tq, LANES), F32), pltpu.VMEM((2, tq, LANES), F32),
                        pltpu.VMEM((2, tq, DA_VDIM), F32),
                        pltpu.VMEM((2, tq, tq), F32), pltpu.VMEM((2, tq, tq), F32)],
        compiler_params=_params("arbitrary", "arbitrary", "arbitrary"),
        name="diff_attn",
    )(da, da, da, lam_q1, lam_k1, lam_q2, lam_k2, da_norm_w, w_gate, w_up, w_down)


def _out_proj_kernel(a_ref, b_ref, wa_ref, wb_ref, x_ref, o_ref):
    acc = jnp.dot(a_ref[...], wa_ref[...], preferred_element_type=F32)
    acc = acc + jnp.dot(b_ref[...], wb_ref[...], preferred_element_type=F32)
    o_ref[...] = x_ref[...] + acc


def _out_proj(o_dn, o_da, w16, x2d, tm, tn):
    m, d = x2d.shape
    ka, kb = o_dn.shape[1], o_da.shape[1]
    assert ka == kb and w16.shape[0] == ka + kb, "the two head groups are equal row halves of w_out"
    return pl.pallas_call(
        _out_proj_kernel,
        grid=(m // tm, d // tn),
        in_specs=[pl.BlockSpec((tm, ka), lambda i, j: (i, 0)),
                  pl.BlockSpec((tm, kb), lambda i, j: (i, 0)),
                  pl.BlockSpec((ka, tn), lambda i, j: (0, j)),
                  pl.BlockSpec((kb, tn), lambda i, j: (1, j)),
                  pl.BlockSpec((tm, tn), lambda i, j: (i, j))],
        out_specs=pl.BlockSpec((tm, tn), lambda i, j: (i, j)),
        out_shape=jax.ShapeDtypeStruct((m, d), F32),
        compiler_params=_params("parallel", "arbitrary"),
        name="out_proj",
    )(o_dn, o_da, w16, w16, x2d)


def _ffn_kernel(x_ref, nw_ref, wg_ref, wu_ref, wd_ref, fw_ref, o_ref, h_ref):
    f = pl.program_id(1)

    @pl.when(f == 0)
    def _():
        x = x_ref[...]
        h_ref[...] = _rmsnorm_rows(x, nw_ref[...]).astype(BF16)
        o_ref[...] = x

    h = h_ref[...]
    g = jnp.dot(h, wg_ref[...], preferred_element_type=F32)
    u = jnp.dot(h, wu_ref[...], preferred_element_type=F32)
    act = (g * _sigmoid(g) * u).astype(BF16)
    o_ref[...] += jnp.dot(act, wd_ref[...], preferred_element_type=F32)

    @pl.when(f == pl.num_programs(1) - 1)
    def _():
        o_ref[...] = _rmsnorm_rows(o_ref[...], fw_ref[...])


def _ffn(x2d, ffn_norm_w, w_gate, w_up, w_down, final_norm_w, tm, tf):
    m, d = x2d.shape
    dff = w_gate.shape[1]
    return pl.pallas_call(
        _ffn_kernel,
        grid=(m // tm, dff // tf),
        in_specs=[pl.BlockSpec((tm, d), lambda i, f: (i, 0)),
                  pl.BlockSpec((1, d), lambda i, f: (0, 0)),
                  pl.BlockSpec((d, tf), lambda i, f: (0, f)),
                  pl.BlockSpec((d, tf), lambda i, f: (0, f)),
                  pl.BlockSpec((tf, d), lambda i, f: (f, 0)),
                  pl.BlockSpec((1, d), lambda i, f: (0, 0))],
        out_specs=pl.BlockSpec((tm, d), lambda i, f: (i, 0)),
        out_shape=jax.ShapeDtypeStruct((m, d), F32),
        scratch_shapes=[pltpu.VMEM((tm, d), BF16)],
        compiler_params=_params("parallel", "arbitrary"),
        name="ffn",
    )(x2d, ffn_norm_w, w_gate, w_up, w_down, final_norm_w)


def _tile(n, pref):
    return pref if n % pref == 0 else n


def _layer(x, lam_init, attn_norm_w, w_in, conv_w, a_log, dt_bias, dn_norm_w, lam_q1, lam_k1,
           lam_q2, lam_k2, da_norm_w, w_out, ffn_norm_w, w_gate, w_up, w_down, final_norm_w):
    b, t, d = x.shape
    m = b * t
    x2d = x.reshape(m, d)
    off_z = 3 * DN_WIDTH
    off_b = off_z + DN_WIDTH
    off_q = off_b + 2 * DN_HEADS
    nw = attn_norm_w.reshape(1, d)

    w_gt = w_in[:, off_b:off_q].T.astype(BF16)
    n_da = w_in.shape[1] - off_q
    da_scale = jnp.concatenate([jnp.full((1, DA_HEADS * 2 * DA_DIM), LOG2E * DA_DIM ** -0.5, F32),
                                jnp.ones((1, n_da - DA_HEADS * 2 * DA_DIM), F32)], axis=1)

    tg = _tile(t, 256)
    assert d % (b * (t // tg) * 16) == 0, "weight rows per gate step must fill bf16 tiles"
    grow, gcol, w_dn, w_da = _gates(x, nw, w_gt, a_log.reshape(DN_HEADS, 1),
                                    dt_bias.reshape(DN_HEADS, 1), w_in, off_b, n_da, tg)
    tm = _tile(m, 1024)
    qkvz = _norm_matmul(x2d, nw, w_dn, jnp.ones((1, off_b), F32), F32, tm, 1024, "in_proj_dn")
    da = _norm_matmul(x2d, nw, w_da, da_scale, BF16, tm, 1024, "in_proj_da")

    qkvz = qkvz.reshape(b, t, off_b)
    u, w, qd, kd, a, w_out16 = _dn_intra(qkvz, conv_w, gcol, grow, w_out, _tile(t, 2048))
    o_dn = _dn_scan(u, w, qd, kd, a, qkvz, gcol, dn_norm_w.reshape(1, DN_DIM),
                    4 if b % 4 == 0 else 1, _tile(t, 128))

    o_da, w_gate16, w_up16, w_down16 = _attention(
        da.reshape(b, t, n_da), lam_q1.reshape(1, DA_DIM), lam_k1.reshape(1, DA_DIM),
        lam_q2.reshape(1, DA_DIM), lam_k2.reshape(1, DA_DIM), da_norm_w.reshape(1, DA_VDIM),
        w_gate, w_up, w_down, lam_init, _tile(t, 512))

    x2d = _out_proj(o_dn.reshape(m, DN_WIDTH), o_da.reshape(m, DA_WIDTH), w_out16, x2d, tm,
                    _tile(d, 1024))
    dff = w_gate.shape[1]
    return _ffn(x2d, ffn_norm_w.reshape(1, d), w_gate16, w_up16, w_down16,
                final_norm_w.reshape(1, d), tm, _tile(dff, 512))


def kernel(x, attn_norm_w, w_in, conv_w, a_log, dt_bias, dn_norm_w, lam_q1, lam_k1, lam_q2, lam_k2,
           da_norm_w, w_out, ffn_norm_w, w_gate, w_up, w_down, final_norm_w):
    b, t, d = x.shape
    depth = w_in.shape[0]
    assert depth == 1, "the final rmsnorm is fused into the last layer's FFN kernel"
    lam_init = 0.8 - 0.6 * math.exp(-0.3 * 0)
    out = _layer(x, lam_init, attn_norm_w[0], w_in[0], conv_w[0], a_log[0], dt_bias[0],
                 dn_norm_w[0], lam_q1[0], lam_k1[0], lam_q2[0], lam_k2[0], da_norm_w[0],
                 w_out[0], ffn_norm_w[0], w_gate[0], w_up[0], w_down[0], final_norm_w)
    return out.reshape(b, t, d)
```

```python
import functools
import math

import jax
import jax.numpy as jnp
from jax import lax
from jax.experimental import pallas as pl
from jax.experimental.pallas import tpu as pltpu

F32 = jnp.float32
BF16 = jnp.bfloat16
EPS = 1e-6
NEG_BIG = -1e30
LOG2E = math.log2(math.e)

DN_HEADS = 8
DN_DIM = 128
DN_WIDTH = DN_HEADS * DN_DIM
CONV_WIDTH = 4
CHUNK = 64
DA_HEADS = 4
DA_DIM = 128
DA_VDIM = 2 * DA_DIM
DA_WIDTH = DA_HEADS * DA_VDIM
LANES = 128
SUBLANES = 8
GROUP = 256
VMEM_LIMIT = 56 * 1024 * 1024

_NT = (((1,), (1,)), ((), ()))
_TN = (((0,), (0,)), ((), ()))


def _params(*sem):
    return pltpu.CompilerParams(dimension_semantics=sem, vmem_limit_bytes=VMEM_LIMIT)


def _sigmoid(x):
    return 1.0 / (1.0 + jnp.exp(-x))


def _lane_tile(x, n):
    return jnp.concatenate([x] * n, axis=1)


def _rmsnorm_rows(x, w):
    return x * lax.rsqrt(jnp.mean(x * x, axis=-1, keepdims=True) + EPS) * w


def _norm_matmul_kernel(x_ref, nw_ref, w_ref, cs_ref, o_ref, h_ref):
    @pl.when(pl.program_id(1) == 0)
    def _():
        h_ref[...] = _rmsnorm_rows(x_ref[...], nw_ref[...]).astype(BF16)

    acc = lax.dot_general(h_ref[...], w_ref[...], _NT, preferred_element_type=F32)
    o_ref[...] = (acc * cs_ref[...]).astype(o_ref.dtype)


def _norm_matmul(x2d, norm_w, w_t_bf16, col_scale, out_dtype, tm, tn, name):
    m, d = x2d.shape
    n = w_t_bf16.shape[0]
    return pl.pallas_call(
        _norm_matmul_kernel,
        grid=(m // tm, n // tn),
        in_specs=[
            pl.BlockSpec((tm, d), lambda i, j: (i, 0)),
            pl.BlockSpec((1, d), lambda i, j: (0, 0)),
            pl.BlockSpec((tn, d), lambda i, j: (j, 0)),
            pl.BlockSpec((1, tn), lambda i, j: (0, j)),
        ],
        out_specs=pl.BlockSpec((tm, tn), lambda i, j: (i, j)),
        out_shape=jax.ShapeDtypeStruct((m, n), out_dtype),
        scratch_shapes=[pltpu.VMEM((tm, d), BF16)],
        compiler_params=_params("parallel", "arbitrary"),
        name=name,
    )(x2d, norm_w, w_t_bf16, col_scale)


def _gates_kernel(x_ref, nw_ref, wt_ref, alog_ref, dtb_ref, wdn_ref, *rest):
    wda_refs, (row_ref, col_ref, wdn16_ref, wda16_ref) = rest[:-4], rest[-4:]
    wdn16_ref[...] = wdn_ref[...].astype(BF16)
    wda16_ref[...] = jnp.concatenate([r[...] for r in wda_refs], axis=0).astype(BF16)

    h = _rmsnorm_rows(x_ref[...], nw_ref[...]).astype(BF16)
    r = lax.dot_general(wt_ref[...].astype(BF16), h, _NT, preferred_element_type=F32)
    beta = _sigmoid(r[0:DN_HEADS])
    a = r[DN_HEADS:2 * DN_HEADS] + dtb_ref[...]
    softplus = jnp.maximum(a, 0.0) + jnp.log1p(jnp.exp(-jnp.abs(a)))
    g = -jnp.exp(alog_ref[...]) * softplus
    tm = g.shape[1]
    ii = lax.broadcasted_iota(jnp.int32, (tm, tm), 0)
    jj = lax.broadcasted_iota(jnp.int32, (tm, tm), 1)
    same = (ii // CHUNK) == (jj // CHUNK)
    sums = jnp.concatenate([same & (ii <= jj), same], axis=1).astype(BF16)
    g1 = g.astype(BF16)
    r1 = g - g1.astype(F32)
    g2 = r1.astype(BF16)
    g3 = (r1 - g2.astype(F32)).astype(BF16)
    pieces = jnp.concatenate([g1, g2, g3, jnp.zeros_like(g1)], axis=0)
    acc = jnp.dot(pieces, sums, preferred_element_type=F32)
    nh = DN_HEADS
    acc = acc[0:nh] + acc[nh:2 * nh] + acc[2 * nh:3 * nh]
    gc, gl = acc[:, :tm], acc[:, tm:]
    rows = jnp.concatenate([beta, gc, gl], axis=0)
    row_ref[...] = rows
    padded = jnp.concatenate([rows, jnp.zeros((LANES - 3 * DN_HEADS, tm), F32)], axis=0)
    col_ref[...] = padded.T


def _gates(x3d, norm_w, w_in_t, a_log_col, dt_bias_col, off_gate, tm):
    b, t, d = x3d.shape
    nj = t // tm
    steps = b * nj
    piece = 2 * DN_HEADS
    off_da = off_gate + piece
    n_da = w_in_t.shape[0] - off_da
    rows_dn, rows_da = off_gate // steps, n_da // steps
    assert rows_dn * steps == off_gate and rows_dn % piece == 0
    assert rows_da * steps == n_da and rows_da % piece == 0 and off_da % piece == 0
    n_pieces = rows_da // piece

    def step(i, j):
        return i * nj + j

    def da_piece(k):
        return pl.BlockSpec((piece, d),
                            lambda i, j: (off_da // piece + step(i, j) * n_pieces + k, 0))

    return pl.pallas_call(
        _gates_kernel,
        grid=(b, nj),
        in_specs=[
            pl.BlockSpec((None, tm, d), lambda i, j: (i, j, 0)),
            pl.BlockSpec((1, d), lambda i, j: (0, 0)),
            pl.BlockSpec((piece, d), lambda i, j: (off_gate // piece, 0)),
            pl.BlockSpec((DN_HEADS, 1), lambda i, j: (0, 0)),
            pl.BlockSpec((DN_HEADS, 1), lambda i, j: (0, 0)),
            pl.BlockSpec((rows_dn, d), lambda i, j: (step(i, j), 0)),
        ] + [da_piece(k) for k in range(n_pieces)],
        out_specs=[
            pl.BlockSpec((None, 3 * DN_HEADS, tm), lambda i, j: (i, 0, j)),
            pl.BlockSpec((None, tm, LANES), lambda i, j: (i, j, 0)),
            pl.BlockSpec((rows_dn, d), lambda i, j: (step(i, j), 0)),
            pl.BlockSpec((rows_da, d), lambda i, j: (step(i, j), 0)),
        ],
        out_shape=[
            jax.ShapeDtypeStruct((b, 3 * DN_HEADS, t), F32),
            jax.ShapeDtypeStruct((b, t, LANES), F32),
            jax.ShapeDtypeStruct((off_gate, d), BF16),
            jax.ShapeDtypeStruct((n_da, d), BF16),
        ],
        compiler_params=_params("parallel", "arbitrary"),
        name="dn_gates",
    )(x3d, norm_w, w_in_t, a_log_col, dt_bias_col, w_in_t, *([w_in_t] * n_pieces))


def _dn_intra_kernel(q_ref, k_ref, v_ref, qh_ref, kh_ref, vh_ref, cwq_ref, cwk_ref, cwv_ref,
                     gcol_ref, grow_ref, wo_ref, u_ref, w_ref, qd_ref, kd_ref, a_ref, wo16_ref, xs_ref):
    wo16_ref[...] = wo_ref[...].astype(BF16)
    head = pl.program_id(1)
    first = pl.program_id(2) == 0
    tb = q_ref.shape[0]
    halo = SUBLANES
    for idx, (x_ref, xh_ref) in enumerate(((q_ref, qh_ref), (k_ref, kh_ref), (v_ref, vh_ref))):
        xs_ref[idx, 0:halo, :] = jnp.where(first, 0.0, xh_ref[...])
        xs_ref[idx, halo:halo + tb, :] = x_ref[...]

    lane = lax.broadcasted_iota(jnp.int32, (GROUP, LANES), 1)
    sel_beta = (lane == head).astype(F32)
    sel_gc = (lane == head + DN_HEADS).astype(F32)
    sel_gl = (lane == head + 2 * DN_HEADS).astype(F32)
    ri = lax.broadcasted_iota(jnp.int32, (GROUP, GROUP), 0)
    ci = lax.broadcasted_iota(jnp.int32, (GROUP, GROUP), 1)
    same = (ri // CHUNK) == (ci // CHUNK)
    incl = same & (ri >= ci)
    strict = same & (ri > ci)

    def conv_silu(idx, cw_ref, r0):
        base = r0 + halo - (CONV_WIDTH - 1)
        acc = xs_ref[idx, base:base + GROUP, :] * cw_ref[0:1, :]
        for i in range(1, CONV_WIDTH):
            acc = acc + xs_ref[idx, base + i:base + i + GROUP, :] * cw_ref[i:i + 1, :]
        return acc * _sigmoid(acc)

    n_groups = tb // GROUP
    levels = int(math.log2(CHUNK))
    n_pows = [None] * n_groups
    xs = [None] * n_groups

    def prepare(gi):
        r0 = gi * GROUP
        q = conv_silu(0, cwq_ref, r0)
        k = conv_silu(1, cwk_ref, r0)
        v = conv_silu(2, cwv_ref, r0)
        q = q * lax.rsqrt(jnp.sum(q * q, axis=-1, keepdims=True) + EPS) * (DN_DIM ** -0.5)
        k = k * lax.rsqrt(jnp.sum(k * k, axis=-1, keepdims=True) + EPS)

        gates = gcol_ref[r0:r0 + GROUP, :]
        beta = jnp.sum(gates * sel_beta, axis=-1, keepdims=True)
        gc = jnp.sum(gates * sel_gc, axis=-1, keepdims=True)
        gl = jnp.sum(gates * sel_gl, axis=-1, keepdims=True)
        gc_row = grow_ref[pl.ds(head + DN_HEADS, 1), r0:r0 + GROUP]

        decay = jnp.exp(jnp.where(incl, gc - gc_row, NEG_BIG))
        kb = k * beta
        k16 = k.astype(BF16)
        gram = lax.dot_general(jnp.concatenate([kb, q], axis=0).astype(BF16), k16, _NT,
                               preferred_element_type=F32)
        n_pows[gi] = jnp.where(strict, -(gram[:GROUP] * decay), 0.0)
        a_qk = gram[GROUP:] * decay

        egc = jnp.exp(gc)
        xs[gi] = jnp.concatenate([v * beta, kb * egc], axis=1)
        qd_ref[r0:r0 + GROUP, :] = (q * egc).astype(BF16)
        kd_ref[r0:r0 + GROUP, :] = (k * jnp.exp(gl - gc)).astype(BF16)
        for c in range(GROUP // CHUNK):
            lo = c * CHUNK
            a_ref[r0 + lo:r0 + lo + CHUNK, :] = a_qk[lo:lo + CHUNK, lo:lo + CHUNK].astype(BF16)

    def level(gi, lvl):
        n16 = n_pows[gi].astype(BF16)
        xs[gi] = xs[gi] + jnp.dot(n16, xs[gi].astype(BF16), preferred_element_type=F32)
        if lvl + 1 < levels:
            n_pows[gi] = jnp.dot(n16, n16, preferred_element_type=F32)

    def finish(gi):
        r0 = gi * GROUP
        u_ref[r0:r0 + GROUP, :] = xs[gi][:, :DN_DIM]
        w_ref[r0:r0 + GROUP, :] = xs[gi][:, DN_DIM:].astype(BF16)

    for t in range(n_groups + levels + 1):
        for gi in range(n_groups):
            stage = t - gi
            if stage == 0:
                prepare(gi)
            elif 1 <= stage <= levels:
                level(gi, stage - 1)
            elif stage == levels + 1:
                finish(gi)


def _dn_intra(qkvz, conv_w, gcol, grow, w_out, tb):
    b, t, _ = qkvz.shape
    hb = tb // SUBLANES
    nh = DN_HEADS
    nj = t // tb
    wo_rows, wo_share = _slab_rows(w_out.shape[0], b * nh * nj)
    assert wo_share == 1, "every grid step converts its own slab of w_out"
    wo_spec = pl.BlockSpec((wo_rows, w_out.shape[1]), lambda i, h, j: ((i * nh + h) * nj + j, 0))

    def tok(off):
        return pl.BlockSpec((None, tb, DN_DIM), lambda i, h, j: (i, j, off + h))

    def halo(off):
        return pl.BlockSpec((None, SUBLANES, DN_DIM),
                            lambda i, h, j: (i, jnp.maximum(j * hb - 1, 0), off + h))

    def cw(off):
        return pl.BlockSpec((CONV_WIDTH, DN_DIM), lambda i, h, j: (0, off + h))

    out_tok = pl.BlockSpec((None, tb, DN_DIM), lambda i, h, j: (i, j, h))
    return pl.pallas_call(
        _dn_intra_kernel,
        grid=(b, nh, nj),
        in_specs=[tok(0), tok(nh), tok(2 * nh), halo(0), halo(nh), halo(2 * nh),
                  cw(0), cw(nh), cw(2 * nh),
                  pl.BlockSpec((None, tb, LANES), lambda i, h, j: (i, j, 0)),
                  pl.BlockSpec((None, 3 * nh, tb), lambda i, h, j: (i, 0, j)),
                  wo_spec],
        out_specs=[out_tok, out_tok, out_tok, out_tok,
                   pl.BlockSpec((None, None, tb, CHUNK), lambda i, h, j: (i, h, j, 0)),
                   wo_spec],
        out_shape=[jax.ShapeDtypeStruct((b, t, DN_WIDTH), F32),
                   jax.ShapeDtypeStruct((b, t, DN_WIDTH), BF16),
                   jax.ShapeDtypeStruct((b, t, DN_WIDTH), BF16),
                   jax.ShapeDtypeStruct((b, t, DN_WIDTH), BF16),
                   jax.ShapeDtypeStruct((b, nh, t, CHUNK), BF16),
                   jax.ShapeDtypeStruct(w_out.shape, BF16)],
        scratch_shapes=[pltpu.VMEM((3, tb + SUBLANES, DN_DIM), F32)],
        compiler_params=_params("parallel", "parallel", "arbitrary"),
        name="dn_intra",
    )(qkvz, qkvz, qkvz, qkvz, qkvz, qkvz, conv_w, conv_w, conv_w, gcol, grow, w_out)


def _dn_scan_kernel(u_ref, w_ref, qd_ref, kd_ref, a_ref, z_ref, gcol_ref, nw_ref, o_ref, s_ref):
    @pl.when(pl.program_id(1) == 0)
    def _():
        s_ref[...] = jnp.zeros_like(s_ref)

    bb, tb = u_ref.shape[0], u_ref.shape[1]
    nw = nw_ref[...]
    units = [(i, h) for i in range(bb) for h in range(DN_HEADS)]

    def chunk(c, carry):
        rows = pl.ds(pl.multiple_of(c * CHUNK, CHUNK), CHUNK)
        cols = [slice(h * DN_DIM, (h + 1) * DN_DIM) for h in range(DN_HEADS)]
        s = [s_ref[i, h] for i, h in units]
        proj = [jnp.dot(jnp.concatenate([w_ref[i, rows, cols[h]], qd_ref[i, rows, cols[h]]], axis=0),
                        s[n].astype(BF16), preferred_element_type=F32)
                for n, (i, h) in enumerate(units)]
        v16 = [(u_ref[i, rows, cols[h]] - proj[n][:CHUNK]).astype(BF16)
               for n, (i, h) in enumerate(units)]
        upd = [lax.dot_general(kd_ref[i, rows, cols[h]], v16[n], _TN, preferred_element_type=F32)
               for n, (i, h) in enumerate(units)]
        for n, (i, h) in enumerate(units):
            gl = gcol_ref[i, pl.ds(pl.multiple_of(c * CHUNK, CHUNK), 1),
                          2 * DN_HEADS + h:2 * DN_HEADS + h + 1]
            s_ref[i, h] = s[n] * jnp.exp(gl) + upd[n]
        o = [proj[n][CHUNK:] + jnp.dot(a_ref[i, h, rows, :], v16[n], preferred_element_type=F32)
             for n, (i, h) in enumerate(units)]
        for n, (i, h) in enumerate(units):
            z = z_ref[i, rows, cols[h]]
            o_ref[i, rows, cols[h]] = (_rmsnorm_rows(o[n], nw) * (z * _sigmoid(z))).astype(o_ref.dtype)
        return carry

    lax.fori_loop(0, tb // CHUNK, chunk, 0)


def _dn_scan(u, w, qd, kd, a, qkvz, gcol, dn_norm_w, bb, tb):
    b, t, _ = u.shape
    tok = pl.BlockSpec((bb, tb, DN_WIDTH), lambda i, j: (i, j, 0))
    z_block = 3 * DN_WIDTH // DN_WIDTH
    return pl.pallas_call(
        _dn_scan_kernel,
        grid=(b // bb, t // tb),
        in_specs=[tok, tok, tok, tok,
                  pl.BlockSpec((bb, DN_HEADS, tb, CHUNK), lambda i, j: (i, 0, j, 0)),
                  pl.BlockSpec((bb, tb, DN_WIDTH), lambda i, j: (i, j, z_block)),
                  pl.BlockSpec((bb, tb, LANES), lambda i, j: (i, j, 0)),
                  pl.BlockSpec((1, DN_DIM), lambda i, j: (0, 0))],
        out_specs=tok,
        out_shape=jax.ShapeDtypeStruct((b, t, DN_WIDTH), BF16),
        scratch_shapes=[pltpu.VMEM((bb, DN_HEADS, DN_DIM, DN_DIM), F32)],
        compiler_params=_params("parallel", "arbitrary"),
        name="dn_scan",
    )(u, w, qd, kd, a, qkvz, gcol, dn_norm_w)


def _attn_kernel(q_ref, k_ref, v_ref, lq1_ref, lk1_ref, lq2_ref, lk2_ref, nw_ref,
                 wg_ref, wu_ref, wd_ref, o_ref, wg16_ref, wu16_ref, wd16_ref,
                 m_ref, l_ref, acc_ref, sa_ref, sb_ref, *, lam_init, down_share):
    wg16_ref[...] = wg_ref[...].astype(BF16)
    wu16_ref[...] = wu_ref[...].astype(BF16)
    grid_step = ((pl.program_id(0) * pl.num_programs(1) + pl.program_id(1)) * pl.num_programs(2)
                 + pl.program_id(2))

    @pl.when(grid_step % down_share == 0)
    def _():
        wd16_ref[...] = wd_ref[...].astype(BF16)

    head = pl.program_id(1)
    qi = pl.program_id(2)
    tq = q_ref.shape[0]
    tk = tq
    slope = jnp.float32(0.0)
    for hh in range(DA_HEADS):
        slope = jnp.where(head == hh, jnp.float32(2.0 ** (-8.0 * (hh + 1) / DA_HEADS)), slope)

    m_ref[...] = jnp.full_like(m_ref, NEG_BIG)
    l_ref[...] = jnp.zeros_like(l_ref)
    acc_ref[...] = jnp.zeros_like(acc_ref)
    kpos = lax.broadcasted_iota(jnp.int32, (1, tk), 1)
    slope2 = slope * LOG2E

    maps = range(2)

    def scores(j, s_ref):
        k0 = pl.multiple_of(j * tk, tk)
        bias = slope2 * (kpos + (j - qi) * tk).astype(F32)
        for c in maps:
            cols = slice(c * DA_DIM, (c + 1) * DA_DIM)
            s_ref[c] = lax.dot_general(q_ref[:, cols], k_ref[pl.ds(k0, tk), cols], _NT,
                                       preferred_element_type=F32) + bias

    def accumulate(j, s_ref, masked):
        k0 = pl.multiple_of(j * tk, tk)
        v = v_ref[pl.ds(k0, tk), :]
        s = [s_ref[c] for c in maps]
        if masked:
            rr = lax.broadcasted_iota(jnp.int32, (tq, tk), 0)
            cc = lax.broadcasted_iota(jnp.int32, (tq, tk), 1)
            s = [jnp.where(cc <= rr, s[c], NEG_BIG) for c in maps]
        m_prev = [m_ref[c] for c in maps]
        m_new = [jnp.maximum(m_prev[c], jnp.max(s[c], axis=-1, keepdims=True)) for c in maps]
        p = [jnp.exp2(s[c] - _lane_tile(m_new[c], tk // LANES)) for c in maps]
        p_sum = [jnp.sum(p[c], axis=-1, keepdims=True) for c in maps]
        p16 = [p[c].astype(BF16) for c in maps]
        pv = [jnp.dot(p16[c], v, preferred_element_type=F32) for c in maps]
        for c in maps:
            alpha = jnp.exp2(m_prev[c] - m_new[c])
            l_ref[c] = alpha * l_ref[c] + p_sum[c]
            acc_ref[c] = _lane_tile(alpha, DA_VDIM // LANES) * acc_ref[c] + pv[c]
            m_ref[c] = m_new[c]

    scores(0, sa_ref)

    def pair(jj, carry):
        j = 2 * jj
        scores(j + 1, sb_ref)
        accumulate(j, sa_ref, False)
        scores(j + 2, sa_ref)
        accumulate(j + 1, sb_ref, False)
        return carry

    lax.fori_loop(0, qi // 2, pair, 0)

    @pl.when(qi % 2 == 1)
    def _():
        scores(qi, sb_ref)
        accumulate(qi - 1, sa_ref, False)
        accumulate(qi, sb_ref, True)

    @pl.when(qi % 2 == 0)
    def _():
        accumulate(qi, sa_ref, True)

    lam = (jnp.exp(jnp.sum(lq1_ref[...] * lk1_ref[...], axis=-1, keepdims=True))
           - jnp.exp(jnp.sum(lq2_ref[...] * lk2_ref[...], axis=-1, keepdims=True)) + lam_init)
    rep = DA_VDIM // LANES
    o = (acc_ref[0] / _lane_tile(l_ref[0], rep)
         - lam * (acc_ref[1] / _lane_tile(l_ref[1], rep)))
    o_ref[...] = (_rmsnorm_rows(o, nw_ref[...]) * (1.0 - lam_init)).astype(o_ref.dtype)


def _slab_rows(n_rows, n_steps):
    for share in (1, 2, 4, 8, 16):
        if n_steps % share == 0 and n_rows % (n_steps // share) == 0:
            rows = n_rows // (n_steps // share)
            if rows % 16 == 0:
                return rows, share
    raise ValueError(f"cannot split {n_rows} weight rows over {n_steps} grid steps")


def _attention(da, lam_q1, lam_k1, lam_q2, lam_k2, da_norm_w, w_gate, w_up, w_down, lam_init, tq):
    b, t, _ = da.shape
    nh = DA_HEADS
    nq = t // tq
    d, dff = w_gate.shape
    steps = b * nh * nq
    rg, sg = _slab_rows(d, steps)
    rd, sd = _slab_rows(dff, steps)

    def step(i, h, j):
        return (i * nh + h) * nq + j

    vec = pl.BlockSpec((1, DA_DIM), lambda i, h, j: (0, 0))
    up_spec = pl.BlockSpec((rg, dff), lambda i, h, j: (step(i, h, j) // sg, 0))
    down_spec = pl.BlockSpec((rd, d), lambda i, h, j: (step(i, h, j) // sd, 0))
    return pl.pallas_call(
        functools.partial(_attn_kernel, lam_init=lam_init, down_share=sd),
        grid=(b, nh, nq),
        in_specs=[pl.BlockSpec((None, tq, DA_VDIM), lambda i, h, j: (i, j, h)),
                  pl.BlockSpec((None, t, DA_VDIM), lambda i, h, j: (i, 0, nh + h)),
                  pl.BlockSpec((None, t, DA_VDIM), lambda i, h, j: (i, 0, 2 * nh + h)),
                  vec, vec, vec, vec,
                  pl.BlockSpec((1, DA_VDIM), lambda i, h, j: (0, 0)),
                  up_spec, up_spec, down_spec],
        out_specs=[pl.BlockSpec((None, tq, DA_VDIM), lambda i, h, j: (i, j, h)),
                   up_spec, up_spec, down_spec],
        out_shape=[jax.ShapeDtypeStruct((b, t, DA_WIDTH), BF16),
                   jax.ShapeDtypeStruct((d, dff), BF16),
                   jax.ShapeDtypeStruct((d, dff), BF16),
                   jax.ShapeDtypeStruct((dff, d), BF16)],
        scratch_shapes=[pltpu.VMEM((2, tq, LANES), F32), pltpu.VMEM((2, tq, LANES), F32),
                        pltpu.VMEM((2, tq, DA_VDIM), F32),
                        pltpu.VMEM((2, tq, tq), F32), pltpu.VMEM((2, tq, tq), F32)],
        compiler_params=_params("arbitrary", "arbitrary", "arbitrary"),
        name="diff_attn",
    )(da, da, da, lam_q1, lam_k1, lam_q2, lam_k2, da_norm_w, w_gate, w_up, w_down)


def _out_proj_kernel(a_ref, b_ref, wa_ref, wb_ref, x_ref, o_ref):
    acc = jnp.dot(a_ref[...], wa_ref[...], preferred_element_type=F32)
    acc = acc + jnp.dot(b_ref[...], wb_ref[...], preferred_element_type=F32)
    o_ref[...] = x_ref[...] + acc


def _out_proj(o_dn, o_da, w16, x2d, tm, tn):
    m, d = x2d.shape
    ka, kb = o_dn.shape[1], o_da.shape[1]
    assert ka == kb and w16.shape[0] == ka + kb, "the two head groups are equal row halves of w_out"
    return pl.pallas_call(
        _out_proj_kernel,
        grid=(m // tm, d // tn),
        in_specs=[pl.BlockSpec((tm, ka), lambda i, j: (i, 0)),
                  pl.BlockSpec((tm, kb), lambda i, j: (i, 0)),
                  pl.BlockSpec((ka, tn), lambda i, j: (0, j)),
                  pl.BlockSpec((kb, tn), lambda i, j: (1, j)),
                  pl.BlockSpec((tm, tn), lambda i, j: (i, j))],
        out_specs=pl.BlockSpec((tm, tn), lambda i, j: (i, j)),
        out_shape=jax.ShapeDtypeStruct((m, d), F32),
        compiler_params=_params("parallel", "arbitrary"),
        name="out_proj",
    )(o_dn, o_da, w16, w16, x2d)


def _ffn_kernel(x_ref, nw_ref, wg_ref, wu_ref, wd_ref, fw_ref, o_ref, h_ref):
    f = pl.program_id(1)

    @pl.when(f == 0)
    def _():
        x = x_ref[...]
        h_ref[...] = _rmsnorm_rows(x, nw_ref[...]).astype(BF16)
        o_ref[...] = x

    h = h_ref[...]
    g = jnp.dot(h, wg_ref[...], preferred_element_type=F32)
    u = jnp.dot(h, wu_ref[...], preferred_element_type=F32)
    act = (g * _sigmoid(g) * u).astype(BF16)
    o_ref[...] += jnp.dot(act, wd_ref[...], preferred_element_type=F32)

    @pl.when(f == pl.num_programs(1) - 1)
    def _():
        o_ref[...] = _rmsnorm_rows(o_ref[...], fw_ref[...])


def _ffn(x2d, ffn_norm_w, w_gate, w_up, w_down, final_norm_w, tm, tf):
    m, d = x2d.shape
    dff = w_gate.shape[1]
    return pl.pallas_call(
        _ffn_kernel,
        grid=(m // tm, dff // tf),
        in_specs=[pl.BlockSpec((tm, d), lambda i, f: (i, 0)),
                  pl.BlockSpec((1, d), lambda i, f: (0, 0)),
                  pl.BlockSpec((d, tf), lambda i, f: (0, f)),
                  pl.BlockSpec((d, tf), lambda i, f: (0, f)),
                  pl.BlockSpec((tf, d), lambda i, f: (f, 0)),
                  pl.BlockSpec((1, d), lambda i, f: (0, 0))],
        out_specs=pl.BlockSpec((tm, d), lambda i, f: (i, 0)),
        out_shape=jax.ShapeDtypeStruct((m, d), F32),
        scratch_shapes=[pltpu.VMEM((tm, d), BF16)],
        compiler_params=_params("parallel", "arbitrary"),
        name="ffn",
    )(x2d, ffn_norm_w, w_gate, w_up, w_down, final_norm_w)


def _tile(n, pref):
    return pref if n % pref == 0 else n


def _layer(x, lam_init, attn_norm_w, w_in, conv_w, a_log, dt_bias, dn_norm_w, lam_q1, lam_k1,
           lam_q2, lam_k2, da_norm_w, w_out, ffn_norm_w, w_gate, w_up, w_down, final_norm_w):
    b, t, d = x.shape
    m = b * t
    x2d = x.reshape(m, d)
    off_z = 3 * DN_WIDTH
    off_b = off_z + DN_WIDTH
    off_q = off_b + 2 * DN_HEADS
    nw = attn_norm_w.reshape(1, d)

    n_da = w_in.shape[1] - off_q
    da_scale = jnp.concatenate([jnp.full((1, DA_HEADS * 2 * DA_DIM), LOG2E * DA_DIM ** -0.5, F32),
                                jnp.ones((1, n_da - DA_HEADS * 2 * DA_DIM), F32)], axis=1)

    w_in_t = w_in.T
    grow, gcol, w_dn, w_da = _gates(x, nw, w_in_t, a_log.reshape(DN_HEADS, 1),
                                    dt_bias.reshape(DN_HEADS, 1), off_b, _tile(t, 256))
    tm = _tile(m, 1024)
    qkvz = _norm_matmul(x2d, nw, w_dn, jnp.ones((1, off_b), F32), F32, tm, 1024, "in_proj_dn")
    da = _norm_matmul(x2d, nw, w_da, da_scale, BF16, tm, 1024, "in_proj_da")

    qkvz = qkvz.reshape(b, t, off_b)
    u, w, qd, kd, a, w_out16 = _dn_intra(qkvz, conv_w, gcol, grow, w_out, _tile(t, 2048))
    o_dn = _dn_scan(u, w, qd, kd, a, qkvz, gcol, dn_norm_w.reshape(1, DN_DIM),
                    4 if b % 4 == 0 else 1, _tile(t, 128))

    o_da, w_gate16, w_up16, w_down16 = _attention(
        da.reshape(b, t, n_da), lam_q1.reshape(1, DA_DIM), lam_k1.reshape(1, DA_DIM),
        lam_q2.reshape(1, DA_DIM), lam_k2.reshape(1, DA_DIM), da_norm_w.reshape(1, DA_VDIM),
        w_gate, w_up, w_down, lam_init, _tile(t, 512))

    x2d = _out_proj(o_dn.reshape(m, DN_WIDTH), o_da.reshape(m, DA_WIDTH), w_out16, x2d, tm,
                    _tile(d, 1024))
    dff = w_gate.shape[1]
    return _ffn(x2d, ffn_norm_w.reshape(1, d), w_gate16, w_up16, w_down16,
                final_norm_w.reshape(1, d), tm, _tile(dff, 512))


def kernel(x, attn_norm_w, w_in, conv_w, a_log, dt_bias, dn_norm_w, lam_q1, lam_k1, lam_q2, lam_k2,
           da_norm_w, w_out, ffn_norm_w, w_gate, w_up, w_down, final_norm_w):
    b, t, d = x.shape
    depth = w_in.shape[0]
    assert depth == 1, "the final rmsnorm is fused into the last layer's FFN kernel"
    lam_init = 0.8 - 0.6 * math.exp(-0.3 * 0)
    out = _layer(x, lam_init, attn_norm_w[0], w_in[0], conv_w[0], a_log[0], dt_bias[0],
                 dn_norm_w[0], lam_q1[0], lam_k1[0], lam_q2[0], lam_k2[0], da_norm_w[0],
                 w_out[0], ffn_norm_w[0], w_gate[0], w_up[0], w_down[0], final_norm_w)
    return out.reshape(b, t, d)
```

```python
import functools
import math

import jax
import jax.numpy as jnp
from jax import lax
from jax.experimental import pallas as pl
from jax.experimental.pallas import tpu as pltpu

F32 = jnp.float32
BF16 = jnp.bfloat16
EPS = 1e-6
NEG_BIG = -1e30
LOG2E = math.log2(math.e)

DN_HEADS = 8
DN_DIM = 128
DN_WIDTH = DN_HEADS * DN_DIM
CONV_WIDTH = 4
CHUNK = 64
DA_HEADS = 4
DA_DIM = 128
DA_VDIM = 2 * DA_DIM
DA_WIDTH = DA_HEADS * DA_VDIM
LANES = 128
SUBLANES = 8
GROUP = 256
VMEM_LIMIT = 56 * 1024 * 1024

_NT = (((1,), (1,)), ((), ()))
_TN = (((0,), (0,)), ((), ()))


def _params(*sem):
    return pltpu.CompilerParams(dimension_semantics=sem, vmem_limit_bytes=VMEM_LIMIT)


def _sigmoid(x):
    return 1.0 / (1.0 + jnp.exp(-x))


def _lane_tile(x, n):
    return jnp.concatenate([x] * n, axis=1)


def _rmsnorm_rows(x, w):
    return x * lax.rsqrt(jnp.mean(x * x, axis=-1, keepdims=True) + EPS) * w


def _proj_kernel(h_ref, w_ref, cs_ref, o_ref):
    acc = lax.dot_general(h_ref[...], w_ref[...], _NT, preferred_element_type=F32)
    o_ref[...] = (acc * cs_ref[...]).astype(o_ref.dtype)


def _proj(h2d, w_t_bf16, col_scale, out_dtype, tm, tn, name):
    m, d = h2d.shape
    n = w_t_bf16.shape[0]
    return pl.pallas_call(
        _proj_kernel,
        grid=(m // tm, n // tn),
        in_specs=[
            pl.BlockSpec((tm, d), lambda i, j: (i, 0)),
            pl.BlockSpec((tn, d), lambda i, j: (j, 0)),
            pl.BlockSpec((1, tn), lambda i, j: (0, j)),
        ],
        out_specs=pl.BlockSpec((tm, tn), lambda i, j: (i, j)),
        out_shape=jax.ShapeDtypeStruct((m, n), out_dtype),
        compiler_params=_params("parallel", "arbitrary"),
        name=name,
    )(h2d, w_t_bf16, col_scale)


def _gates_kernel(x_ref, nw_ref, wt_ref, alog_ref, dtb_ref, wdn_ref, *rest):
    wda_refs, (row_ref, col_ref, h_ref, wdn16_ref, wda16_ref) = rest[:-5], rest[-5:]
    wdn16_ref[...] = wdn_ref[...].astype(BF16)
    wda16_ref[...] = jnp.concatenate([r[...] for r in wda_refs], axis=0).astype(BF16)

    h = _rmsnorm_rows(x_ref[...], nw_ref[...]).astype(BF16)
    h_ref[...] = h
    r = lax.dot_general(wt_ref[...].astype(BF16), h, _NT, preferred_element_type=F32)
    beta = _sigmoid(r[0:DN_HEADS])
    a = r[DN_HEADS:2 * DN_HEADS] + dtb_ref[...]
    softplus = jnp.maximum(a, 0.0) + jnp.log1p(jnp.exp(-jnp.abs(a)))
    g = -jnp.exp(alog_ref[...]) * softplus
    tm = g.shape[1]
    ii = lax.broadcasted_iota(jnp.int32, (tm, tm), 0)
    jj = lax.broadcasted_iota(jnp.int32, (tm, tm), 1)
    same = (ii // CHUNK) == (jj // CHUNK)
    sums = jnp.concatenate([same & (ii <= jj), same], axis=1).astype(BF16)
    g1 = g.astype(BF16)
    r1 = g - g1.astype(F32)
    g2 = r1.astype(BF16)
    g3 = (r1 - g2.astype(F32)).astype(BF16)
    pieces = jnp.concatenate([g1, g2, g3, jnp.zeros_like(g1)], axis=0)
    acc = jnp.dot(pieces, sums, preferred_element_type=F32)
    nh = DN_HEADS
    acc = acc[0:nh] + acc[nh:2 * nh] + acc[2 * nh:3 * nh]
    gc, gl = acc[:, :tm], acc[:, tm:]
    rows = jnp.concatenate([beta, gc, gl], axis=0)
    row_ref[...] = rows
    padded = jnp.concatenate([rows, jnp.zeros((LANES - 3 * DN_HEADS, tm), F32)], axis=0)
    col_ref[...] = padded.T


def _gates(x3d, norm_w, w_in_t, a_log_col, dt_bias_col, off_gate, tm):
    b, t, d = x3d.shape
    nj = t // tm
    steps = b * nj
    piece = 2 * DN_HEADS
    off_da = off_gate + piece
    n_da = w_in_t.shape[0] - off_da
    rows_dn, rows_da = off_gate // steps, n_da // steps
    assert rows_dn * steps == off_gate and rows_dn % piece == 0
    assert rows_da * steps == n_da and rows_da % piece == 0 and off_da % piece == 0
    n_pieces = rows_da // piece

    def step(i, j):
        return i * nj + j

    def da_piece(k):
        return pl.BlockSpec((piece, d),
                            lambda i, j: (off_da // piece + step(i, j) * n_pieces + k, 0))

    return pl.pallas_call(
        _gates_kernel,
        grid=(b, nj),
        in_specs=[
            pl.BlockSpec((None, tm, d), lambda i, j: (i, j, 0)),
            pl.BlockSpec((1, d), lambda i, j: (0, 0)),
            pl.BlockSpec((piece, d), lambda i, j: (off_gate // piece, 0)),
            pl.BlockSpec((DN_HEADS, 1), lambda i, j: (0, 0)),
            pl.BlockSpec((DN_HEADS, 1), lambda i, j: (0, 0)),
            pl.BlockSpec((rows_dn, d), lambda i, j: (step(i, j), 0)),
        ] + [da_piece(k) for k in range(n_pieces)],
        out_specs=[
            pl.BlockSpec((None, 3 * DN_HEADS, tm), lambda i, j: (i, 0, j)),
            pl.BlockSpec((None, tm, LANES), lambda i, j: (i, j, 0)),
            pl.BlockSpec((None, tm, d), lambda i, j: (i, j, 0)),
            pl.BlockSpec((rows_dn, d), lambda i, j: (step(i, j), 0)),
            pl.BlockSpec((rows_da, d), lambda i, j: (step(i, j), 0)),
        ],
        out_shape=[
            jax.ShapeDtypeStruct((b, 3 * DN_HEADS, t), F32),
            jax.ShapeDtypeStruct((b, t, LANES), F32),
            jax.ShapeDtypeStruct((b, t, d), BF16),
            jax.ShapeDtypeStruct((off_gate, d), BF16),
            jax.ShapeDtypeStruct((n_da, d), BF16),
        ],
        compiler_params=_params("parallel", "arbitrary"),
        name="dn_gates",
    )(x3d, norm_w, w_in_t, a_log_col, dt_bias_col, w_in_t, *([w_in_t] * n_pieces))


def _dn_intra_kernel(q_ref, k_ref, v_ref, qh_ref, kh_ref, vh_ref, cwq_ref, cwk_ref, cwv_ref,
                     gcol_ref, grow_ref, wo_ref, u_ref, w_ref, qd_ref, kd_ref, a_ref, wo16_ref, xs_ref):
    wo16_ref[...] = wo_ref[...].astype(BF16)
    head = pl.program_id(1)
    first = pl.program_id(2) == 0
    tb = q_ref.shape[0]
    halo = SUBLANES
    for idx, (x_ref, xh_ref) in enumerate(((q_ref, qh_ref), (k_ref, kh_ref), (v_ref, vh_ref))):
        xs_ref[idx, 0:halo, :] = jnp.where(first, 0.0, xh_ref[...])
        xs_ref[idx, halo:halo + tb, :] = x_ref[...]

    lane = lax.broadcasted_iota(jnp.int32, (GROUP, LANES), 1)
    sel_beta = (lane == head).astype(F32)
    sel_gc = (lane == head + DN_HEADS).astype(F32)
    sel_gl = (lane == head + 2 * DN_HEADS).astype(F32)
    ri = lax.broadcasted_iota(jnp.int32, (GROUP, GROUP), 0)
    ci = lax.broadcasted_iota(jnp.int32, (GROUP, GROUP), 1)
    same = (ri // CHUNK) == (ci // CHUNK)
    incl = same & (ri >= ci)
    strict = same & (ri > ci)

    def conv_silu(idx, cw_ref, r0):
        base = r0 + halo - (CONV_WIDTH - 1)
        acc = xs_ref[idx, base:base + GROUP, :] * cw_ref[0:1, :]
        for i in range(1, CONV_WIDTH):
            acc = acc + xs_ref[idx, base + i:base + i + GROUP, :] * cw_ref[i:i + 1, :]
        return acc * _sigmoid(acc)

    n_groups = tb // GROUP
    levels = int(math.log2(CHUNK))
    n_pows = [None] * n_groups
    xs = [None] * n_groups

    def prepare(gi):
        r0 = gi * GROUP
        q = conv_silu(0, cwq_ref, r0)
        k = conv_silu(1, cwk_ref, r0)
        v = conv_silu(2, cwv_ref, r0)
        q = q * lax.rsqrt(jnp.sum(q * q, axis=-1, keepdims=True) + EPS) * (DN_DIM ** -0.5)
        k = k * lax.rsqrt(jnp.sum(k * k, axis=-1, keepdims=True) + EPS)

        gates = gcol_ref[r0:r0 + GROUP, :]
        beta = jnp.sum(gates * sel_beta, axis=-1, keepdims=True)
        gc = jnp.sum(gates * sel_gc, axis=-1, keepdims=True)
        gl = jnp.sum(gates * sel_gl, axis=-1, keepdims=True)
        gc_row = grow_ref[pl.ds(head + DN_HEADS, 1), r0:r0 + GROUP]

        decay = jnp.exp(jnp.where(incl, gc - gc_row, NEG_BIG))
        kb = k * beta
        k16 = k.astype(BF16)
        gram = lax.dot_general(jnp.concatenate([kb, q], axis=0).astype(BF16), k16, _NT,
                               preferred_element_type=F32)
        n_pows[gi] = jnp.where(strict, -(gram[:GROUP] * decay), 0.0)
        a_qk = gram[GROUP:] * decay

        egc = jnp.exp(gc)
        xs[gi] = jnp.concatenate([v * beta, kb * egc], axis=1)
        qd_ref[r0:r0 + GROUP, :] = (q * egc).astype(BF16)
        kd_ref[r0:r0 + GROUP, :] = (k * jnp.exp(gl - gc)).astype(BF16)
        for c in range(GROUP // CHUNK):
            lo = c * CHUNK
            a_ref[r0 + lo:r0 + lo + CHUNK, :] = a_qk[lo:lo + CHUNK, lo:lo + CHUNK].astype(BF16)

    def level(gi, lvl):
        n16 = n_pows[gi].astype(BF16)
        xs[gi] = xs[gi] + jnp.dot(n16, xs[gi].astype(BF16), preferred_element_type=F32)
        if lvl + 1 < levels:
            n_pows[gi] = jnp.dot(n16, n16, preferred_element_type=F32)

    def finish(gi):
        r0 = gi * GROUP
        u_ref[r0:r0 + GROUP, :] = xs[gi][:, :DN_DIM]
        w_ref[r0:r0 + GROUP, :] = xs[gi][:, DN_DIM:].astype(BF16)

    for t in range(n_groups + levels + 1):
        for gi in range(n_groups):
            stage = t - gi
            if stage == 0:
                prepare(gi)
            elif 1 <= stage <= levels:
                level(gi, stage - 1)
            elif stage == levels + 1:
                finish(gi)


def _dn_intra(qkvz, conv_w, gcol, grow, w_out, tb):
    b, t, _ = qkvz.shape
    hb = tb // SUBLANES
    nh = DN_HEADS
    nj = t // tb
    wo_rows, wo_share = _slab_rows(w_out.shape[0], b * nh * nj)
    assert wo_share == 1, "every grid step converts its own slab of w_out"
    wo_spec = pl.BlockSpec((wo_rows, w_out.shape[1]), lambda i, h, j: ((i * nh + h) * nj + j, 0))

    def tok(off):
        return pl.BlockSpec((None, tb, DN_DIM), lambda i, h, j: (i, j, off + h))

    def halo(off):
        return pl.BlockSpec((None, SUBLANES, DN_DIM),
                            lambda i, h, j: (i, jnp.maximum(j * hb - 1, 0), off + h))

    def cw(off):
        return pl.BlockSpec((CONV_WIDTH, DN_DIM), lambda i, h, j: (0, off + h))

    out_tok = pl.BlockSpec((None, tb, DN_DIM), lambda i, h, j: (i, j, h))
    return pl.pallas_call(
        _dn_intra_kernel,
        grid=(b, nh, nj),
        in_specs=[tok(0), tok(nh), tok(2 * nh), halo(0), halo(nh), halo(2 * nh),
                  cw(0), cw(nh), cw(2 * nh),
                  pl.BlockSpec((None, tb, LANES), lambda i, h, j: (i, j, 0)),
                  pl.BlockSpec((None, 3 * nh, tb), lambda i, h, j: (i, 0, j)),
                  wo_spec],
        out_specs=[out_tok, out_tok, out_tok, out_tok,
                   pl.BlockSpec((None, None, tb, CHUNK), lambda i, h, j: (i, h, j, 0)),
                   wo_spec],
        out_shape=[jax.ShapeDtypeStruct((b, t, DN_WIDTH), F32),
                   jax.ShapeDtypeStruct((b, t, DN_WIDTH), BF16),
                   jax.ShapeDtypeStruct((b, t, DN_WIDTH), BF16),
                   jax.ShapeDtypeStruct((b, t, DN_WIDTH), BF16),
                   jax.ShapeDtypeStruct((b, nh, t, CHUNK), BF16),
                   jax.ShapeDtypeStruct(w_out.shape, BF16)],
        scratch_shapes=[pltpu.VMEM((3, tb + SUBLANES, DN_DIM), F32)],
        compiler_params=_params("parallel", "parallel", "arbitrary"),
        name="dn_intra",
    )(qkvz, qkvz, qkvz, qkvz, qkvz, qkvz, conv_w, conv_w, conv_w, gcol, grow, w_out)


def _dn_scan_kernel(u_ref, w_ref, qd_ref, kd_ref, a_ref, z_ref, gcol_ref, nw_ref, o_ref, s_ref):
    @pl.when(pl.program_id(1) == 0)
    def _():
        s_ref[...] = jnp.zeros_like(s_ref)

    bb, tb = u_ref.shape[0], u_ref.shape[1]
    nw = nw_ref[...]
    units = [(i, h) for i in range(bb) for h in range(DN_HEADS)]

    def chunk(c, carry):
        rows = pl.ds(pl.multiple_of(c * CHUNK, CHUNK), CHUNK)
        cols = [slice(h * DN_DIM, (h + 1) * DN_DIM) for h in range(DN_HEADS)]
        s = [s_ref[i, h] for i, h in units]
        proj = [jnp.dot(jnp.concatenate([w_ref[i, rows, cols[h]], qd_ref[i, rows, cols[h]]], axis=0),
                        s[n].astype(BF16), preferred_element_type=F32)
                for n, (i, h) in enumerate(units)]
        v16 = [(u_ref[i, rows, cols[h]] - proj[n][:CHUNK]).astype(BF16)
               for n, (i, h) in enumerate(units)]
        upd = [lax.dot_general(kd_ref[i, rows, cols[h]], v16[n], _TN, preferred_element_type=F32)
               for n, (i, h) in enumerate(units)]
        for n, (i, h) in enumerate(units):
            gl = gcol_ref[i, pl.ds(pl.multiple_of(c * CHUNK, CHUNK), 1),
                          2 * DN_HEADS + h:2 * DN_HEADS + h + 1]
            s_ref[i, h] = s[n] * jnp.exp(gl) + upd[n]
        o = [proj[n][CHUNK:] + jnp.dot(a_ref[i, h, rows, :], v16[n], preferred_element_type=F32)
             for n, (i, h) in enumerate(units)]
        for n, (i, h) in enumerate(units):
            z = z_ref[i, rows, cols[h]]
            o_ref[i, rows, cols[h]] = (_rmsnorm_rows(o[n], nw) * (z * _sigmoid(z))).astype(o_ref.dtype)
        return carry

    lax.fori_loop(0, tb // CHUNK, chunk, 0)


def _dn_scan(u, w, qd, kd, a, qkvz, gcol, dn_norm_w, bb, tb):
    b, t, _ = u.shape
    tok = pl.BlockSpec((bb, tb, DN_WIDTH), lambda i, j: (i, j, 0))
    z_block = 3 * DN_WIDTH // DN_WIDTH
    return pl.pallas_call(
        _dn_scan_kernel,
        grid=(b // bb, t // tb),
        in_specs=[tok, tok, tok, tok,
                  pl.BlockSpec((bb, DN_HEADS, tb, CHUNK), lambda i, j: (i, 0, j, 0)),
                  pl.BlockSpec((bb, tb, DN_WIDTH), lambda i, j: (i, j, z_block)),
                  pl.BlockSpec((bb, tb, LANES), lambda i, j: (i, j, 0)),
                  pl.BlockSpec((1, DN_DIM), lambda i, j: (0, 0))],
        out_specs=tok,
        out_shape=jax.ShapeDtypeStruct((b, t, DN_WIDTH), BF16),
        scratch_shapes=[pltpu.VMEM((bb, DN_HEADS, DN_DIM, DN_DIM), F32)],
        compiler_params=_params("parallel", "arbitrary"),
        name="dn_scan",
    )(u, w, qd, kd, a, qkvz, gcol, dn_norm_w)


def _attn_kernel(q_ref, k_ref, v_ref, lq1_ref, lk1_ref, lq2_ref, lk2_ref, nw_ref,
                 wg_ref, wu_ref, wd_ref, o_ref, wg16_ref, wu16_ref, wd16_ref,
                 m_ref, l_ref, acc_ref, sa_ref, sb_ref, *, lam_init, down_share):
    wg16_ref[...] = wg_ref[...].astype(BF16)
    wu16_ref[...] = wu_ref[...].astype(BF16)
    grid_step = ((pl.program_id(0) * pl.num_programs(1) + pl.program_id(1)) * pl.num_programs(2)
                 + pl.program_id(2))

    @pl.when(grid_step % down_share == 0)
    def _():
        wd16_ref[...] = wd_ref[...].astype(BF16)

    head = pl.program_id(1)
    qi = pl.program_id(2)
    tq = q_ref.shape[0]
    tk = tq
    slope = jnp.float32(0.0)
    for hh in range(DA_HEADS):
        slope = jnp.where(head == hh, jnp.float32(2.0 ** (-8.0 * (hh + 1) / DA_HEADS)), slope)

    m_ref[...] = jnp.full_like(m_ref, NEG_BIG)
    l_ref[...] = jnp.zeros_like(l_ref)
    acc_ref[...] = jnp.zeros_like(acc_ref)
    kpos = lax.broadcasted_iota(jnp.int32, (1, tk), 1)
    slope2 = slope * LOG2E

    maps = range(2)

    def scores(j, s_ref):
        k0 = pl.multiple_of(j * tk, tk)
        bias = slope2 * (kpos + (j - qi) * tk).astype(F32)
        for c in maps:
            cols = slice(c * DA_DIM, (c + 1) * DA_DIM)
            s_ref[c] = lax.dot_general(q_ref[:, cols], k_ref[pl.ds(k0, tk), cols], _NT,
                                       preferred_element_type=F32) + bias

    def accumulate(j, s_ref, masked):
        k0 = pl.multiple_of(j * tk, tk)
        v = v_ref[pl.ds(k0, tk), :]
        s = [s_ref[c] for c in maps]
        if masked:
            rr = lax.broadcasted_iota(jnp.int32, (tq, tk), 0)
            cc = lax.broadcasted_iota(jnp.int32, (tq, tk), 1)
            s = [jnp.where(cc <= rr, s[c], NEG_BIG) for c in maps]
        m_prev = [m_ref[c] for c in maps]
        m_new = [jnp.maximum(m_prev[c], jnp.max(s[c], axis=-1, keepdims=True)) for c in maps]
        p = [jnp.exp2(s[c] - _lane_tile(m_new[c], tk // LANES)) for c in maps]
        p_sum = [jnp.sum(p[c], axis=-1, keepdims=True) for c in maps]
        p16 = [p[c].astype(BF16) for c in maps]
        pv = [jnp.dot(p16[c], v, preferred_element_type=F32) for c in maps]
        for c in maps:
            alpha = jnp.exp2(m_prev[c] - m_new[c])
            l_ref[c] = alpha * l_ref[c] + p_sum[c]
            acc_ref[c] = _lane_tile(alpha, DA_VDIM // LANES) * acc_ref[c] + pv[c]
            m_ref[c] = m_new[c]

    scores(0, sa_ref)

    def pair(jj, carry):
        j = 2 * jj
        scores(j + 1, sb_ref)
        accumulate(j, sa_ref, False)
        scores(j + 2, sa_ref)
        accumulate(j + 1, sb_ref, False)
        return carry

    lax.fori_loop(0, qi // 2, pair, 0)

    @pl.when(qi % 2 == 1)
    def _():
        scores(qi, sb_ref)
        accumulate(qi - 1, sa_ref, False)
        accumulate(qi, sb_ref, True)

    @pl.when(qi % 2 == 0)
    def _():
        accumulate(qi, sa_ref, True)

    lam = (jnp.exp(jnp.sum(lq1_ref[...] * lk1_ref[...], axis=-1, keepdims=True))
           - jnp.exp(jnp.sum(lq2_ref[...] * lk2_ref[...], axis=-1, keepdims=True)) + lam_init)
    rep = DA_VDIM // LANES
    o = (acc_ref[0] / _lane_tile(l_ref[0], rep)
         - lam * (acc_ref[1] / _lane_tile(l_ref[1], rep)))
    o_ref[...] = (_rmsnorm_rows(o, nw_ref[...]) * (1.0 - lam_init)).astype(o_ref.dtype)


def _slab_rows(n_rows, n_steps):
    for share in (1, 2, 4, 8, 16):
        if n_steps % share == 0 and n_rows % (n_steps // share) == 0:
            rows = n_rows // (n_steps // share)
            if rows % 16 == 0:
                return rows, share
    raise ValueError(f"cannot split {n_rows} weight rows over {n_steps} grid steps")


def _attention(da, lam_q1, lam_k1, lam_q2, lam_k2, da_norm_w, w_gate, w_up, w_down, lam_init, tq):
    b, t, _ = da.shape
    nh = DA_HEADS
    nq = t // tq
    d, dff = w_gate.shape
    steps = b * nh * nq
    rg, sg = _slab_rows(d, steps)
    rd, sd = _slab_rows(dff, steps)

    def step(i, h, j):
        return (i * nh + h) * nq + j

    vec = pl.BlockSpec((1, DA_DIM), lambda i, h, j: (0, 0))
    up_spec = pl.BlockSpec((rg, dff), lambda i, h, j: (step(i, h, j) // sg, 0))
    down_spec = pl.BlockSpec((rd, d), lambda i, h, j: (step(i, h, j) // sd, 0))
    return pl.pallas_call(
        functools.partial(_attn_kernel, lam_init=lam_init, down_share=sd),
        grid=(b, nh, nq),
        in_specs=[pl.BlockSpec((None, tq, DA_VDIM), lambda i, h, j: (i, j, h)),
                  pl.BlockSpec((None, t, DA_VDIM), lambda i, h, j: (i, 0, nh + h)),
                  pl.BlockSpec((None, t, DA_VDIM), lambda i, h, j: (i, 0, 2 * nh + h)),
                  vec, vec, vec, vec,
                  pl.BlockSpec((1, DA_VDIM), lambda i, h, j: (0, 0)),
                  up_spec, up_spec, down_spec],
        out_specs=[pl.BlockSpec((None, tq, DA_VDIM), lambda i, h, j: (i, j, h)),
                   up_spec, up_spec, down_spec],
        out_shape=[jax.ShapeDtypeStruct((b, t, DA_WIDTH), BF16),
                   jax.ShapeDtypeStruct((d, dff), BF16),
                   jax.ShapeDtypeStruct((d, dff), BF16),
                   jax.ShapeDtypeStruct((dff, d), BF16)],
        scratch_shapes=[pltpu.VMEM((2, tq, LANES), F32), pltpu.VMEM((2, tq, LANES), F32),
                        pltpu.VMEM((2, tq, DA_VDIM), F32),
                        pltpu.VMEM((2, tq, tq), F32), pltpu.VMEM((2, tq, tq), F32)],
        compiler_params=_params("arbitrary", "arbitrary", "arbitrary"),
        name="diff_attn",
    )(da, da, da, lam_q1, lam_k1, lam_q2, lam_k2, da_norm_w, w_gate, w_up, w_down)


def _out_proj_kernel(a_ref, b_ref, wa_ref, wb_ref, x_ref, o_ref):
    acc = jnp.dot(a_ref[...], wa_ref[...], preferred_element_type=F32)
    acc = acc + jnp.dot(b_ref[...], wb_ref[...], preferred_element_type=F32)
    o_ref[...] = x_ref[...] + acc


def _out_proj(o_dn, o_da, w16, x2d, tm, tn):
    m, d = x2d.shape
    ka, kb = o_dn.shape[1], o_da.shape[1]
    assert ka == kb and w16.shape[0] == ka + kb, "the two head groups are equal row halves of w_out"
    return pl.pallas_call(
        _out_proj_kernel,
        grid=(m // tm, d // tn),
        in_specs=[pl.BlockSpec((tm, ka), lambda i, j: (i, 0)),
                  pl.BlockSpec((tm, kb), lambda i, j: (i, 0)),
                  pl.BlockSpec((ka, tn), lambda i, j: (0, j)),
                  pl.BlockSpec((kb, tn), lambda i, j: (1, j)),
                  pl.BlockSpec((tm, tn), lambda i, j: (i, j))],
        out_specs=pl.BlockSpec((tm, tn), lambda i, j: (i, j)),
        out_shape=jax.ShapeDtypeStruct((m, d), F32),
        compiler_params=_params("parallel", "arbitrary"),
        name="out_proj",
    )(o_dn, o_da, w16, w16, x2d)


def _ffn_kernel(x_ref, nw_ref, wg_ref, wu_ref, wd_ref, fw_ref, o_ref, h_ref):
    f = pl.program_id(1)

    @pl.when(f == 0)
    def _():
        x = x_ref[...]
        h_ref[...] = _rmsnorm_rows(x, nw_ref[...]).astype(BF16)
        o_ref[...] = x

    h = h_ref[...]
    g = jnp.dot(h, wg_ref[...], preferred_element_type=F32)
    u = jnp.dot(h, wu_ref[...], preferred_element_type=F32)
    act = (g * _sigmoid(g) * u).astype(BF16)
    o_ref[...] += jnp.dot(act, wd_ref[...], preferred_element_type=F32)

    @pl.when(f == pl.num_programs(1) - 1)
    def _():
        o_ref[...] = _rmsnorm_rows(o_ref[...], fw_ref[...])


def _ffn(x2d, ffn_norm_w, w_gate, w_up, w_down, final_norm_w, tm, tf):
    m, d = x2d.shape
    dff = w_gate.shape[1]
    return pl.pallas_call(
        _ffn_kernel,
        grid=(m // tm, dff // tf),
        in_specs=[pl.BlockSpec((tm, d), lambda i, f: (i, 0)),
                  pl.BlockSpec((1, d), lambda i, f: (0, 0)),
                  pl.BlockSpec((d, tf), lambda i, f: (0, f)),
                  pl.BlockSpec((d, tf), lambda i, f: (0, f)),
                  pl.BlockSpec((tf, d), lambda i, f: (f, 0)),
                  pl.BlockSpec((1, d), lambda i, f: (0, 0))],
        out_specs=pl.BlockSpec((tm, d), lambda i, f: (i, 0)),
        out_shape=jax.ShapeDtypeStruct((m, d), F32),
        scratch_shapes=[pltpu.VMEM((tm, d), BF16)],
        compiler_params=_params("parallel", "arbitrary"),
        name="ffn",
    )(x2d, ffn_norm_w, w_gate, w_up, w_down, final_norm_w)


def _tile(n, pref):
    return pref if n % pref == 0 else n


def _layer(x, lam_init, attn_norm_w, w_in, conv_w, a_log, dt_bias, dn_norm_w, lam_q1, lam_k1,
           lam_q2, lam_k2, da_norm_w, w_out, ffn_norm_w, w_gate, w_up, w_down, final_norm_w):
    b, t, d = x.shape
    m = b * t
    x2d = x.reshape(m, d)
    off_z = 3 * DN_WIDTH
    off_b = off_z + DN_WIDTH
    off_q = off_b + 2 * DN_HEADS
    nw = attn_norm_w.reshape(1, d)

    n_da = w_in.shape[1] - off_q
    da_scale = jnp.concatenate([jnp.full((1, DA_HEADS * 2 * DA_DIM), LOG2E * DA_DIM ** -0.5, F32),
                                jnp.ones((1, n_da - DA_HEADS * 2 * DA_DIM), F32)], axis=1)

    w_in_t = w_in.T
    grow, gcol, h, w_dn, w_da = _gates(x, nw, w_in_t, a_log.reshape(DN_HEADS, 1),
                                       dt_bias.reshape(DN_HEADS, 1), off_b, _tile(t, 256))
    h2d = h.reshape(m, d)
    tp = _tile(m, 2048)
    qkvz = _proj(h2d, w_dn, jnp.ones((1, off_b), F32), F32, tp, 1024, "in_proj_dn")
    da = _proj(h2d, w_da, da_scale, BF16, tp, 1024, "in_proj_da")
    tm = _tile(m, 1024)

    qkvz = qkvz.reshape(b, t, off_b)
    u, w, qd, kd, a, w_out16 = _dn_intra(qkvz, conv_w, gcol, grow, w_out, _tile(t, 2048))
    o_dn = _dn_scan(u, w, qd, kd, a, qkvz, gcol, dn_norm_w.reshape(1, DN_DIM),
                    4 if b % 4 == 0 else 1, _tile(t, 128))

    o_da, w_gate16, w_up16, w_down16 = _attention(
        da.reshape(b, t, n_da), lam_q1.reshape(1, DA_DIM), lam_k1.reshape(1, DA_DIM),
        lam_q2.reshape(1, DA_DIM), lam_k2.reshape(1, DA_DIM), da_norm_w.reshape(1, DA_VDIM),
        w_gate, w_up, w_down, lam_init, _tile(t, 512))

    x2d = _out_proj(o_dn.reshape(m, DN_WIDTH), o_da.reshape(m, DA_WIDTH), w_out16, x2d, tm,
                    _tile(d, 1024))
    dff = w_gate.shape[1]
    return _ffn(x2d, ffn_norm_w.reshape(1, d), w_gate16, w_up16, w_down16,
                final_norm_w.reshape(1, d), tm, _tile(dff, 512))


def kernel(x, attn_norm_w, w_in, conv_w, a_log, dt_bias, dn_norm_w, lam_q1, lam_k1, lam_q2, lam_k2,
           da_norm_w, w_out, ffn_norm_w, w_gate, w_up, w_down, final_norm_w):
    b, t, d = x.shape
    depth = w_in.shape[0]
    assert depth == 1, "the final rmsnorm is fused into the last layer's FFN kernel"
    lam_init = 0.8 - 0.6 * math.exp(-0.3 * 0)
    out = _layer(x, lam_init, attn_norm_w[0], w_in[0], conv_w[0], a_log[0], dt_bias[0],
                 dn_norm_w[0], lam_q1[0], lam_k1[0], lam_q2[0], lam_k2[0], da_norm_w[0],
                 w_out[0], ffn_norm_w[0], w_gate[0], w_up[0], w_down[0], final_norm_w)
    return out.reshape(b, t, d)
```

```python
import functools
import math

import jax
import jax.numpy as jnp
from jax import lax
from jax.experimental import pallas as pl
from jax.experimental.pallas import tpu as pltpu

F32 = jnp.float32
BF16 = jnp.bfloat16
EPS = 1e-6
NEG_BIG = -1e30
LOG2E = math.log2(math.e)

DN_HEADS = 8
DN_DIM = 128
DN_WIDTH = DN_HEADS * DN_DIM
CONV_WIDTH = 4
CHUNK = 64
DA_HEADS = 4
DA_DIM = 128
DA_VDIM = 2 * DA_DIM
DA_WIDTH = DA_HEADS * DA_VDIM
LANES = 128
SUBLANES = 8
GROUP = 256
VMEM_LIMIT = 56 * 1024 * 1024

_NT = (((1,), (1,)), ((), ()))
_TN = (((0,), (0,)), ((), ()))


def _params(*sem):
    return pltpu.CompilerParams(dimension_semantics=sem, vmem_limit_bytes=VMEM_LIMIT)


def _sigmoid(x):
    return 1.0 / (1.0 + jnp.exp(-x))


def _lane_tile(x, n):
    return jnp.concatenate([x] * n, axis=1)


def _rmsnorm_rows(x, w):
    return x * lax.rsqrt(jnp.mean(x * x, axis=-1, keepdims=True) + EPS) * w


def _proj_kernel(h_ref, w_ref, cs_ref, o_ref):
    acc = lax.dot_general(h_ref[...], w_ref[...], _NT, preferred_element_type=F32)
    o_ref[...] = (acc * cs_ref[...]).astype(o_ref.dtype)


def _proj(h2d, w_t_bf16, col_scale, out_dtype, tm, tn, name):
    m, d = h2d.shape
    n = w_t_bf16.shape[0]
    return pl.pallas_call(
        _proj_kernel,
        grid=(m // tm, n // tn),
        in_specs=[
            pl.BlockSpec((tm, d), lambda i, j: (i, 0)),
            pl.BlockSpec((tn, d), lambda i, j: (j, 0)),
            pl.BlockSpec((1, tn), lambda i, j: (0, j)),
        ],
        out_specs=pl.BlockSpec((tm, tn), lambda i, j: (i, j)),
        out_shape=jax.ShapeDtypeStruct((m, n), out_dtype),
        compiler_params=_params("parallel", "arbitrary"),
        name=name,
    )(h2d, w_t_bf16, col_scale)


def _gates_kernel(x_ref, nw_ref, wt_ref, alog_ref, dtb_ref, wdn_ref, *rest):
    wda_refs, (row_ref, col_ref, h_ref, wdn16_ref, wda16_ref) = rest[:-5], rest[-5:]
    wdn16_ref[...] = wdn_ref[...].astype(BF16)
    wda16_ref[...] = jnp.concatenate([r[...] for r in wda_refs], axis=0).astype(BF16)

    h = _rmsnorm_rows(x_ref[...], nw_ref[...]).astype(BF16)
    h_ref[...] = h
    r = lax.dot_general(wt_ref[...].astype(BF16), h, _NT, preferred_element_type=F32)
    beta = _sigmoid(r[0:DN_HEADS])
    a = r[DN_HEADS:2 * DN_HEADS] + dtb_ref[...]
    softplus = jnp.maximum(a, 0.0) + jnp.log1p(jnp.exp(-jnp.abs(a)))
    g = -jnp.exp(alog_ref[...]) * softplus
    tm = g.shape[1]
    ii = lax.broadcasted_iota(jnp.int32, (tm, tm), 0)
    jj = lax.broadcasted_iota(jnp.int32, (tm, tm), 1)
    same = (ii // CHUNK) == (jj // CHUNK)
    sums = jnp.concatenate([same & (ii <= jj), same], axis=1).astype(BF16)
    g1 = g.astype(BF16)
    r1 = g - g1.astype(F32)
    g2 = r1.astype(BF16)
    g3 = (r1 - g2.astype(F32)).astype(BF16)
    pieces = jnp.concatenate([g1, g2, g3, jnp.zeros_like(g1)], axis=0)
    acc = jnp.dot(pieces, sums, preferred_element_type=F32)
    nh = DN_HEADS
    acc = acc[0:nh] + acc[nh:2 * nh] + acc[2 * nh:3 * nh]
    gc, gl = acc[:, :tm], acc[:, tm:]
    rows = jnp.concatenate([beta, gc, gl], axis=0)
    row_ref[...] = rows
    padded = jnp.concatenate([rows, jnp.zeros((LANES - 3 * DN_HEADS, tm), F32)], axis=0)
    col_ref[...] = padded.T


def _gates(x3d, norm_w, w_in_t, a_log_col, dt_bias_col, off_gate, tm):
    b, t, d = x3d.shape
    nj = t // tm
    steps = b * nj
    piece = 2 * DN_HEADS
    off_da = off_gate + piece
    n_da = w_in_t.shape[0] - off_da
    rows_dn, rows_da = off_gate // steps, n_da // steps
    assert rows_dn * steps == off_gate and rows_dn % piece == 0
    assert rows_da * steps == n_da and rows_da % piece == 0 and off_da % piece == 0
    n_pieces = rows_da // piece

    def step(i, j):
        return i * nj + j

    def da_piece(k):
        return pl.BlockSpec((piece, d),
                            lambda i, j: (off_da // piece + step(i, j) * n_pieces + k, 0))

    return pl.pallas_call(
        _gates_kernel,
        grid=(b, nj),
        in_specs=[
            pl.BlockSpec((None, tm, d), lambda i, j: (i, j, 0)),
            pl.BlockSpec((1, d), lambda i, j: (0, 0)),
            pl.BlockSpec((piece, d), lambda i, j: (off_gate // piece, 0)),
            pl.BlockSpec((DN_HEADS, 1), lambda i, j: (0, 0)),
            pl.BlockSpec((DN_HEADS, 1), lambda i, j: (0, 0)),
            pl.BlockSpec((rows_dn, d), lambda i, j: (step(i, j), 0)),
        ] + [da_piece(k) for k in range(n_pieces)],
        out_specs=[
            pl.BlockSpec((None, 3 * DN_HEADS, tm), lambda i, j: (i, 0, j)),
            pl.BlockSpec((None, tm, LANES), lambda i, j: (i, j, 0)),
            pl.BlockSpec((None, tm, d), lambda i, j: (i, j, 0)),
            pl.BlockSpec((rows_dn, d), lambda i, j: (step(i, j), 0)),
            pl.BlockSpec((rows_da, d), lambda i, j: (step(i, j), 0)),
        ],
        out_shape=[
            jax.ShapeDtypeStruct((b, 3 * DN_HEADS, t), F32),
            jax.ShapeDtypeStruct((b, t, LANES), F32),
            jax.ShapeDtypeStruct((b, t, d), BF16),
            jax.ShapeDtypeStruct((off_gate, d), BF16),
            jax.ShapeDtypeStruct((n_da, d), BF16),
        ],
        compiler_params=_params("parallel", "arbitrary"),
        name="dn_gates",
    )(x3d, norm_w, w_in_t, a_log_col, dt_bias_col, w_in_t, *([w_in_t] * n_pieces))


def _dn_intra_kernel(q_ref, k_ref, v_ref, qh_ref, kh_ref, vh_ref, cwq_ref, cwk_ref, cwv_ref,
                     gcol_ref, grow_ref, wo_ref, u_ref, w_ref, qd_ref, kd_ref, a_ref, wo16_ref, xs_ref):
    wo16_ref[...] = wo_ref[...].astype(BF16)
    head = pl.program_id(1)
    first = pl.program_id(2) == 0
    tb = q_ref.shape[0]
    halo = SUBLANES
    for idx, (x_ref, xh_ref) in enumerate(((q_ref, qh_ref), (k_ref, kh_ref), (v_ref, vh_ref))):
        xs_ref[idx, 0:halo, :] = jnp.where(first, 0.0, xh_ref[...])
        xs_ref[idx, halo:halo + tb, :] = x_ref[...]

    lane = lax.broadcasted_iota(jnp.int32, (GROUP, LANES), 1)
    sel_beta = (lane == head).astype(F32)
    sel_gc = (lane == head + DN_HEADS).astype(F32)
    sel_gl = (lane == head + 2 * DN_HEADS).astype(F32)
    ri = lax.broadcasted_iota(jnp.int32, (GROUP, GROUP), 0)
    ci = lax.broadcasted_iota(jnp.int32, (GROUP, GROUP), 1)
    same = (ri // CHUNK) == (ci // CHUNK)
    incl = same & (ri >= ci)
    strict = same & (ri > ci)

    def conv_silu(idx, cw_ref, r0):
        base = r0 + halo - (CONV_WIDTH - 1)
        acc = xs_ref[idx, base:base + GROUP, :] * cw_ref[0:1, :]
        for i in range(1, CONV_WIDTH):
            acc = acc + xs_ref[idx, base + i:base + i + GROUP, :] * cw_ref[i:i + 1, :]
        return acc * _sigmoid(acc)

    n_groups = tb // GROUP
    levels = int(math.log2(CHUNK))
    n_pows = [None] * n_groups
    xs = [None] * n_groups

    def prepare(gi):
        r0 = gi * GROUP
        q = conv_silu(0, cwq_ref, r0)
        k = conv_silu(1, cwk_ref, r0)
        v = conv_silu(2, cwv_ref, r0)
        q = q * lax.rsqrt(jnp.sum(q * q, axis=-1, keepdims=True) + EPS) * (DN_DIM ** -0.5)
        k = k * lax.rsqrt(jnp.sum(k * k, axis=-1, keepdims=True) + EPS)

        gates = gcol_ref[r0:r0 + GROUP, :]
        beta = jnp.sum(gates * sel_beta, axis=-1, keepdims=True)
        gc = jnp.sum(gates * sel_gc, axis=-1, keepdims=True)
        gl = jnp.sum(gates * sel_gl, axis=-1, keepdims=True)
        gc_row = grow_ref[pl.ds(head + DN_HEADS, 1), r0:r0 + GROUP]

        decay = jnp.exp(jnp.where(incl, gc - gc_row, NEG_BIG))
        kb = k * beta
        k16 = k.astype(BF16)
        gram = lax.dot_general(jnp.concatenate([kb, q], axis=0).astype(BF16), k16, _NT,
                               preferred_element_type=F32)
        n_pows[gi] = jnp.where(strict, -(gram[:GROUP] * decay), 0.0)
        a_qk = gram[GROUP:] * decay

        egc = jnp.exp(gc)
        xs[gi] = jnp.concatenate([v * beta, kb * egc], axis=1)
        qd_ref[r0:r0 + GROUP, :] = (q * egc).astype(BF16)
        kd_ref[r0:r0 + GROUP, :] = (k * jnp.exp(gl - gc)).astype(BF16)
        for c in range(GROUP // CHUNK):
            lo = c * CHUNK
            a_ref[r0 + lo:r0 + lo + CHUNK, :] = a_qk[lo:lo + CHUNK, lo:lo + CHUNK].astype(BF16)

    def level(gi, lvl):
        n16 = n_pows[gi].astype(BF16)
        xs[gi] = xs[gi] + jnp.dot(n16, xs[gi].astype(BF16), preferred_element_type=F32)
        if lvl + 1 < levels:
            n_pows[gi] = jnp.dot(n16, n16, preferred_element_type=F32)

    def finish(gi):
        r0 = gi * GROUP
        u_ref[r0:r0 + GROUP, :] = xs[gi][:, :DN_DIM]
        w_ref[r0:r0 + GROUP, :] = xs[gi][:, DN_DIM:].astype(BF16)

    for t in range(n_groups + levels + 1):
        for gi in range(n_groups):
            stage = t - gi
            if stage == 0:
                prepare(gi)
            elif 1 <= stage <= levels:
                level(gi, stage - 1)
            elif stage == levels + 1:
                finish(gi)


def _dn_intra(qkvz, conv_w, gcol, grow, w_out, tb):
    b, t, _ = qkvz.shape
    hb = tb // SUBLANES
    nh = DN_HEADS
    nj = t // tb
    wo_rows, wo_share = _slab_rows(w_out.shape[0], b * nh * nj)
    assert wo_share == 1, "every grid step converts its own slab of w_out"
    wo_spec = pl.BlockSpec((wo_rows, w_out.shape[1]), lambda i, h, j: ((i * nh + h) * nj + j, 0))

    def tok(off):
        return pl.BlockSpec((None, tb, DN_DIM), lambda i, h, j: (i, j, off + h))

    def halo(off):
        return pl.BlockSpec((None, SUBLANES, DN_DIM),
                            lambda i, h, j: (i, jnp.maximum(j * hb - 1, 0), off + h))

    def cw(off):
        return pl.BlockSpec((CONV_WIDTH, DN_DIM), lambda i, h, j: (0, off + h))

    out_tok = pl.BlockSpec((None, tb, DN_DIM), lambda i, h, j: (i, j, h))
    return pl.pallas_call(
        _dn_intra_kernel,
        grid=(b, nh, nj),
        in_specs=[tok(0), tok(nh), tok(2 * nh), halo(0), halo(nh), halo(2 * nh),
                  cw(0), cw(nh), cw(2 * nh),
                  pl.BlockSpec((None, tb, LANES), lambda i, h, j: (i, j, 0)),
                  pl.BlockSpec((None, 3 * nh, tb), lambda i, h, j: (i, 0, j)),
                  wo_spec],
        out_specs=[out_tok, out_tok, out_tok, out_tok,
                   pl.BlockSpec((None, None, tb, CHUNK), lambda i, h, j: (i, h, j, 0)),
                   wo_spec],
        out_shape=[jax.ShapeDtypeStruct((b, t, DN_WIDTH), F32),
                   jax.ShapeDtypeStruct((b, t, DN_WIDTH), BF16),
                   jax.ShapeDtypeStruct((b, t, DN_WIDTH), BF16),
                   jax.ShapeDtypeStruct((b, t, DN_WIDTH), BF16),
                   jax.ShapeDtypeStruct((b, nh, t, CHUNK), BF16),
                   jax.ShapeDtypeStruct(w_out.shape, BF16)],
        scratch_shapes=[pltpu.VMEM((3, tb + SUBLANES, DN_DIM), F32)],
        compiler_params=_params("parallel", "parallel", "arbitrary"),
        name="dn_intra",
    )(qkvz, qkvz, qkvz, qkvz, qkvz, qkvz, conv_w, conv_w, conv_w, gcol, grow, w_out)


def _dn_scan_kernel(u_ref, w_ref, qd_ref, kd_ref, a_ref, z_ref, gcol_ref, nw_ref, o_ref, s_ref):
    @pl.when(pl.program_id(1) == 0)
    def _():
        s_ref[...] = jnp.zeros_like(s_ref)

    bb, tb = u_ref.shape[0], u_ref.shape[1]
    nw = nw_ref[...]
    units = [(i, h) for i in range(bb) for h in range(DN_HEADS)]

    def chunk(c, carry):
        rows = pl.ds(pl.multiple_of(c * CHUNK, CHUNK), CHUNK)
        cols = [slice(h * DN_DIM, (h + 1) * DN_DIM) for h in range(DN_HEADS)]
        s = [s_ref[i, h] for i, h in units]
        proj = [jnp.dot(jnp.concatenate([w_ref[i, rows, cols[h]], qd_ref[i, rows, cols[h]]], axis=0),
                        s[n].astype(BF16), preferred_element_type=F32)
                for n, (i, h) in enumerate(units)]
        v16 = [(u_ref[i, rows, cols[h]] - proj[n][:CHUNK]).astype(BF16)
               for n, (i, h) in enumerate(units)]
        upd = [lax.dot_general(kd_ref[i, rows, cols[h]], v16[n], _TN, preferred_element_type=F32)
               for n, (i, h) in enumerate(units)]
        for n, (i, h) in enumerate(units):
            gl = gcol_ref[i, pl.ds(pl.multiple_of(c * CHUNK, CHUNK), 1),
                          2 * DN_HEADS + h:2 * DN_HEADS + h + 1]
            s_ref[i, h] = s[n] * jnp.exp(gl) + upd[n]
        o = [proj[n][CHUNK:] + jnp.dot(a_ref[i, h, rows, :], v16[n], preferred_element_type=F32)
             for n, (i, h) in enumerate(units)]
        for n, (i, h) in enumerate(units):
            z = z_ref[i, rows, cols[h]]
            o_ref[i, rows, cols[h]] = (_rmsnorm_rows(o[n], nw) * (z * _sigmoid(z))).astype(o_ref.dtype)
        return carry

    lax.fori_loop(0, tb // CHUNK, chunk, 0)


def _dn_scan(u, w, qd, kd, a, qkvz, gcol, dn_norm_w, bb, tb):
    b, t, _ = u.shape
    tok = pl.BlockSpec((bb, tb, DN_WIDTH), lambda i, j: (i, j, 0))
    z_block = 3 * DN_WIDTH // DN_WIDTH
    return pl.pallas_call(
        _dn_scan_kernel,
        grid=(b // bb, t // tb),
        in_specs=[tok, tok, tok, tok,
                  pl.BlockSpec((bb, DN_HEADS, tb, CHUNK), lambda i, j: (i, 0, j, 0)),
                  pl.BlockSpec((bb, tb, DN_WIDTH), lambda i, j: (i, j, z_block)),
                  pl.BlockSpec((bb, tb, LANES), lambda i, j: (i, j, 0)),
                  pl.BlockSpec((1, DN_DIM), lambda i, j: (0, 0))],
        out_specs=tok,
        out_shape=jax.ShapeDtypeStruct((b, t, DN_WIDTH), BF16),
        scratch_shapes=[pltpu.VMEM((bb, DN_HEADS, DN_DIM, DN_DIM), F32)],
        compiler_params=_params("parallel", "arbitrary"),
        name="dn_scan",
    )(u, w, qd, kd, a, qkvz, gcol, dn_norm_w)


def _attn_kernel(q_ref, k_ref, v_ref, lq1_ref, lk1_ref, lq2_ref, lk2_ref, nw_ref,
                 wg_ref, wu_ref, wd_ref, o_ref, wg16_ref, wu16_ref, wd16_ref,
                 m_ref, l_ref, acc_ref, sa_ref, sb_ref, *, lam_init, down_share):
    wg16_ref[...] = wg_ref[...].astype(BF16)
    wu16_ref[...] = wu_ref[...].astype(BF16)
    grid_step = ((pl.program_id(0) * pl.num_programs(1) + pl.program_id(1)) * pl.num_programs(2)
                 + pl.program_id(2))

    @pl.when(grid_step % down_share == 0)
    def _():
        wd16_ref[...] = wd_ref[...].astype(BF16)

    head = pl.program_id(1)
    qi = pl.program_id(2)
    tq = q_ref.shape[0]
    tk = tq
    slope = jnp.float32(0.0)
    for hh in range(DA_HEADS):
        slope = jnp.where(head == hh, jnp.float32(2.0 ** (-8.0 * (hh + 1) / DA_HEADS)), slope)

    m_ref[...] = jnp.full_like(m_ref, NEG_BIG)
    l_ref[...] = jnp.zeros_like(l_ref)
    acc_ref[...] = jnp.zeros_like(acc_ref)
    kpos = lax.broadcasted_iota(jnp.int32, (1, tk), 1)
    slope2 = slope * LOG2E

    maps = range(2)

    def scores(j, s_ref):
        k0 = pl.multiple_of(j * tk, tk)
        bias = slope2 * (kpos + (j - qi) * tk).astype(F32)
        for c in maps:
            cols = slice(c * DA_DIM, (c + 1) * DA_DIM)
            s_ref[c] = lax.dot_general(q_ref[:, cols], k_ref[pl.ds(k0, tk), cols], _NT,
                                       preferred_element_type=F32) + bias

    def accumulate(j, s_ref, masked):
        k0 = pl.multiple_of(j * tk, tk)
        v = v_ref[pl.ds(k0, tk), :]
        s = [s_ref[c] for c in maps]
        if masked:
            rr = lax.broadcasted_iota(jnp.int32, (tq, tk), 0)
            cc = lax.broadcasted_iota(jnp.int32, (tq, tk), 1)
            s = [jnp.where(cc <= rr, s[c], NEG_BIG) for c in maps]
        m_prev = [m_ref[c] for c in maps]
        m_new = [jnp.maximum(m_prev[c], jnp.max(s[c], axis=-1, keepdims=True)) for c in maps]
        p = [jnp.exp2(s[c] - _lane_tile(m_new[c], tk // LANES)) for c in maps]
        p_sum = [jnp.sum(p[c], axis=-1, keepdims=True) for c in maps]
        p16 = [p[c].astype(BF16) for c in maps]
        pv = [jnp.dot(p16[c], v, preferred_element_type=F32) for c in maps]
        for c in maps:
            alpha = jnp.exp2(m_prev[c] - m_new[c])
            l_ref[c] = alpha * l_ref[c] + p_sum[c]
            acc_ref[c] = _lane_tile(alpha, DA_VDIM // LANES) * acc_ref[c] + pv[c]
            m_ref[c] = m_new[c]

    scores(0, sa_ref)

    def pair(jj, carry):
        j = 2 * jj
        scores(j + 1, sb_ref)
        accumulate(j, sa_ref, False)
        scores(j + 2, sa_ref)
        accumulate(j + 1, sb_ref, False)
        return carry

    lax.fori_loop(0, qi // 2, pair, 0)

    @pl.when(qi % 2 == 1)
    def _():
        scores(qi, sb_ref)
        accumulate(qi - 1, sa_ref, False)
        accumulate(qi, sb_ref, True)

    @pl.when(qi % 2 == 0)
    def _():
        accumulate(qi, sa_ref, True)

    lam = (jnp.exp(jnp.sum(lq1_ref[...] * lk1_ref[...], axis=-1, keepdims=True))
           - jnp.exp(jnp.sum(lq2_ref[...] * lk2_ref[...], axis=-1, keepdims=True)) + lam_init)
    rep = DA_VDIM // LANES
    o = (acc_ref[0] / _lane_tile(l_ref[0], rep)
         - lam * (acc_ref[1] / _lane_tile(l_ref[1], rep)))
    o_ref[...] = (_rmsnorm_rows(o, nw_ref[...]) * (1.0 - lam_init)).astype(o_ref.dtype)


def _slab_rows(n_rows, n_steps):
    for share in (1, 2, 4, 8, 16):
        if n_steps % share == 0 and n_rows % (n_steps // share) == 0:
            rows = n_rows // (n_steps // share)
            if rows % 16 == 0:
                return rows, share
    raise ValueError(f"cannot split {n_rows} weight rows over {n_steps} grid steps")


def _attention(da, lam_q1, lam_k1, lam_q2, lam_k2, da_norm_w, w_gate, w_up, w_down, lam_init, tq):
    b, t, _ = da.shape
    nh = DA_HEADS
    nq = t // tq
    d, dff = w_gate.shape
    steps = b * nh * nq
    rg, sg = _slab_rows(d, steps)
    rd, sd = _slab_rows(dff, steps)

    def step(i, h, j):
        return (i * nh + h) * nq + j

    vec = pl.BlockSpec((1, DA_DIM), lambda i, h, j: (0, 0))
    up_spec = pl.BlockSpec((rg, dff), lambda i, h, j: (step(i, h, j) // sg, 0))
    down_spec = pl.BlockSpec((rd, d), lambda i, h, j: (step(i, h, j) // sd, 0))
    return pl.pallas_call(
        functools.partial(_attn_kernel, lam_init=lam_init, down_share=sd),
        grid=(b, nh, nq),
        in_specs=[pl.BlockSpec((None, tq, DA_VDIM), lambda i, h, j: (i, j, h)),
                  pl.BlockSpec((None, t, DA_VDIM), lambda i, h, j: (i, 0, nh + h)),
                  pl.BlockSpec((None, t, DA_VDIM), lambda i, h, j: (i, 0, 2 * nh + h)),
                  vec, vec, vec, vec,
                  pl.BlockSpec((1, DA_VDIM), lambda i, h, j: (0, 0)),
                  up_spec, up_spec, down_spec],
        out_specs=[pl.BlockSpec((None, tq, DA_VDIM), lambda i, h, j: (i, j, h)),
                   up_spec, up_spec, down_spec],
        out_shape=[jax.ShapeDtypeStruct((b, t, DA_WIDTH), BF16),
                   jax.ShapeDtypeStruct((d, dff), BF16),
                   jax.ShapeDtypeStruct((d, dff), BF16),
                   jax.ShapeDtypeStruct((dff, d), BF16)],
        scratch_shapes=[pltpu.VMEM((2, tq, LANES), F32), pltpu.VMEM((2, tq, LANES), F32),
                        pltpu.VMEM((2, tq, DA_VDIM), F32),
                        pltpu.VMEM((2, tq, tq), F32), pltpu.VMEM((2, tq, tq), F32)],
        compiler_params=_params("arbitrary", "arbitrary", "arbitrary"),
        name="diff_attn",
    )(da, da, da, lam_q1, lam_k1, lam_q2, lam_k2, da_norm_w, w_gate, w_up, w_down)


def _out_proj_kernel(a_ref, b_ref, wa_ref, wb_ref, x_ref, o_ref):
    acc = jnp.dot(a_ref[...], wa_ref[...], preferred_element_type=F32)
    acc = acc + jnp.dot(b_ref[...], wb_ref[...], preferred_element_type=F32)
    o_ref[...] = x_ref[...] + acc


def _out_proj(o_dn, o_da, w16, x2d, tm, tn):
    m, d = x2d.shape
    ka, kb = o_dn.shape[1], o_da.shape[1]
    assert ka == kb and w16.shape[0] == ka + kb, "the two head groups are equal row halves of w_out"
    return pl.pallas_call(
        _out_proj_kernel,
        grid=(m // tm, d // tn),
        in_specs=[pl.BlockSpec((tm, ka), lambda i, j: (i, 0)),
                  pl.BlockSpec((tm, kb), lambda i, j: (i, 0)),
                  pl.BlockSpec((ka, tn), lambda i, j: (0, j)),
                  pl.BlockSpec((kb, tn), lambda i, j: (1, j)),
                  pl.BlockSpec((tm, tn), lambda i, j: (i, j))],
        out_specs=pl.BlockSpec((tm, tn), lambda i, j: (i, j)),
        out_shape=jax.ShapeDtypeStruct((m, d), F32),
        compiler_params=_params("parallel", "arbitrary"),
        name="out_proj",
    )(o_dn, o_da, w16, w16, x2d)


def _ffn_kernel(x_ref, nw_ref, wg_ref, wu_ref, wd_ref, fw_ref, o_ref, h_ref):
    f = pl.program_id(1)

    @pl.when(f == 0)
    def _():
        x = x_ref[...]
        h_ref[...] = _rmsnorm_rows(x, nw_ref[...]).astype(BF16)
        o_ref[...] = x

    h = h_ref[...]
    g = jnp.dot(h, wg_ref[...], preferred_element_type=F32)
    u = jnp.dot(h, wu_ref[...], preferred_element_type=F32)
    act = (g * _sigmoid(g) * u).astype(BF16)
    o_ref[...] += jnp.dot(act, wd_ref[...], preferred_element_type=F32)

    @pl.when(f == pl.num_programs(1) - 1)
    def _():
        o_ref[...] = _rmsnorm_rows(o_ref[...], fw_ref[...])


def _ffn(x2d, ffn_norm_w, w_gate, w_up, w_down, final_norm_w, tm, tf):
    m, d = x2d.shape
    dff = w_gate.shape[1]
    return pl.pallas_call(
        _ffn_kernel,
        grid=(m // tm, dff // tf),
        in_specs=[pl.BlockSpec((tm, d), lambda i, f: (i, 0)),
                  pl.BlockSpec((1, d), lambda i, f: (0, 0)),
                  pl.BlockSpec((d, tf), lambda i, f: (0, f)),
                  pl.BlockSpec((d, tf), lambda i, f: (0, f)),
                  pl.BlockSpec((tf, d), lambda i, f: (f, 0)),
                  pl.BlockSpec((1, d), lambda i, f: (0, 0))],
        out_specs=pl.BlockSpec((tm, d), lambda i, f: (i, 0)),
        out_shape=jax.ShapeDtypeStruct((m, d), F32),
        scratch_shapes=[pltpu.VMEM((tm, d), BF16)],
        compiler_params=_params("parallel", "arbitrary"),
        name="ffn",
    )(x2d, ffn_norm_w, w_gate, w_up, w_down, final_norm_w)


def _tile(n, pref):
    return pref if n % pref == 0 else n


def _layer(x, lam_init, attn_norm_w, w_in, conv_w, a_log, dt_bias, dn_norm_w, lam_q1, lam_k1,
           lam_q2, lam_k2, da_norm_w, w_out, ffn_norm_w, w_gate, w_up, w_down, final_norm_w):
    b, t, d = x.shape
    m = b * t
    x2d = x.reshape(m, d)
    off_z = 3 * DN_WIDTH
    off_b = off_z + DN_WIDTH
    off_q = off_b + 2 * DN_HEADS
    nw = attn_norm_w.reshape(1, d)

    n_da = w_in.shape[1] - off_q
    da_scale = jnp.concatenate([jnp.full((1, DA_HEADS * 2 * DA_DIM), LOG2E * DA_DIM ** -0.5, F32),
                                jnp.ones((1, n_da - DA_HEADS * 2 * DA_DIM), F32)], axis=1)

    w_in_t = w_in.T
    grow, gcol, h, w_dn, w_da = _gates(x, nw, w_in_t, a_log.reshape(DN_HEADS, 1),
                                       dt_bias.reshape(DN_HEADS, 1), off_b, _tile(t, 256))
    h2d = h.reshape(m, d)
    tp = _tile(m, 2048)
    qkvz = _proj(h2d, w_dn, jnp.ones((1, off_b), F32), F32, tp, 1024, "in_proj_dn")
    da = _proj(h2d, w_da, da_scale, BF16, tp, 1024, "in_proj_da")
    tm = _tile(m, 1024)

    qkvz = qkvz.reshape(b, t, off_b)
    u, w, qd, kd, a, w_out16 = _dn_intra(qkvz, conv_w, gcol, grow, w_out, _tile(t, 4096))
    o_dn = _dn_scan(u, w, qd, kd, a, qkvz, gcol, dn_norm_w.reshape(1, DN_DIM),
                    4 if b % 4 == 0 else 1, _tile(t, 128))

    o_da, w_gate16, w_up16, w_down16 = _attention(
        da.reshape(b, t, n_da), lam_q1.reshape(1, DA_DIM), lam_k1.reshape(1, DA_DIM),
        lam_q2.reshape(1, DA_DIM), lam_k2.reshape(1, DA_DIM), da_norm_w.reshape(1, DA_VDIM),
        w_gate, w_up, w_down, lam_init, _tile(t, 512))

    x2d = _out_proj(o_dn.reshape(m, DN_WIDTH), o_da.reshape(m, DA_WIDTH), w_out16, x2d,
                    _tile(m, 512), d)
    dff = w_gate.shape[1]
    return _ffn(x2d, ffn_norm_w.reshape(1, d), w_gate16, w_up16, w_down16,
                final_norm_w.reshape(1, d), tm, _tile(dff, 512))


def kernel(x, attn_norm_w, w_in, conv_w, a_log, dt_bias, dn_norm_w, lam_q1, lam_k1, lam_q2, lam_k2,
           da_norm_w, w_out, ffn_norm_w, w_gate, w_up, w_down, final_norm_w):
    b, t, d = x.shape
    depth = w_in.shape[0]
    assert depth == 1, "the final rmsnorm is fused into the last layer's FFN kernel"
    lam_init = 0.8 - 0.6 * math.exp(-0.3 * 0)
    out = _layer(x, lam_init, attn_norm_w[0], w_in[0], conv_w[0], a_log[0], dt_bias[0],
                 dn_norm_w[0], lam_q1[0], lam_k1[0], lam_q2[0], lam_k2[0], da_norm_w[0],
                 w_out[0], ffn_norm_w[0], w_gate[0], w_up[0], w_down[0], final_norm_w)
    return out.reshape(b, t, d)
```

```python
import functools
import math

import jax
import jax.numpy as jnp
from jax import lax
from jax.experimental import pallas as pl
from jax.experimental.pallas import tpu as pltpu

F32 = jnp.float32
BF16 = jnp.bfloat16
EPS = 1e-6
NEG_BIG = -1e30
LOG2E = math.log2(math.e)

DN_HEADS = 8
DN_DIM = 128
DN_WIDTH = DN_HEADS * DN_DIM
CONV_WIDTH = 4
CHUNK = 64
DA_HEADS = 4
DA_DIM = 128
DA_VDIM = 2 * DA_DIM
DA_WIDTH = DA_HEADS * DA_VDIM
LANES = 128
SUBLANES = 8
GROUP = 256
VMEM_LIMIT = 56 * 1024 * 1024

_NT = (((1,), (1,)), ((), ()))
_TN = (((0,), (0,)), ((), ()))


def _params(*sem):
    return pltpu.CompilerParams(dimension_semantics=sem, vmem_limit_bytes=VMEM_LIMIT)


def _sigmoid(x):
    return 1.0 / (1.0 + jnp.exp(-x))


def _silu(x):
    h = 0.5 * x
    return h + h * jnp.tanh(h)


def _lane_tile(x, n):
    return jnp.concatenate([x] * n, axis=1)


def _rmsnorm_rows(x, w):
    return x * lax.rsqrt(jnp.mean(x * x, axis=-1, keepdims=True) + EPS) * w


def _proj_kernel(h_ref, w_ref, cs_ref, o_ref):
    acc = lax.dot_general(h_ref[...], w_ref[...], _NT, preferred_element_type=F32)
    o_ref[...] = (acc * cs_ref[...]).astype(o_ref.dtype)


def _proj(h2d, w_t_bf16, col_scale, out_dtype, tm, tn, name):
    m, d = h2d.shape
    n = w_t_bf16.shape[0]
    return pl.pallas_call(
        _proj_kernel,
        grid=(m // tm, n // tn),
        in_specs=[
            pl.BlockSpec((tm, d), lambda i, j: (i, 0)),
            pl.BlockSpec((tn, d), lambda i, j: (j, 0)),
            pl.BlockSpec((1, tn), lambda i, j: (0, j)),
        ],
        out_specs=pl.BlockSpec((tm, tn), lambda i, j: (i, j)),
        out_shape=jax.ShapeDtypeStruct((m, n), out_dtype),
        compiler_params=_params("parallel", "arbitrary"),
        name=name,
    )(h2d, w_t_bf16, col_scale)


def _gates_kernel(x_ref, nw_ref, wt_ref, alog_ref, dtb_ref, wdn_ref, *rest):
    wda_refs, (row_ref, col_ref, h_ref, wdn16_ref, wda16_ref) = rest[:-5], rest[-5:]
    wdn16_ref[...] = wdn_ref[...].astype(BF16)
    wda16_ref[...] = jnp.concatenate([r[...] for r in wda_refs], axis=0).astype(BF16)

    h = _rmsnorm_rows(x_ref[...], nw_ref[...]).astype(BF16)
    h_ref[...] = h
    r = lax.dot_general(wt_ref[...].astype(BF16), h, _NT, preferred_element_type=F32)
    beta = _sigmoid(r[0:DN_HEADS])
    a = r[DN_HEADS:2 * DN_HEADS] + dtb_ref[...]
    softplus = jnp.maximum(a, 0.0) + jnp.log1p(jnp.exp(-jnp.abs(a)))
    g = -jnp.exp(alog_ref[...]) * softplus
    tm = g.shape[1]
    ii = lax.broadcasted_iota(jnp.int32, (tm, tm), 0)
    jj = lax.broadcasted_iota(jnp.int32, (tm, tm), 1)
    same = (ii // CHUNK) == (jj // CHUNK)
    sums = jnp.concatenate([same & (ii <= jj), same], axis=1).astype(BF16)
    g1 = g.astype(BF16)
    r1 = g - g1.astype(F32)
    g2 = r1.astype(BF16)
    g3 = (r1 - g2.astype(F32)).astype(BF16)
    pieces = jnp.concatenate([g1, g2, g3, jnp.zeros_like(g1)], axis=0)
    acc = jnp.dot(pieces, sums, preferred_element_type=F32)
    nh = DN_HEADS
    acc = acc[0:nh] + acc[nh:2 * nh] + acc[2 * nh:3 * nh]
    gc, gl = acc[:, :tm], acc[:, tm:]
    rows = jnp.concatenate([beta, gc, gl], axis=0)
    row_ref[...] = rows
    padded = jnp.concatenate([rows, jnp.zeros((LANES - 3 * DN_HEADS, tm), F32)], axis=0)
    col_ref[...] = padded.T


def _gates(x3d, norm_w, w_in_t, a_log_col, dt_bias_col, off_gate, tm):
    b, t, d = x3d.shape
    nj = t // tm
    steps = b * nj
    piece = 2 * DN_HEADS
    off_da = off_gate + piece
    n_da = w_in_t.shape[0] - off_da
    rows_dn, rows_da = off_gate // steps, n_da // steps
    assert rows_dn * steps == off_gate and rows_dn % piece == 0
    assert rows_da * steps == n_da and rows_da % piece == 0 and off_da % piece == 0
    n_pieces = rows_da // piece

    def step(i, j):
        return i * nj + j

    def da_piece(k):
        return pl.BlockSpec((piece, d),
                            lambda i, j: (off_da // piece + step(i, j) * n_pieces + k, 0))

    return pl.pallas_call(
        _gates_kernel,
        grid=(b, nj),
        in_specs=[
            pl.BlockSpec((None, tm, d), lambda i, j: (i, j, 0)),
            pl.BlockSpec((1, d), lambda i, j: (0, 0)),
            pl.BlockSpec((piece, d), lambda i, j: (off_gate // piece, 0)),
            pl.BlockSpec((DN_HEADS, 1), lambda i, j: (0, 0)),
            pl.BlockSpec((DN_HEADS, 1), lambda i, j: (0, 0)),
            pl.BlockSpec((rows_dn, d), lambda i, j: (step(i, j), 0)),
        ] + [da_piece(k) for k in range(n_pieces)],
        out_specs=[
            pl.BlockSpec((None, 3 * DN_HEADS, tm), lambda i, j: (i, 0, j)),
            pl.BlockSpec((None, tm, LANES), lambda i, j: (i, j, 0)),
            pl.BlockSpec((None, tm, d), lambda i, j: (i, j, 0)),
            pl.BlockSpec((rows_dn, d), lambda i, j: (step(i, j), 0)),
            pl.BlockSpec((rows_da, d), lambda i, j: (step(i, j), 0)),
        ],
        out_shape=[
            jax.ShapeDtypeStruct((b, 3 * DN_HEADS, t), F32),
            jax.ShapeDtypeStruct((b, t, LANES), F32),
            jax.ShapeDtypeStruct((b, t, d), BF16),
            jax.ShapeDtypeStruct((off_gate, d), BF16),
            jax.ShapeDtypeStruct((n_da, d), BF16),
        ],
        compiler_params=_params("parallel", "arbitrary"),
        name="dn_gates",
    )(x3d, norm_w, w_in_t, a_log_col, dt_bias_col, w_in_t, *([w_in_t] * n_pieces))


def _dn_intra_kernel(q_ref, k_ref, v_ref, qh_ref, kh_ref, vh_ref, cwq_ref, cwk_ref, cwv_ref,
                     gcol_ref, grow_ref, wo_ref, u_ref, w_ref, qd_ref, kd_ref, a_ref, wo16_ref, xs_ref):
    wo16_ref[...] = wo_ref[...].astype(BF16)
    head = pl.program_id(1)
    first = pl.program_id(2) == 0
    tb = q_ref.shape[0]
    halo = SUBLANES
    for idx, (x_ref, xh_ref) in enumerate(((q_ref, qh_ref), (k_ref, kh_ref), (v_ref, vh_ref))):
        xs_ref[idx, 0:halo, :] = jnp.where(first, 0.0, xh_ref[...])
        xs_ref[idx, halo:halo + tb, :] = x_ref[...]

    lane = lax.broadcasted_iota(jnp.int32, (GROUP, LANES), 1)
    sel_beta = (lane == head).astype(F32)
    sel_gc = (lane == head + DN_HEADS).astype(F32)
    sel_gl = (lane == head + 2 * DN_HEADS).astype(F32)
    ri = lax.broadcasted_iota(jnp.int32, (GROUP, GROUP), 0)
    ci = lax.broadcasted_iota(jnp.int32, (GROUP, GROUP), 1)
    same = (ri // CHUNK) == (ci // CHUNK)
    incl = same & (ri >= ci)
    strict = same & (ri > ci)

    def conv_silu(idx, cw_ref, r0):
        base = r0 + halo - (CONV_WIDTH - 1)
        acc = xs_ref[idx, base:base + GROUP, :] * cw_ref[0:1, :]
        for i in range(1, CONV_WIDTH):
            acc = acc + xs_ref[idx, base + i:base + i + GROUP, :] * cw_ref[i:i + 1, :]
        return _silu(acc)

    n_groups = tb // GROUP
    levels = int(math.log2(CHUNK))
    n_pows = [None] * n_groups
    xs = [None] * n_groups

    def prepare(gi):
        r0 = gi * GROUP
        q = conv_silu(0, cwq_ref, r0)
        k = conv_silu(1, cwk_ref, r0)
        v = conv_silu(2, cwv_ref, r0)
        q = q * lax.rsqrt(jnp.sum(q * q, axis=-1, keepdims=True) + EPS) * (DN_DIM ** -0.5)
        k = k * lax.rsqrt(jnp.sum(k * k, axis=-1, keepdims=True) + EPS)

        gates = gcol_ref[r0:r0 + GROUP, :]
        beta = jnp.sum(gates * sel_beta, axis=-1, keepdims=True)
        gc = jnp.sum(gates * sel_gc, axis=-1, keepdims=True)
        gl = jnp.sum(gates * sel_gl, axis=-1, keepdims=True)
        gc_row = grow_ref[pl.ds(head + DN_HEADS, 1), r0:r0 + GROUP]

        decay = jnp.exp(jnp.where(incl, gc - gc_row, NEG_BIG))
        kb = k * beta
        k16 = k.astype(BF16)
        gram = lax.dot_general(jnp.concatenate([kb, q], axis=0).astype(BF16), k16, _NT,
                               preferred_element_type=F32)
        n_pows[gi] = jnp.where(strict, -(gram[:GROUP] * decay), 0.0)
        a_qk = gram[GROUP:] * decay

        egc = jnp.exp(gc)
        xs[gi] = jnp.concatenate([v * beta, kb * egc], axis=1)
        qd_ref[r0:r0 + GROUP, :] = (q * egc).astype(BF16)
        kd_ref[r0:r0 + GROUP, :] = (k * jnp.exp(gl - gc)).astype(BF16)
        for c in range(GROUP // CHUNK):
            lo = c * CHUNK
            a_ref[r0 + lo:r0 + lo + CHUNK, :] = a_qk[lo:lo + CHUNK, lo:lo + CHUNK].astype(BF16)

    def level(gi, lvl):
        n16 = n_pows[gi].astype(BF16)
        xs[gi] = xs[gi] + jnp.dot(n16, xs[gi].astype(BF16), preferred_element_type=F32)
        if lvl + 1 < levels:
            n_pows[gi] = jnp.dot(n16, n16, preferred_element_type=F32)

    def finish(gi):
        r0 = gi * GROUP
        u_ref[r0:r0 + GROUP, :] = xs[gi][:, :DN_DIM]
        w_ref[r0:r0 + GROUP, :] = xs[gi][:, DN_DIM:].astype(BF16)

    for t in range(n_groups + levels + 1):
        for gi in range(n_groups):
            stage = t - gi
            if stage == 0:
                prepare(gi)
            elif 1 <= stage <= levels:
                level(gi, stage - 1)
            elif stage == levels + 1:
                finish(gi)


def _dn_intra(qkvz, conv_w, gcol, grow, w_out, tb):
    b, t, _ = qkvz.shape
    hb = tb // SUBLANES
    nh = DN_HEADS
    nj = t // tb
    wo_rows, wo_share = _slab_rows(w_out.shape[0], b * nh * nj)
    assert wo_share == 1, "every grid step converts its own slab of w_out"
    wo_spec = pl.BlockSpec((wo_rows, w_out.shape[1]), lambda i, h, j: ((i * nh + h) * nj + j, 0))

    def tok(off):
        return pl.BlockSpec((None, tb, DN_DIM), lambda i, h, j: (i, j, off + h))

    def halo(off):
        return pl.BlockSpec((None, SUBLANES, DN_DIM),
                            lambda i, h, j: (i, jnp.maximum(j * hb - 1, 0), off + h))

    def cw(off):
        return pl.BlockSpec((CONV_WIDTH, DN_DIM), lambda i, h, j: (0, off + h))

    out_tok = pl.BlockSpec((None, tb, DN_DIM), lambda i, h, j: (i, j, h))
    return pl.pallas_call(
        _dn_intra_kernel,
        grid=(b, nh, nj),
        in_specs=[tok(0), tok(nh), tok(2 * nh), halo(0), halo(nh), halo(2 * nh),
                  cw(0), cw(nh), cw(2 * nh),
                  pl.BlockSpec((None, tb, LANES), lambda i, h, j: (i, j, 0)),
                  pl.BlockSpec((None, 3 * nh, tb), lambda i, h, j: (i, 0, j)),
                  wo_spec],
        out_specs=[out_tok, out_tok, out_tok, out_tok,
                   pl.BlockSpec((None, None, tb, CHUNK), lambda i, h, j: (i, h, j, 0)),
                   wo_spec],
        out_shape=[jax.ShapeDtypeStruct((b, t, DN_WIDTH), F32),
                   jax.ShapeDtypeStruct((b, t, DN_WIDTH), BF16),
                   jax.ShapeDtypeStruct((b, t, DN_WIDTH), BF16),
                   jax.ShapeDtypeStruct((b, t, DN_WIDTH), BF16),
                   jax.ShapeDtypeStruct((b, nh, t, CHUNK), BF16),
                   jax.ShapeDtypeStruct(w_out.shape, BF16)],
        scratch_shapes=[pltpu.VMEM((3, tb + SUBLANES, DN_DIM), F32)],
        compiler_params=_params("parallel", "parallel", "arbitrary"),
        name="dn_intra",
    )(qkvz, qkvz, qkvz, qkvz, qkvz, qkvz, conv_w, conv_w, conv_w, gcol, grow, w_out)


def _dn_scan_kernel(u_ref, w_ref, qd_ref, kd_ref, a_ref, z_ref, gcol_ref, nw_ref, o_ref, s_ref):
    @pl.when(pl.program_id(1) == 0)
    def _():
        s_ref[...] = jnp.zeros_like(s_ref)

    bb, tb = u_ref.shape[0], u_ref.shape[1]
    nw = nw_ref[...]
    units = [(i, h) for i in range(bb) for h in range(DN_HEADS)]

    def chunk(c, carry):
        rows = pl.ds(pl.multiple_of(c * CHUNK, CHUNK), CHUNK)
        cols = [slice(h * DN_DIM, (h + 1) * DN_DIM) for h in range(DN_HEADS)]
        s = [s_ref[i, h] for i, h in units]
        proj = [jnp.dot(jnp.concatenate([w_ref[i, rows, cols[h]], qd_ref[i, rows, cols[h]]], axis=0),
                        s[n].astype(BF16), preferred_element_type=F32)
                for n, (i, h) in enumerate(units)]
        v16 = [(u_ref[i, rows, cols[h]] - proj[n][:CHUNK]).astype(BF16)
               for n, (i, h) in enumerate(units)]
        upd = [lax.dot_general(kd_ref[i, rows, cols[h]], v16[n], _TN, preferred_element_type=F32)
               for n, (i, h) in enumerate(units)]
        for n, (i, h) in enumerate(units):
            gl = gcol_ref[i, pl.ds(pl.multiple_of(c * CHUNK, CHUNK), 1),
                          2 * DN_HEADS + h:2 * DN_HEADS + h + 1]
            s_ref[i, h] = s[n] * jnp.exp(gl) + upd[n]
        o = [proj[n][CHUNK:] + jnp.dot(a_ref[i, h, rows, :], v16[n], preferred_element_type=F32)
             for n, (i, h) in enumerate(units)]
        for n, (i, h) in enumerate(units):
            z = z_ref[i, rows, cols[h]]
            o_ref[i, rows, cols[h]] = (_rmsnorm_rows(o[n], nw) * _silu(z)).astype(o_ref.dtype)
        return carry

    lax.fori_loop(0, tb // CHUNK, chunk, 0)


def _dn_scan(u, w, qd, kd, a, qkvz, gcol, dn_norm_w, bb, tb):
    b, t, _ = u.shape
    tok = pl.BlockSpec((bb, tb, DN_WIDTH), lambda i, j: (i, j, 0))
    z_block = 3 * DN_WIDTH // DN_WIDTH
    return pl.pallas_call(
        _dn_scan_kernel,
        grid=(b // bb, t // tb),
        in_specs=[tok, tok, tok, tok,
                  pl.BlockSpec((bb, DN_HEADS, tb, CHUNK), lambda i, j: (i, 0, j, 0)),
                  pl.BlockSpec((bb, tb, DN_WIDTH), lambda i, j: (i, j, z_block)),
                  pl.BlockSpec((bb, tb, LANES), lambda i, j: (i, j, 0)),
                  pl.BlockSpec((1, DN_DIM), lambda i, j: (0, 0))],
        out_specs=tok,
        out_shape=jax.ShapeDtypeStruct((b, t, DN_WIDTH), BF16),
        scratch_shapes=[pltpu.VMEM((bb, DN_HEADS, DN_DIM, DN_DIM), F32)],
        compiler_params=_params("parallel", "arbitrary"),
        name="dn_scan",
    )(u, w, qd, kd, a, qkvz, gcol, dn_norm_w)


def _attn_kernel(q_ref, k_ref, v_ref, lq1_ref, lk1_ref, lq2_ref, lk2_ref, nw_ref,
                 wg_ref, wu_ref, wd_ref, o_ref, wg16_ref, wu16_ref, wd16_ref,
                 m_ref, l_ref, acc_ref, sa_ref, sb_ref, *, lam_init, down_share):
    wg16_ref[...] = wg_ref[...].astype(BF16)
    wu16_ref[...] = wu_ref[...].astype(BF16)
    grid_step = ((pl.program_id(0) * pl.num_programs(1) + pl.program_id(1)) * pl.num_programs(2)
                 + pl.program_id(2))

    @pl.when(grid_step % down_share == 0)
    def _():
        wd16_ref[...] = wd_ref[...].astype(BF16)

    head = pl.program_id(1)
    qi = pl.program_id(2)
    tq = q_ref.shape[0]
    tk = tq
    slope = jnp.float32(0.0)
    for hh in range(DA_HEADS):
        slope = jnp.where(head == hh, jnp.float32(2.0 ** (-8.0 * (hh + 1) / DA_HEADS)), slope)

    m_ref[...] = jnp.full_like(m_ref, NEG_BIG)
    l_ref[...] = jnp.zeros_like(l_ref)
    acc_ref[...] = jnp.zeros_like(acc_ref)
    kpos = lax.broadcasted_iota(jnp.int32, (1, tk), 1)
    slope2 = slope * LOG2E

    maps = range(2)

    def scores(j, s_ref):
        k0 = pl.multiple_of(j * tk, tk)
        bias = slope2 * (kpos + (j - qi) * tk).astype(F32)
        for c in maps:
            cols = slice(c * DA_DIM, (c + 1) * DA_DIM)
            s_ref[c] = lax.dot_general(q_ref[:, cols], k_ref[pl.ds(k0, tk), cols], _NT,
                                       preferred_element_type=F32) + bias

    def accumulate(j, s_ref, masked):
        k0 = pl.multiple_of(j * tk, tk)
        v = v_ref[pl.ds(k0, tk), :]
        s = [s_ref[c] for c in maps]
        if masked:
            rr = lax.broadcasted_iota(jnp.int32, (tq, tk), 0)
            cc = lax.broadcasted_iota(jnp.int32, (tq, tk), 1)
            s = [jnp.where(cc <= rr, s[c], NEG_BIG) for c in maps]
        m_prev = [m_ref[c] for c in maps]
        m_new = [jnp.maximum(m_prev[c], jnp.max(s[c], axis=-1, keepdims=True)) for c in maps]
        p = [jnp.exp2(s[c] - _lane_tile(m_new[c], tk // LANES)) for c in maps]
        p_sum = [jnp.sum(p[c], axis=-1, keepdims=True) for c in maps]
        p16 = [p[c].astype(BF16) for c in maps]
        pv = [jnp.dot(p16[c], v, preferred_element_type=F32) for c in maps]
        for c in maps:
            alpha = jnp.exp2(m_prev[c] - m_new[c])
            l_ref[c] = alpha * l_ref[c] + p_sum[c]
            acc_ref[c] = _lane_tile(alpha, DA_VDIM // LANES) * acc_ref[c] + pv[c]
            m_ref[c] = m_new[c]

    scores(0, sa_ref)

    def pair(jj, carry):
        j = 2 * jj
        scores(j + 1, sb_ref)
        accumulate(j, sa_ref, False)
        scores(j + 2, sa_ref)
        accumulate(j + 1, sb_ref, False)
        return carry

    lax.fori_loop(0, qi // 2, pair, 0)

    @pl.when(qi % 2 == 1)
    def _():
        scores(qi, sb_ref)
        accumulate(qi - 1, sa_ref, False)
        accumulate(qi, sb_ref, True)

    @pl.when(qi % 2 == 0)
    def _():
        accumulate(qi, sa_ref, True)

    lam = (jnp.exp(jnp.sum(lq1_ref[...] * lk1_ref[...], axis=-1, keepdims=True))
           - jnp.exp(jnp.sum(lq2_ref[...] * lk2_ref[...], axis=-1, keepdims=True)) + lam_init)
    rep = DA_VDIM // LANES
    inv_l0 = 1.0 / l_ref[0]
    inv_l1 = lam / l_ref[1]
    o = acc_ref[0] * _lane_tile(inv_l0, rep) - acc_ref[1] * _lane_tile(inv_l1, rep)
    o_ref[...] = (_rmsnorm_rows(o, nw_ref[...]) * (1.0 - lam_init)).astype(o_ref.dtype)


def _slab_rows(n_rows, n_steps):
    for share in (1, 2, 4, 8, 16):
        if n_steps % share == 0 and n_rows % (n_steps // share) == 0:
            rows = n_rows // (n_steps // share)
            if rows % 16 == 0:
                return rows, share
    raise ValueError(f"cannot split {n_rows} weight rows over {n_steps} grid steps")


def _attention(da, lam_q1, lam_k1, lam_q2, lam_k2, da_norm_w, w_gate, w_up, w_down, lam_init, tq):
    b, t, _ = da.shape
    nh = DA_HEADS
    nq = t // tq
    d, dff = w_gate.shape
    steps = b * nh * nq
    rg, sg = _slab_rows(d, steps)
    rd, sd = _slab_rows(dff, steps)

    def step(i, h, j):
        return (i * nh + h) * nq + j

    vec = pl.BlockSpec((1, DA_DIM), lambda i, h, j: (0, 0))
    up_spec = pl.BlockSpec((rg, dff), lambda i, h, j: (step(i, h, j) // sg, 0))
    down_spec = pl.BlockSpec((rd, d), lambda i, h, j: (step(i, h, j) // sd, 0))
    return pl.pallas_call(
        functools.partial(_attn_kernel, lam_init=lam_init, down_share=sd),
        grid=(b, nh, nq),
        in_specs=[pl.BlockSpec((None, tq, DA_VDIM), lambda i, h, j: (i, j, h)),
                  pl.BlockSpec((None, t, DA_VDIM), lambda i, h, j: (i, 0, nh + h)),
                  pl.BlockSpec((None, t, DA_VDIM), lambda i, h, j: (i, 0, 2 * nh + h)),
                  vec, vec, vec, vec,
                  pl.BlockSpec((1, DA_VDIM), lambda i, h, j: (0, 0)),
                  up_spec, up_spec, down_spec],
        out_specs=[pl.BlockSpec((None, tq, DA_VDIM), lambda i, h, j: (i, j, h)),
                   up_spec, up_spec, down_spec],
        out_shape=[jax.ShapeDtypeStruct((b, t, DA_WIDTH), BF16),
                   jax.ShapeDtypeStruct((d, dff), BF16),
                   jax.ShapeDtypeStruct((d, dff), BF16),
                   jax.ShapeDtypeStruct((dff, d), BF16)],
        scratch_shapes=[pltpu.VMEM((2, tq, LANES), F32), pltpu.VMEM((2, tq, LANES), F32),
                        pltpu.VMEM((2, tq, DA_VDIM), F32),
                        pltpu.VMEM((2, tq, tq), F32), pltpu.VMEM((2, tq, tq), F32)],
        compiler_params=_params("arbitrary", "arbitrary", "arbitrary"),
        name="diff_attn",
    )(da, da, da, lam_q1, lam_k1, lam_q2, lam_k2, da_norm_w, w_gate, w_up, w_down)


def _out_proj_kernel(a_ref, b_ref, wa_ref, wb_ref, x_ref, o_ref):
    acc = jnp.dot(a_ref[...], wa_ref[...], preferred_element_type=F32)
    acc = acc + jnp.dot(b_ref[...], wb_ref[...], preferred_element_type=F32)
    o_ref[...] = x_ref[...] + acc


def _out_proj(o_dn, o_da, w16, x2d, tm, tn):
    m, d = x2d.shape
    ka, kb = o_dn.shape[1], o_da.shape[1]
    assert ka == kb and w16.shape[0] == ka + kb, "the two head groups are equal row halves of w_out"
    return pl.pallas_call(
        _out_proj_kernel,
        grid=(m // tm, d // tn),
        in_specs=[pl.BlockSpec((tm, ka), lambda i, j: (i, 0)),
                  pl.BlockSpec((tm, kb), lambda i, j: (i, 0)),
                  pl.BlockSpec((ka, tn), lambda i, j: (0, j)),
                  pl.BlockSpec((kb, tn), lambda i, j: (1, j)),
                  pl.BlockSpec((tm, tn), lambda i, j: (i, j))],
        out_specs=pl.BlockSpec((tm, tn), lambda i, j: (i, j)),
        out_shape=jax.ShapeDtypeStruct((m, d), F32),
        compiler_params=_params("parallel", "arbitrary"),
        name="out_proj",
    )(o_dn, o_da, w16, w16, x2d)


def _ffn_kernel(x_ref, nw_ref, wg_ref, wu_ref, wd_ref, fw_ref, o_ref, h_ref):
    f = pl.program_id(1)

    @pl.when(f == 0)
    def _():
        x = x_ref[...]
        h_ref[...] = _rmsnorm_rows(x, nw_ref[...]).astype(BF16)
        o_ref[...] = x

    h = h_ref[...]
    g = jnp.dot(h, wg_ref[...], preferred_element_type=F32)
    u = jnp.dot(h, wu_ref[...], preferred_element_type=F32)
    act = (_silu(g) * u).astype(BF16)
    o_ref[...] += jnp.dot(act, wd_ref[...], preferred_element_type=F32)

    @pl.when(f == pl.num_programs(1) - 1)
    def _():
        o_ref[...] = _rmsnorm_rows(o_ref[...], fw_ref[...])


def _ffn(x2d, ffn_norm_w, w_gate, w_up, w_down, final_norm_w, tm, tf):
    m, d = x2d.shape
    dff = w_gate.shape[1]
    return pl.pallas_call(
        _ffn_kernel,
        grid=(m // tm, dff // tf),
        in_specs=[pl.BlockSpec((tm, d), lambda i, f: (i, 0)),
                  pl.BlockSpec((1, d), lambda i, f: (0, 0)),
                  pl.BlockSpec((d, tf), lambda i, f: (0, f)),
                  pl.BlockSpec((d, tf), lambda i, f: (0, f)),
                  pl.BlockSpec((tf, d), lambda i, f: (f, 0)),
                  pl.BlockSpec((1, d), lambda i, f: (0, 0))],
        out_specs=pl.BlockSpec((tm, d), lambda i, f: (i, 0)),
        out_shape=jax.ShapeDtypeStruct((m, d), F32),
        scratch_shapes=[pltpu.VMEM((tm, d), BF16)],
        compiler_params=_params("parallel", "arbitrary"),
        name="ffn",
    )(x2d, ffn_norm_w, w_gate, w_up, w_down, final_norm_w)


def _tile(n, pref):
    return pref if n % pref == 0 else n


def _layer(x, lam_init, attn_norm_w, w_in, conv_w, a_log, dt_bias, dn_norm_w, lam_q1, lam_k1,
           lam_q2, lam_k2, da_norm_w, w_out, ffn_norm_w, w_gate, w_up, w_down, final_norm_w):
    b, t, d = x.shape
    m = b * t
    x2d = x.reshape(m, d)
    off_z = 3 * DN_WIDTH
    off_b = off_z + DN_WIDTH
    off_q = off_b + 2 * DN_HEADS
    nw = attn_norm_w.reshape(1, d)

    n_da = w_in.shape[1] - off_q
    da_scale = jnp.concatenate([jnp.full((1, DA_HEADS * 2 * DA_DIM), LOG2E * DA_DIM ** -0.5, F32),
                                jnp.ones((1, n_da - DA_HEADS * 2 * DA_DIM), F32)], axis=1)

    w_in_t = w_in.T
    grow, gcol, h, w_dn, w_da = _gates(x, nw, w_in_t, a_log.reshape(DN_HEADS, 1),
                                       dt_bias.reshape(DN_HEADS, 1), off_b, _tile(t, 256))
    h2d = h.reshape(m, d)
    tp = _tile(m, 2048)
    qkvz = _proj(h2d, w_dn, jnp.ones((1, off_b), F32), F32, tp, 1024, "in_proj_dn")
    da = _proj(h2d, w_da, da_scale, BF16, tp, 1024, "in_proj_da")
    tm = _tile(m, 1024)

    qkvz = qkvz.reshape(b, t, off_b)
    u, w, qd, kd, a, w_out16 = _dn_intra(qkvz, conv_w, gcol, grow, w_out, _tile(t, 4096))
    o_dn = _dn_scan(u, w, qd, kd, a, qkvz, gcol, dn_norm_w.reshape(1, DN_DIM),
                    4 if b % 4 == 0 else 1, _tile(t, 128))

    o_da, w_gate16, w_up16, w_down16 = _attention(
        da.reshape(b, t, n_da), lam_q1.reshape(1, DA_DIM), lam_k1.reshape(1, DA_DIM),
        lam_q2.reshape(1, DA_DIM), lam_k2.reshape(1, DA_DIM), da_norm_w.reshape(1, DA_VDIM),
        w_gate, w_up, w_down, lam_init, _tile(t, 512))

    x2d = _out_proj(o_dn.reshape(m, DN_WIDTH), o_da.reshape(m, DA_WIDTH), w_out16, x2d,
                    _tile(m, 512), d)
    dff = w_gate.shape[1]
    return _ffn(x2d, ffn_norm_w.reshape(1, d), w_gate16, w_up16, w_down16,
                final_norm_w.reshape(1, d), tm, _tile(dff, 512))


def kernel(x, attn_norm_w, w_in, conv_w, a_log, dt_bias, dn_norm_w, lam_q1, lam_k1, lam_q2, lam_k2,
           da_norm_w, w_out, ffn_norm_w, w_gate, w_up, w_down, final_norm_w):
    b, t, d = x.shape
    depth = w_in.shape[0]
    assert depth == 1, "the final rmsnorm is fused into the last layer's FFN kernel"
    lam_init = 0.8 - 0.6 * math.exp(-0.3 * 0)
    out = _layer(x, lam_init, attn_norm_w[0], w_in[0], conv_w[0], a_log[0], dt_bias[0],
                 dn_norm_w[0], lam_q1[0], lam_k1[0], lam_q2[0], lam_k2[0], da_norm_w[0],
                 w_out[0], ffn_norm_w[0], w_gate[0], w_up[0], w_down[0], final_norm_w)
    return out.reshape(b, t, d)
```

```python
import functools
import math

import jax
import jax.numpy as jnp
from jax import lax
from jax.experimental import pallas as pl
from jax.experimental.pallas import tpu as pltpu

F32 = jnp.float32
BF16 = jnp.bfloat16
EPS = 1e-6
NEG_BIG = -1e30
LOG2E = math.log2(math.e)

DN_HEADS = 8
DN_DIM = 128
DN_WIDTH = DN_HEADS * DN_DIM
CONV_WIDTH = 4
CHUNK = 64
DA_HEADS = 4
DA_DIM = 128
DA_VDIM = 2 * DA_DIM
DA_WIDTH = DA_HEADS * DA_VDIM
LANES = 128
SUBLANES = 8
GROUP = 256
VMEM_LIMIT = 56 * 1024 * 1024

_NT = (((1,), (1,)), ((), ()))
_TN = (((0,), (0,)), ((), ()))


def _params(*sem):
    return pltpu.CompilerParams(dimension_semantics=sem, vmem_limit_bytes=VMEM_LIMIT)


def _sigmoid(x):
    return 1.0 / (1.0 + jnp.exp(-x))


def _silu(x):
    h = 0.5 * x
    return h + h * jnp.tanh(h)


def _lane_tile(x, n):
    return jnp.concatenate([x] * n, axis=1)


def _rmsnorm_rows(x, w):
    return x * lax.rsqrt(jnp.mean(x * x, axis=-1, keepdims=True) + EPS) * w


def _proj_kernel(h_ref, w_ref, cs_ref, o_ref):
    acc = lax.dot_general(h_ref[...], w_ref[...], _NT, preferred_element_type=F32)
    o_ref[...] = (acc * cs_ref[...]).astype(o_ref.dtype)


def _proj(h2d, w_t_bf16, col_scale, out_dtype, tm, tn, name):
    m, d = h2d.shape
    n = w_t_bf16.shape[0]
    return pl.pallas_call(
        _proj_kernel,
        grid=(m // tm, n // tn),
        in_specs=[
            pl.BlockSpec((tm, d), lambda i, j: (i, 0)),
            pl.BlockSpec((tn, d), lambda i, j: (j, 0)),
            pl.BlockSpec((1, tn), lambda i, j: (0, j)),
        ],
        out_specs=pl.BlockSpec((tm, tn), lambda i, j: (i, j)),
        out_shape=jax.ShapeDtypeStruct((m, n), out_dtype),
        compiler_params=_params("parallel", "arbitrary"),
        name=name,
    )(h2d, w_t_bf16, col_scale)


def _gates_kernel(x_ref, nw_ref, wt_ref, alog_ref, dtb_ref, wdn_ref, *rest):
    wda_refs, (row_ref, col_ref, h_ref, wdn16_ref, wda16_ref) = rest[:-5], rest[-5:]
    wdn16_ref[...] = wdn_ref[...].astype(BF16)
    wda16_ref[...] = jnp.concatenate([r[...] for r in wda_refs], axis=0).astype(BF16)

    h = _rmsnorm_rows(x_ref[...], nw_ref[...]).astype(BF16)
    h_ref[...] = h
    r = lax.dot_general(wt_ref[...].astype(BF16), h, _NT, preferred_element_type=F32)
    beta = _sigmoid(r[0:DN_HEADS])
    a = r[DN_HEADS:2 * DN_HEADS] + dtb_ref[...]
    softplus = jnp.maximum(a, 0.0) + jnp.log1p(jnp.exp(-jnp.abs(a)))
    g = -jnp.exp(alog_ref[...]) * softplus
    tm = g.shape[1]
    ii = lax.broadcasted_iota(jnp.int32, (tm, tm), 0)
    jj = lax.broadcasted_iota(jnp.int32, (tm, tm), 1)
    same = (ii // CHUNK) == (jj // CHUNK)
    sums = jnp.concatenate([same & (ii <= jj), same], axis=1).astype(BF16)
    g1 = g.astype(BF16)
    r1 = g - g1.astype(F32)
    g2 = r1.astype(BF16)
    g3 = (r1 - g2.astype(F32)).astype(BF16)
    pieces = jnp.concatenate([g1, g2, g3, jnp.zeros_like(g1)], axis=0)
    acc = jnp.dot(pieces, sums, preferred_element_type=F32)
    nh = DN_HEADS
    acc = acc[0:nh] + acc[nh:2 * nh] + acc[2 * nh:3 * nh]
    gc, gl = acc[:, :tm], acc[:, tm:]
    rows = jnp.concatenate([beta, gc, gl], axis=0)
    row_ref[...] = rows
    padded = jnp.concatenate([rows, jnp.zeros((LANES - 3 * DN_HEADS, tm), F32)], axis=0)
    col_ref[...] = padded.T


def _gates(x3d, norm_w, w_in_t, a_log_col, dt_bias_col, off_gate, tm):
    b, t, d = x3d.shape
    nj = t // tm
    steps = b * nj
    piece = 2 * DN_HEADS
    off_da = off_gate + piece
    n_da = w_in_t.shape[0] - off_da
    rows_dn, rows_da = off_gate // steps, n_da // steps
    assert rows_dn * steps == off_gate and rows_dn % piece == 0
    assert rows_da * steps == n_da and rows_da % piece == 0 and off_da % piece == 0
    n_pieces = rows_da // piece

    def step(i, j):
        return i * nj + j

    def da_piece(k):
        return pl.BlockSpec((piece, d),
                            lambda i, j: (off_da // piece + step(i, j) * n_pieces + k, 0))

    return pl.pallas_call(
        _gates_kernel,
        grid=(b, nj),
        in_specs=[
            pl.BlockSpec((None, tm, d), lambda i, j: (i, j, 0)),
            pl.BlockSpec((1, d), lambda i, j: (0, 0)),
            pl.BlockSpec((piece, d), lambda i, j: (off_gate // piece, 0)),
            pl.BlockSpec((DN_HEADS, 1), lambda i, j: (0, 0)),
            pl.BlockSpec((DN_HEADS, 1), lambda i, j: (0, 0)),
            pl.BlockSpec((rows_dn, d), lambda i, j: (step(i, j), 0)),
        ] + [da_piece(k) for k in range(n_pieces)],
        out_specs=[
            pl.BlockSpec((None, 3 * DN_HEADS, tm), lambda i, j: (i, 0, j)),
            pl.BlockSpec((None, tm, LANES), lambda i, j: (i, j, 0)),
            pl.BlockSpec((None, tm, d), lambda i, j: (i, j, 0)),
            pl.BlockSpec((rows_dn, d), lambda i, j: (step(i, j), 0)),
            pl.BlockSpec((rows_da, d), lambda i, j: (step(i, j), 0)),
        ],
        out_shape=[
            jax.ShapeDtypeStruct((b, 3 * DN_HEADS, t), F32),
            jax.ShapeDtypeStruct((b, t, LANES), F32),
            jax.ShapeDtypeStruct((b, t, d), BF16),
            jax.ShapeDtypeStruct((off_gate, d), BF16),
            jax.ShapeDtypeStruct((n_da, d), BF16),
        ],
        compiler_params=_params("parallel", "arbitrary"),
        name="dn_gates",
    )(x3d, norm_w, w_in_t, a_log_col, dt_bias_col, w_in_t, *([w_in_t] * n_pieces))


def _dn_intra_kernel(q_ref, k_ref, v_ref, qh_ref, kh_ref, vh_ref, cwq_ref, cwk_ref, cwv_ref,
                     gcol_ref, grow_ref, wo_ref, u_ref, w_ref, qd_ref, kd_ref, a_ref, wo16_ref, xs_ref):
    wo16_ref[...] = wo_ref[...].astype(BF16)
    head = pl.program_id(1)
    first = pl.program_id(2) == 0
    tb = q_ref.shape[0]
    halo = SUBLANES
    for idx, (x_ref, xh_ref) in enumerate(((q_ref, qh_ref), (k_ref, kh_ref), (v_ref, vh_ref))):
        xs_ref[idx, 0:halo, :] = jnp.where(first, 0.0, xh_ref[...])
        xs_ref[idx, halo:halo + tb, :] = x_ref[...]

    lane = lax.broadcasted_iota(jnp.int32, (GROUP, LANES), 1)
    sel_beta = (lane == head).astype(F32)
    sel_gc = (lane == head + DN_HEADS).astype(F32)
    sel_gl = (lane == head + 2 * DN_HEADS).astype(F32)
    ri = lax.broadcasted_iota(jnp.int32, (GROUP, GROUP), 0)
    ci = lax.broadcasted_iota(jnp.int32, (GROUP, GROUP), 1)
    same = (ri // CHUNK) == (ci // CHUNK)
    incl = same & (ri >= ci)
    strict = same & (ri > ci)

    def conv_silu(idx, cw_ref, r0):
        base = r0 + halo - (CONV_WIDTH - 1)
        acc = xs_ref[idx, base:base + GROUP, :] * cw_ref[0:1, :]
        for i in range(1, CONV_WIDTH):
            acc = acc + xs_ref[idx, base + i:base + i + GROUP, :] * cw_ref[i:i + 1, :]
        return _silu(acc)

    n_groups = tb // GROUP
    levels = int(math.log2(CHUNK))
    n_pows = [None] * n_groups
    xs = [None] * n_groups

    def prepare(gi):
        r0 = gi * GROUP
        q = conv_silu(0, cwq_ref, r0)
        k = conv_silu(1, cwk_ref, r0)
        v = conv_silu(2, cwv_ref, r0)
        q = q * lax.rsqrt(jnp.sum(q * q, axis=-1, keepdims=True) + EPS) * (DN_DIM ** -0.5)
        k = k * lax.rsqrt(jnp.sum(k * k, axis=-1, keepdims=True) + EPS)

        gates = gcol_ref[r0:r0 + GROUP, :]
        beta = jnp.sum(gates * sel_beta, axis=-1, keepdims=True)
        gc = jnp.sum(gates * sel_gc, axis=-1, keepdims=True)
        gl = jnp.sum(gates * sel_gl, axis=-1, keepdims=True)
        gc_row = grow_ref[pl.ds(head + DN_HEADS, 1), r0:r0 + GROUP]

        decay = jnp.exp(jnp.where(incl, gc - gc_row, NEG_BIG))
        kb = k * beta
        k16 = k.astype(BF16)
        gram = lax.dot_general(jnp.concatenate([kb, q], axis=0).astype(BF16), k16, _NT,
                               preferred_element_type=F32)
        n_pows[gi] = jnp.where(strict, -(gram[:GROUP] * decay), 0.0)
        a_qk = gram[GROUP:] * decay

        egc = jnp.exp(gc)
        xs[gi] = jnp.concatenate([v * beta, kb * egc], axis=1)
        qd_ref[r0:r0 + GROUP, :] = (q * egc).astype(BF16)
        kd_ref[r0:r0 + GROUP, :] = (k * jnp.exp(gl - gc)).astype(BF16)
        for c in range(GROUP // CHUNK):
            lo = c * CHUNK
            a_ref[r0 + lo:r0 + lo + CHUNK, :] = a_qk[lo:lo + CHUNK, lo:lo + CHUNK].astype(BF16)

    def level(gi, lvl):
        n16 = n_pows[gi].astype(BF16)
        xs[gi] = xs[gi] + jnp.dot(n16, xs[gi].astype(BF16), preferred_element_type=F32)
        if lvl + 1 < levels:
            n_pows[gi] = jnp.dot(n16, n16, preferred_element_type=F32)

    def finish(gi):
        r0 = gi * GROUP
        u_ref[r0:r0 + GROUP, :] = xs[gi][:, :DN_DIM]
        w_ref[r0:r0 + GROUP, :] = xs[gi][:, DN_DIM:].astype(BF16)

    for t in range(n_groups + levels + 1):
        for gi in range(n_groups):
            stage = t - gi
            if stage == 0:
                prepare(gi)
            elif 1 <= stage <= levels:
                level(gi, stage - 1)
            elif stage == levels + 1:
                finish(gi)


def _dn_intra(qkvz, conv_w, gcol, grow, w_out, tb):
    b, t, _ = qkvz.shape
    hb = tb // SUBLANES
    nh = DN_HEADS
    nj = t // tb
    wo_rows, wo_share = _slab_rows(w_out.shape[0], b * nh * nj)
    assert wo_share == 1, "every grid step converts its own slab of w_out"
    wo_spec = pl.BlockSpec((wo_rows, w_out.shape[1]), lambda i, h, j: ((i * nh + h) * nj + j, 0))

    def tok(off):
        return pl.BlockSpec((None, tb, DN_DIM), lambda i, h, j: (i, j, off + h))

    def halo(off):
        return pl.BlockSpec((None, SUBLANES, DN_DIM),
                            lambda i, h, j: (i, jnp.maximum(j * hb - 1, 0), off + h))

    def cw(off):
        return pl.BlockSpec((CONV_WIDTH, DN_DIM), lambda i, h, j: (0, off + h))

    out_tok = pl.BlockSpec((None, tb, DN_DIM), lambda i, h, j: (i, j, h))
    return pl.pallas_call(
        _dn_intra_kernel,
        grid=(b, nh, nj),
        in_specs=[tok(0), tok(nh), tok(2 * nh), halo(0), halo(nh), halo(2 * nh),
                  cw(0), cw(nh), cw(2 * nh),
                  pl.BlockSpec((None, tb, LANES), lambda i, h, j: (i, j, 0)),
                  pl.BlockSpec((None, 3 * nh, tb), lambda i, h, j: (i, 0, j)),
                  wo_spec],
        out_specs=[out_tok, out_tok, out_tok, out_tok,
                   pl.BlockSpec((None, None, tb, CHUNK), lambda i, h, j: (i, h, j, 0)),
                   wo_spec],
        out_shape=[jax.ShapeDtypeStruct((b, t, DN_WIDTH), F32),
                   jax.ShapeDtypeStruct((b, t, DN_WIDTH), BF16),
                   jax.ShapeDtypeStruct((b, t, DN_WIDTH), BF16),
                   jax.ShapeDtypeStruct((b, t, DN_WIDTH), BF16),
                   jax.ShapeDtypeStruct((b, nh, t, CHUNK), BF16),
                   jax.ShapeDtypeStruct(w_out.shape, BF16)],
        scratch_shapes=[pltpu.VMEM((3, tb + SUBLANES, DN_DIM), F32)],
        compiler_params=_params("parallel", "parallel", "arbitrary"),
        name="dn_intra",
    )(qkvz, qkvz, qkvz, qkvz, qkvz, qkvz, conv_w, conv_w, conv_w, gcol, grow, w_out)


def _dn_scan_kernel(u_ref, w_ref, qd_ref, kd_ref, a_ref, z_ref, gcol_ref, nw_ref, o_ref, s_ref):
    @pl.when(pl.program_id(1) == 0)
    def _():
        s_ref[...] = jnp.zeros_like(s_ref)

    bb, tb = u_ref.shape[0], u_ref.shape[1]
    nw = nw_ref[...]
    units = [(i, h) for i in range(bb) for h in range(DN_HEADS)]

    def chunk(c, carry):
        rows = pl.ds(pl.multiple_of(c * CHUNK, CHUNK), CHUNK)
        cols = [slice(h * DN_DIM, (h + 1) * DN_DIM) for h in range(DN_HEADS)]
        s = [s_ref[i, h] for i, h in units]
        proj = [jnp.dot(jnp.concatenate([w_ref[i, rows, cols[h]], qd_ref[i, rows, cols[h]]], axis=0),
                        s[n].astype(BF16), preferred_element_type=F32)
                for n, (i, h) in enumerate(units)]
        v16 = [(u_ref[i, rows, cols[h]] - proj[n][:CHUNK]).astype(BF16)
               for n, (i, h) in enumerate(units)]
        upd = [lax.dot_general(kd_ref[i, rows, cols[h]], v16[n], _TN, preferred_element_type=F32)
               for n, (i, h) in enumerate(units)]
        for n, (i, h) in enumerate(units):
            gl = gcol_ref[i, pl.ds(pl.multiple_of(c * CHUNK, CHUNK), 1),
                          2 * DN_HEADS + h:2 * DN_HEADS + h + 1]
            s_ref[i, h] = s[n] * jnp.exp(gl) + upd[n]
        o = [proj[n][CHUNK:] + jnp.dot(a_ref[i, h, rows, :], v16[n], preferred_element_type=F32)
             for n, (i, h) in enumerate(units)]
        for n, (i, h) in enumerate(units):
            z = z_ref[i, rows, cols[h]]
            o_ref[i, rows, cols[h]] = (_rmsnorm_rows(o[n], nw) * _silu(z)).astype(o_ref.dtype)
        return carry

    lax.fori_loop(0, tb // CHUNK, chunk, 0)


def _dn_scan(u, w, qd, kd, a, qkvz, gcol, dn_norm_w, bb, tb):
    b, t, _ = u.shape
    tok = pl.BlockSpec((bb, tb, DN_WIDTH), lambda i, j: (i, j, 0))
    z_block = 3 * DN_WIDTH // DN_WIDTH
    return pl.pallas_call(
        _dn_scan_kernel,
        grid=(b // bb, t // tb),
        in_specs=[tok, tok, tok, tok,
                  pl.BlockSpec((bb, DN_HEADS, tb, CHUNK), lambda i, j: (i, 0, j, 0)),
                  pl.BlockSpec((bb, tb, DN_WIDTH), lambda i, j: (i, j, z_block)),
                  pl.BlockSpec((bb, tb, LANES), lambda i, j: (i, j, 0)),
                  pl.BlockSpec((1, DN_DIM), lambda i, j: (0, 0))],
        out_specs=tok,
        out_shape=jax.ShapeDtypeStruct((b, t, DN_WIDTH), BF16),
        scratch_shapes=[pltpu.VMEM((bb, DN_HEADS, DN_DIM, DN_DIM), F32)],
        compiler_params=_params("parallel", "arbitrary"),
        name="dn_scan",
    )(u, w, qd, kd, a, qkvz, gcol, dn_norm_w)


def _attn_kernel(q_ref, k_ref, v_ref, lq1_ref, lk1_ref, lq2_ref, lk2_ref, nw_ref,
                 wg_ref, wu_ref, wd_ref, o_ref, wg16_ref, wu16_ref, wd16_ref,
                 m_ref, l_ref, acc_ref, sa_ref, sb_ref, *, lam_init, down_share):
    wg16_ref[...] = wg_ref[...].astype(BF16)
    wu16_ref[...] = wu_ref[...].astype(BF16)
    grid_step = ((pl.program_id(0) * pl.num_programs(1) + pl.program_id(1)) * pl.num_programs(2)
                 + pl.program_id(2))

    @pl.when(grid_step % down_share == 0)
    def _():
        wd16_ref[...] = wd_ref[...].astype(BF16)

    head = pl.program_id(1)
    qi = pl.program_id(2)
    tq = q_ref.shape[0]
    tk = tq
    slope = jnp.float32(0.0)
    for hh in range(DA_HEADS):
        slope = jnp.where(head == hh, jnp.float32(2.0 ** (-8.0 * (hh + 1) / DA_HEADS)), slope)

    kpos = lax.broadcasted_iota(jnp.int32, (1, tk), 1)
    slope2 = slope * LOG2E

    maps = range(2)

    def scores(j, s_ref):
        k0 = pl.multiple_of(j * tk, tk)
        bias = slope2 * (kpos + (j - qi) * tk).astype(F32)
        for c in maps:
            cols = slice(c * DA_DIM, (c + 1) * DA_DIM)
            s_ref[c] = lax.dot_general(q_ref[:, cols], k_ref[pl.ds(k0, tk), cols], _NT,
                                       preferred_element_type=F32) + bias

    def accumulate(j, s_ref, masked, first=False):
        k0 = pl.multiple_of(j * tk, tk)
        v = v_ref[pl.ds(k0, tk), :]
        s = [s_ref[c] for c in maps]
        if masked:
            rr = lax.broadcasted_iota(jnp.int32, (tq, tk), 0)
            cc = lax.broadcasted_iota(jnp.int32, (tq, tk), 1)
            s = [jnp.where(cc <= rr, s[c], NEG_BIG) for c in maps]
        m_cur = [jnp.broadcast_to(jnp.max(s[c], axis=-1, keepdims=True), (tq, LANES)) for c in maps]
        if first:
            m_new = m_cur
        else:
            m_prev = [m_ref[c] for c in maps]
            m_new = [jnp.maximum(m_prev[c], m_cur[c]) for c in maps]
        p = [jnp.exp2(s[c] - _lane_tile(m_new[c], tk // LANES)) for c in maps]
        p_sum = [jnp.broadcast_to(jnp.sum(p[c], axis=-1, keepdims=True), (tq, LANES)) for c in maps]
        p16 = [p[c].astype(BF16) for c in maps]
        pv = [jnp.dot(p16[c], v, preferred_element_type=F32) for c in maps]
        for c in maps:
            if first:
                l_ref[c] = p_sum[c]
                acc_ref[c] = pv[c]
            else:
                alpha = jnp.exp2(m_prev[c] - m_new[c])
                l_ref[c] = alpha * l_ref[c] + p_sum[c]
                acc_ref[c] = _lane_tile(alpha, DA_VDIM // LANES) * acc_ref[c] + pv[c]
            m_ref[c] = m_new[c]

    scores(0, sa_ref)

    @pl.when(qi == 0)
    def _():
        accumulate(0, sa_ref, True, first=True)

    @pl.when(qi > 0)
    def _():
        scores(1, sb_ref)
        accumulate(0, sa_ref, False, first=True)

        def pair(jj, carry):
            j = 2 * jj + 1
            scores(j + 1, sa_ref)
            accumulate(j, sb_ref, False)
            scores(j + 2, sb_ref)
            accumulate(j + 1, sa_ref, False)
            return carry

        lax.fori_loop(0, (qi - 1) // 2, pair, 0)

        @pl.when(qi % 2 == 0)
        def _():
            scores(qi, sa_ref)
            accumulate(qi - 1, sb_ref, False)
            accumulate(qi, sa_ref, True)

        @pl.when(qi % 2 == 1)
        def _():
            accumulate(qi, sb_ref, True)

    lam = (jnp.exp(jnp.sum(lq1_ref[...] * lk1_ref[...], axis=-1, keepdims=True))
           - jnp.exp(jnp.sum(lq2_ref[...] * lk2_ref[...], axis=-1, keepdims=True)) + lam_init)
    rep = DA_VDIM // LANES
    inv_l0 = 1.0 / l_ref[0]
    inv_l1 = lam / l_ref[1]
    o = acc_ref[0] * _lane_tile(inv_l0, rep) - acc_ref[1] * _lane_tile(inv_l1, rep)
    o_ref[...] = (_rmsnorm_rows(o, nw_ref[...]) * (1.0 - lam_init)).astype(o_ref.dtype)


def _slab_rows(n_rows, n_steps):
    for share in (1, 2, 4, 8, 16):
        if n_steps % share == 0 and n_rows % (n_steps // share) == 0:
            rows = n_rows // (n_steps // share)
            if rows % 16 == 0:
                return rows, share
    raise ValueError(f"cannot split {n_rows} weight rows over {n_steps} grid steps")


def _attention(da, lam_q1, lam_k1, lam_q2, lam_k2, da_norm_w, w_gate, w_up, w_down, lam_init, tq):
    b, t, _ = da.shape
    nh = DA_HEADS
    nq = t // tq
    d, dff = w_gate.shape
    steps = b * nh * nq
    rg, sg = _slab_rows(d, steps)
    rd, sd = _slab_rows(dff, steps)

    def step(i, h, j):
        return (i * nh + h) * nq + j

    vec = pl.BlockSpec((1, DA_DIM), lambda i, h, j: (0, 0))
    up_spec = pl.BlockSpec((rg, dff), lambda i, h, j: (step(i, h, j) // sg, 0))
    down_spec = pl.BlockSpec((rd, d), lambda i, h, j: (step(i, h, j) // sd, 0))
    return pl.pallas_call(
        functools.partial(_attn_kernel, lam_init=lam_init, down_share=sd),
        grid=(b, nh, nq),
        in_specs=[pl.BlockSpec((None, tq, DA_VDIM), lambda i, h, j: (i, j, h)),
                  pl.BlockSpec((None, t, DA_VDIM), lambda i, h, j: (i, 0, nh + h)),
                  pl.BlockSpec((None, t, DA_VDIM), lambda i, h, j: (i, 0, 2 * nh + h)),
                  vec, vec, vec, vec,
                  pl.BlockSpec((1, DA_VDIM), lambda i, h, j: (0, 0)),
                  up_spec, up_spec, down_spec],
        out_specs=[pl.BlockSpec((None, tq, DA_VDIM), lambda i, h, j: (i, j, h)),
                   up_spec, up_spec, down_spec],
        out_shape=[jax.ShapeDtypeStruct((b, t, DA_WIDTH), BF16),
                   jax.ShapeDtypeStruct((d, dff), BF16),
                   jax.ShapeDtypeStruct((d, dff), BF16),
                   jax.ShapeDtypeStruct((dff, d), BF16)],
        scratch_shapes=[pltpu.VMEM((2, tq, LANES), F32), pltpu.VMEM((2, tq, LANES), F32),
                        pltpu.VMEM((2, tq, DA_VDIM), F32),
                        pltpu.VMEM((2, tq, tq), F32), pltpu.VMEM((2, tq, tq), F32)],
        compiler_params=_params("arbitrary", "arbitrary", "arbitrary"),
        name="diff_attn",
    )(da, da, da, lam_q1, lam_k1, lam_q2, lam_k2, da_norm_w, w_gate, w_up, w_down)


def _out_proj_kernel(a_ref, b_ref, wa_ref, wb_ref, x_ref, o_ref):
    acc = jnp.dot(a_ref[...], wa_ref[...], preferred_element_type=F32)
    acc = acc + jnp.dot(b_ref[...], wb_ref[...], preferred_element_type=F32)
    o_ref[...] = x_ref[...] + acc


def _out_proj(o_dn, o_da, w16, x2d, tm, tn):
    m, d = x2d.shape
    ka, kb = o_dn.shape[1], o_da.shape[1]
    assert ka == kb and w16.shape[0] == ka + kb, "the two head groups are equal row halves of w_out"
    return pl.pallas_call(
        _out_proj_kernel,
        grid=(m // tm, d // tn),
        in_specs=[pl.BlockSpec((tm, ka), lambda i, j: (i, 0)),
                  pl.BlockSpec((tm, kb), lambda i, j: (i, 0)),
                  pl.BlockSpec((ka, tn), lambda i, j: (0, j)),
                  pl.BlockSpec((kb, tn), lambda i, j: (1, j)),
                  pl.BlockSpec((tm, tn), lambda i, j: (i, j))],
        out_specs=pl.BlockSpec((tm, tn), lambda i, j: (i, j)),
        out_shape=jax.ShapeDtypeStruct((m, d), F32),
        compiler_params=_params("parallel", "arbitrary"),
        name="out_proj",
    )(o_dn, o_da, w16, w16, x2d)


def _ffn_kernel(x_ref, nw_ref, wg_ref, wu_ref, wd_ref, fw_ref, o_ref, h_ref):
    f = pl.program_id(1)

    @pl.when(f == 0)
    def _():
        x = x_ref[...]
        h_ref[...] = _rmsnorm_rows(x, nw_ref[...]).astype(BF16)
        o_ref[...] = x

    h = h_ref[...]
    g = jnp.dot(h, wg_ref[...], preferred_element_type=F32)
    u = jnp.dot(h, wu_ref[...], preferred_element_type=F32)
    act = (_silu(g) * u).astype(BF16)
    o_ref[...] += jnp.dot(act, wd_ref[...], preferred_element_type=F32)

    @pl.when(f == pl.num_programs(1) - 1)
    def _():
        o_ref[...] = _rmsnorm_rows(o_ref[...], fw_ref[...])


def _ffn(x2d, ffn_norm_w, w_gate, w_up, w_down, final_norm_w, tm, tf):
    m, d = x2d.shape
    dff = w_gate.shape[1]
    return pl.pallas_call(
        _ffn_kernel,
        grid=(m // tm, dff // tf),
        in_specs=[pl.BlockSpec((tm, d), lambda i, f: (i, 0)),
                  pl.BlockSpec((1, d), lambda i, f: (0, 0)),
                  pl.BlockSpec((d, tf), lambda i, f: (0, f)),
                  pl.BlockSpec((d, tf), lambda i, f: (0, f)),
                  pl.BlockSpec((tf, d), lambda i, f: (f, 0)),
                  pl.BlockSpec((1, d), lambda i, f: (0, 0))],
        out_specs=pl.BlockSpec((tm, d), lambda i, f: (i, 0)),
        out_shape=jax.ShapeDtypeStruct((m, d), F32),
        scratch_shapes=[pltpu.VMEM((tm, d), BF16)],
        compiler_params=_params("parallel", "arbitrary"),
        name="ffn",
    )(x2d, ffn_norm_w, w_gate, w_up, w_down, final_norm_w)


def _tile(n, pref):
    return pref if n % pref == 0 else n


def _layer(x, lam_init, attn_norm_w, w_in, conv_w, a_log, dt_bias, dn_norm_w, lam_q1, lam_k1,
           lam_q2, lam_k2, da_norm_w, w_out, ffn_norm_w, w_gate, w_up, w_down, final_norm_w):
    b, t, d = x.shape
    m = b * t
    x2d = x.reshape(m, d)
    off_z = 3 * DN_WIDTH
    off_b = off_z + DN_WIDTH
    off_q = off_b + 2 * DN_HEADS
    nw = attn_norm_w.reshape(1, d)

    n_da = w_in.shape[1] - off_q
    da_scale = jnp.concatenate([jnp.full((1, DA_HEADS * 2 * DA_DIM), LOG2E * DA_DIM ** -0.5, F32),
                                jnp.ones((1, n_da - DA_HEADS * 2 * DA_DIM), F32)], axis=1)

    w_in_t = w_in.T
    grow, gcol, h, w_dn, w_da = _gates(x, nw, w_in_t, a_log.reshape(DN_HEADS, 1),
                                       dt_bias.reshape(DN_HEADS, 1), off_b, _tile(t, 256))
    h2d = h.reshape(m, d)
    tp = _tile(m, 2048)
    qkvz = _proj(h2d, w_dn, jnp.ones((1, off_b), F32), F32, tp, 1024, "in_proj_dn")
    da = _proj(h2d, w_da, da_scale, BF16, tp, 1024, "in_proj_da")
    tm = _tile(m, 1024)

    qkvz = qkvz.reshape(b, t, off_b)
    u, w, qd, kd, a, w_out16 = _dn_intra(qkvz, conv_w, gcol, grow, w_out, _tile(t, 4096))
    o_dn = _dn_scan(u, w, qd, kd, a, qkvz, gcol, dn_norm_w.reshape(1, DN_DIM),
                    4 if b % 4 == 0 else 1, _tile(t, 128))

    o_da, w_gate16, w_up16, w_down16 = _attention(
        da.reshape(b, t, n_da), lam_q1.reshape(1, DA_DIM), lam_k1.reshape(1, DA_DIM),
        lam_q2.reshape(1, DA_DIM), lam_k2.reshape(1, DA_DIM), da_norm_w.reshape(1, DA_VDIM),
        w_gate, w_up, w_down, lam_init, _tile(t, 512))

    x2d = _out_proj(o_dn.reshape(m, DN_WIDTH), o_da.reshape(m, DA_WIDTH), w_out16, x2d,
                    _tile(m, 512), d)
    dff = w_gate.shape[1]
    return _ffn(x2d, ffn_norm_w.reshape(1, d), w_gate16, w_up16, w_down16,
                final_norm_w.reshape(1, d), tm, _tile(dff, 512))


def kernel(x, attn_norm_w, w_in, conv_w, a_log, dt_bias, dn_norm_w, lam_q1, lam_k1, lam_q2, lam_k2,
           da_norm_w, w_out, ffn_norm_w, w_gate, w_up, w_down, final_norm_w):
    b, t, d = x.shape
    depth = w_in.shape[0]
    assert depth == 1, "the final rmsnorm is fused into the last layer's FFN kernel"
    lam_init = 0.8 - 0.6 * math.exp(-0.3 * 0)
    out = _layer(x, lam_init, attn_norm_w[0], w_in[0], conv_w[0], a_log[0], dt_bias[0],
                 dn_norm_w[0], lam_q1[0], lam_k1[0], lam_q2[0], lam_k2[0], da_norm_w[0],
                 w_out[0], ffn_norm_w[0], w_gate[0], w_up[0], w_down[0], final_norm_w)
    return out.reshape(b, t, d)
```

```python
import functools
import math

import jax
import jax.numpy as jnp
from jax import lax
from jax.experimental import pallas as pl
from jax.experimental.pallas import tpu as pltpu

F32 = jnp.float32
BF16 = jnp.bfloat16
EPS = 1e-6
NEG_BIG = -1e30
LOG2E = math.log2(math.e)

DN_HEADS = 8
DN_DIM = 128
DN_WIDTH = DN_HEADS * DN_DIM
CONV_WIDTH = 4
CHUNK = 64
DA_HEADS = 4
DA_DIM = 128
DA_VDIM = 2 * DA_DIM
DA_WIDTH = DA_HEADS * DA_VDIM
LANES = 128
SUBLANES = 8
GROUP = 256
IN_FLIGHT = 4
VMEM_LIMIT = 56 * 1024 * 1024

_NT = (((1,), (1,)), ((), ()))
_TN = (((0,), (0,)), ((), ()))


def _params(*sem):
    return pltpu.CompilerParams(dimension_semantics=sem, vmem_limit_bytes=VMEM_LIMIT)


def _sigmoid(x):
    return 1.0 / (1.0 + jnp.exp(-x))


def _silu(x):
    h = 0.5 * x
    return h + h * jnp.tanh(h)


def _lane_tile(x, n):
    return jnp.concatenate([x] * n, axis=1)


def _rmsnorm_rows(x, w):
    return x * lax.rsqrt(jnp.mean(x * x, axis=-1, keepdims=True) + EPS) * w


def _proj_kernel(h_ref, w_ref, cs_ref, o_ref):
    acc = lax.dot_general(h_ref[...], w_ref[...], _NT, preferred_element_type=F32)
    o_ref[...] = (acc * cs_ref[...]).astype(o_ref.dtype)


def _proj(h2d, w_t_bf16, col_scale, out_dtype, tm, tn, name):
    m, d = h2d.shape
    n = w_t_bf16.shape[0]
    return pl.pallas_call(
        _proj_kernel,
        grid=(m // tm, n // tn),
        in_specs=[
            pl.BlockSpec((tm, d), lambda i, j: (i, 0)),
            pl.BlockSpec((tn, d), lambda i, j: (j, 0)),
            pl.BlockSpec((1, tn), lambda i, j: (0, j)),
        ],
        out_specs=pl.BlockSpec((tm, tn), lambda i, j: (i, j)),
        out_shape=jax.ShapeDtypeStruct((m, n), out_dtype),
        compiler_params=_params("parallel", "arbitrary"),
        name=name,
    )(h2d, w_t_bf16, col_scale)


def _gates_kernel(x_ref, nw_ref, wt_ref, alog_ref, dtb_ref, wdn_ref, *rest):
    wda_refs, (row_ref, col_ref, h_ref, wdn16_ref, wda16_ref) = rest[:-5], rest[-5:]
    wdn16_ref[...] = wdn_ref[...].astype(BF16)
    wda16_ref[...] = jnp.concatenate([r[...] for r in wda_refs], axis=0).astype(BF16)

    h = _rmsnorm_rows(x_ref[...], nw_ref[...]).astype(BF16)
    h_ref[...] = h
    r = lax.dot_general(wt_ref[...].astype(BF16), h, _NT, preferred_element_type=F32)
    beta = _sigmoid(r[0:DN_HEADS])
    a = r[DN_HEADS:2 * DN_HEADS] + dtb_ref[...]
    softplus = jnp.maximum(a, 0.0) + jnp.log1p(jnp.exp(-jnp.abs(a)))
    g = -jnp.exp(alog_ref[...]) * softplus
    tm = g.shape[1]
    ii = lax.broadcasted_iota(jnp.int32, (tm, tm), 0)
    jj = lax.broadcasted_iota(jnp.int32, (tm, tm), 1)
    same = (ii // CHUNK) == (jj // CHUNK)
    sums = jnp.concatenate([same & (ii <= jj), same], axis=1).astype(BF16)
    g1 = g.astype(BF16)
    r1 = g - g1.astype(F32)
    g2 = r1.astype(BF16)
    g3 = (r1 - g2.astype(F32)).astype(BF16)
    pieces = jnp.concatenate([g1, g2, g3, jnp.zeros_like(g1)], axis=0)
    acc = jnp.dot(pieces, sums, preferred_element_type=F32)
    nh = DN_HEADS
    acc = acc[0:nh] + acc[nh:2 * nh] + acc[2 * nh:3 * nh]
    gc, gl = acc[:, :tm], acc[:, tm:]
    rows = jnp.concatenate([beta, gc, gl], axis=0)
    row_ref[...] = rows
    padded = jnp.concatenate([rows, jnp.zeros((LANES - 3 * DN_HEADS, tm), F32)], axis=0)
    col_ref[...] = padded.T


def _gates(x3d, norm_w, w_in_t, a_log_col, dt_bias_col, off_gate, tm):
    b, t, d = x3d.shape
    nj = t // tm
    steps = b * nj
    piece = 2 * DN_HEADS
    off_da = off_gate + piece
    n_da = w_in_t.shape[0] - off_da
    rows_dn, rows_da = off_gate // steps, n_da // steps
    assert rows_dn * steps == off_gate and rows_dn % piece == 0
    assert rows_da * steps == n_da and rows_da % piece == 0 and off_da % piece == 0
    n_pieces = rows_da // piece

    def step(i, j):
        return i * nj + j

    def da_piece(k):
        return pl.BlockSpec((piece, d),
                            lambda i, j: (off_da // piece + step(i, j) * n_pieces + k, 0))

    return pl.pallas_call(
        _gates_kernel,
        grid=(b, nj),
        in_specs=[
            pl.BlockSpec((None, tm, d), lambda i, j: (i, j, 0)),
            pl.BlockSpec((1, d), lambda i, j: (0, 0)),
            pl.BlockSpec((piece, d), lambda i, j: (off_gate // piece, 0)),
            pl.BlockSpec((DN_HEADS, 1), lambda i, j: (0, 0)),
            pl.BlockSpec((DN_HEADS, 1), lambda i, j: (0, 0)),
            pl.BlockSpec((rows_dn, d), lambda i, j: (step(i, j), 0)),
        ] + [da_piece(k) for k in range(n_pieces)],
        out_specs=[
            pl.BlockSpec((None, 3 * DN_HEADS, tm), lambda i, j: (i, 0, j)),
            pl.BlockSpec((None, tm, LANES), lambda i, j: (i, j, 0)),
            pl.BlockSpec((None, tm, d), lambda i, j: (i, j, 0)),
            pl.BlockSpec((rows_dn, d), lambda i, j: (step(i, j), 0)),
            pl.BlockSpec((rows_da, d), lambda i, j: (step(i, j), 0)),
        ],
        out_shape=[
            jax.ShapeDtypeStruct((b, 3 * DN_HEADS, t), F32),
            jax.ShapeDtypeStruct((b, t, LANES), F32),
            jax.ShapeDtypeStruct((b, t, d), BF16),
            jax.ShapeDtypeStruct((off_gate, d), BF16),
            jax.ShapeDtypeStruct((n_da, d), BF16),
        ],
        compiler_params=_params("parallel", "arbitrary"),
        name="dn_gates",
    )(x3d, norm_w, w_in_t, a_log_col, dt_bias_col, w_in_t, *([w_in_t] * n_pieces))


def _dn_intra_kernel(q_ref, k_ref, v_ref, qh_ref, kh_ref, vh_ref, cwq_ref, cwk_ref, cwv_ref,
                     gcol_ref, grow_ref, wo_ref, u_ref, w_ref, qd_ref, kd_ref, a_ref, wo16_ref, xs_ref):
    wo16_ref[...] = wo_ref[...].astype(BF16)
    head = pl.program_id(1)
    first = pl.program_id(2) == 0
    tb = q_ref.shape[0]
    halo = SUBLANES
    for idx, (x_ref, xh_ref) in enumerate(((q_ref, qh_ref), (k_ref, kh_ref), (v_ref, vh_ref))):
        xs_ref[idx, 0:halo, :] = jnp.where(first, 0.0, xh_ref[...])
        xs_ref[idx, halo:halo + tb, :] = x_ref[...]

    lane = lax.broadcasted_iota(jnp.int32, (GROUP, LANES), 1)
    sel_beta = (lane == head).astype(F32)
    sel_gc = (lane == head + DN_HEADS).astype(F32)
    sel_gl = (lane == head + 2 * DN_HEADS).astype(F32)
    ri = lax.broadcasted_iota(jnp.int32, (GROUP, GROUP), 0)
    ci = lax.broadcasted_iota(jnp.int32, (GROUP, GROUP), 1)
    same = (ri // CHUNK) == (ci // CHUNK)
    incl = same & (ri >= ci)
    strict = same & (ri > ci)

    def conv_silu(idx, cw_ref, r0, anchor):
        base = r0 + halo - (CONV_WIDTH - 1)
        taps = cw_ref[...] if anchor is None else cw_ref[...] + anchor
        acc = xs_ref[idx, base:base + GROUP, :] * taps[0:1, :]
        for i in range(1, CONV_WIDTH):
            acc = acc + xs_ref[idx, base + i:base + i + GROUP, :] * taps[i:i + 1, :]
        return _silu(acc)

    n_groups = tb // GROUP
    levels = int(math.log2(CHUNK))
    n_pows = [None] * n_groups
    xs = [None] * n_groups

    def prepare(gi):
        r0 = gi * GROUP
        anchor = None if gi < IN_FLIGHT else xs[gi - IN_FLIGHT][0:1, :DN_DIM] * 0.0
        q = conv_silu(0, cwq_ref, r0, anchor)
        k = conv_silu(1, cwk_ref, r0, anchor)
        v = conv_silu(2, cwv_ref, r0, anchor)
        q = q * lax.rsqrt(jnp.sum(q * q, axis=-1, keepdims=True) + EPS) * (DN_DIM ** -0.5)
        k = k * lax.rsqrt(jnp.sum(k * k, axis=-1, keepdims=True) + EPS)

        gates = gcol_ref[r0:r0 + GROUP, :]
        beta = jnp.sum(gates * sel_beta, axis=-1, keepdims=True)
        gc = jnp.sum(gates * sel_gc, axis=-1, keepdims=True)
        gl = jnp.sum(gates * sel_gl, axis=-1, keepdims=True)
        gc_row = grow_ref[pl.ds(head + DN_HEADS, 1), r0:r0 + GROUP]

        decay = jnp.exp(jnp.where(incl, gc - gc_row, NEG_BIG))
        kb = k * beta
        k16 = k.astype(BF16)
        gram = lax.dot_general(jnp.concatenate([kb, q], axis=0).astype(BF16), k16, _NT,
                               preferred_element_type=F32)
        n_pows[gi] = jnp.where(strict, -(gram[:GROUP] * decay), 0.0)
        a_qk = gram[GROUP:] * decay

        egc = jnp.exp(gc)
        xs[gi] = jnp.concatenate([v * beta, kb * egc], axis=1)
        qd_ref[r0:r0 + GROUP, :] = (q * egc).astype(BF16)
        kd_ref[r0:r0 + GROUP, :] = (k * jnp.exp(gl - gc)).astype(BF16)
        for c in range(GROUP // CHUNK):
            lo = c * CHUNK
            a_ref[r0 + lo:r0 + lo + CHUNK, :] = a_qk[lo:lo + CHUNK, lo:lo + CHUNK].astype(BF16)

    def level(gi, lvl):
        n16 = n_pows[gi].astype(BF16)
        xs[gi] = xs[gi] + jnp.dot(n16, xs[gi].astype(BF16), preferred_element_type=F32)
        if lvl + 1 < levels:
            n_pows[gi] = jnp.dot(n16, n16, preferred_element_type=F32)

    def finish(gi):
        r0 = gi * GROUP
        u_ref[r0:r0 + GROUP, :] = xs[gi][:, :DN_DIM]
        w_ref[r0:r0 + GROUP, :] = xs[gi][:, DN_DIM:].astype(BF16)

    for t in range(n_groups + levels + 1):
        for gi in range(n_groups):
            stage = t - gi
            if stage == 0:
                prepare(gi)
            elif 1 <= stage <= levels:
                level(gi, stage - 1)
            elif stage == levels + 1:
                finish(gi)


def _dn_intra(qkvz, conv_w, gcol, grow, w_out, tb):
    b, t, _ = qkvz.shape
    hb = tb // SUBLANES
    nh = DN_HEADS
    nj = t // tb
    wo_rows, wo_share = _slab_rows(w_out.shape[0], b * nh * nj)
    assert wo_share == 1, "every grid step converts its own slab of w_out"
    wo_spec = pl.BlockSpec((wo_rows, w_out.shape[1]), lambda i, h, j: ((i * nh + h) * nj + j, 0))

    def tok(off):
        return pl.BlockSpec((None, tb, DN_DIM), lambda i, h, j: (i, j, off + h))

    def halo(off):
        return pl.BlockSpec((None, SUBLANES, DN_DIM),
                            lambda i, h, j: (i, jnp.maximum(j * hb - 1, 0), off + h))

    def cw(off):
        return pl.BlockSpec((CONV_WIDTH, DN_DIM), lambda i, h, j: (0, off + h))

    out_tok = pl.BlockSpec((None, tb, DN_DIM), lambda i, h, j: (i, j, h))
    return pl.pallas_call(
        _dn_intra_kernel,
        grid=(b, nh, nj),
        in_specs=[tok(0), tok(nh), tok(2 * nh), halo(0), halo(nh), halo(2 * nh),
                  cw(0), cw(nh), cw(2 * nh),
                  pl.BlockSpec((None, tb, LANES), lambda i, h, j: (i, j, 0)),
                  pl.BlockSpec((None, 3 * nh, tb), lambda i, h, j: (i, 0, j)),
                  wo_spec],
        out_specs=[out_tok, out_tok, out_tok, out_tok,
                   pl.BlockSpec((None, None, tb, CHUNK), lambda i, h, j: (i, h, j, 0)),
                   wo_spec],
        out_shape=[jax.ShapeDtypeStruct((b, t, DN_WIDTH), F32),
                   jax.ShapeDtypeStruct((b, t, DN_WIDTH), BF16),
                   jax.ShapeDtypeStruct((b, t, DN_WIDTH), BF16),
                   jax.ShapeDtypeStruct((b, t, DN_WIDTH), BF16),
                   jax.ShapeDtypeStruct((b, nh, t, CHUNK), BF16),
                   jax.ShapeDtypeStruct(w_out.shape, BF16)],
        scratch_shapes=[pltpu.VMEM((3, tb + SUBLANES, DN_DIM), F32)],
        compiler_params=_params("parallel", "parallel", "arbitrary"),
        name="dn_intra",
    )(qkvz, qkvz, qkvz, qkvz, qkvz, qkvz, conv_w, conv_w, conv_w, gcol, grow, w_out)


def _dn_scan_kernel(u_ref, w_ref, qd_ref, kd_ref, a_ref, z_ref, gcol_ref, nw_ref, o_ref, s_ref):
    @pl.when(pl.program_id(1) == 0)
    def _():
        s_ref[...] = jnp.zeros_like(s_ref)

    bb, tb = u_ref.shape[0], u_ref.shape[1]
    nw = nw_ref[...]
    units = [(i, h) for i in range(bb) for h in range(DN_HEADS)]

    def chunk(c, carry):
        rows = pl.ds(pl.multiple_of(c * CHUNK, CHUNK), CHUNK)
        cols = [slice(h * DN_DIM, (h + 1) * DN_DIM) for h in range(DN_HEADS)]
        s = [s_ref[i, h] for i, h in units]
        proj = [jnp.dot(jnp.concatenate([w_ref[i, rows, cols[h]], qd_ref[i, rows, cols[h]]], axis=0),
                        s[n].astype(BF16), preferred_element_type=F32)
                for n, (i, h) in enumerate(units)]
        v16 = [(u_ref[i, rows, cols[h]] - proj[n][:CHUNK]).astype(BF16)
               for n, (i, h) in enumerate(units)]
        upd = [lax.dot_general(kd_ref[i, rows, cols[h]], v16[n], _TN, preferred_element_type=F32)
               for n, (i, h) in enumerate(units)]
        for n, (i, h) in enumerate(units):
            gl = gcol_ref[i, pl.ds(pl.multiple_of(c * CHUNK, CHUNK), 1),
                          2 * DN_HEADS + h:2 * DN_HEADS + h + 1]
            s_ref[i, h] = s[n] * jnp.exp(gl) + upd[n]
        o = [proj[n][CHUNK:] + jnp.dot(a_ref[i, h, rows, :], v16[n], preferred_element_type=F32)
             for n, (i, h) in enumerate(units)]
        for n, (i, h) in enumerate(units):
            z = z_ref[i, rows, cols[h]]
            o_ref[i, rows, cols[h]] = (_rmsnorm_rows(o[n], nw) * _silu(z)).astype(o_ref.dtype)
        return carry

    lax.fori_loop(0, tb // CHUNK, chunk, 0)


def _dn_scan(u, w, qd, kd, a, qkvz, gcol, dn_norm_w, bb, tb):
    b, t, _ = u.shape
    tok = pl.BlockSpec((bb, tb, DN_WIDTH), lambda i, j: (i, j, 0))
    z_block = 3 * DN_WIDTH // DN_WIDTH
    return pl.pallas_call(
        _dn_scan_kernel,
        grid=(b // bb, t // tb),
        in_specs=[tok, tok, tok, tok,
                  pl.BlockSpec((bb, DN_HEADS, tb, CHUNK), lambda i, j: (i, 0, j, 0)),
                  pl.BlockSpec((bb, tb, DN_WIDTH), lambda i, j: (i, j, z_block)),
                  pl.BlockSpec((bb, tb, LANES), lambda i, j: (i, j, 0)),
                  pl.BlockSpec((1, DN_DIM), lambda i, j: (0, 0))],
        out_specs=tok,
        out_shape=jax.ShapeDtypeStruct((b, t, DN_WIDTH), BF16),
        scratch_shapes=[pltpu.VMEM((bb, DN_HEADS, DN_DIM, DN_DIM), F32)],
        compiler_params=_params("parallel", "arbitrary"),
        name="dn_scan",
    )(u, w, qd, kd, a, qkvz, gcol, dn_norm_w)


def _attn_kernel(q_ref, k_ref, v_ref, lq1_ref, lk1_ref, lq2_ref, lk2_ref, nw_ref,
                 wg_ref, wu_ref, wd_ref, o_ref, wg16_ref, wu16_ref, wd16_ref,
                 m_ref, l_ref, acc_ref, sa_ref, sb_ref, *, lam_init, down_share):
    wg16_ref[...] = wg_ref[...].astype(BF16)
    wu16_ref[...] = wu_ref[...].astype(BF16)
    grid_step = ((pl.program_id(0) * pl.num_programs(1) + pl.program_id(1)) * pl.num_programs(2)
                 + pl.program_id(2))

    @pl.when(grid_step % down_share == 0)
    def _():
        wd16_ref[...] = wd_ref[...].astype(BF16)

    head = pl.program_id(1)
    qi = pl.program_id(2)
    tq = q_ref.shape[0]
    tk = tq
    slope = jnp.float32(0.0)
    for hh in range(DA_HEADS):
        slope = jnp.where(head == hh, jnp.float32(2.0 ** (-8.0 * (hh + 1) / DA_HEADS)), slope)

    kpos = lax.broadcasted_iota(jnp.int32, (1, tk), 1)
    slope2 = slope * LOG2E

    maps = range(2)

    def scores(j, s_ref):
        k0 = pl.multiple_of(j * tk, tk)
        bias = slope2 * (kpos + (j - qi) * tk).astype(F32)
        for c in maps:
            cols = slice(c * DA_DIM, (c + 1) * DA_DIM)
            s_ref[c] = lax.dot_general(q_ref[:, cols], k_ref[pl.ds(k0, tk), cols], _NT,
                                       preferred_element_type=F32) + bias

    def accumulate(j, s_ref, masked, first=False):
        k0 = pl.multiple_of(j * tk, tk)
        v = v_ref[pl.ds(k0, tk), :]
        s = [s_ref[c] for c in maps]
        if masked:
            rr = lax.broadcasted_iota(jnp.int32, (tq, tk), 0)
            cc = lax.broadcasted_iota(jnp.int32, (tq, tk), 1)
            s = [jnp.where(cc <= rr, s[c], NEG_BIG) for c in maps]
        m_cur = [jnp.broadcast_to(jnp.max(s[c], axis=-1, keepdims=True), (tq, LANES)) for c in maps]
        if first:
            m_new = m_cur
        else:
            m_prev = [m_ref[c] for c in maps]
            m_new = [jnp.maximum(m_prev[c], m_cur[c]) for c in maps]
        p = [jnp.exp2(s[c] - _lane_tile(m_new[c], tk // LANES)) for c in maps]
        p_sum = [jnp.broadcast_to(jnp.sum(p[c], axis=-1, keepdims=True), (tq, LANES)) for c in maps]
        p16 = [p[c].astype(BF16) for c in maps]
        pv = [jnp.dot(p16[c], v, preferred_element_type=F32) for c in maps]
        for c in maps:
            if first:
                l_ref[c] = p_sum[c]
                acc_ref[c] = pv[c]
            else:
                alpha = jnp.exp2(m_prev[c] - m_new[c])
                l_ref[c] = alpha * l_ref[c] + p_sum[c]
                acc_ref[c] = _lane_tile(alpha, DA_VDIM // LANES) * acc_ref[c] + pv[c]
            m_ref[c] = m_new[c]

    scores(0, sa_ref)

    @pl.when(qi == 0)
    def _():
        accumulate(0, sa_ref, True, first=True)

    @pl.when(qi > 0)
    def _():
        scores(1, sb_ref)
        accumulate(0, sa_ref, False, first=True)

        def pair(jj, carry):
            j = 2 * jj + 1
            scores(j + 1, sa_ref)
            accumulate(j, sb_ref, False)
            scores(j + 2, sb_ref)
            accumulate(j + 1, sa_ref, False)
            return carry

        lax.fori_loop(0, (qi - 1) // 2, pair, 0)

        @pl.when(qi % 2 == 0)
        def _():
            scores(qi, sa_ref)
            accumulate(qi - 1, sb_ref, False)
            accumulate(qi, sa_ref, True)

        @pl.when(qi % 2 == 1)
        def _():
            accumulate(qi, sb_ref, True)

    lam = (jnp.exp(jnp.sum(lq1_ref[...] * lk1_ref[...], axis=-1, keepdims=True))
           - jnp.exp(jnp.sum(lq2_ref[...] * lk2_ref[...], axis=-1, keepdims=True)) + lam_init)
    rep = DA_VDIM // LANES
    inv_l0 = 1.0 / l_ref[0]
    inv_l1 = lam / l_ref[1]
    o = acc_ref[0] * _lane_tile(inv_l0, rep) - acc_ref[1] * _lane_tile(inv_l1, rep)
    o_ref[...] = (_rmsnorm_rows(o, nw_ref[...]) * (1.0 - lam_init)).astype(o_ref.dtype)


def _slab_rows(n_rows, n_steps):
    for share in (1, 2, 4, 8, 16):
        if n_steps % share == 0 and n_rows % (n_steps // share) == 0:
            rows = n_rows // (n_steps // share)
            if rows % 16 == 0:
                return rows, share
    raise ValueError(f"cannot split {n_rows} weight rows over {n_steps} grid steps")


def _attention(da, lam_q1, lam_k1, lam_q2, lam_k2, da_norm_w, w_gate, w_up, w_down, lam_init, tq):
    b, t, _ = da.shape
    nh = DA_HEADS
    nq = t // tq
    d, dff = w_gate.shape
    steps = b * nh * nq
    rg, sg = _slab_rows(d, steps)
    rd, sd = _slab_rows(dff, steps)

    def step(i, h, j):
        return (i * nh + h) * nq + j

    vec = pl.BlockSpec((1, DA_DIM), lambda i, h, j: (0, 0))
    up_spec = pl.BlockSpec((rg, dff), lambda i, h, j: (step(i, h, j) // sg, 0))
    down_spec = pl.BlockSpec((rd, d), lambda i, h, j: (step(i, h, j) // sd, 0))
    return pl.pallas_call(
        functools.partial(_attn_kernel, lam_init=lam_init, down_share=sd),
        grid=(b, nh, nq),
        in_specs=[pl.BlockSpec((None, tq, DA_VDIM), lambda i, h, j: (i, j, h)),
                  pl.BlockSpec((None, t, DA_VDIM), lambda i, h, j: (i, 0, nh + h)),
                  pl.BlockSpec((None, t, DA_VDIM), lambda i, h, j: (i, 0, 2 * nh + h)),
                  vec, vec, vec, vec,
                  pl.BlockSpec((1, DA_VDIM), lambda i, h, j: (0, 0)),
                  up_spec, up_spec, down_spec],
        out_specs=[pl.BlockSpec((None, tq, DA_VDIM), lambda i, h, j: (i, j, h)),
                   up_spec, up_spec, down_spec],
        out_shape=[jax.ShapeDtypeStruct((b, t, DA_WIDTH), BF16),
                   jax.ShapeDtypeStruct((d, dff), BF16),
                   jax.ShapeDtypeStruct((d, dff), BF16),
                   jax.ShapeDtypeStruct((dff, d), BF16)],
        scratch_shapes=[pltpu.VMEM((2, tq, LANES), F32), pltpu.VMEM((2, tq, LANES), F32),
                        pltpu.VMEM((2, tq, DA_VDIM), F32),
                        pltpu.VMEM((2, tq, tq), F32), pltpu.VMEM((2, tq, tq), F32)],
        compiler_params=_params("arbitrary", "arbitrary", "arbitrary"),
        name="diff_attn",
    )(da, da, da, lam_q1, lam_k1, lam_q2, lam_k2, da_norm_w, w_gate, w_up, w_down)


def _out_proj_kernel(a_ref, b_ref, wa_ref, wb_ref, x_ref, o_ref):
    acc = jnp.dot(a_ref[...], wa_ref[...], preferred_element_type=F32)
    acc = acc + jnp.dot(b_ref[...], wb_ref[...], preferred_element_type=F32)
    o_ref[...] = x_ref[...] + acc


def _out_proj(o_dn, o_da, w16, x2d, tm, tn):
    m, d = x2d.shape
    ka, kb = o_dn.shape[1], o_da.shape[1]
    assert ka == kb and w16.shape[0] == ka + kb, "the two head groups are equal row halves of w_out"
    return pl.pallas_call(
        _out_proj_kernel,
        grid=(m // tm, d // tn),
        in_specs=[pl.BlockSpec((tm, ka), lambda i, j: (i, 0)),
                  pl.BlockSpec((tm, kb), lambda i, j: (i, 0)),
                  pl.BlockSpec((ka, tn), lambda i, j: (0, j)),
                  pl.BlockSpec((kb, tn), lambda i, j: (1, j)),
                  pl.BlockSpec((tm, tn), lambda i, j: (i, j))],
        out_specs=pl.BlockSpec((tm, tn), lambda i, j: (i, j)),
        out_shape=jax.ShapeDtypeStruct((m, d), F32),
        compiler_params=_params("parallel", "arbitrary"),
        name="out_proj",
    )(o_dn, o_da, w16, w16, x2d)


def _ffn_kernel(x_ref, nw_ref, wg_ref, wu_ref, wd_ref, fw_ref, o_ref, h_ref):
    f = pl.program_id(1)

    @pl.when(f == 0)
    def _():
        x = x_ref[...]
        h_ref[...] = _rmsnorm_rows(x, nw_ref[...]).astype(BF16)
        o_ref[...] = x

    h = h_ref[...]
    g = jnp.dot(h, wg_ref[...], preferred_element_type=F32)
    u = jnp.dot(h, wu_ref[...], preferred_element_type=F32)
    act = (_silu(g) * u).astype(BF16)
    o_ref[...] += jnp.dot(act, wd_ref[...], preferred_element_type=F32)

    @pl.when(f == pl.num_programs(1) - 1)
    def _():
        o_ref[...] = _rmsnorm_rows(o_ref[...], fw_ref[...])


def _ffn(x2d, ffn_norm_w, w_gate, w_up, w_down, final_norm_w, tm, tf):
    m, d = x2d.shape
    dff = w_gate.shape[1]
    return pl.pallas_call(
        _ffn_kernel,
        grid=(m // tm, dff // tf),
        in_specs=[pl.BlockSpec((tm, d), lambda i, f: (i, 0)),
                  pl.BlockSpec((1, d), lambda i, f: (0, 0)),
                  pl.BlockSpec((d, tf), lambda i, f: (0, f)),
                  pl.BlockSpec((d, tf), lambda i, f: (0, f)),
                  pl.BlockSpec((tf, d), lambda i, f: (f, 0)),
                  pl.BlockSpec((1, d), lambda i, f: (0, 0))],
        out_specs=pl.BlockSpec((tm, d), lambda i, f: (i, 0)),
        out_shape=jax.ShapeDtypeStruct((m, d), F32),
        scratch_shapes=[pltpu.VMEM((tm, d), BF16)],
        compiler_params=_params("parallel", "arbitrary"),
        name="ffn",
    )(x2d, ffn_norm_w, w_gate, w_up, w_down, final_norm_w)


def _tile(n, pref):
    return pref if n % pref == 0 else n


def _layer(x, lam_init, attn_norm_w, w_in, conv_w, a_log, dt_bias, dn_norm_w, lam_q1, lam_k1,
           lam_q2, lam_k2, da_norm_w, w_out, ffn_norm_w, w_gate, w_up, w_down, final_norm_w):
    b, t, d = x.shape
    m = b * t
    x2d = x.reshape(m, d)
    off_z = 3 * DN_WIDTH
    off_b = off_z + DN_WIDTH
    off_q = off_b + 2 * DN_HEADS
    nw = attn_norm_w.reshape(1, d)

    n_da = w_in.shape[1] - off_q
    da_scale = jnp.concatenate([jnp.full((1, DA_HEADS * 2 * DA_DIM), LOG2E * DA_DIM ** -0.5, F32),
                                jnp.ones((1, n_da - DA_HEADS * 2 * DA_DIM), F32)], axis=1)

    w_in_t = w_in.T
    grow, gcol, h, w_dn, w_da = _gates(x, nw, w_in_t, a_log.reshape(DN_HEADS, 1),
                                       dt_bias.reshape(DN_HEADS, 1), off_b, _tile(t, 256))
    h2d = h.reshape(m, d)
    tp = _tile(m, 2048)
    qkvz = _proj(h2d, w_dn, jnp.ones((1, off_b), F32), F32, tp, 1024, "in_proj_dn")
    da = _proj(h2d, w_da, da_scale, BF16, tp, 1024, "in_proj_da")
    tm = _tile(m, 1024)

    qkvz = qkvz.reshape(b, t, off_b)
    u, w, qd, kd, a, w_out16 = _dn_intra(qkvz, conv_w, gcol, grow, w_out, _tile(t, 4096))
    o_dn = _dn_scan(u, w, qd, kd, a, qkvz, gcol, dn_norm_w.reshape(1, DN_DIM),
                    4 if b % 4 == 0 else 1, _tile(t, 128))

    o_da, w_gate16, w_up16, w_down16 = _attention(
        da.reshape(b, t, n_da), lam_q1.reshape(1, DA_DIM), lam_k1.reshape(1, DA_DIM),
        lam_q2.reshape(1, DA_DIM), lam_k2.reshape(1, DA_DIM), da_norm_w.reshape(1, DA_VDIM),
        w_gate, w_up, w_down, lam_init, _tile(t, 512))

    x2d = _out_proj(o_dn.reshape(m, DN_WIDTH), o_da.reshape(m, DA_WIDTH), w_out16, x2d,
                    _tile(m, 512), d)
    dff = w_gate.shape[1]
    return _ffn(x2d, ffn_norm_w.reshape(1, d), w_gate16, w_up16, w_down16,
                final_norm_w.reshape(1, d), tm, _tile(dff, 512))


def kernel(x, attn_norm_w, w_in, conv_w, a_log, dt_bias, dn_norm_w, lam_q1, lam_k1, lam_q2, lam_k2,
           da_norm_w, w_out, ffn_norm_w, w_gate, w_up, w_down, final_norm_w):
    b, t, d = x.shape
    depth = w_in.shape[0]
    assert depth == 1, "the final rmsnorm is fused into the last layer's FFN kernel"
    lam_init = 0.8 - 0.6 * math.exp(-0.3 * 0)
    out = _layer(x, lam_init, attn_norm_w[0], w_in[0], conv_w[0], a_log[0], dt_bias[0],
                 dn_norm_w[0], lam_q1[0], lam_k1[0], lam_q2[0], lam_k2[0], da_norm_w[0],
                 w_out[0], ffn_norm_w[0], w_gate[0], w_up[0], w_down[0], final_norm_w)
    return out.reshape(b, t, d)
```

```python
import functools
import math

import jax
import jax.numpy as jnp
from jax import lax
from jax.experimental import pallas as pl
from jax.experimental.pallas import tpu as pltpu

F32 = jnp.float32
BF16 = jnp.bfloat16
EPS = 1e-6
NEG_BIG = -1e30
LOG2E = math.log2(math.e)

DN_HEADS = 8
DN_DIM = 128
DN_WIDTH = DN_HEADS * DN_DIM
CONV_WIDTH = 4
CHUNK = 64
DA_HEADS = 4
DA_DIM = 128
DA_VDIM = 2 * DA_DIM
DA_WIDTH = DA_HEADS * DA_VDIM
LANES = 128
SUBLANES = 8
GROUP = 256
IN_FLIGHT = 4
GATE_SLAB = 256
VMEM_LIMIT = 56 * 1024 * 1024

_NT = (((1,), (1,)), ((), ()))
_TN = (((0,), (0,)), ((), ()))


def _params(*sem):
    return pltpu.CompilerParams(dimension_semantics=sem, vmem_limit_bytes=VMEM_LIMIT)


def _sigmoid(x):
    return 1.0 / (1.0 + jnp.exp(-x))


def _silu(x):
    h = 0.5 * x
    return h + h * jnp.tanh(h)


def _lane_tile(x, n):
    return jnp.concatenate([x] * n, axis=1)


def _rmsnorm_rows(x, w):
    return x * lax.rsqrt(jnp.mean(x * x, axis=-1, keepdims=True) + EPS) * w


def _proj_kernel(h_ref, w_ref, cs_ref, o_ref):
    acc = lax.dot_general(h_ref[...], w_ref[...], _NT, preferred_element_type=F32)
    o_ref[...] = (acc * cs_ref[...]).astype(o_ref.dtype)


def _proj(h2d, w_t_bf16, col_scale, out_dtype, tm, tn, name):
    m, d = h2d.shape
    n = w_t_bf16.shape[0]
    return pl.pallas_call(
        _proj_kernel,
        grid=(m // tm, n // tn),
        in_specs=[
            pl.BlockSpec((tm, d), lambda i, j: (i, 0)),
            pl.BlockSpec((tn, d), lambda i, j: (j, 0)),
            pl.BlockSpec((1, tn), lambda i, j: (0, j)),
        ],
        out_specs=pl.BlockSpec((tm, tn), lambda i, j: (i, j)),
        out_shape=jax.ShapeDtypeStruct((m, n), out_dtype),
        compiler_params=_params("parallel", "arbitrary"),
        name=name,
    )(h2d, w_t_bf16, col_scale)


def _gates_kernel(x_ref, nw_ref, wt_ref, alog_ref, dtb_ref, wdn_ref, *rest):
    wda_refs, (row_ref, col_ref, h_ref, wdn16_ref, wda16_ref) = rest[:-5], rest[-5:]
    wdn16_ref[...] = wdn_ref[...].astype(BF16)
    wda16_ref[...] = jnp.concatenate([r[...] for r in wda_refs], axis=0).astype(BF16)

    h = _rmsnorm_rows(x_ref[...], nw_ref[...]).astype(BF16)
    h_ref[...] = h
    r = lax.dot_general(wt_ref[...].astype(BF16), h, _NT, preferred_element_type=F32)
    beta = _sigmoid(r[0:DN_HEADS])
    a = r[DN_HEADS:2 * DN_HEADS] + dtb_ref[...]
    softplus = jnp.maximum(a, 0.0) + jnp.log1p(jnp.exp(-jnp.abs(a)))
    g = -jnp.exp(alog_ref[...]) * softplus
    tm = g.shape[1]
    sw = min(tm, GATE_SLAB)
    ii = lax.broadcasted_iota(jnp.int32, (sw, sw), 0)
    jj = lax.broadcasted_iota(jnp.int32, (sw, sw), 1)
    same = (ii // CHUNK) == (jj // CHUNK)
    sums = jnp.concatenate([same & (ii <= jj), same], axis=1).astype(BF16)
    g1 = g.astype(BF16)
    r1 = g - g1.astype(F32)
    g2 = r1.astype(BF16)
    g3 = (r1 - g2.astype(F32)).astype(BF16)
    pieces = jnp.concatenate([g1, g2, g3, jnp.zeros_like(g1)], axis=0)
    nh = DN_HEADS
    gc, gl = [], []
    for s0 in range(0, tm, sw):
        acc = jnp.dot(pieces[:, s0:s0 + sw], sums, preferred_element_type=F32)
        acc = acc[0:nh] + acc[nh:2 * nh] + acc[2 * nh:3 * nh]
        gc.append(acc[:, :sw])
        gl.append(acc[:, sw:])
    gc, gl = jnp.concatenate(gc, axis=1), jnp.concatenate(gl, axis=1)
    rows = jnp.concatenate([beta, gc, gl], axis=0)
    row_ref[...] = rows
    padded = jnp.concatenate([rows, jnp.zeros((LANES - 3 * DN_HEADS, tm), F32)], axis=0)
    col_ref[...] = padded.T


def _gates(x3d, norm_w, w_in_t, a_log_col, dt_bias_col, off_gate, tm):
    b, t, d = x3d.shape
    nj = t // tm
    steps = b * nj
    piece = 2 * DN_HEADS
    off_da = off_gate + piece
    n_da = w_in_t.shape[0] - off_da
    rows_dn, rows_da = off_gate // steps, n_da // steps
    assert rows_dn * steps == off_gate and rows_dn % piece == 0
    assert rows_da * steps == n_da and rows_da % piece == 0 and off_da % piece == 0
    n_pieces = rows_da // piece

    def step(i, j):
        return i * nj + j

    def da_piece(k):
        return pl.BlockSpec((piece, d),
                            lambda i, j: (off_da // piece + step(i, j) * n_pieces + k, 0))

    return pl.pallas_call(
        _gates_kernel,
        grid=(b, nj),
        in_specs=[
            pl.BlockSpec((None, tm, d), lambda i, j: (i, j, 0)),
            pl.BlockSpec((1, d), lambda i, j: (0, 0)),
            pl.BlockSpec((piece, d), lambda i, j: (off_gate // piece, 0)),
            pl.BlockSpec((DN_HEADS, 1), lambda i, j: (0, 0)),
            pl.BlockSpec((DN_HEADS, 1), lambda i, j: (0, 0)),
            pl.BlockSpec((rows_dn, d), lambda i, j: (step(i, j), 0)),
        ] + [da_piece(k) for k in range(n_pieces)],
        out_specs=[
            pl.BlockSpec((None, 3 * DN_HEADS, tm), lambda i, j: (i, 0, j)),
            pl.BlockSpec((None, tm, LANES), lambda i, j: (i, j, 0)),
            pl.BlockSpec((None, tm, d), lambda i, j: (i, j, 0)),
            pl.BlockSpec((rows_dn, d), lambda i, j: (step(i, j), 0)),
            pl.BlockSpec((rows_da, d), lambda i, j: (step(i, j), 0)),
        ],
        out_shape=[
            jax.ShapeDtypeStruct((b, 3 * DN_HEADS, t), F32),
            jax.ShapeDtypeStruct((b, t, LANES), F32),
            jax.ShapeDtypeStruct((b, t, d), BF16),
            jax.ShapeDtypeStruct((off_gate, d), BF16),
            jax.ShapeDtypeStruct((n_da, d), BF16),
        ],
        compiler_params=_params("parallel", "arbitrary"),
        name="dn_gates",
    )(x3d, norm_w, w_in_t, a_log_col, dt_bias_col, w_in_t, *([w_in_t] * n_pieces))


def _dn_intra_kernel(q_ref, k_ref, v_ref, qh_ref, kh_ref, vh_ref, cwq_ref, cwk_ref, cwv_ref,
                     gcol_ref, grow_ref, wo_ref, u_ref, w_ref, qd_ref, kd_ref, a_ref, wo16_ref, xs_ref):
    wo16_ref[...] = wo_ref[...].astype(BF16)
    head = pl.program_id(1)
    first = pl.program_id(2) == 0
    tb = q_ref.shape[0]
    halo = SUBLANES
    for idx, (x_ref, xh_ref) in enumerate(((q_ref, qh_ref), (k_ref, kh_ref), (v_ref, vh_ref))):
        xs_ref[idx, 0:halo, :] = jnp.where(first, 0.0, xh_ref[...])
        xs_ref[idx, halo:halo + tb, :] = x_ref[...]

    lane = lax.broadcasted_iota(jnp.int32, (GROUP, LANES), 1)
    sel_beta = (lane == head).astype(F32)
    sel_gc = (lane == head + DN_HEADS).astype(F32)
    sel_gl = (lane == head + 2 * DN_HEADS).astype(F32)
    ri = lax.broadcasted_iota(jnp.int32, (GROUP, GROUP), 0)
    ci = lax.broadcasted_iota(jnp.int32, (GROUP, GROUP), 1)
    same = (ri // CHUNK) == (ci // CHUNK)
    incl = same & (ri >= ci)
    strict = same & (ri > ci)

    def conv_silu(idx, cw_ref, r0, anchor):
        base = r0 + halo - (CONV_WIDTH - 1)
        taps = cw_ref[...] if anchor is None else cw_ref[...] + anchor
        acc = xs_ref[idx, base:base + GROUP, :] * taps[0:1, :]
        for i in range(1, CONV_WIDTH):
            acc = acc + xs_ref[idx, base + i:base + i + GROUP, :] * taps[i:i + 1, :]
        return _silu(acc)

    n_groups = tb // GROUP
    levels = int(math.log2(CHUNK))
    n_pows = [None] * n_groups
    xs = [None] * n_groups

    def prepare(gi):
        r0 = gi * GROUP
        anchor = None if gi < IN_FLIGHT else xs[gi - IN_FLIGHT][0:1, :DN_DIM] * 0.0
        q = conv_silu(0, cwq_ref, r0, anchor)
        k = conv_silu(1, cwk_ref, r0, anchor)
        v = conv_silu(2, cwv_ref, r0, anchor)
        q = q * lax.rsqrt(jnp.sum(q * q, axis=-1, keepdims=True) + EPS) * (DN_DIM ** -0.5)
        k = k * lax.rsqrt(jnp.sum(k * k, axis=-1, keepdims=True) + EPS)

        gates = gcol_ref[r0:r0 + GROUP, :]
        beta = jnp.sum(gates * sel_beta, axis=-1, keepdims=True)
        gc = jnp.sum(gates * sel_gc, axis=-1, keepdims=True)
        gl = jnp.sum(gates * sel_gl, axis=-1, keepdims=True)
        gc_row = grow_ref[pl.ds(head + DN_HEADS, 1), r0:r0 + GROUP]

        decay = jnp.exp(jnp.where(incl, gc - gc_row, NEG_BIG))
        kb = k * beta
        k16 = k.astype(BF16)
        gram = lax.dot_general(jnp.concatenate([kb, q], axis=0).astype(BF16), k16, _NT,
                               preferred_element_type=F32)
        n_pows[gi] = jnp.where(strict, gram[:GROUP] * decay, 0.0)
        a_qk = gram[GROUP:] * decay

        egc = jnp.exp(gc)
        xs[gi] = jnp.concatenate([v * beta, kb * egc], axis=1)
        qd_ref[r0:r0 + GROUP, :] = (q * egc).astype(BF16)
        kd_ref[r0:r0 + GROUP, :] = (k * jnp.exp(gl - gc)).astype(BF16)
        for c in range(GROUP // CHUNK):
            lo = c * CHUNK
            a_ref[r0 + lo:r0 + lo + CHUNK, :] = a_qk[lo:lo + CHUNK, lo:lo + CHUNK].astype(BF16)

    def level(gi, lvl):
        n16 = n_pows[gi].astype(BF16)
        step = jnp.dot(n16, xs[gi].astype(BF16), preferred_element_type=F32)
        xs[gi] = xs[gi] - step if lvl == 0 else xs[gi] + step
        if lvl + 1 < levels:
            n_pows[gi] = jnp.dot(n16, n16, preferred_element_type=F32)

    def finish(gi):
        r0 = gi * GROUP
        u_ref[r0:r0 + GROUP, :] = xs[gi][:, :DN_DIM]
        w_ref[r0:r0 + GROUP, :] = xs[gi][:, DN_DIM:].astype(BF16)

    for t in range(n_groups + levels + 1):
        for gi in range(n_groups):
            stage = t - gi
            if stage == 0:
                prepare(gi)
            elif 1 <= stage <= levels:
                level(gi, stage - 1)
            elif stage == levels + 1:
                finish(gi)


def _dn_intra(qkvz, conv_w, gcol, grow, w_out, tb):
    b, t, _ = qkvz.shape
    hb = tb // SUBLANES
    nh = DN_HEADS
    nj = t // tb
    wo_rows, wo_share = _slab_rows(w_out.shape[0], b * nh * nj)
    assert wo_share == 1, "every grid step converts its own slab of w_out"
    wo_spec = pl.BlockSpec((wo_rows, w_out.shape[1]), lambda i, h, j: ((i * nh + h) * nj + j, 0))

    def tok(off):
        return pl.BlockSpec((None, tb, DN_DIM), lambda i, h, j: (i, j, off + h))

    def halo(off):
        return pl.BlockSpec((None, SUBLANES, DN_DIM),
                            lambda i, h, j: (i, jnp.maximum(j * hb - 1, 0), off + h))

    def cw(off):
        return pl.BlockSpec((CONV_WIDTH, DN_DIM), lambda i, h, j: (0, off + h))

    out_tok = pl.BlockSpec((None, tb, DN_DIM), lambda i, h, j: (i, j, h))
    return pl.pallas_call(
        _dn_intra_kernel,
        grid=(b, nh, nj),
        in_specs=[tok(0), tok(nh), tok(2 * nh), halo(0), halo(nh), halo(2 * nh),
                  cw(0), cw(nh), cw(2 * nh),
                  pl.BlockSpec((None, tb, LANES), lambda i, h, j: (i, j, 0)),
                  pl.BlockSpec((None, 3 * nh, tb), lambda i, h, j: (i, 0, j)),
                  wo_spec],
        out_specs=[out_tok, out_tok, out_tok, out_tok,
                   pl.BlockSpec((None, None, tb, CHUNK), lambda i, h, j: (i, h, j, 0)),
                   wo_spec],
        out_shape=[jax.ShapeDtypeStruct((b, t, DN_WIDTH), F32),
                   jax.ShapeDtypeStruct((b, t, DN_WIDTH), BF16),
                   jax.ShapeDtypeStruct((b, t, DN_WIDTH), BF16),
                   jax.ShapeDtypeStruct((b, t, DN_WIDTH), BF16),
                   jax.ShapeDtypeStruct((b, nh, t, CHUNK), BF16),
                   jax.ShapeDtypeStruct(w_out.shape, BF16)],
        scratch_shapes=[pltpu.VMEM((3, tb + SUBLANES, DN_DIM), F32)],
        compiler_params=_params("parallel", "parallel", "arbitrary"),
        name="dn_intra",
    )(qkvz, qkvz, qkvz, qkvz, qkvz, qkvz, conv_w, conv_w, conv_w, gcol, grow, w_out)


def _dn_scan_kernel(u_ref, w_ref, qd_ref, kd_ref, a_ref, z_ref, gcol_ref, nw_ref, o_ref, s_ref):
    @pl.when(pl.program_id(1) == 0)
    def _():
        s_ref[...] = jnp.zeros_like(s_ref)

    bb, tb = u_ref.shape[0], u_ref.shape[1]
    nw = nw_ref[...]
    units = [(i, h) for i in range(bb) for h in range(DN_HEADS)]

    def chunk(c, carry):
        rows = pl.ds(pl.multiple_of(c * CHUNK, CHUNK), CHUNK)
        cols = [slice(h * DN_DIM, (h + 1) * DN_DIM) for h in range(DN_HEADS)]
        s = [s_ref[i, h] for i, h in units]
        proj = [jnp.dot(jnp.concatenate([w_ref[i, rows, cols[h]], qd_ref[i, rows, cols[h]]], axis=0),
                        s[n].astype(BF16), preferred_element_type=F32)
                for n, (i, h) in enumerate(units)]
        v16 = [(u_ref[i, rows, cols[h]] - proj[n][:CHUNK]).astype(BF16)
               for n, (i, h) in enumerate(units)]
        upd = [lax.dot_general(kd_ref[i, rows, cols[h]], v16[n], _TN, preferred_element_type=F32)
               for n, (i, h) in enumerate(units)]
        for n, (i, h) in enumerate(units):
            gl = gcol_ref[i, pl.ds(pl.multiple_of(c * CHUNK, CHUNK), 1),
                          2 * DN_HEADS + h:2 * DN_HEADS + h + 1]
            s_ref[i, h] = s[n] * jnp.exp(gl) + upd[n]
        o = [proj[n][CHUNK:] + jnp.dot(a_ref[i, h, rows, :], v16[n], preferred_element_type=F32)
             for n, (i, h) in enumerate(units)]
        for n, (i, h) in enumerate(units):
            z = z_ref[i, rows, cols[h]]
            o_ref[i, rows, cols[h]] = (_rmsnorm_rows(o[n], nw) * _silu(z)).astype(o_ref.dtype)
        return carry

    lax.fori_loop(0, tb // CHUNK, chunk, 0)


def _dn_scan(u, w, qd, kd, a, qkvz, gcol, dn_norm_w, bb, tb):
    b, t, _ = u.shape
    tok = pl.BlockSpec((bb, tb, DN_WIDTH), lambda i, j: (i, j, 0))
    z_block = 3 * DN_WIDTH // DN_WIDTH
    return pl.pallas_call(
        _dn_scan_kernel,
        grid=(b // bb, t // tb),
        in_specs=[tok, tok, tok, tok,
                  pl.BlockSpec((bb, DN_HEADS, tb, CHUNK), lambda i, j: (i, 0, j, 0)),
                  pl.BlockSpec((bb, tb, DN_WIDTH), lambda i, j: (i, j, z_block)),
                  pl.BlockSpec((bb, tb, LANES), lambda i, j: (i, j, 0)),
                  pl.BlockSpec((1, DN_DIM), lambda i, j: (0, 0))],
        out_specs=tok,
        out_shape=jax.ShapeDtypeStruct((b, t, DN_WIDTH), BF16),
        scratch_shapes=[pltpu.VMEM((bb, DN_HEADS, DN_DIM, DN_DIM), F32)],
        compiler_params=_params("parallel", "arbitrary"),
        name="dn_scan",
    )(u, w, qd, kd, a, qkvz, gcol, dn_norm_w)


def _attn_kernel(q_ref, k_ref, v_ref, lq1_ref, lk1_ref, lq2_ref, lk2_ref, nw_ref,
                 wg_ref, wu_ref, wd_ref, o_ref, wg16_ref, wu16_ref, wd16_ref,
                 m_ref, l_ref, acc_ref, sa_ref, sb_ref, *, lam_init, down_share):
    wg16_ref[...] = wg_ref[...].astype(BF16)
    wu16_ref[...] = wu_ref[...].astype(BF16)
    grid_step = ((pl.program_id(0) * pl.num_programs(1) + pl.program_id(1)) * pl.num_programs(2)
                 + pl.program_id(2))

    @pl.when(grid_step % down_share == 0)
    def _():
        wd16_ref[...] = wd_ref[...].astype(BF16)

    head = pl.program_id(1)
    qi = pl.program_id(2)
    tq = q_ref.shape[0]
    tk = tq
    slope = jnp.float32(0.0)
    for hh in range(DA_HEADS):
        slope = jnp.where(head == hh, jnp.float32(2.0 ** (-8.0 * (hh + 1) / DA_HEADS)), slope)

    kpos = lax.broadcasted_iota(jnp.int32, (1, tk), 1)
    slope2 = slope * LOG2E

    maps = range(2)

    def scores(j, s_ref):
        k0 = pl.multiple_of(j * tk, tk)
        bias = slope2 * (kpos + (j - qi) * tk).astype(F32)
        for c in maps:
            cols = slice(c * DA_DIM, (c + 1) * DA_DIM)
            s_ref[c] = lax.dot_general(q_ref[:, cols], k_ref[pl.ds(k0, tk), cols], _NT,
                                       preferred_element_type=F32) + bias

    def accumulate(j, s_ref, masked, first=False):
        k0 = pl.multiple_of(j * tk, tk)
        v = v_ref[pl.ds(k0, tk), :]
        s = [s_ref[c] for c in maps]
        if masked:
            rr = lax.broadcasted_iota(jnp.int32, (tq, tk), 0)
            cc = lax.broadcasted_iota(jnp.int32, (tq, tk), 1)
            s = [jnp.where(cc <= rr, s[c], NEG_BIG) for c in maps]
        m_cur = [jnp.broadcast_to(jnp.max(s[c], axis=-1, keepdims=True), (tq, LANES)) for c in maps]
        if first:
            m_new = m_cur
        else:
            m_prev = [m_ref[c] for c in maps]
            m_new = [jnp.maximum(m_prev[c], m_cur[c]) for c in maps]
        p = [jnp.exp2(s[c] - _lane_tile(m_new[c], tk // LANES)) for c in maps]
        p_sum = [jnp.broadcast_to(jnp.sum(p[c], axis=-1, keepdims=True), (tq, LANES)) for c in maps]
        p16 = [p[c].astype(BF16) for c in maps]
        pv = [jnp.dot(p16[c], v, preferred_element_type=F32) for c in maps]
        for c in maps:
            if first:
                l_ref[c] = p_sum[c]
                acc_ref[c] = pv[c]
            else:
                alpha = jnp.exp2(m_prev[c] - m_new[c])
                l_ref[c] = alpha * l_ref[c] + p_sum[c]
                acc_ref[c] = _lane_tile(alpha, DA_VDIM // LANES) * acc_ref[c] + pv[c]
            m_ref[c] = m_new[c]

    scores(0, sa_ref)

    @pl.when(qi == 0)
    def _():
        accumulate(0, sa_ref, True, first=True)

    @pl.when(qi > 0)
    def _():
        scores(1, sb_ref)
        accumulate(0, sa_ref, False, first=True)

        def pair(jj, carry):
            j = 2 * jj + 1
            scores(j + 1, sa_ref)
            accumulate(j, sb_ref, False)
            scores(j + 2, sb_ref)
            accumulate(j + 1, sa_ref, False)
            return carry

        lax.fori_loop(0, (qi - 1) // 2, pair, 0)

        @pl.when(qi % 2 == 0)
        def _():
            scores(qi, sa_ref)
            accumulate(qi - 1, sb_ref, False)
            accumulate(qi, sa_ref, True)

        @pl.when(qi % 2 == 1)
        def _():
            accumulate(qi, sb_ref, True)

    lam = (jnp.exp(jnp.sum(lq1_ref[...] * lk1_ref[...], axis=-1, keepdims=True))
           - jnp.exp(jnp.sum(lq2_ref[...] * lk2_ref[...], axis=-1, keepdims=True)) + lam_init)
    rep = DA_VDIM // LANES
    inv_l0 = 1.0 / l_ref[0]
    inv_l1 = lam / l_ref[1]
    o = acc_ref[0] * _lane_tile(inv_l0, rep) - acc_ref[1] * _lane_tile(inv_l1, rep)
    o_ref[...] = (_rmsnorm_rows(o, nw_ref[...]) * (1.0 - lam_init)).astype(o_ref.dtype)


def _slab_rows(n_rows, n_steps):
    for share in (1, 2, 4, 8, 16):
        if n_steps % share == 0 and n_rows % (n_steps // share) == 0:
            rows = n_rows // (n_steps // share)
            if rows % 16 == 0:
                return rows, share
    raise ValueError(f"cannot split {n_rows} weight rows over {n_steps} grid steps")


def _attention(da, lam_q1, lam_k1, lam_q2, lam_k2, da_norm_w, w_gate, w_up, w_down, lam_init, tq):
    b, t, _ = da.shape
    nh = DA_HEADS
    nq = t // tq
    d, dff = w_gate.shape
    steps = b * nh * nq
    rg, sg = _slab_rows(d, steps)
    rd, sd = _slab_rows(dff, steps)

    def step(i, h, j):
        return (i * nh + h) * nq + j

    vec = pl.BlockSpec((1, DA_DIM), lambda i, h, j: (0, 0))
    up_spec = pl.BlockSpec((rg, dff), lambda i, h, j: (step(i, h, j) // sg, 0))
    down_spec = pl.BlockSpec((rd, d), lambda i, h, j: (step(i, h, j) // sd, 0))
    return pl.pallas_call(
        functools.partial(_attn_kernel, lam_init=lam_init, down_share=sd),
        grid=(b, nh, nq),
        in_specs=[pl.BlockSpec((None, tq, DA_VDIM), lambda i, h, j: (i, j, h)),
                  pl.BlockSpec((None, t, DA_VDIM), lambda i, h, j: (i, 0, nh + h)),
                  pl.BlockSpec((None, t, DA_VDIM), lambda i, h, j: (i, 0, 2 * nh + h)),
                  vec, vec, vec, vec,
                  pl.BlockSpec((1, DA_VDIM), lambda i, h, j: (0, 0)),
                  up_spec, up_spec, down_spec],
        out_specs=[pl.BlockSpec((None, tq, DA_VDIM), lambda i, h, j: (i, j, h)),
                   up_spec, up_spec, down_spec],
        out_shape=[jax.ShapeDtypeStruct((b, t, DA_WIDTH), BF16),
                   jax.ShapeDtypeStruct((d, dff), BF16),
                   jax.ShapeDtypeStruct((d, dff), BF16),
                   jax.ShapeDtypeStruct((dff, d), BF16)],
        scratch_shapes=[pltpu.VMEM((2, tq, LANES), F32), pltpu.VMEM((2, tq, LANES), F32),
                        pltpu.VMEM((2, tq, DA_VDIM), F32),
                        pltpu.VMEM((2, tq, tq), F32), pltpu.VMEM((2, tq, tq), F32)],
        compiler_params=_params("arbitrary", "arbitrary", "arbitrary"),
        name="diff_attn",
    )(da, da, da, lam_q1, lam_k1, lam_q2, lam_k2, da_norm_w, w_gate, w_up, w_down)


def _out_proj_kernel(a_ref, b_ref, wa_ref, wb_ref, x_ref, o_ref):
    acc = jnp.dot(a_ref[...], wa_ref[...], preferred_element_type=F32)
    acc = acc + jnp.dot(b_ref[...], wb_ref[...], preferred_element_type=F32)
    o_ref[...] = x_ref[...] + acc


def _out_proj(o_dn, o_da, w16, x2d, tm, tn):
    m, d = x2d.shape
    ka, kb = o_dn.shape[1], o_da.shape[1]
    assert ka == kb and w16.shape[0] == ka + kb, "the two head groups are equal row halves of w_out"
    return pl.pallas_call(
        _out_proj_kernel,
        grid=(m // tm, d // tn),
        in_specs=[pl.BlockSpec((tm, ka), lambda i, j: (i, 0)),
                  pl.BlockSpec((tm, kb), lambda i, j: (i, 0)),
                  pl.BlockSpec((ka, tn), lambda i, j: (0, j)),
                  pl.BlockSpec((kb, tn), lambda i, j: (1, j)),
                  pl.BlockSpec((tm, tn), lambda i, j: (i, j))],
        out_specs=pl.BlockSpec((tm, tn), lambda i, j: (i, j)),
        out_shape=jax.ShapeDtypeStruct((m, d), F32),
        compiler_params=_params("parallel", "arbitrary"),
        name="out_proj",
    )(o_dn, o_da, w16, w16, x2d)


def _ffn_kernel(x_ref, nw_ref, wg_ref, wu_ref, wd_ref, fw_ref, o_ref, h_ref):
    f = pl.program_id(1)

    @pl.when(f == 0)
    def _():
        x = x_ref[...]
        h_ref[...] = _rmsnorm_rows(x, nw_ref[...]).astype(BF16)
        o_ref[...] = x

    h = h_ref[...]
    g = jnp.dot(h, wg_ref[...], preferred_element_type=F32)
    u = jnp.dot(h, wu_ref[...], preferred_element_type=F32)
    act = (_silu(g) * u).astype(BF16)
    o_ref[...] += jnp.dot(act, wd_ref[...], preferred_element_type=F32)

    @pl.when(f == pl.num_programs(1) - 1)
    def _():
        o_ref[...] = _rmsnorm_rows(o_ref[...], fw_ref[...])


def _ffn(x2d, ffn_norm_w, w_gate, w_up, w_down, final_norm_w, tm, tf):
    m, d = x2d.shape
    dff = w_gate.shape[1]
    return pl.pallas_call(
        _ffn_kernel,
        grid=(m // tm, dff // tf),
        in_specs=[pl.BlockSpec((tm, d), lambda i, f: (i, 0)),
                  pl.BlockSpec((1, d), lambda i, f: (0, 0)),
                  pl.BlockSpec((d, tf), lambda i, f: (0, f)),
                  pl.BlockSpec((d, tf), lambda i, f: (0, f)),
                  pl.BlockSpec((tf, d), lambda i, f: (f, 0)),
                  pl.BlockSpec((1, d), lambda i, f: (0, 0))],
        out_specs=pl.BlockSpec((tm, d), lambda i, f: (i, 0)),
        out_shape=jax.ShapeDtypeStruct((m, d), F32),
        scratch_shapes=[pltpu.VMEM((tm, d), BF16)],
        compiler_params=_params("parallel", "arbitrary"),
        name="ffn",
    )(x2d, ffn_norm_w, w_gate, w_up, w_down, final_norm_w)


def _tile(n, pref):
    return pref if n % pref == 0 else n


def _layer(x, lam_init, attn_norm_w, w_in, conv_w, a_log, dt_bias, dn_norm_w, lam_q1, lam_k1,
           lam_q2, lam_k2, da_norm_w, w_out, ffn_norm_w, w_gate, w_up, w_down, final_norm_w):
    b, t, d = x.shape
    m = b * t
    x2d = x.reshape(m, d)
    off_z = 3 * DN_WIDTH
    off_b = off_z + DN_WIDTH
    off_q = off_b + 2 * DN_HEADS
    nw = attn_norm_w.reshape(1, d)

    n_da = w_in.shape[1] - off_q
    da_scale = jnp.concatenate([jnp.full((1, DA_HEADS * 2 * DA_DIM), LOG2E * DA_DIM ** -0.5, F32),
                                jnp.ones((1, n_da - DA_HEADS * 2 * DA_DIM), F32)], axis=1)

    w_in_t = w_in.T
    grow, gcol, h, w_dn, w_da = _gates(x, nw, w_in_t, a_log.reshape(DN_HEADS, 1),
                                       dt_bias.reshape(DN_HEADS, 1), off_b, _tile(t, 512))
    h2d = h.reshape(m, d)
    tp = _tile(m, 2048)
    qkvz = _proj(h2d, w_dn, jnp.ones((1, off_b), F32), F32, tp, 1024, "in_proj_dn")
    da = _proj(h2d, w_da, da_scale, BF16, tp, 1024, "in_proj_da")
    tm = _tile(m, 1024)

    qkvz = qkvz.reshape(b, t, off_b)
    u, w, qd, kd, a, w_out16 = _dn_intra(qkvz, conv_w, gcol, grow, w_out, _tile(t, 4096))
    o_dn = _dn_scan(u, w, qd, kd, a, qkvz, gcol, dn_norm_w.reshape(1, DN_DIM),
                    4 if b % 4 == 0 else 1, _tile(t, 128))

    o_da, w_gate16, w_up16, w_down16 = _attention(
        da.reshape(b, t, n_da), lam_q1.reshape(1, DA_DIM), lam_k1.reshape(1, DA_DIM),
        lam_q2.reshape(1, DA_DIM), lam_k2.reshape(1, DA_DIM), da_norm_w.reshape(1, DA_VDIM),
        w_gate, w_up, w_down, lam_init, _tile(t, 512))

    x2d = _out_proj(o_dn.reshape(m, DN_WIDTH), o_da.reshape(m, DA_WIDTH), w_out16, x2d,
                    _tile(m, 512), d)
    dff = w_gate.shape[1]
    return _ffn(x2d, ffn_norm_w.reshape(1, d), w_gate16, w_up16, w_down16,
                final_norm_w.reshape(1, d), tm, _tile(dff, 512))


def kernel(x, attn_norm_w, w_in, conv_w, a_log, dt_bias, dn_norm_w, lam_q1, lam_k1, lam_q2, lam_k2,
           da_norm_w, w_out, ffn_norm_w, w_gate, w_up, w_down, final_norm_w):
    b, t, d = x.shape
    depth = w_in.shape[0]
    assert depth == 1, "the final rmsnorm is fused into the last layer's FFN kernel"
    lam_init = 0.8 - 0.6 * math.exp(-0.3 * 0)
    out = _layer(x, lam_init, attn_norm_w[0], w_in[0], conv_w[0], a_log[0], dt_bias[0],
                 dn_norm_w[0], lam_q1[0], lam_k1[0], lam_q2[0], lam_k2[0], da_norm_w[0],
                 w_out[0], ffn_norm_w[0], w_gate[0], w_up[0], w_down[0], final_norm_w)
    return out.reshape(b, t, d)
```

```python
import functools
import math

import jax
import jax.numpy as jnp
from jax import lax
from jax.experimental import pallas as pl
from jax.experimental.pallas import tpu as pltpu

F32 = jnp.float32
BF16 = jnp.bfloat16
EPS = 1e-6
NEG_BIG = -1e30
LOG2E = math.log2(math.e)

DN_HEADS = 8
DN_DIM = 128
DN_WIDTH = DN_HEADS * DN_DIM
CONV_WIDTH = 4
CHUNK = 64
DA_HEADS = 4
DA_DIM = 128
DA_VDIM = 2 * DA_DIM
DA_WIDTH = DA_HEADS * DA_VDIM
LANES = 128
SUBLANES = 8
GROUP = 128
IN_FLIGHT = 8
GATE_SLAB = 256
VMEM_LIMIT = 56 * 1024 * 1024

_NT = (((1,), (1,)), ((), ()))
_TN = (((0,), (0,)), ((), ()))


def _params(*sem):
    return pltpu.CompilerParams(dimension_semantics=sem, vmem_limit_bytes=VMEM_LIMIT)


def _sigmoid(x):
    return 1.0 / (1.0 + jnp.exp(-x))


def _silu(x):
    h = 0.5 * x
    return h + h * jnp.tanh(h)


def _lane_tile(x, n):
    return jnp.concatenate([x] * n, axis=1)


def _rmsnorm_rows(x, w):
    return x * lax.rsqrt(jnp.mean(x * x, axis=-1, keepdims=True) + EPS) * w


def _proj_kernel(h_ref, w_ref, cs_ref, o_ref):
    acc = lax.dot_general(h_ref[...], w_ref[...], _NT, preferred_element_type=F32)
    o_ref[...] = (acc * cs_ref[...]).astype(o_ref.dtype)


def _proj(h2d, w_t_bf16, col_scale, out_dtype, tm, tn, name):
    m, d = h2d.shape
    n = w_t_bf16.shape[0]
    return pl.pallas_call(
        _proj_kernel,
        grid=(m // tm, n // tn),
        in_specs=[
            pl.BlockSpec((tm, d), lambda i, j: (i, 0)),
            pl.BlockSpec((tn, d), lambda i, j: (j, 0)),
            pl.BlockSpec((1, tn), lambda i, j: (0, j)),
        ],
        out_specs=pl.BlockSpec((tm, tn), lambda i, j: (i, j)),
        out_shape=jax.ShapeDtypeStruct((m, n), out_dtype),
        compiler_params=_params("parallel", "arbitrary"),
        name=name,
    )(h2d, w_t_bf16, col_scale)


def _gates_kernel(x_ref, nw_ref, wt_ref, alog_ref, dtb_ref, wdn_ref, *rest):
    wda_refs, (row_ref, col_ref, h_ref, wdn16_ref, wda16_ref) = rest[:-5], rest[-5:]
    wdn16_ref[...] = wdn_ref[...].astype(BF16)
    wda16_ref[...] = jnp.concatenate([r[...] for r in wda_refs], axis=0).astype(BF16)

    h = _rmsnorm_rows(x_ref[...], nw_ref[...]).astype(BF16)
    h_ref[...] = h
    r = lax.dot_general(wt_ref[...].astype(BF16), h, _NT, preferred_element_type=F32)
    beta = _sigmoid(r[0:DN_HEADS])
    a = r[DN_HEADS:2 * DN_HEADS] + dtb_ref[...]
    softplus = jnp.maximum(a, 0.0) + jnp.log1p(jnp.exp(-jnp.abs(a)))
    g = -jnp.exp(alog_ref[...]) * softplus
    tm = g.shape[1]
    sw = min(tm, GATE_SLAB)
    ii = lax.broadcasted_iota(jnp.int32, (sw, sw), 0)
    jj = lax.broadcasted_iota(jnp.int32, (sw, sw), 1)
    same = (ii // CHUNK) == (jj // CHUNK)
    sums = jnp.concatenate([same & (ii <= jj), same], axis=1).astype(BF16)
    g1 = g.astype(BF16)
    r1 = g - g1.astype(F32)
    g2 = r1.astype(BF16)
    g3 = (r1 - g2.astype(F32)).astype(BF16)
    pieces = jnp.concatenate([g1, g2, g3, jnp.zeros_like(g1)], axis=0)
    nh = DN_HEADS
    gc, gl = [], []
    for s0 in range(0, tm, sw):
        acc = jnp.dot(pieces[:, s0:s0 + sw], sums, preferred_element_type=F32)
        acc = acc[0:nh] + acc[nh:2 * nh] + acc[2 * nh:3 * nh]
        gc.append(acc[:, :sw])
        gl.append(acc[:, sw:])
    gc, gl = jnp.concatenate(gc, axis=1), jnp.concatenate(gl, axis=1)
    rows = jnp.concatenate([beta, gc, gl], axis=0)
    row_ref[...] = rows
    padded = jnp.concatenate([rows, jnp.zeros((LANES - 3 * DN_HEADS, tm), F32)], axis=0)
    col_ref[...] = padded.T


def _gates(x3d, norm_w, w_in_t, a_log_col, dt_bias_col, off_gate, tm):
    b, t, d = x3d.shape
    nj = t // tm
    steps = b * nj
    piece = 2 * DN_HEADS
    off_da = off_gate + piece
    n_da = w_in_t.shape[0] - off_da
    rows_dn, rows_da = off_gate // steps, n_da // steps
    assert rows_dn * steps == off_gate and rows_dn % piece == 0
    assert rows_da * steps == n_da and rows_da % piece == 0 and off_da % piece == 0
    n_pieces = rows_da // piece

    def step(i, j):
        return i * nj + j

    def da_piece(k):
        return pl.BlockSpec((piece, d),
                            lambda i, j: (off_da // piece + step(i, j) * n_pieces + k, 0))

    return pl.pallas_call(
        _gates_kernel,
        grid=(b, nj),
        in_specs=[
            pl.BlockSpec((None, tm, d), lambda i, j: (i, j, 0)),
            pl.BlockSpec((1, d), lambda i, j: (0, 0)),
            pl.BlockSpec((piece, d), lambda i, j: (off_gate // piece, 0)),
            pl.BlockSpec((DN_HEADS, 1), lambda i, j: (0, 0)),
            pl.BlockSpec((DN_HEADS, 1), lambda i, j: (0, 0)),
            pl.BlockSpec((rows_dn, d), lambda i, j: (step(i, j), 0)),
        ] + [da_piece(k) for k in range(n_pieces)],
        out_specs=[
            pl.BlockSpec((None, 3 * DN_HEADS, tm), lambda i, j: (i, 0, j)),
            pl.BlockSpec((None, tm, LANES), lambda i, j: (i, j, 0)),
            pl.BlockSpec((None, tm, d), lambda i, j: (i, j, 0)),
            pl.BlockSpec((rows_dn, d), lambda i, j: (step(i, j), 0)),
            pl.BlockSpec((rows_da, d), lambda i, j: (step(i, j), 0)),
        ],
        out_shape=[
            jax.ShapeDtypeStruct((b, 3 * DN_HEADS, t), F32),
            jax.ShapeDtypeStruct((b, t, LANES), F32),
            jax.ShapeDtypeStruct((b, t, d), BF16),
            jax.ShapeDtypeStruct((off_gate, d), BF16),
            jax.ShapeDtypeStruct((n_da, d), BF16),
        ],
        compiler_params=_params("parallel", "arbitrary"),
        name="dn_gates",
    )(x3d, norm_w, w_in_t, a_log_col, dt_bias_col, w_in_t, *([w_in_t] * n_pieces))


def _dn_intra_kernel(q_ref, k_ref, v_ref, qh_ref, kh_ref, vh_ref, cwq_ref, cwk_ref, cwv_ref,
                     gcol_ref, grow_ref, wo_ref, u_ref, w_ref, qd_ref, kd_ref, a_ref, wo16_ref, xs_ref):
    wo16_ref[...] = wo_ref[...].astype(BF16)
    head = pl.program_id(1)
    first = pl.program_id(2) == 0
    tb = q_ref.shape[0]
    halo = SUBLANES
    for idx, (x_ref, xh_ref) in enumerate(((q_ref, qh_ref), (k_ref, kh_ref), (v_ref, vh_ref))):
        xs_ref[idx, 0:halo, :] = jnp.where(first, 0.0, xh_ref[...])
        xs_ref[idx, halo:halo + tb, :] = x_ref[...]

    lane = lax.broadcasted_iota(jnp.int32, (GROUP, LANES), 1)
    sel_beta = (lane == head).astype(F32)
    sel_gc = (lane == head + DN_HEADS).astype(F32)
    sel_gl = (lane == head + 2 * DN_HEADS).astype(F32)
    ri = lax.broadcasted_iota(jnp.int32, (LANES, LANES), 0)
    ci = lax.broadcasted_iota(jnp.int32, (LANES, LANES), 1)
    same = (ri // CHUNK) == (ci // CHUNK)
    incl = same & (ri >= ci)
    strict = same & (ri > ci)

    def conv_silu(idx, cw_ref, r0, anchor):
        base = r0 + halo - (CONV_WIDTH - 1)
        taps = cw_ref[...] if anchor is None else cw_ref[...] + anchor
        acc = xs_ref[idx, base:base + GROUP, :] * taps[0:1, :]
        for i in range(1, CONV_WIDTH):
            acc = acc + xs_ref[idx, base + i:base + i + GROUP, :] * taps[i:i + 1, :]
        return _silu(acc)

    n_groups = tb // GROUP
    levels = int(math.log2(CHUNK))
    n_pows = [None] * n_groups
    xs = [None] * n_groups
    gc_rows = grow_ref[pl.ds(head + DN_HEADS, 1), :]

    def prepare(gi):
        r0 = gi * GROUP
        anchor = None if gi < IN_FLIGHT else xs[gi - IN_FLIGHT][0:1, :DN_DIM] * 0.0
        q = conv_silu(0, cwq_ref, r0, anchor)
        k = conv_silu(1, cwk_ref, r0, anchor)
        v = conv_silu(2, cwv_ref, r0, anchor)
        q = q * lax.rsqrt(jnp.sum(q * q, axis=-1, keepdims=True) + EPS) * (DN_DIM ** -0.5)
        k = k * lax.rsqrt(jnp.sum(k * k, axis=-1, keepdims=True) + EPS)

        gates = gcol_ref[r0:r0 + GROUP, :]
        beta = jnp.sum(gates * sel_beta, axis=-1, keepdims=True)
        gc = jnp.sum(gates * sel_gc, axis=-1, keepdims=True)
        gl = jnp.sum(gates * sel_gl, axis=-1, keepdims=True)
        gc_row = gc_rows[:, r0:r0 + GROUP]

        kb = k * beta
        k16 = k.astype(BF16)
        gram = lax.dot_general(jnp.concatenate([kb, q], axis=0).astype(BF16), k16, _NT,
                               preferred_element_type=F32)
        m_blocks, a_blocks = [], []
        for d0 in range(0, GROUP, LANES):
            dd = slice(d0, d0 + LANES)
            decay = jnp.exp(jnp.where(incl, gc[dd] - gc_row[:, dd], NEG_BIG))
            m_blocks.append(jnp.where(strict, gram[dd, dd] * decay, 0.0))
            a_blocks.append(gram[GROUP + d0:GROUP + d0 + LANES, dd] * decay)
        zero = jnp.zeros((LANES, LANES), F32)
        n_pows[gi] = jnp.concatenate(
            [jnp.concatenate([m_blocks[i] if j == i else zero for j in range(len(m_blocks))], axis=1)
             for i in range(len(m_blocks))], axis=0)

        egc = jnp.exp(gc)
        xs[gi] = jnp.concatenate([v * beta, kb * egc], axis=1)
        qd_ref[r0:r0 + GROUP, :] = (q * egc).astype(BF16)
        kd_ref[r0:r0 + GROUP, :] = (k * jnp.exp(gl - gc)).astype(BF16)
        for c in range(GROUP // CHUNK):
            lo = c * CHUNK
            in_block = lo % LANES
            a_ref[r0 + lo:r0 + lo + CHUNK, :] = a_blocks[lo // LANES][
                in_block:in_block + CHUNK, in_block:in_block + CHUNK].astype(BF16)

    def level(gi, lvl):
        n16 = n_pows[gi].astype(BF16)
        step = jnp.dot(n16, xs[gi].astype(BF16), preferred_element_type=F32)
        xs[gi] = xs[gi] - step if lvl == 0 else xs[gi] + step
        if lvl + 1 < levels:
            n_pows[gi] = jnp.dot(n16, n16, preferred_element_type=F32)

    def finish(gi):
        r0 = gi * GROUP
        u_ref[r0:r0 + GROUP, :] = xs[gi][:, :DN_DIM]
        w_ref[r0:r0 + GROUP, :] = xs[gi][:, DN_DIM:].astype(BF16)

    for t in range(n_groups + levels + 1):
        for gi in range(n_groups):
            stage = t - gi
            if stage == 0:
                prepare(gi)
            elif 1 <= stage <= levels:
                level(gi, stage - 1)
            elif stage == levels + 1:
                finish(gi)


def _dn_intra(qkvz, conv_w, gcol, grow, w_out, tb):
    b, t, _ = qkvz.shape
    hb = tb // SUBLANES
    nh = DN_HEADS
    nj = t // tb
    wo_rows, wo_share = _slab_rows(w_out.shape[0], b * nh * nj)
    assert wo_share == 1, "every grid step converts its own slab of w_out"
    wo_spec = pl.BlockSpec((wo_rows, w_out.shape[1]), lambda i, h, j: ((i * nh + h) * nj + j, 0))

    def tok(off):
        return pl.BlockSpec((None, tb, DN_DIM), lambda i, h, j: (i, j, off + h))

    def halo(off):
        return pl.BlockSpec((None, SUBLANES, DN_DIM),
                            lambda i, h, j: (i, jnp.maximum(j * hb - 1, 0), off + h))

    def cw(off):
        return pl.BlockSpec((CONV_WIDTH, DN_DIM), lambda i, h, j: (0, off + h))

    out_tok = pl.BlockSpec((None, tb, DN_DIM), lambda i, h, j: (i, j, h))
    return pl.pallas_call(
        _dn_intra_kernel,
        grid=(b, nh, nj),
        in_specs=[tok(0), tok(nh), tok(2 * nh), halo(0), halo(nh), halo(2 * nh),
                  cw(0), cw(nh), cw(2 * nh),
                  pl.BlockSpec((None, tb, LANES), lambda i, h, j: (i, j, 0)),
                  pl.BlockSpec((None, 3 * nh, tb), lambda i, h, j: (i, 0, j)),
                  wo_spec],
        out_specs=[out_tok, out_tok, out_tok, out_tok,
                   pl.BlockSpec((None, None, tb, CHUNK), lambda i, h, j: (i, h, j, 0)),
                   wo_spec],
        out_shape=[jax.ShapeDtypeStruct((b, t, DN_WIDTH), F32),
                   jax.ShapeDtypeStruct((b, t, DN_WIDTH), BF16),
                   jax.ShapeDtypeStruct((b, t, DN_WIDTH), BF16),
                   jax.ShapeDtypeStruct((b, t, DN_WIDTH), BF16),
                   jax.ShapeDtypeStruct((b, nh, t, CHUNK), BF16),
                   jax.ShapeDtypeStruct(w_out.shape, BF16)],
        scratch_shapes=[pltpu.VMEM((3, tb + SUBLANES, DN_DIM), F32)],
        compiler_params=_params("parallel", "parallel", "arbitrary"),
        name="dn_intra",
    )(qkvz, qkvz, qkvz, qkvz, qkvz, qkvz, conv_w, conv_w, conv_w, gcol, grow, w_out)


def _dn_scan_kernel(u_ref, w_ref, qd_ref, kd_ref, a_ref, z_ref, gcol_ref, nw_ref, o_ref, s_ref):
    @pl.when(pl.program_id(1) == 0)
    def _():
        s_ref[...] = jnp.zeros_like(s_ref)

    bb, tb = u_ref.shape[0], u_ref.shape[1]
    nw = nw_ref[...]
    units = [(i, h) for i in range(bb) for h in range(DN_HEADS)]

    def chunk(c, carry):
        rows = pl.ds(pl.multiple_of(c * CHUNK, CHUNK), CHUNK)
        cols = [slice(h * DN_DIM, (h + 1) * DN_DIM) for h in range(DN_HEADS)]
        s = [s_ref[i, h] for i, h in units]
        proj = [jnp.dot(jnp.concatenate([w_ref[i, rows, cols[h]], qd_ref[i, rows, cols[h]]], axis=0),
                        s[n].astype(BF16), preferred_element_type=F32)
                for n, (i, h) in enumerate(units)]
        v16 = [(u_ref[i, rows, cols[h]] - proj[n][:CHUNK]).astype(BF16)
               for n, (i, h) in enumerate(units)]
        upd = [lax.dot_general(kd_ref[i, rows, cols[h]], v16[n], _TN, preferred_element_type=F32)
               for n, (i, h) in enumerate(units)]
        for n, (i, h) in enumerate(units):
            gl = gcol_ref[i, pl.ds(pl.multiple_of(c * CHUNK, CHUNK), 1),
                          2 * DN_HEADS + h:2 * DN_HEADS + h + 1]
            s_ref[i, h] = s[n] * jnp.exp(gl) + upd[n]
        o = [proj[n][CHUNK:] + jnp.dot(a_ref[i, h, rows, :], v16[n], preferred_element_type=F32)
             for n, (i, h) in enumerate(units)]
        for n, (i, h) in enumerate(units):
            z = z_ref[i, rows, cols[h]]
            o_ref[i, rows, cols[h]] = (_rmsnorm_rows(o[n], nw) * _silu(z)).astype(o_ref.dtype)
        return carry

    lax.fori_loop(0, tb // CHUNK, chunk, 0)


def _dn_scan(u, w, qd, kd, a, qkvz, gcol, dn_norm_w, bb, tb):
    b, t, _ = u.shape
    tok = pl.BlockSpec((bb, tb, DN_WIDTH), lambda i, j: (i, j, 0))
    z_block = 3 * DN_WIDTH // DN_WIDTH
    return pl.pallas_call(
        _dn_scan_kernel,
        grid=(b // bb, t // tb),
        in_specs=[tok, tok, tok, tok,
                  pl.BlockSpec((bb, DN_HEADS, tb, CHUNK), lambda i, j: (i, 0, j, 0)),
                  pl.BlockSpec((bb, tb, DN_WIDTH), lambda i, j: (i, j, z_block)),
                  pl.BlockSpec((bb, tb, LANES), lambda i, j: (i, j, 0)),
                  pl.BlockSpec((1, DN_DIM), lambda i, j: (0, 0))],
        out_specs=tok,
        out_shape=jax.ShapeDtypeStruct((b, t, DN_WIDTH), BF16),
        scratch_shapes=[pltpu.VMEM((bb, DN_HEADS, DN_DIM, DN_DIM), F32)],
        compiler_params=_params("parallel", "arbitrary"),
        name="dn_scan",
    )(u, w, qd, kd, a, qkvz, gcol, dn_norm_w)


def _attn_kernel(q_ref, k_ref, v_ref, lq1_ref, lk1_ref, lq2_ref, lk2_ref, nw_ref,
                 wg_ref, wu_ref, wd_ref, o_ref, wg16_ref, wu16_ref, wd16_ref,
                 m_ref, l_ref, acc_ref, sa_ref, sb_ref, *, lam_init, down_share):
    wg16_ref[...] = wg_ref[...].astype(BF16)
    wu16_ref[...] = wu_ref[...].astype(BF16)
    grid_step = ((pl.program_id(0) * pl.num_programs(1) + pl.program_id(1)) * pl.num_programs(2)
                 + pl.program_id(2))

    @pl.when(grid_step % down_share == 0)
    def _():
        wd16_ref[...] = wd_ref[...].astype(BF16)

    head = pl.program_id(1)
    qi = pl.program_id(2)
    tq = q_ref.shape[0]
    tk = tq
    slope = jnp.float32(0.0)
    for hh in range(DA_HEADS):
        slope = jnp.where(head == hh, jnp.float32(2.0 ** (-8.0 * (hh + 1) / DA_HEADS)), slope)

    kpos = lax.broadcasted_iota(jnp.int32, (1, tk), 1)
    slope2 = slope * LOG2E

    maps = range(2)

    def scores(j, s_ref):
        k0 = pl.multiple_of(j * tk, tk)
        bias = slope2 * (kpos + (j - qi) * tk).astype(F32)
        for c in maps:
            cols = slice(c * DA_DIM, (c + 1) * DA_DIM)
            s_ref[c] = lax.dot_general(q_ref[:, cols], k_ref[pl.ds(k0, tk), cols], _NT,
                                       preferred_element_type=F32) + bias

    def accumulate(j, s_ref, masked, first=False):
        k0 = pl.multiple_of(j * tk, tk)
        v = v_ref[pl.ds(k0, tk), :]
        s = [s_ref[c] for c in maps]
        if masked:
            rr = lax.broadcasted_iota(jnp.int32, (tq, tk), 0)
            cc = lax.broadcasted_iota(jnp.int32, (tq, tk), 1)
            s = [jnp.where(cc <= rr, s[c], NEG_BIG) for c in maps]
        m_cur = [jnp.broadcast_to(jnp.max(s[c], axis=-1, keepdims=True), (tq, LANES)) for c in maps]
        if first:
            m_new = m_cur
        else:
            m_prev = [m_ref[c] for c in maps]
            m_new = [jnp.maximum(m_prev[c], m_cur[c]) for c in maps]
        p = [jnp.exp2(s[c] - _lane_tile(m_new[c], tk // LANES)) for c in maps]
        p_sum = [jnp.broadcast_to(jnp.sum(p[c], axis=-1, keepdims=True), (tq, LANES)) for c in maps]
        p16 = [p[c].astype(BF16) for c in maps]
        pv = [jnp.dot(p16[c], v, preferred_element_type=F32) for c in maps]
        for c in maps:
            if first:
                l_ref[c] = p_sum[c]
                acc_ref[c] = pv[c]
            else:
                alpha = jnp.exp2(m_prev[c] - m_new[c])
                l_ref[c] = alpha * l_ref[c] + p_sum[c]
                acc_ref[c] = _lane_tile(alpha, DA_VDIM // LANES) * acc_ref[c] + pv[c]
            m_ref[c] = m_new[c]

    scores(0, sa_ref)

    @pl.when(qi == 0)
    def _():
        accumulate(0, sa_ref, True, first=True)

    @pl.when(qi > 0)
    def _():
        scores(1, sb_ref)
        accumulate(0, sa_ref, False, first=True)

        def pair(jj, carry):
            j = 2 * jj + 1
            scores(j + 1, sa_ref)
            accumulate(j, sb_ref, False)
            scores(j + 2, sb_ref)
            accumulate(j + 1, sa_ref, False)
            return carry

        lax.fori_loop(0, (qi - 1) // 2, pair, 0)

        @pl.when(qi % 2 == 0)
        def _():
            scores(qi, sa_ref)
            accumulate(qi - 1, sb_ref, False)
            accumulate(qi, sa_ref, True)

        @pl.when(qi % 2 == 1)
        def _():
            accumulate(qi, sb_ref, True)

    lam = (jnp.exp(jnp.sum(lq1_ref[...] * lk1_ref[...], axis=-1, keepdims=True))
           - jnp.exp(jnp.sum(lq2_ref[...] * lk2_ref[...], axis=-1, keepdims=True)) + lam_init)
    rep = DA_VDIM // LANES
    inv_l0 = 1.0 / l_ref[0]
    inv_l1 = lam / l_ref[1]
    o = acc_ref[0] * _lane_tile(inv_l0, rep) - acc_ref[1] * _lane_tile(inv_l1, rep)
    o_ref[...] = (_rmsnorm_rows(o, nw_ref[...]) * (1.0 - lam_init)).astype(o_ref.dtype)


def _slab_rows(n_rows, n_steps):
    for share in (1, 2, 4, 8, 16):
        if n_steps % share == 0 and n_rows % (n_steps // share) == 0:
            rows = n_rows // (n_steps // share)
            if rows % 16 == 0:
                return rows, share
    raise ValueError(f"cannot split {n_rows} weight rows over {n_steps} grid steps")


def _attention(da, lam_q1, lam_k1, lam_q2, lam_k2, da_norm_w, w_gate, w_up, w_down, lam_init, tq):
    b, t, _ = da.shape
    nh = DA_HEADS
    nq = t // tq
    d, dff = w_gate.shape
    steps = b * nh * nq
    rg, sg = _slab_rows(d, steps)
    rd, sd = _slab_rows(dff, steps)

    def step(i, h, j):
        return (i * nh + h) * nq + j

    vec = pl.BlockSpec((1, DA_DIM), lambda i, h, j: (0, 0))
    up_spec = pl.BlockSpec((rg, dff), lambda i, h, j: (step(i, h, j) // sg, 0))
    down_spec = pl.BlockSpec((rd, d), lambda i, h, j: (step(i, h, j) // sd, 0))
    return pl.pallas_call(
        functools.partial(_attn_kernel, lam_init=lam_init, down_share=sd),
        grid=(b, nh, nq),
        in_specs=[pl.BlockSpec((None, tq, DA_VDIM), lambda i, h, j: (i, j, h)),
                  pl.BlockSpec((None, t, DA_VDIM), lambda i, h, j: (i, 0, nh + h)),
                  pl.BlockSpec((None, t, DA_VDIM), lambda i, h, j: (i, 0, 2 * nh + h)),
                  vec, vec, vec, vec,
                  pl.BlockSpec((1, DA_VDIM), lambda i, h, j: (0, 0)),
                  up_spec, up_spec, down_spec],
        out_specs=[pl.BlockSpec((None, tq, DA_VDIM), lambda i, h, j: (i, j, h)),
                   up_spec, up_spec, down_spec],
        out_shape=[jax.ShapeDtypeStruct((b, t, DA_WIDTH), BF16),
                   jax.ShapeDtypeStruct((d, dff), BF16),
                   jax.ShapeDtypeStruct((d, dff), BF16),
                   jax.ShapeDtypeStruct((dff, d), BF16)],
        scratch_shapes=[pltpu.VMEM((2, tq, LANES), F32), pltpu.VMEM((2, tq, LANES), F32),
                        pltpu.VMEM((2, tq, DA_VDIM), F32),
                        pltpu.VMEM((2, tq, tq), F32), pltpu.VMEM((2, tq, tq), F32)],
        compiler_params=_params("arbitrary", "arbitrary", "arbitrary"),
        name="diff_attn",
    )(da, da, da, lam_q1, lam_k1, lam_q2, lam_k2, da_norm_w, w_gate, w_up, w_down)


def _out_proj_kernel(a_ref, b_ref, wa_ref, wb_ref, x_ref, o_ref):
    acc = jnp.dot(a_ref[...], wa_ref[...], preferred_element_type=F32)
    acc = acc + jnp.dot(b_ref[...], wb_ref[...], preferred_element_type=F32)
    o_ref[...] = x_ref[...] + acc


def _out_proj(o_dn, o_da, w16, x2d, tm, tn):
    m, d = x2d.shape
    ka, kb = o_dn.shape[1], o_da.shape[1]
    assert ka == kb and w16.shape[0] == ka + kb, "the two head groups are equal row halves of w_out"
    return pl.pallas_call(
        _out_proj_kernel,
        grid=(m // tm, d // tn),
        in_specs=[pl.BlockSpec((tm, ka), lambda i, j: (i, 0)),
                  pl.BlockSpec((tm, kb), lambda i, j: (i, 0)),
                  pl.BlockSpec((ka, tn), lambda i, j: (0, j)),
                  pl.BlockSpec((kb, tn), lambda i, j: (1, j)),
                  pl.BlockSpec((tm, tn), lambda i, j: (i, j))],
        out_specs=pl.BlockSpec((tm, tn), lambda i, j: (i, j)),
        out_shape=jax.ShapeDtypeStruct((m, d), F32),
        compiler_params=_params("parallel", "arbitrary"),
        name="out_proj",
    )(o_dn, o_da, w16, w16, x2d)


def _ffn_kernel(x_ref, nw_ref, wg_ref, wu_ref, wd_ref, fw_ref, o_ref, h_ref):
    f = pl.program_id(1)

    @pl.when(f == 0)
    def _():
        x = x_ref[...]
        h_ref[...] = _rmsnorm_rows(x, nw_ref[...]).astype(BF16)
        o_ref[...] = x

    h = h_ref[...]
    g = jnp.dot(h, wg_ref[...], preferred_element_type=F32)
    u = jnp.dot(h, wu_ref[...], preferred_element_type=F32)
    act = (_silu(g) * u).astype(BF16)
    o_ref[...] += jnp.dot(act, wd_ref[...], preferred_element_type=F32)

    @pl.when(f == pl.num_programs(1) - 1)
    def _():
        o_ref[...] = _rmsnorm_rows(o_ref[...], fw_ref[...])


def _ffn(x2d, ffn_norm_w, w_gate, w_up, w_down, final_norm_w, tm, tf):
    m, d = x2d.shape
    dff = w_gate.shape[1]
    return pl.pallas_call(
        _ffn_kernel,
        grid=(m // tm, dff // tf),
        in_specs=[pl.BlockSpec((tm, d), lambda i, f: (i, 0)),
                  pl.BlockSpec((1, d), lambda i, f: (0, 0)),
                  pl.BlockSpec((d, tf), lambda i, f: (0, f)),
                  pl.BlockSpec((d, tf), lambda i, f: (0, f)),
                  pl.BlockSpec((tf, d), lambda i, f: (f, 0)),
                  pl.BlockSpec((1, d), lambda i, f: (0, 0))],
        out_specs=pl.BlockSpec((tm, d), lambda i, f: (i, 0)),
        out_shape=jax.ShapeDtypeStruct((m, d), F32),
        scratch_shapes=[pltpu.VMEM((tm, d), BF16)],
        compiler_params=_params("parallel", "arbitrary"),
        name="ffn",
    )(x2d, ffn_norm_w, w_gate, w_up, w_down, final_norm_w)


def _tile(n, pref):
    return pref if n % pref == 0 else n


def _layer(x, lam_init, attn_norm_w, w_in, conv_w, a_log, dt_bias, dn_norm_w, lam_q1, lam_k1,
           lam_q2, lam_k2, da_norm_w, w_out, ffn_norm_w, w_gate, w_up, w_down, final_norm_w):
    b, t, d = x.shape
    m = b * t
    x2d = x.reshape(m, d)
    off_z = 3 * DN_WIDTH
    off_b = off_z + DN_WIDTH
    off_q = off_b + 2 * DN_HEADS
    nw = attn_norm_w.reshape(1, d)

    n_da = w_in.shape[1] - off_q
    da_scale = jnp.concatenate([jnp.full((1, DA_HEADS * 2 * DA_DIM), LOG2E * DA_DIM ** -0.5, F32),
                                jnp.ones((1, n_da - DA_HEADS * 2 * DA_DIM), F32)], axis=1)

    w_in_t = w_in.T
    grow, gcol, h, w_dn, w_da = _gates(x, nw, w_in_t, a_log.reshape(DN_HEADS, 1),
                                       dt_bias.reshape(DN_HEADS, 1), off_b, _tile(t, 512))
    h2d = h.reshape(m, d)
    tp = _tile(m, 2048)
    qkvz = _proj(h2d, w_dn, jnp.ones((1, off_b), F32), F32, tp, 1024, "in_proj_dn")
    da = _proj(h2d, w_da, da_scale, BF16, tp, 1024, "in_proj_da")
    tm = _tile(m, 1024)

    qkvz = qkvz.reshape(b, t, off_b)
    u, w, qd, kd, a, w_out16 = _dn_intra(qkvz, conv_w, gcol, grow, w_out, _tile(t, 4096))
    o_dn = _dn_scan(u, w, qd, kd, a, qkvz, gcol, dn_norm_w.reshape(1, DN_DIM),
                    4 if b % 4 == 0 else 1, _tile(t, 128))

    o_da, w_gate16, w_up16, w_down16 = _attention(
        da.reshape(b, t, n_da), lam_q1.reshape(1, DA_DIM), lam_k1.reshape(1, DA_DIM),
        lam_q2.reshape(1, DA_DIM), lam_k2.reshape(1, DA_DIM), da_norm_w.reshape(1, DA_VDIM),
        w_gate, w_up, w_down, lam_init, _tile(t, 512))

    x2d = _out_proj(o_dn.reshape(m, DN_WIDTH), o_da.reshape(m, DA_WIDTH), w_out16, x2d,
                    _tile(m, 512), d)
    dff = w_gate.shape[1]
    return _ffn(x2d, ffn_norm_w.reshape(1, d), w_gate16, w_up16, w_down16,
                final_norm_w.reshape(1, d), tm, _tile(dff, 512))


def kernel(x, attn_norm_w, w_in, conv_w, a_log, dt_bias, dn_norm_w, lam_q1, lam_k1, lam_q2, lam_k2,
           da_norm_w, w_out, ffn_norm_w, w_gate, w_up, w_down, final_norm_w):
    b, t, d = x.shape
    depth = w_in.shape[0]
    assert depth == 1, "the final rmsnorm is fused into the last layer's FFN kernel"
    lam_init = 0.8 - 0.6 * math.exp(-0.3 * 0)
    out = _layer(x, lam_init, attn_norm_w[0], w_in[0], conv_w[0], a_log[0], dt_bias[0],
                 dn_norm_w[0], lam_q1[0], lam_k1[0], lam_q2[0], lam_k2[0], da_norm_w[0],
                 w_out[0], ffn_norm_w[0], w_gate[0], w_up[0], w_down[0], final_norm_w)
    return out.reshape(b, t, d)
```

```python
import functools
import math

import jax
import jax.numpy as jnp
from jax import lax
from jax.experimental import pallas as pl
from jax.experimental.pallas import tpu as pltpu

F32 = jnp.float32
BF16 = jnp.bfloat16
EPS = 1e-6
NEG_BIG = -1e30
LOG2E = math.log2(math.e)

DN_HEADS = 8
DN_DIM = 128
DN_WIDTH = DN_HEADS * DN_DIM
CONV_WIDTH = 4
CHUNK = 64
DA_HEADS = 4
DA_DIM = 128
DA_VDIM = 2 * DA_DIM
DA_WIDTH = DA_HEADS * DA_VDIM
LANES = 128
SUBLANES = 8
GROUP = 128
IN_FLIGHT = 8
GATE_SLAB = 256
VMEM_LIMIT = 56 * 1024 * 1024

_NT = (((1,), (1,)), ((), ()))
_TN = (((0,), (0,)), ((), ()))


def _params(*sem):
    return pltpu.CompilerParams(dimension_semantics=sem, vmem_limit_bytes=VMEM_LIMIT)


def _sigmoid(x):
    return 1.0 / (1.0 + jnp.exp(-x))


def _silu(x):
    h = 0.5 * x
    return h + h * jnp.tanh(h)


def _lane_tile(x, n):
    return jnp.concatenate([x] * n, axis=1)


def _rmsnorm_rows(x, w):
    return x * lax.rsqrt(jnp.mean(x * x, axis=-1, keepdims=True) + EPS) * w


def _proj_kernel(h_ref, w_ref, cs_ref, o_ref):
    acc = lax.dot_general(h_ref[...], w_ref[...], _NT, preferred_element_type=F32)
    o_ref[...] = (acc * cs_ref[...]).astype(o_ref.dtype)


def _proj(h2d, w_t_bf16, col_scale, out_dtype, tm, tn, name):
    m, d = h2d.shape
    n = w_t_bf16.shape[0]
    return pl.pallas_call(
        _proj_kernel,
        grid=(m // tm, n // tn),
        in_specs=[
            pl.BlockSpec((tm, d), lambda i, j: (i, 0)),
            pl.BlockSpec((tn, d), lambda i, j: (j, 0)),
            pl.BlockSpec((1, tn), lambda i, j: (0, j)),
        ],
        out_specs=pl.BlockSpec((tm, tn), lambda i, j: (i, j)),
        out_shape=jax.ShapeDtypeStruct((m, n), out_dtype),
        compiler_params=_params("parallel", "arbitrary"),
        name=name,
    )(h2d, w_t_bf16, col_scale)


def _gates_kernel(x_ref, nw_ref, wt_ref, alog_ref, dtb_ref, wdn_ref, *rest):
    wda_refs, (row_ref, col_ref, h_ref, wdn16_ref, wda16_ref) = rest[:-5], rest[-5:]
    wdn16_ref[...] = wdn_ref[...].astype(BF16)
    wda16_ref[...] = jnp.concatenate([r[...] for r in wda_refs], axis=0).astype(BF16)

    h = _rmsnorm_rows(x_ref[...], nw_ref[...]).astype(BF16)
    h_ref[...] = h
    r = lax.dot_general(wt_ref[...].astype(BF16), h, _NT, preferred_element_type=F32)
    beta = _sigmoid(r[0:DN_HEADS])
    a = r[DN_HEADS:2 * DN_HEADS] + dtb_ref[...]
    softplus = jnp.maximum(a, 0.0) + jnp.log1p(jnp.exp(-jnp.abs(a)))
    g = -jnp.exp(alog_ref[...]) * softplus
    tm = g.shape[1]
    sw = min(tm, GATE_SLAB)
    ii = lax.broadcasted_iota(jnp.int32, (sw, sw), 0)
    jj = lax.broadcasted_iota(jnp.int32, (sw, sw), 1)
    same = (ii // CHUNK) == (jj // CHUNK)
    sums = jnp.concatenate([same & (ii <= jj), same], axis=1).astype(BF16)
    g1 = g.astype(BF16)
    r1 = g - g1.astype(F32)
    g2 = r1.astype(BF16)
    g3 = (r1 - g2.astype(F32)).astype(BF16)
    pieces = jnp.concatenate([g1, g2, g3, jnp.zeros_like(g1)], axis=0)
    nh = DN_HEADS
    gc, gl = [], []
    for s0 in range(0, tm, sw):
        acc = jnp.dot(pieces[:, s0:s0 + sw], sums, preferred_element_type=F32)
        acc = acc[0:nh] + acc[nh:2 * nh] + acc[2 * nh:3 * nh]
        gc.append(acc[:, :sw])
        gl.append(acc[:, sw:])
    gc, gl = jnp.concatenate(gc, axis=1), jnp.concatenate(gl, axis=1)
    rows = jnp.concatenate([beta, gc, gl], axis=0)
    row_ref[...] = rows
    padded = jnp.concatenate([rows, jnp.zeros((LANES - 3 * DN_HEADS, tm), F32)], axis=0)
    col_ref[...] = padded.T


def _gates(x3d, norm_w, w_in_t, a_log_col, dt_bias_col, off_gate, tm):
    b, t, d = x3d.shape
    nj = t // tm
    steps = b * nj
    piece = 2 * DN_HEADS
    off_da = off_gate + piece
    n_da = w_in_t.shape[0] - off_da
    rows_dn, rows_da = off_gate // steps, n_da // steps
    assert rows_dn * steps == off_gate and rows_dn % piece == 0
    assert rows_da * steps == n_da and rows_da % piece == 0 and off_da % piece == 0
    n_pieces = rows_da // piece

    def step(i, j):
        return i * nj + j

    def da_piece(k):
        return pl.BlockSpec((piece, d),
                            lambda i, j: (off_da // piece + step(i, j) * n_pieces + k, 0))

    return pl.pallas_call(
        _gates_kernel,
        grid=(b, nj),
        in_specs=[
            pl.BlockSpec((None, tm, d), lambda i, j: (i, j, 0)),
            pl.BlockSpec((1, d), lambda i, j: (0, 0)),
            pl.BlockSpec((piece, d), lambda i, j: (off_gate // piece, 0)),
            pl.BlockSpec((DN_HEADS, 1), lambda i, j: (0, 0)),
            pl.BlockSpec((DN_HEADS, 1), lambda i, j: (0, 0)),
            pl.BlockSpec((rows_dn, d), lambda i, j: (step(i, j), 0)),
        ] + [da_piece(k) for k in range(n_pieces)],
        out_specs=[
            pl.BlockSpec((None, 3 * DN_HEADS, tm), lambda i, j: (i, 0, j)),
            pl.BlockSpec((None, tm, LANES), lambda i, j: (i, j, 0)),
            pl.BlockSpec((None, tm, d), lambda i, j: (i, j, 0)),
            pl.BlockSpec((rows_dn, d), lambda i, j: (step(i, j), 0)),
            pl.BlockSpec((rows_da, d), lambda i, j: (step(i, j), 0)),
        ],
        out_shape=[
            jax.ShapeDtypeStruct((b, 3 * DN_HEADS, t), F32),
            jax.ShapeDtypeStruct((b, t, LANES), F32),
            jax.ShapeDtypeStruct((b, t, d), BF16),
            jax.ShapeDtypeStruct((off_gate, d), BF16),
            jax.ShapeDtypeStruct((n_da, d), BF16),
        ],
        compiler_params=_params("parallel", "arbitrary"),
        name="dn_gates",
    )(x3d, norm_w, w_in_t, a_log_col, dt_bias_col, w_in_t, *([w_in_t] * n_pieces))


def _dn_intra_kernel(q_ref, k_ref, v_ref, qh_ref, kh_ref, vh_ref, cwq_ref, cwk_ref, cwv_ref,
                     gcol_ref, grow_ref, wo_ref, u_ref, w_ref, qd_ref, kd_ref, a_ref, wo16_ref, xs_ref):
    wo16_ref[...] = wo_ref[...].astype(BF16)
    head = pl.program_id(1)
    first = pl.program_id(2) == 0
    tb = q_ref.shape[0]
    halo = SUBLANES
    for idx, (x_ref, xh_ref) in enumerate(((q_ref, qh_ref), (k_ref, kh_ref), (v_ref, vh_ref))):
        xs_ref[idx, 0:halo, :] = jnp.where(first, 0.0, xh_ref[...])
        xs_ref[idx, halo:halo + tb, :] = x_ref[...]

    lane = lax.broadcasted_iota(jnp.int32, (GROUP, LANES), 1)
    sel_beta = (lane == head).astype(F32)
    sel_gc = (lane == head + DN_HEADS).astype(F32)
    sel_gl = (lane == head + 2 * DN_HEADS).astype(F32)
    ri = lax.broadcasted_iota(jnp.int32, (LANES, LANES), 0)
    ci = lax.broadcasted_iota(jnp.int32, (LANES, LANES), 1)
    same = (ri // CHUNK) == (ci // CHUNK)
    incl = same & (ri >= ci)
    strict = same & (ri > ci)

    def conv_silu(idx, cw_ref, r0, anchor):
        base = r0 + halo - (CONV_WIDTH - 1)
        taps = cw_ref[...] if anchor is None else cw_ref[...] + anchor
        acc = xs_ref[idx, base:base + GROUP, :] * taps[0:1, :]
        for i in range(1, CONV_WIDTH):
            acc = acc + xs_ref[idx, base + i:base + i + GROUP, :] * taps[i:i + 1, :]
        return _silu(acc)

    n_groups = tb // GROUP
    levels = int(math.log2(CHUNK))
    n_pows = [None] * n_groups
    xs = [None] * n_groups
    gc_rows = grow_ref[pl.ds(head + DN_HEADS, 1), :]

    def prepare(gi):
        r0 = gi * GROUP
        anchor = None if gi < IN_FLIGHT else xs[gi - IN_FLIGHT][0:1, :DN_DIM] * 0.0
        q = conv_silu(0, cwq_ref, r0, anchor)
        k = conv_silu(1, cwk_ref, r0, anchor)
        v = conv_silu(2, cwv_ref, r0, anchor)
        q = q * lax.rsqrt(jnp.sum(q * q, axis=-1, keepdims=True) + EPS) * (DN_DIM ** -0.5)
        k = k * lax.rsqrt(jnp.sum(k * k, axis=-1, keepdims=True) + EPS)

        gates = gcol_ref[r0:r0 + GROUP, :]
        beta = jnp.sum(gates * sel_beta, axis=-1, keepdims=True)
        gc = jnp.sum(gates * sel_gc, axis=-1, keepdims=True)
        gl = jnp.sum(gates * sel_gl, axis=-1, keepdims=True)
        gc_row = gc_rows[:, r0:r0 + GROUP]

        kb = k * beta
        k16 = k.astype(BF16)
        gram = lax.dot_general(jnp.concatenate([kb, q], axis=0).astype(BF16), k16, _NT,
                               preferred_element_type=F32)
        m_blocks, a_blocks = [], []
        for d0 in range(0, GROUP, LANES):
            dd = slice(d0, d0 + LANES)
            decay = jnp.exp(jnp.where(incl, gc[dd] - gc_row[:, dd], NEG_BIG))
            m_blocks.append(jnp.where(strict, gram[dd, dd] * decay, 0.0))
            a_blocks.append(gram[GROUP + d0:GROUP + d0 + LANES, dd] * decay)
        zero = jnp.zeros((LANES, LANES), F32)
        n_pows[gi] = jnp.concatenate(
            [jnp.concatenate([m_blocks[i] if j == i else zero for j in range(len(m_blocks))], axis=1)
             for i in range(len(m_blocks))], axis=0)

        egc = jnp.exp(gc)
        xs[gi] = jnp.concatenate([v * beta, kb * egc], axis=1)
        qd_ref[r0:r0 + GROUP, :] = (q * egc).astype(BF16)
        kd_ref[r0:r0 + GROUP, :] = (k * jnp.exp(gl - gc)).astype(BF16)
        for c in range(GROUP // CHUNK):
            lo = c * CHUNK
            in_block = lo % LANES
            a_ref[r0 + lo:r0 + lo + CHUNK, :] = a_blocks[lo // LANES][
                in_block:in_block + CHUNK, in_block:in_block + CHUNK].astype(BF16)

    def level(gi, lvl):
        n16 = n_pows[gi].astype(BF16)
        step = jnp.dot(n16, xs[gi].astype(BF16), preferred_element_type=F32)
        xs[gi] = xs[gi] - step if lvl == 0 else xs[gi] + step
        if lvl + 1 < levels:
            n_pows[gi] = jnp.dot(n16, n16, preferred_element_type=F32)

    def finish(gi):
        r0 = gi * GROUP
        u_ref[r0:r0 + GROUP, :] = xs[gi][:, :DN_DIM]
        w_ref[r0:r0 + GROUP, :] = xs[gi][:, DN_DIM:].astype(BF16)

    for t in range(n_groups + levels + 1):
        for gi in range(n_groups):
            stage = t - gi
            if stage == 0:
                prepare(gi)
            elif 1 <= stage <= levels:
                level(gi, stage - 1)
            elif stage == levels + 1:
                finish(gi)


def _dn_intra(qkvz, conv_w, gcol, grow, w_out, tb):
    b, t, _ = qkvz.shape
    hb = tb // SUBLANES
    nh = DN_HEADS
    nj = t // tb
    wo_rows, wo_share = _slab_rows(w_out.shape[0], b * nh * nj)
    assert wo_share == 1, "every grid step converts its own slab of w_out"
    wo_spec = pl.BlockSpec((wo_rows, w_out.shape[1]), lambda i, h, j: ((i * nh + h) * nj + j, 0))

    def tok(off):
        return pl.BlockSpec((None, tb, DN_DIM), lambda i, h, j: (i, j, off + h))

    def halo(off):
        return pl.BlockSpec((None, SUBLANES, DN_DIM),
                            lambda i, h, j: (i, jnp.maximum(j * hb - 1, 0), off + h))

    def cw(off):
        return pl.BlockSpec((CONV_WIDTH, DN_DIM), lambda i, h, j: (0, off + h))

    out_tok = pl.BlockSpec((None, tb, DN_DIM), lambda i, h, j: (i, j, h))
    return pl.pallas_call(
        _dn_intra_kernel,
        grid=(b, nh, nj),
        in_specs=[tok(0), tok(nh), tok(2 * nh), halo(0), halo(nh), halo(2 * nh),
                  cw(0), cw(nh), cw(2 * nh),
                  pl.BlockSpec((None, tb, LANES), lambda i, h, j: (i, j, 0)),
                  pl.BlockSpec((None, 3 * nh, tb), lambda i, h, j: (i, 0, j)),
                  wo_spec],
        out_specs=[out_tok, out_tok, out_tok, out_tok,
                   pl.BlockSpec((None, None, tb, CHUNK), lambda i, h, j: (i, h, j, 0)),
                   wo_spec],
        out_shape=[jax.ShapeDtypeStruct((b, t, DN_WIDTH), F32),
                   jax.ShapeDtypeStruct((b, t, DN_WIDTH), BF16),
                   jax.ShapeDtypeStruct((b, t, DN_WIDTH), BF16),
                   jax.ShapeDtypeStruct((b, t, DN_WIDTH), BF16),
                   jax.ShapeDtypeStruct((b, nh, t, CHUNK), BF16),
                   jax.ShapeDtypeStruct(w_out.shape, BF16)],
        scratch_shapes=[pltpu.VMEM((3, tb + SUBLANES, DN_DIM), F32)],
        compiler_params=_params("parallel", "parallel", "arbitrary"),
        name="dn_intra",
    )(qkvz, qkvz, qkvz, qkvz, qkvz, qkvz, conv_w, conv_w, conv_w, gcol, grow, w_out)


def _dn_scan_kernel(u_ref, w_ref, qd_ref, kd_ref, a_ref, z_ref, gcol_ref, nw_ref, o_ref, s_ref):
    @pl.when(pl.program_id(1) == 0)
    def _():
        s_ref[...] = jnp.zeros_like(s_ref)

    bb, tb = u_ref.shape[0], u_ref.shape[1]
    nw = nw_ref[...]
    units = [(i, h) for i in range(bb) for h in range(DN_HEADS)]

    def chunk(c, carry):
        rows = pl.ds(pl.multiple_of(c * CHUNK, CHUNK), CHUNK)
        cols = [slice(h * DN_DIM, (h + 1) * DN_DIM) for h in range(DN_HEADS)]
        s = [s_ref[i, h] for i, h in units]
        proj = [jnp.dot(jnp.concatenate([w_ref[i, rows, cols[h]], qd_ref[i, rows, cols[h]]], axis=0),
                        s[n].astype(BF16), preferred_element_type=F32)
                for n, (i, h) in enumerate(units)]
        v16 = [(u_ref[i, rows, cols[h]] - proj[n][:CHUNK]).astype(BF16)
               for n, (i, h) in enumerate(units)]
        upd = [lax.dot_general(kd_ref[i, rows, cols[h]], v16[n], _TN, preferred_element_type=F32)
               for n, (i, h) in enumerate(units)]
        for n, (i, h) in enumerate(units):
            gl = gcol_ref[i, pl.ds(pl.multiple_of(c * CHUNK, CHUNK), 1),
                          2 * DN_HEADS + h:2 * DN_HEADS + h + 1]
            s_ref[i, h] = s[n] * jnp.exp(gl) + upd[n]
        o = [proj[n][CHUNK:] + jnp.dot(a_ref[i, h, rows, :], v16[n], preferred_element_type=F32)
             for n, (i, h) in enumerate(units)]
        for n, (i, h) in enumerate(units):
            z = z_ref[i, rows, cols[h]]
            o_ref[i, rows, cols[h]] = (_rmsnorm_rows(o[n], nw) * _silu(z)).astype(o_ref.dtype)
        return carry

    lax.fori_loop(0, tb // CHUNK, chunk, 0)


def _dn_scan(u, w, qd, kd, a, qkvz, gcol, dn_norm_w, bb, tb):
    b, t, _ = u.shape
    tok = pl.BlockSpec((bb, tb, DN_WIDTH), lambda i, j: (i, j, 0))
    z_block = 3 * DN_WIDTH // DN_WIDTH
    return pl.pallas_call(
        _dn_scan_kernel,
        grid=(b // bb, t // tb),
        in_specs=[tok, tok, tok, tok,
                  pl.BlockSpec((bb, DN_HEADS, tb, CHUNK), lambda i, j: (i, 0, j, 0)),
                  pl.BlockSpec((bb, tb, DN_WIDTH), lambda i, j: (i, j, z_block)),
                  pl.BlockSpec((bb, tb, LANES), lambda i, j: (i, j, 0)),
                  pl.BlockSpec((1, DN_DIM), lambda i, j: (0, 0))],
        out_specs=tok,
        out_shape=jax.ShapeDtypeStruct((b, t, DN_WIDTH), BF16),
        scratch_shapes=[pltpu.VMEM((bb, DN_HEADS, DN_DIM, DN_DIM), F32)],
        compiler_params=_params("parallel", "arbitrary"),
        name="dn_scan",
    )(u, w, qd, kd, a, qkvz, gcol, dn_norm_w)


def _attn_kernel(q_ref, k_ref, v_ref, lq1_ref, lk1_ref, lq2_ref, lk2_ref, nw_ref,
                 wg_ref, wu_ref, wd_ref, o_ref, wg16_ref, wu16_ref, wd16_ref,
                 m_ref, l_ref, acc_ref, sa_ref, sb_ref, *, lam_init, down_share):
    wg16_ref[...] = wg_ref[...].astype(BF16)
    wu16_ref[...] = wu_ref[...].astype(BF16)
    grid_step = ((pl.program_id(0) * pl.num_programs(1) + pl.program_id(1)) * pl.num_programs(2)
                 + pl.program_id(2))

    @pl.when(grid_step % down_share == 0)
    def _():
        wd16_ref[...] = wd_ref[...].astype(BF16)

    head = pl.program_id(1)
    qi = pl.program_id(2)
    tq = q_ref.shape[0]
    tk = tq
    slope = jnp.float32(0.0)
    for hh in range(DA_HEADS):
        slope = jnp.where(head == hh, jnp.float32(2.0 ** (-8.0 * (hh + 1) / DA_HEADS)), slope)

    kpos = lax.broadcasted_iota(jnp.int32, (1, tk), 1)
    slope2 = slope * LOG2E

    maps = range(2)

    def scores(j, s_ref):
        k0 = pl.multiple_of(j * tk, tk)
        bias = slope2 * (kpos + (j - qi) * tk).astype(F32)
        for c in maps:
            cols = slice(c * DA_DIM, (c + 1) * DA_DIM)
            s_ref[c] = lax.dot_general(q_ref[:, cols], k_ref[pl.ds(k0, tk), cols], _NT,
                                       preferred_element_type=F32) + bias

    def accumulate(j, s_ref, masked, first=False):
        k0 = pl.multiple_of(j * tk, tk)
        v = v_ref[pl.ds(k0, tk), :]
        s = [s_ref[c] for c in maps]
        if masked:
            rr = lax.broadcasted_iota(jnp.int32, (tq, tk), 0)
            cc = lax.broadcasted_iota(jnp.int32, (tq, tk), 1)
            s = [jnp.where(cc <= rr, s[c], NEG_BIG) for c in maps]
        m_cur = [jnp.broadcast_to(jnp.max(s[c], axis=-1, keepdims=True), (tq, LANES)) for c in maps]
        if first:
            m_new = m_cur
        else:
            m_prev = [m_ref[c] for c in maps]
            m_new = [jnp.maximum(m_prev[c], m_cur[c]) for c in maps]
        p = [jnp.exp2(s[c] - _lane_tile(m_new[c], tk // LANES)) for c in maps]
        p_sum = [jnp.broadcast_to(jnp.sum(p[c], axis=-1, keepdims=True), (tq, LANES)) for c in maps]
        p16 = [p[c].astype(BF16) for c in maps]
        pv = [jnp.dot(p16[c], v, preferred_element_type=F32) for c in maps]
        for c in maps:
            if first:
                l_ref[c] = p_sum[c]
                acc_ref[c] = pv[c]
            else:
                alpha = jnp.exp2(m_prev[c] - m_new[c])
                l_ref[c] = alpha * l_ref[c] + p_sum[c]
                acc_ref[c] = _lane_tile(alpha, DA_VDIM // LANES) * acc_ref[c] + pv[c]
            m_ref[c] = m_new[c]

    scores(0, sa_ref)

    @pl.when(qi == 0)
    def _():
        accumulate(0, sa_ref, True, first=True)

    @pl.when(qi > 0)
    def _():
        scores(1, sb_ref)
        accumulate(0, sa_ref, False, first=True)

        def pair(jj, carry):
            j = 2 * jj + 1
            scores(j + 1, sa_ref)
            accumulate(j, sb_ref, False)
            scores(j + 2, sb_ref)
            accumulate(j + 1, sa_ref, False)
            return carry

        lax.fori_loop(0, (qi - 1) // 2, pair, 0)

        @pl.when(qi % 2 == 0)
        def _():
            scores(qi, sa_ref)
            accumulate(qi - 1, sb_ref, False)
            accumulate(qi, sa_ref, True)

        @pl.when(qi % 2 == 1)
        def _():
            accumulate(qi, sb_ref, True)

    lam = (jnp.exp(jnp.sum(lq1_ref[...] * lk1_ref[...], axis=-1, keepdims=True))
           - jnp.exp(jnp.sum(lq2_ref[...] * lk2_ref[...], axis=-1, keepdims=True)) + lam_init)
    rep = DA_VDIM // LANES
    inv_l0 = 1.0 / l_ref[0]
    inv_l1 = lam / l_ref[1]
    o = acc_ref[0] * _lane_tile(inv_l0, rep) - acc_ref[1] * _lane_tile(inv_l1, rep)
    o_ref[...] = (_rmsnorm_rows(o, nw_ref[...]) * (1.0 - lam_init)).astype(o_ref.dtype)


def _slab_rows(n_rows, n_steps):
    for share in (1, 2, 4, 8, 16):
        if n_steps % share == 0 and n_rows % (n_steps // share) == 0:
            rows = n_rows // (n_steps // share)
            if rows % 16 == 0:
                return rows, share
    raise ValueError(f"cannot split {n_rows} weight rows over {n_steps} grid steps")


def _attention(da, lam_q1, lam_k1, lam_q2, lam_k2, da_norm_w, w_gate, w_up, w_down, lam_init, tq):
    b, t, _ = da.shape
    nh = DA_HEADS
    nq = t // tq
    d, dff = w_gate.shape
    steps = b * nh * nq
    rg, sg = _slab_rows(d, steps)
    rd, sd = _slab_rows(dff, steps)

    def step(i, h, j):
        return (i * nh + h) * nq + j

    vec = pl.BlockSpec((1, DA_DIM), lambda i, h, j: (0, 0))
    up_spec = pl.BlockSpec((rg, dff), lambda i, h, j: (step(i, h, j) // sg, 0))
    down_spec = pl.BlockSpec((rd, d), lambda i, h, j: (step(i, h, j) // sd, 0))
    return pl.pallas_call(
        functools.partial(_attn_kernel, lam_init=lam_init, down_share=sd),
        grid=(b, nh, nq),
        in_specs=[pl.BlockSpec((None, tq, DA_VDIM), lambda i, h, j: (i, j, h)),
                  pl.BlockSpec((None, t, DA_VDIM), lambda i, h, j: (i, 0, nh + h)),
                  pl.BlockSpec((None, t, DA_VDIM), lambda i, h, j: (i, 0, 2 * nh + h)),
                  vec, vec, vec, vec,
                  pl.BlockSpec((1, DA_VDIM), lambda i, h, j: (0, 0)),
                  up_spec, up_spec, down_spec],
        out_specs=[pl.BlockSpec((None, tq, DA_VDIM), lambda i, h, j: (i, j, h)),
                   up_spec, up_spec, down_spec],
        out_shape=[jax.ShapeDtypeStruct((b, t, DA_WIDTH), BF16),
                   jax.ShapeDtypeStruct((d, dff), BF16),
                   jax.ShapeDtypeStruct((d, dff), BF16),
                   jax.ShapeDtypeStruct((dff, d), BF16)],
        scratch_shapes=[pltpu.VMEM((2, tq, LANES), F32), pltpu.VMEM((2, tq, LANES), F32),
                        pltpu.VMEM((2, tq, DA_VDIM), F32),
                        pltpu.VMEM((2, tq, tq), F32), pltpu.VMEM((2, tq, tq), F32)],
        compiler_params=_params("arbitrary", "arbitrary", "arbitrary"),
        name="diff_attn",
    )(da, da, da, lam_q1, lam_k1, lam_q2, lam_k2, da_norm_w, w_gate, w_up, w_down)


def _out_proj_kernel(a_ref, b_ref, wa_ref, wb_ref, x_ref, o_ref):
    acc = jnp.dot(a_ref[...], wa_ref[...], preferred_element_type=F32)
    acc = acc + jnp.dot(b_ref[...], wb_ref[...], preferred_element_type=F32)
    o_ref[...] = x_ref[...] + acc


def _out_proj(o_dn, o_da, w16, x2d, tm, tn):
    m, d = x2d.shape
    ka, kb = o_dn.shape[1], o_da.shape[1]
    assert ka == kb and w16.shape[0] == ka + kb, "the two head groups are equal row halves of w_out"
    return pl.pallas_call(
        _out_proj_kernel,
        grid=(m // tm, d // tn),
        in_specs=[pl.BlockSpec((tm, ka), lambda i, j: (i, 0)),
                  pl.BlockSpec((tm, kb), lambda i, j: (i, 0)),
                  pl.BlockSpec((ka, tn), lambda i, j: (0, j)),
                  pl.BlockSpec((kb, tn), lambda i, j: (1, j)),
                  pl.BlockSpec((tm, tn), lambda i, j: (i, j))],
        out_specs=pl.BlockSpec((tm, tn), lambda i, j: (i, j)),
        out_shape=jax.ShapeDtypeStruct((m, d), F32),
        compiler_params=_params("parallel", "arbitrary"),
        name="out_proj",
    )(o_dn, o_da, w16, w16, x2d)


def _ffn_kernel(x_ref, nw_ref, wg_ref, wu_ref, wd_ref, fw_ref, o_ref, h_ref):
    f = pl.program_id(1)

    def ffn_slab(h):
        g = jnp.dot(h, wg_ref[...], preferred_element_type=F32)
        u = jnp.dot(h, wu_ref[...], preferred_element_type=F32)
        act = (_silu(g) * u).astype(BF16)
        o_ref[...] += jnp.dot(act, wd_ref[...], preferred_element_type=F32)

    @pl.when(f == 0)
    def _():
        x = x_ref[...]
        h = _rmsnorm_rows(x, nw_ref[...]).astype(BF16)
        h_ref[...] = h
        o_ref[...] = x
        ffn_slab(h)

    @pl.when(f > 0)
    def _():
        ffn_slab(h_ref[...])

    @pl.when(f == pl.num_programs(1) - 1)
    def _():
        o_ref[...] = _rmsnorm_rows(o_ref[...], fw_ref[...])


def _ffn(x2d, ffn_norm_w, w_gate, w_up, w_down, final_norm_w, tm, tf):
    m, d = x2d.shape
    dff = w_gate.shape[1]
    return pl.pallas_call(
        _ffn_kernel,
        grid=(m // tm, dff // tf),
        in_specs=[pl.BlockSpec((tm, d), lambda i, f: (i, 0)),
                  pl.BlockSpec((1, d), lambda i, f: (0, 0)),
                  pl.BlockSpec((d, tf), lambda i, f: (0, f)),
                  pl.BlockSpec((d, tf), lambda i, f: (0, f)),
                  pl.BlockSpec((tf, d), lambda i, f: (f, 0)),
                  pl.BlockSpec((1, d), lambda i, f: (0, 0))],
        out_specs=pl.BlockSpec((tm, d), lambda i, f: (i, 0)),
        out_shape=jax.ShapeDtypeStruct((m, d), F32),
        scratch_shapes=[pltpu.VMEM((tm, d), BF16)],
        compiler_params=_params("parallel", "arbitrary"),
        name="ffn",
    )(x2d, ffn_norm_w, w_gate, w_up, w_down, final_norm_w)


def _tile(n, pref):
    return pref if n % pref == 0 else n


def _layer(x, lam_init, attn_norm_w, w_in, conv_w, a_log, dt_bias, dn_norm_w, lam_q1, lam_k1,
           lam_q2, lam_k2, da_norm_w, w_out, ffn_norm_w, w_gate, w_up, w_down, final_norm_w):
    b, t, d = x.shape
    m = b * t
    x2d = x.reshape(m, d)
    off_z = 3 * DN_WIDTH
    off_b = off_z + DN_WIDTH
    off_q = off_b + 2 * DN_HEADS
    nw = attn_norm_w.reshape(1, d)

    n_da = w_in.shape[1] - off_q
    da_scale = jnp.concatenate([jnp.full((1, DA_HEADS * 2 * DA_DIM), LOG2E * DA_DIM ** -0.5, F32),
                                jnp.ones((1, n_da - DA_HEADS * 2 * DA_DIM), F32)], axis=1)

    w_in_t = w_in.T
    grow, gcol, h, w_dn, w_da = _gates(x, nw, w_in_t, a_log.reshape(DN_HEADS, 1),
                                       dt_bias.reshape(DN_HEADS, 1), off_b, _tile(t, 512))
    h2d = h.reshape(m, d)
    tp = _tile(m, 2048)
    qkvz = _proj(h2d, w_dn, jnp.ones((1, off_b), F32), F32, tp, 1024, "in_proj_dn")
    da = _proj(h2d, w_da, da_scale, BF16, tp, 1024, "in_proj_da")
    tm = _tile(m, 1024)

    qkvz = qkvz.reshape(b, t, off_b)
    u, w, qd, kd, a, w_out16 = _dn_intra(qkvz, conv_w, gcol, grow, w_out, _tile(t, 4096))
    o_dn = _dn_scan(u, w, qd, kd, a, qkvz, gcol, dn_norm_w.reshape(1, DN_DIM),
                    4 if b % 4 == 0 else 1, _tile(t, 128))

    o_da, w_gate16, w_up16, w_down16 = _attention(
        da.reshape(b, t, n_da), lam_q1.reshape(1, DA_DIM), lam_k1.reshape(1, DA_DIM),
        lam_q2.reshape(1, DA_DIM), lam_k2.reshape(1, DA_DIM), da_norm_w.reshape(1, DA_VDIM),
        w_gate, w_up, w_down, lam_init, _tile(t, 512))

    x2d = _out_proj(o_dn.reshape(m, DN_WIDTH), o_da.reshape(m, DA_WIDTH), w_out16, x2d,
                    _tile(m, 512), d)
    dff = w_gate.shape[1]
    return _ffn(x2d, ffn_norm_w.reshape(1, d), w_gate16, w_up16, w_down16,
                final_norm_w.reshape(1, d), tm, _tile(dff, 512))


def kernel(x, attn_norm_w, w_in, conv_w, a_log, dt_bias, dn_norm_w, lam_q1, lam_k1, lam_q2, lam_k2,
           da_norm_w, w_out, ffn_norm_w, w_gate, w_up, w_down, final_norm_w):
    b, t, d = x.shape
    depth = w_in.shape[0]
    assert depth == 1, "the final rmsnorm is fused into the last layer's FFN kernel"
    lam_init = 0.8 - 0.6 * math.exp(-0.3 * 0)
    out = _layer(x, lam_init, attn_norm_w[0], w_in[0], conv_w[0], a_log[0], dt_bias[0],
                 dn_norm_w[0], lam_q1[0], lam_k1[0], lam_q2[0], lam_k2[0], da_norm_w[0],
                 w_out[0], ffn_norm_w[0], w_gate[0], w_up[0], w_down[0], final_norm_w)
    return out.reshape(b, t, d)
```

```python
import functools
import math

import jax
import jax.numpy as jnp
from jax import lax
from jax.experimental import pallas as pl
from jax.experimental.pallas import tpu as pltpu

F32 = jnp.float32
BF16 = jnp.bfloat16
EPS = 1e-6
NEG_BIG = -1e30
LOG2E = math.log2(math.e)

DN_HEADS = 8
DN_DIM = 128
DN_WIDTH = DN_HEADS * DN_DIM
CONV_WIDTH = 4
CHUNK = 64
DA_HEADS = 4
DA_DIM = 128
DA_VDIM = 2 * DA_DIM
DA_WIDTH = DA_HEADS * DA_VDIM
LANES = 128
SUBLANES = 8
GROUP = 128
IN_FLIGHT = 7
GATE_SLAB = 256
VMEM_LIMIT = 56 * 1024 * 1024

_NT = (((1,), (1,)), ((), ()))
_TN = (((0,), (0,)), ((), ()))


def _params(*sem):
    return pltpu.CompilerParams(dimension_semantics=sem, vmem_limit_bytes=VMEM_LIMIT)


def _sigmoid(x):
    return 1.0 / (1.0 + jnp.exp(-x))


def _silu(x):
    h = 0.5 * x
    return h + h * jnp.tanh(h)


def _lane_tile(x, n):
    return jnp.concatenate([x] * n, axis=1)


def _rmsnorm_rows(x, w):
    return x * lax.rsqrt(jnp.mean(x * x, axis=-1, keepdims=True) + EPS) * w


def _proj_kernel(h_ref, w_ref, cs_ref, o_ref):
    acc = lax.dot_general(h_ref[...], w_ref[...], _NT, preferred_element_type=F32)
    o_ref[...] = (acc * cs_ref[...]).astype(o_ref.dtype)


def _proj(h2d, w_t_bf16, col_scale, out_dtype, tm, tn, name):
    m, d = h2d.shape
    n = w_t_bf16.shape[0]
    return pl.pallas_call(
        _proj_kernel,
        grid=(m // tm, n // tn),
        in_specs=[
            pl.BlockSpec((tm, d), lambda i, j: (i, 0)),
            pl.BlockSpec((tn, d), lambda i, j: (j, 0)),
            pl.BlockSpec((1, tn), lambda i, j: (0, j)),
        ],
        out_specs=pl.BlockSpec((tm, tn), lambda i, j: (i, j)),
        out_shape=jax.ShapeDtypeStruct((m, n), out_dtype),
        compiler_params=_params("parallel", "arbitrary"),
        name=name,
    )(h2d, w_t_bf16, col_scale)


def _gates_kernel(x_ref, nw_ref, wt_ref, alog_ref, dtb_ref, wdn_ref, *rest):
    wda_refs, (row_ref, col_ref, h_ref, wdn16_ref, wda16_ref) = rest[:-5], rest[-5:]
    wdn16_ref[...] = wdn_ref[...].astype(BF16)
    wda16_ref[...] = jnp.concatenate([r[...] for r in wda_refs], axis=0).astype(BF16)

    h = _rmsnorm_rows(x_ref[...], nw_ref[...]).astype(BF16)
    h_ref[...] = h
    r = lax.dot_general(wt_ref[...].astype(BF16), h, _NT, preferred_element_type=F32)
    beta = _sigmoid(r[0:DN_HEADS])
    a = r[DN_HEADS:2 * DN_HEADS] + dtb_ref[...]
    softplus = jnp.maximum(a, 0.0) + jnp.log1p(jnp.exp(-jnp.abs(a)))
    g = -jnp.exp(alog_ref[...]) * softplus
    tm = g.shape[1]
    sw = min(tm, GATE_SLAB)
    ii = lax.broadcasted_iota(jnp.int32, (sw, sw), 0)
    jj = lax.broadcasted_iota(jnp.int32, (sw, sw), 1)
    same = (ii // CHUNK) == (jj // CHUNK)
    sums = jnp.concatenate([same & (ii <= jj), same], axis=1).astype(BF16)
    g1 = g.astype(BF16)
    r1 = g - g1.astype(F32)
    g2 = r1.astype(BF16)
    g3 = (r1 - g2.astype(F32)).astype(BF16)
    pieces = jnp.concatenate([g1, g2, g3, jnp.zeros_like(g1)], axis=0)
    nh = DN_HEADS
    gc, gl = [], []
    for s0 in range(0, tm, sw):
        acc = jnp.dot(pieces[:, s0:s0 + sw], sums, preferred_element_type=F32)
        acc = acc[0:nh] + acc[nh:2 * nh] + acc[2 * nh:3 * nh]
        gc.append(acc[:, :sw])
        gl.append(acc[:, sw:])
    gc, gl = jnp.concatenate(gc, axis=1), jnp.concatenate(gl, axis=1)
    rows = jnp.concatenate([beta, gc, gl], axis=0)
    row_ref[...] = rows
    padded = jnp.concatenate([rows, jnp.zeros((LANES - 3 * DN_HEADS, tm), F32)], axis=0)
    col_ref[...] = padded.T


def _gates(x3d, norm_w, w_in_t, a_log_col, dt_bias_col, off_gate, tm):
    b, t, d = x3d.shape
    nj = t // tm
    steps = b * nj
    piece = 2 * DN_HEADS
    off_da = off_gate + piece
    n_da = w_in_t.shape[0] - off_da
    rows_dn, rows_da = off_gate // steps, n_da // steps
    assert rows_dn * steps == off_gate and rows_dn % piece == 0
    assert rows_da * steps == n_da and rows_da % piece == 0 and off_da % piece == 0
    n_pieces = rows_da // piece

    def step(i, j):
        return i * nj + j

    def da_piece(k):
        return pl.BlockSpec((piece, d),
                            lambda i, j: (off_da // piece + step(i, j) * n_pieces + k, 0))

    return pl.pallas_call(
        _gates_kernel,
        grid=(b, nj),
        in_specs=[
            pl.BlockSpec((None, tm, d), lambda i, j: (i, j, 0)),
            pl.BlockSpec((1, d), lambda i, j: (0, 0)),
            pl.BlockSpec((piece, d), lambda i, j: (off_gate // piece, 0)),
            pl.BlockSpec((DN_HEADS, 1), lambda i, j: (0, 0)),
            pl.BlockSpec((DN_HEADS, 1), lambda i, j: (0, 0)),
            pl.BlockSpec((rows_dn, d), lambda i, j: (step(i, j), 0)),
        ] + [da_piece(k) for k in range(n_pieces)],
        out_specs=[
            pl.BlockSpec((None, 3 * DN_HEADS, tm), lambda i, j: (i, 0, j)),
            pl.BlockSpec((None, tm, LANES), lambda i, j: (i, j, 0)),
            pl.BlockSpec((None, tm, d), lambda i, j: (i, j, 0)),
            pl.BlockSpec((rows_dn, d), lambda i, j: (step(i, j), 0)),
            pl.BlockSpec((rows_da, d), lambda i, j: (step(i, j), 0)),
        ],
        out_shape=[
            jax.ShapeDtypeStruct((b, 3 * DN_HEADS, t), F32),
            jax.ShapeDtypeStruct((b, t, LANES), F32),
            jax.ShapeDtypeStruct((b, t, d), BF16),
            jax.ShapeDtypeStruct((off_gate, d), BF16),
            jax.ShapeDtypeStruct((n_da, d), BF16),
        ],
        compiler_params=_params("parallel", "arbitrary"),
        name="dn_gates",
    )(x3d, norm_w, w_in_t, a_log_col, dt_bias_col, w_in_t, *([w_in_t] * n_pieces))


def _dn_intra_kernel(q_ref, k_ref, v_ref, qh_ref, kh_ref, vh_ref, cwq_ref, cwk_ref, cwv_ref,
                     gcol_ref, grow_ref, wo_ref, u_ref, w_ref, qd_ref, kd_ref, a_ref, wo16_ref, xs_ref):
    wo16_ref[...] = wo_ref[...].astype(BF16)
    head = pl.program_id(1)
    first = pl.program_id(2) == 0
    tb = q_ref.shape[0]
    halo = SUBLANES
    for idx, (x_ref, xh_ref) in enumerate(((q_ref, qh_ref), (k_ref, kh_ref), (v_ref, vh_ref))):
        xs_ref[idx, 0:halo, :] = jnp.where(first, 0.0, xh_ref[...])
        xs_ref[idx, halo:halo + tb, :] = x_ref[...]

    lane = lax.broadcasted_iota(jnp.int32, (GROUP, LANES), 1)
    sel_beta = (lane == head).astype(F32)
    sel_gc = (lane == head + DN_HEADS).astype(F32)
    sel_gl = (lane == head + 2 * DN_HEADS).astype(F32)
    ri = lax.broadcasted_iota(jnp.int32, (LANES, LANES), 0)
    ci = lax.broadcasted_iota(jnp.int32, (LANES, LANES), 1)
    same = (ri // CHUNK) == (ci // CHUNK)
    incl = same & (ri >= ci)
    strict = same & (ri > ci)

    def conv_silu(idx, cw_ref, r0, anchor):
        base = r0 + halo - (CONV_WIDTH - 1)
        taps = cw_ref[...] if anchor is None else cw_ref[...] + anchor
        acc = xs_ref[idx, base:base + GROUP, :] * taps[0:1, :]
        for i in range(1, CONV_WIDTH):
            acc = acc + xs_ref[idx, base + i:base + i + GROUP, :] * taps[i:i + 1, :]
        return _silu(acc)

    n_groups = tb // GROUP
    levels = int(math.log2(CHUNK))
    n_pows = [None] * n_groups
    xs = [None] * n_groups
    gc_rows = grow_ref[pl.ds(head + DN_HEADS, 1), :]

    def prepare(gi):
        r0 = gi * GROUP
        anchor = None if gi < IN_FLIGHT else xs[gi - IN_FLIGHT][0:1, :DN_DIM] * 0.0
        q = conv_silu(0, cwq_ref, r0, anchor)
        k = conv_silu(1, cwk_ref, r0, anchor)
        v = conv_silu(2, cwv_ref, r0, anchor)
        q = q * lax.rsqrt(jnp.sum(q * q, axis=-1, keepdims=True) + EPS) * (DN_DIM ** -0.5)
        k = k * lax.rsqrt(jnp.sum(k * k, axis=-1, keepdims=True) + EPS)

        gates = gcol_ref[r0:r0 + GROUP, :]
        beta = jnp.sum(gates * sel_beta, axis=-1, keepdims=True)
        gc = jnp.sum(gates * sel_gc, axis=-1, keepdims=True)
        gl = jnp.sum(gates * sel_gl, axis=-1, keepdims=True)
        gc_row = gc_rows[:, r0:r0 + GROUP]

        kb = k * beta
        k16 = k.astype(BF16)
        gram = lax.dot_general(jnp.concatenate([kb, q], axis=0).astype(BF16), k16, _NT,
                               preferred_element_type=F32)
        m_blocks, a_blocks = [], []
        for d0 in range(0, GROUP, LANES):
            dd = slice(d0, d0 + LANES)
            decay = jnp.exp(jnp.where(incl, gc[dd] - gc_row[:, dd], NEG_BIG))
            m_blocks.append(jnp.where(strict, gram[dd, dd] * decay, 0.0))
            a_blocks.append(gram[GROUP + d0:GROUP + d0 + LANES, dd] * decay)
        zero = jnp.zeros((LANES, LANES), F32)
        n_pows[gi] = jnp.concatenate(
            [jnp.concatenate([m_blocks[i] if j == i else zero for j in range(len(m_blocks))], axis=1)
             for i in range(len(m_blocks))], axis=0)

        egc = jnp.exp(gc)
        xs[gi] = jnp.concatenate([v * beta, kb * egc], axis=1)
        qd_ref[r0:r0 + GROUP, :] = (q * egc).astype(BF16)
        kd_ref[r0:r0 + GROUP, :] = (k * jnp.exp(gl - gc)).astype(BF16)
        for c in range(GROUP // CHUNK):
            lo = c * CHUNK
            in_block = lo % LANES
            a_ref[r0 + lo:r0 + lo + CHUNK, :] = a_blocks[lo // LANES][
                in_block:in_block + CHUNK, in_block:in_block + CHUNK].astype(BF16)

    def level(gi, lvl):
        n16 = n_pows[gi].astype(BF16)
        step = jnp.dot(n16, xs[gi].astype(BF16), preferred_element_type=F32)
        xs[gi] = xs[gi] - step if lvl == 0 else xs[gi] + step
        if lvl + 1 < levels:
            n_pows[gi] = jnp.dot(n16, n16, preferred_element_type=F32)

    def finish(gi):
        r0 = gi * GROUP
        u_ref[r0:r0 + GROUP, :] = xs[gi][:, :DN_DIM]
        w_ref[r0:r0 + GROUP, :] = xs[gi][:, DN_DIM:].astype(BF16)

    for t in range(n_groups + levels + 1):
        for gi in range(n_groups):
            stage = t - gi
            if stage == 0:
                prepare(gi)
            elif 1 <= stage <= levels:
                level(gi, stage - 1)
            elif stage == levels + 1:
                finish(gi)


def _dn_intra(qkvz, conv_w, gcol, grow, w_out, tb):
    b, t, _ = qkvz.shape
    hb = tb // SUBLANES
    nh = DN_HEADS
    nj = t // tb
    wo_rows, wo_share = _slab_rows(w_out.shape[0], b * nh * nj)
    assert wo_share == 1, "every grid step converts its own slab of w_out"
    wo_spec = pl.BlockSpec((wo_rows, w_out.shape[1]), lambda i, h, j: ((i * nh + h) * nj + j, 0))

    def tok(off):
        return pl.BlockSpec((None, tb, DN_DIM), lambda i, h, j: (i, j, off + h))

    def halo(off):
        return pl.BlockSpec((None, SUBLANES, DN_DIM),
                            lambda i, h, j: (i, jnp.maximum(j * hb - 1, 0), off + h))

    def cw(off):
        return pl.BlockSpec((CONV_WIDTH, DN_DIM), lambda i, h, j: (0, off + h))

    out_tok = pl.BlockSpec((None, tb, DN_DIM), lambda i, h, j: (i, j, h))
    return pl.pallas_call(
        _dn_intra_kernel,
        grid=(b, nh, nj),
        in_specs=[tok(0), tok(nh), tok(2 * nh), halo(0), halo(nh), halo(2 * nh),
                  cw(0), cw(nh), cw(2 * nh),
                  pl.BlockSpec((None, tb, LANES), lambda i, h, j: (i, j, 0)),
                  pl.BlockSpec((None, 3 * nh, tb), lambda i, h, j: (i, 0, j)),
                  wo_spec],
        out_specs=[out_tok, out_tok, out_tok, out_tok,
                   pl.BlockSpec((None, None, tb, CHUNK), lambda i, h, j: (i, h, j, 0)),
                   wo_spec],
        out_shape=[jax.ShapeDtypeStruct((b, t, DN_WIDTH), F32),
                   jax.ShapeDtypeStruct((b, t, DN_WIDTH), BF16),
                   jax.ShapeDtypeStruct((b, t, DN_WIDTH), BF16),
                   jax.ShapeDtypeStruct((b, t, DN_WIDTH), BF16),
                   jax.ShapeDtypeStruct((b, nh, t, CHUNK), BF16),
                   jax.ShapeDtypeStruct(w_out.shape, BF16)],
        scratch_shapes=[pltpu.VMEM((3, tb + SUBLANES, DN_DIM), F32)],
        compiler_params=_params("parallel", "parallel", "arbitrary"),
        name="dn_intra",
    )(qkvz, qkvz, qkvz, qkvz, qkvz, qkvz, conv_w, conv_w, conv_w, gcol, grow, w_out)


def _dn_scan_kernel(u_ref, w_ref, qd_ref, kd_ref, a_ref, z_ref, gcol_ref, nw_ref, o_ref, s_ref):
    @pl.when(pl.program_id(1) == 0)
    def _():
        s_ref[...] = jnp.zeros_like(s_ref)

    bb, tb = u_ref.shape[0], u_ref.shape[1]
    nw = nw_ref[...]
    units = [(i, h) for i in range(bb) for h in range(DN_HEADS)]

    def chunk(c, carry):
        rows = pl.ds(pl.multiple_of(c * CHUNK, CHUNK), CHUNK)
        cols = [slice(h * DN_DIM, (h + 1) * DN_DIM) for h in range(DN_HEADS)]
        s = [s_ref[i, h] for i, h in units]
        proj = [jnp.dot(jnp.concatenate([w_ref[i, rows, cols[h]], qd_ref[i, rows, cols[h]]], axis=0),
                        s[n].astype(BF16), preferred_element_type=F32)
                for n, (i, h) in enumerate(units)]
        v16 = [(u_ref[i, rows, cols[h]] - proj[n][:CHUNK]).astype(BF16)
               for n, (i, h) in enumerate(units)]
        upd = [lax.dot_general(kd_ref[i, rows, cols[h]], v16[n], _TN, preferred_element_type=F32)
               for n, (i, h) in enumerate(units)]
        for n, (i, h) in enumerate(units):
            gl = gcol_ref[i, pl.ds(pl.multiple_of(c * CHUNK, CHUNK), 1),
                          2 * DN_HEADS + h:2 * DN_HEADS + h + 1]
            s_ref[i, h] = s[n] * jnp.exp(gl) + upd[n]
        o = [proj[n][CHUNK:] + jnp.dot(a_ref[i, h, rows, :], v16[n], preferred_element_type=F32)
             for n, (i, h) in enumerate(units)]
        for n, (i, h) in enumerate(units):
            z = z_ref[i, rows, cols[h]]
            o_ref[i, rows, cols[h]] = (_rmsnorm_rows(o[n], nw) * _silu(z)).astype(o_ref.dtype)
        return carry

    lax.fori_loop(0, tb // CHUNK, chunk, 0)


def _dn_scan(u, w, qd, kd, a, qkvz, gcol, dn_norm_w, bb, tb):
    b, t, _ = u.shape
    tok = pl.BlockSpec((bb, tb, DN_WIDTH), lambda i, j: (i, j, 0))
    z_block = 3 * DN_WIDTH // DN_WIDTH
    return pl.pallas_call(
        _dn_scan_kernel,
        grid=(b // bb, t // tb),
        in_specs=[tok, tok, tok, tok,
                  pl.BlockSpec((bb, DN_HEADS, tb, CHUNK), lambda i, j: (i, 0, j, 0)),
                  pl.BlockSpec((bb, tb, DN_WIDTH), lambda i, j: (i, j, z_block)),
                  pl.BlockSpec((bb, tb, LANES), lambda i, j: (i, j, 0)),
                  pl.BlockSpec((1, DN_DIM), lambda i, j: (0, 0))],
        out_specs=tok,
        out_shape=jax.ShapeDtypeStruct((b, t, DN_WIDTH), BF16),
        scratch_shapes=[pltpu.VMEM((bb, DN_HEADS, DN_DIM, DN_DIM), F32)],
        compiler_params=_params("parallel", "arbitrary"),
        name="dn_scan",
    )(u, w, qd, kd, a, qkvz, gcol, dn_norm_w)


def _attn_kernel(q_ref, k_ref, v_ref, lq1_ref, lk1_ref, lq2_ref, lk2_ref, nw_ref,
                 wg_ref, wu_ref, wd_ref, o_ref, wg16_ref, wu16_ref, wd16_ref,
                 m_ref, l_ref, acc_ref, sa_ref, sb_ref, *, lam_init, down_share):
    def convert_gate_up():
        wg16_ref[...] = wg_ref[...].astype(BF16)
        wu16_ref[...] = wu_ref[...].astype(BF16)

    grid_step = ((pl.program_id(0) * pl.num_programs(1) + pl.program_id(1)) * pl.num_programs(2)
                 + pl.program_id(2))

    @pl.when(grid_step % down_share == 0)
    def _():
        wd16_ref[...] = wd_ref[...].astype(BF16)

    head = pl.program_id(1)
    qi = pl.program_id(2)
    tq = q_ref.shape[0]
    tk = tq
    slope = jnp.float32(0.0)
    for hh in range(DA_HEADS):
        slope = jnp.where(head == hh, jnp.float32(2.0 ** (-8.0 * (hh + 1) / DA_HEADS)), slope)

    kpos = lax.broadcasted_iota(jnp.int32, (1, tk), 1)
    slope2 = slope * LOG2E

    maps = range(2)

    def scores(j, s_ref):
        k0 = pl.multiple_of(j * tk, tk)
        bias = slope2 * (kpos + (j - qi) * tk).astype(F32)
        for c in maps:
            cols = slice(c * DA_DIM, (c + 1) * DA_DIM)
            s_ref[c] = lax.dot_general(q_ref[:, cols], k_ref[pl.ds(k0, tk), cols], _NT,
                                       preferred_element_type=F32) + bias

    def accumulate(j, s_ref, masked, first=False):
        k0 = pl.multiple_of(j * tk, tk)
        v = v_ref[pl.ds(k0, tk), :]
        s = [s_ref[c] for c in maps]
        if masked:
            rr = lax.broadcasted_iota(jnp.int32, (tq, tk), 0)
            cc = lax.broadcasted_iota(jnp.int32, (tq, tk), 1)
            s = [jnp.where(cc <= rr, s[c], NEG_BIG) for c in maps]
        m_cur = [jnp.broadcast_to(jnp.max(s[c], axis=-1, keepdims=True), (tq, LANES)) for c in maps]
        if first:
            m_new = m_cur
        else:
            m_prev = [m_ref[c] for c in maps]
            m_new = [jnp.maximum(m_prev[c], m_cur[c]) for c in maps]
        p = [jnp.exp2(s[c] - _lane_tile(m_new[c], tk // LANES)) for c in maps]
        p_sum = [jnp.broadcast_to(jnp.sum(p[c], axis=-1, keepdims=True), (tq, LANES)) for c in maps]
        p16 = [p[c].astype(BF16) for c in maps]
        pv = [jnp.dot(p16[c], v, preferred_element_type=F32) for c in maps]
        for c in maps:
            if first:
                l_ref[c] = p_sum[c]
                acc_ref[c] = pv[c]
            else:
                alpha = jnp.exp2(m_prev[c] - m_new[c])
                l_ref[c] = alpha * l_ref[c] + p_sum[c]
                acc_ref[c] = _lane_tile(alpha, DA_VDIM // LANES) * acc_ref[c] + pv[c]
            m_ref[c] = m_new[c]

    def finalize():
        lam = (jnp.exp(jnp.sum(lq1_ref[...] * lk1_ref[...], axis=-1, keepdims=True))
               - jnp.exp(jnp.sum(lq2_ref[...] * lk2_ref[...], axis=-1, keepdims=True)) + lam_init)
        rep = DA_VDIM // LANES
        inv_l0 = 1.0 / l_ref[0]
        inv_l1 = lam / l_ref[1]
        o = acc_ref[0] * _lane_tile(inv_l0, rep) - acc_ref[1] * _lane_tile(inv_l1, rep)
        o_ref[...] = (_rmsnorm_rows(o, nw_ref[...]) * (1.0 - lam_init)).astype(o_ref.dtype)

    @pl.when(qi == 0)
    def _():
        convert_gate_up()
        scores(0, sa_ref)
        accumulate(0, sa_ref, True, first=True)
        finalize()

    @pl.when(qi > 0)
    def _():
        convert_gate_up()
        scores(0, sa_ref)
        scores(1, sb_ref)
        accumulate(0, sa_ref, False, first=True)

        def pair(jj, carry):
            j = 2 * jj + 1
            scores(j + 1, sa_ref)
            accumulate(j, sb_ref, False)
            scores(j + 2, sb_ref)
            accumulate(j + 1, sa_ref, False)
            return carry

        lax.fori_loop(0, (qi - 1) // 2, pair, 0)

        @pl.when(qi % 2 == 0)
        def _():
            scores(qi, sa_ref)
            accumulate(qi - 1, sb_ref, False)
            accumulate(qi, sa_ref, True)
            finalize()

        @pl.when(qi % 2 == 1)
        def _():
            accumulate(qi, sb_ref, True)
            finalize()


def _slab_rows(n_rows, n_steps):
    for share in (1, 2, 4, 8, 16):
        if n_steps % share == 0 and n_rows % (n_steps // share) == 0:
            rows = n_rows // (n_steps // share)
            if rows % 16 == 0:
                return rows, share
    raise ValueError(f"cannot split {n_rows} weight rows over {n_steps} grid steps")


def _attention(da, lam_q1, lam_k1, lam_q2, lam_k2, da_norm_w, w_gate, w_up, w_down, lam_init, tq):
    b, t, _ = da.shape
    nh = DA_HEADS
    nq = t // tq
    d, dff = w_gate.shape
    steps = b * nh * nq
    rg, sg = _slab_rows(d, steps)
    rd, sd = _slab_rows(dff, steps)

    def step(i, h, j):
        return (i * nh + h) * nq + j

    vec = pl.BlockSpec((1, DA_DIM), lambda i, h, j: (0, 0))
    up_spec = pl.BlockSpec((rg, dff), lambda i, h, j: (step(i, h, j) // sg, 0))
    down_spec = pl.BlockSpec((rd, d), lambda i, h, j: (step(i, h, j) // sd, 0))
    return pl.pallas_call(
        functools.partial(_attn_kernel, lam_init=lam_init, down_share=sd),
        grid=(b, nh, nq),
        in_specs=[pl.BlockSpec((None, tq, DA_VDIM), lambda i, h, j: (i, j, h)),
                  pl.BlockSpec((None, t, DA_VDIM), lambda i, h, j: (i, 0, nh + h)),
                  pl.BlockSpec((None, t, DA_VDIM), lambda i, h, j: (i, 0, 2 * nh + h)),
                  vec, vec, vec, vec,
                  pl.BlockSpec((1, DA_VDIM), lambda i, h, j: (0, 0)),
                  up_spec, up_spec, down_spec],
        out_specs=[pl.BlockSpec((None, tq, DA_VDIM), lambda i, h, j: (i, j, h)),
                   up_spec, up_spec, down_spec],
        out_shape=[jax.ShapeDtypeStruct((b, t, DA_WIDTH), BF16),
                   jax.ShapeDtypeStruct((d, dff), BF16),
                   jax.ShapeDtypeStruct((d, dff), BF16),
                   jax.ShapeDtypeStruct((dff, d), BF16)],
        scratch_shapes=[pltpu.VMEM((2, tq, LANES), F32), pltpu.VMEM((2, tq, LANES), F32),
                        pltpu.VMEM((2, tq, DA_VDIM), F32),
                        pltpu.VMEM((2, tq, tq), F32), pltpu.VMEM((2, tq, tq), F32)],
        compiler_params=_params("arbitrary", "arbitrary", "arbitrary"),
        name="diff_attn",
    )(da, da, da, lam_q1, lam_k1, lam_q2, lam_k2, da_norm_w, w_gate, w_up, w_down)


def _out_proj_kernel(a_ref, b_ref, wa_ref, wb_ref, x_ref, o_ref):
    acc = jnp.dot(a_ref[...], wa_ref[...], preferred_element_type=F32)
    acc = acc + jnp.dot(b_ref[...], wb_ref[...], preferred_element_type=F32)
    o_ref[...] = x_ref[...] + acc


def _out_proj(o_dn, o_da, w16, x2d, tm, tn):
    m, d = x2d.shape
    ka, kb = o_dn.shape[1], o_da.shape[1]
    assert ka == kb and w16.shape[0] == ka + kb, "the two head groups are equal row halves of w_out"
    return pl.pallas_call(
        _out_proj_kernel,
        grid=(m // tm, d // tn),
        in_specs=[pl.BlockSpec((tm, ka), lambda i, j: (i, 0)),
                  pl.BlockSpec((tm, kb), lambda i, j: (i, 0)),
                  pl.BlockSpec((ka, tn), lambda i, j: (0, j)),
                  pl.BlockSpec((kb, tn), lambda i, j: (1, j)),
                  pl.BlockSpec((tm, tn), lambda i, j: (i, j))],
        out_specs=pl.BlockSpec((tm, tn), lambda i, j: (i, j)),
        out_shape=jax.ShapeDtypeStruct((m, d), F32),
        compiler_params=_params("parallel", "arbitrary"),
        name="out_proj",
    )(o_dn, o_da, w16, w16, x2d)


def _ffn_kernel(x_ref, nw_ref, wg_ref, wu_ref, wd_ref, fw_ref, o_ref, h_ref):
    f = pl.program_id(1)

    @pl.when(f == 0)
    def _():
        x = x_ref[...]
        h_ref[...] = _rmsnorm_rows(x, nw_ref[...]).astype(BF16)
        o_ref[...] = x

    h = h_ref[...]
    g = jnp.dot(h, wg_ref[...], preferred_element_type=F32)
    u = jnp.dot(h, wu_ref[...], preferred_element_type=F32)
    act = (_silu(g) * u).astype(BF16)
    o_ref[...] += jnp.dot(act, wd_ref[...], preferred_element_type=F32)

    @pl.when(f == pl.num_programs(1) - 1)
    def _():
        o_ref[...] = _rmsnorm_rows(o_ref[...], fw_ref[...])


def _ffn(x2d, ffn_norm_w, w_gate, w_up, w_down, final_norm_w, tm, tf):
    m, d = x2d.shape
    dff = w_gate.shape[1]
    return pl.pallas_call(
        _ffn_kernel,
        grid=(m // tm, dff // tf),
        in_specs=[pl.BlockSpec((tm, d), lambda i, f: (i, 0)),
                  pl.BlockSpec((1, d), lambda i, f: (0, 0)),
                  pl.BlockSpec((d, tf), lambda i, f: (0, f)),
                  pl.BlockSpec((d, tf), lambda i, f: (0, f)),
                  pl.BlockSpec((tf, d), lambda i, f: (f, 0)),
                  pl.BlockSpec((1, d), lambda i, f: (0, 0))],
        out_specs=pl.BlockSpec((tm, d), lambda i, f: (i, 0)),
        out_shape=jax.ShapeDtypeStruct((m, d), F32),
        scratch_shapes=[pltpu.VMEM((tm, d), BF16)],
        compiler_params=_params("parallel", "arbitrary"),
        name="ffn",
    )(x2d, ffn_norm_w, w_gate, w_up, w_down, final_norm_w)


def _tile(n, pref):
    return pref if n % pref == 0 else n


def _layer(x, lam_init, attn_norm_w, w_in, conv_w, a_log, dt_bias, dn_norm_w, lam_q1, lam_k1,
           lam_q2, lam_k2, da_norm_w, w_out, ffn_norm_w, w_gate, w_up, w_down, final_norm_w):
    b, t, d = x.shape
    m = b * t
    x2d = x.reshape(m, d)
    off_z = 3 * DN_WIDTH
    off_b = off_z + DN_WIDTH
    off_q = off_b + 2 * DN_HEADS
    nw = attn_norm_w.reshape(1, d)

    n_da = w_in.shape[1] - off_q
    da_scale = jnp.concatenate([jnp.full((1, DA_HEADS * 2 * DA_DIM), LOG2E * DA_DIM ** -0.5, F32),
                                jnp.ones((1, n_da - DA_HEADS * 2 * DA_DIM), F32)], axis=1)

    w_in_t = w_in.T
    grow, gcol, h, w_dn, w_da = _gates(x, nw, w_in_t, a_log.reshape(DN_HEADS, 1),
                                       dt_bias.reshape(DN_HEADS, 1), off_b, _tile(t, 512))
    h2d = h.reshape(m, d)
    tp = _tile(m, 2048)
    qkvz = _proj(h2d, w_dn, jnp.ones((1, off_b), F32), F32, tp, 1024, "in_proj_dn")
    da = _proj(h2d, w_da, da_scale, BF16, tp, 1024, "in_proj_da")
    tm = _tile(m, 1024)

    qkvz = qkvz.reshape(b, t, off_b)
    u, w, qd, kd, a, w_out16 = _dn_intra(qkvz, conv_w, gcol, grow, w_out, _tile(t, 4096))
    o_dn = _dn_scan(u, w, qd, kd, a, qkvz, gcol, dn_norm_w.reshape(1, DN_DIM),
                    4 if b % 4 == 0 else 1, _tile(t, 256))

    o_da, w_gate16, w_up16, w_down16 = _attention(
        da.reshape(b, t, n_da), lam_q1.reshape(1, DA_DIM), lam_k1.reshape(1, DA_DIM),
        lam_q2.reshape(1, DA_DIM), lam_k2.reshape(1, DA_DIM), da_norm_w.reshape(1, DA_VDIM),
        w_gate, w_up, w_down, lam_init, _tile(t, 512))

    x2d = _out_proj(o_dn.reshape(m, DN_WIDTH), o_da.reshape(m, DA_WIDTH), w_out16, x2d,
                    _tile(m, 512), d)
    dff = w_gate.shape[1]
    return _ffn(x2d, ffn_norm_w.reshape(1, d), w_gate16, w_up16, w_down16,
                final_norm_w.reshape(1, d), tm, _tile(dff, 512))


def kernel(x, attn_norm_w, w_in, conv_w, a_log, dt_bias, dn_norm_w, lam_q1, lam_k1, lam_q2, lam_k2,
           da_norm_w, w_out, ffn_norm_w, w_gate, w_up, w_down, final_norm_w):
    b, t, d = x.shape
    depth = w_in.shape[0]
    assert depth == 1, "the final rmsnorm is fused into the last layer's FFN kernel"
    lam_init = 0.8 - 0.6 * math.exp(-0.3 * 0)
    out = _layer(x, lam_init, attn_norm_w[0], w_in[0], conv_w[0], a_log[0], dt_bias[0],
                 dn_norm_w[0], lam_q1[0], lam_k1[0], lam_q2[0], lam_k2[0], da_norm_w[0],
                 w_out[0], ffn_norm_w[0], w_gate[0], w_up[0], w_down[0], final_norm_w)
    return out.reshape(b, t, d)
```

```python
import functools
import math

import jax
import jax.numpy as jnp
from jax import lax
from jax.experimental import pallas as pl
from jax.experimental.pallas import tpu as pltpu

F32 = jnp.float32
BF16 = jnp.bfloat16
EPS = 1e-6
NEG_BIG = -1e30
LOG2E = math.log2(math.e)

DN_HEADS = 8
DN_DIM = 128
DN_WIDTH = DN_HEADS * DN_DIM
CONV_WIDTH = 4
CHUNK = 64
DA_HEADS = 4
DA_DIM = 128
DA_VDIM = 2 * DA_DIM
DA_WIDTH = DA_HEADS * DA_VDIM
LANES = 128
SUBLANES = 8
GROUP = 128
IN_FLIGHT = 7
GATE_SLAB = 256
VMEM_LIMIT = 56 * 1024 * 1024

_NT = (((1,), (1,)), ((), ()))
_TN = (((0,), (0,)), ((), ()))


def _params(*sem):
    return pltpu.CompilerParams(dimension_semantics=sem, vmem_limit_bytes=VMEM_LIMIT)


def _sigmoid(x):
    return 1.0 / (1.0 + jnp.exp(-x))


def _silu(x):
    h = 0.5 * x
    return h + h * jnp.tanh(h)


def _lane_tile(x, n):
    return jnp.concatenate([x] * n, axis=1)


def _rmsnorm_rows(x, w):
    return x * lax.rsqrt(jnp.mean(x * x, axis=-1, keepdims=True) + EPS) * w


def _proj_kernel(h_ref, w_ref, cs_ref, o_ref):
    acc = lax.dot_general(h_ref[...], w_ref[...], _NT, preferred_element_type=F32)
    o_ref[...] = (acc * cs_ref[...]).astype(o_ref.dtype)


def _proj(h2d, w_t_bf16, col_scale, out_dtype, tm, tn, name):
    m, d = h2d.shape
    n = w_t_bf16.shape[0]
    return pl.pallas_call(
        _proj_kernel,
        grid=(m // tm, n // tn),
        in_specs=[
            pl.BlockSpec((tm, d), lambda i, j: (i, 0)),
            pl.BlockSpec((tn, d), lambda i, j: (j, 0)),
            pl.BlockSpec((1, tn), lambda i, j: (0, j)),
        ],
        out_specs=pl.BlockSpec((tm, tn), lambda i, j: (i, j)),
        out_shape=jax.ShapeDtypeStruct((m, n), out_dtype),
        compiler_params=_params("parallel", "arbitrary"),
        name=name,
    )(h2d, w_t_bf16, col_scale)


def _gates_kernel(x_ref, nw_ref, wt_ref, alog_ref, dtb_ref, wdn_ref, *rest):
    wda_refs, (row_ref, col_ref, h_ref, wdn16_ref, wda16_ref) = rest[:-5], rest[-5:]
    wdn16_ref[...] = wdn_ref[...].astype(BF16)
    wda16_ref[...] = jnp.concatenate([r[...] for r in wda_refs], axis=0).astype(BF16)

    h = _rmsnorm_rows(x_ref[...], nw_ref[...]).astype(BF16)
    h_ref[...] = h
    r = lax.dot_general(wt_ref[...].astype(BF16), h, _NT, preferred_element_type=F32)
    beta = _sigmoid(r[0:DN_HEADS])
    a = r[DN_HEADS:2 * DN_HEADS] + dtb_ref[...]
    softplus = jnp.maximum(a, 0.0) + jnp.log1p(jnp.exp(-jnp.abs(a)))
    g = -jnp.exp(alog_ref[...]) * softplus
    tm = g.shape[1]
    sw = min(tm, GATE_SLAB)
    ii = lax.broadcasted_iota(jnp.int32, (sw, sw), 0)
    jj = lax.broadcasted_iota(jnp.int32, (sw, sw), 1)
    same = (ii // CHUNK) == (jj // CHUNK)
    sums = jnp.concatenate([same & (ii <= jj), same], axis=1).astype(BF16)
    g1 = g.astype(BF16)
    r1 = g - g1.astype(F32)
    g2 = r1.astype(BF16)
    g3 = (r1 - g2.astype(F32)).astype(BF16)
    pieces = jnp.concatenate([g1, g2, g3, jnp.zeros_like(g1)], axis=0)
    nh = DN_HEADS
    gc, gl = [], []
    for s0 in range(0, tm, sw):
        acc = jnp.dot(pieces[:, s0:s0 + sw], sums, preferred_element_type=F32)
        acc = acc[0:nh] + acc[nh:2 * nh] + acc[2 * nh:3 * nh]
        gc.append(acc[:, :sw])
        gl.append(acc[:, sw:])
    gc, gl = jnp.concatenate(gc, axis=1), jnp.concatenate(gl, axis=1)
    rows = jnp.concatenate([beta, gc, gl], axis=0)
    row_ref[...] = rows
    padded = jnp.concatenate([rows, jnp.zeros((LANES - 3 * DN_HEADS, tm), F32)], axis=0)
    col_ref[...] = padded.T


def _gates(x3d, norm_w, w_in_t, a_log_col, dt_bias_col, off_gate, tm):
    b, t, d = x3d.shape
    nj = t // tm
    steps = b * nj
    piece = 2 * DN_HEADS
    off_da = off_gate + piece
    n_da = w_in_t.shape[0] - off_da
    rows_dn, rows_da = off_gate // steps, n_da // steps
    assert rows_dn * steps == off_gate and rows_dn % piece == 0
    assert rows_da * steps == n_da and rows_da % piece == 0 and off_da % piece == 0
    n_pieces = rows_da // piece

    def step(i, j):
        return i * nj + j

    def da_piece(k):
        return pl.BlockSpec((piece, d),
                            lambda i, j: (off_da // piece + step(i, j) * n_pieces + k, 0))

    return pl.pallas_call(
        _gates_kernel,
        grid=(b, nj),
        in_specs=[
            pl.BlockSpec((None, tm, d), lambda i, j: (i, j, 0)),
            pl.BlockSpec((1, d), lambda i, j: (0, 0)),
            pl.BlockSpec((piece, d), lambda i, j: (off_gate // piece, 0)),
            pl.BlockSpec((DN_HEADS, 1), lambda i, j: (0, 0)),
            pl.BlockSpec((DN_HEADS, 1), lambda i, j: (0, 0)),
            pl.BlockSpec((rows_dn, d), lambda i, j: (step(i, j), 0)),
        ] + [da_piece(k) for k in range(n_pieces)],
        out_specs=[
            pl.BlockSpec((None, 3 * DN_HEADS, tm), lambda i, j: (i, 0, j)),
            pl.BlockSpec((None, tm, LANES), lambda i, j: (i, j, 0)),
            pl.BlockSpec((None, tm, d), lambda i, j: (i, j, 0)),
            pl.BlockSpec((rows_dn, d), lambda i, j: (step(i, j), 0)),
            pl.BlockSpec((rows_da, d), lambda i, j: (step(i, j), 0)),
        ],
        out_shape=[
            jax.ShapeDtypeStruct((b, 3 * DN_HEADS, t), F32),
            jax.ShapeDtypeStruct((b, t, LANES), F32),
            jax.ShapeDtypeStruct((b, t, d), BF16),
            jax.ShapeDtypeStruct((off_gate, d), BF16),
            jax.ShapeDtypeStruct((n_da, d), BF16),
        ],
        compiler_params=_params("parallel", "arbitrary"),
        name="dn_gates",
    )(x3d, norm_w, w_in_t, a_log_col, dt_bias_col, w_in_t, *([w_in_t] * n_pieces))


def _dn_intra_kernel(q_ref, k_ref, v_ref, qh_ref, kh_ref, vh_ref, cwq_ref, cwk_ref, cwv_ref,
                     gcol_ref, grow_ref, wo_ref, u_ref, w_ref, qd_ref, kd_ref, a_ref, wo16_ref, xs_ref):
    wo16_ref[...] = wo_ref[...].astype(BF16)
    head = pl.program_id(1)
    first = pl.program_id(2) == 0
    tb = q_ref.shape[0]
    halo = SUBLANES
    for idx, (x_ref, xh_ref) in enumerate(((q_ref, qh_ref), (k_ref, kh_ref), (v_ref, vh_ref))):
        xs_ref[idx, 0:halo, :] = jnp.where(first, 0.0, xh_ref[...])
        xs_ref[idx, halo:halo + tb, :] = x_ref[...]

    lane = lax.broadcasted_iota(jnp.int32, (GROUP, LANES), 1)
    sel_beta = (lane == head).astype(F32)
    sel_gc = (lane == head + DN_HEADS).astype(F32)
    sel_gl = (lane == head + 2 * DN_HEADS).astype(F32)
    ri = lax.broadcasted_iota(jnp.int32, (LANES, LANES), 0)
    ci = lax.broadcasted_iota(jnp.int32, (LANES, LANES), 1)
    same = (ri // CHUNK) == (ci // CHUNK)
    incl = same & (ri >= ci)
    strict = same & (ri > ci)

    def conv_silu(idx, cw_ref, r0, anchor):
        base = r0 + halo - (CONV_WIDTH - 1)
        taps = cw_ref[...] if anchor is None else cw_ref[...] + anchor
        acc = xs_ref[idx, base:base + GROUP, :] * taps[0:1, :]
        for i in range(1, CONV_WIDTH):
            acc = acc + xs_ref[idx, base + i:base + i + GROUP, :] * taps[i:i + 1, :]
        return _silu(acc)

    n_groups = tb // GROUP
    levels = int(math.log2(CHUNK))
    n_pows = [None] * n_groups
    xs = [None] * n_groups
    gc_rows = grow_ref[pl.ds(head + DN_HEADS, 1), :]

    def prepare(gi):
        r0 = gi * GROUP
        anchor = None if gi < IN_FLIGHT else xs[gi - IN_FLIGHT][0:1, :DN_DIM] * 0.0
        q = conv_silu(0, cwq_ref, r0, anchor)
        k = conv_silu(1, cwk_ref, r0, anchor)
        v = conv_silu(2, cwv_ref, r0, anchor)
        q = q * lax.rsqrt(jnp.sum(q * q, axis=-1, keepdims=True) + EPS) * (DN_DIM ** -0.5)
        k = k * lax.rsqrt(jnp.sum(k * k, axis=-1, keepdims=True) + EPS)

        gates = gcol_ref[r0:r0 + GROUP, :]
        beta = jnp.sum(gates * sel_beta, axis=-1, keepdims=True)
        gc = jnp.sum(gates * sel_gc, axis=-1, keepdims=True)
        gl = jnp.sum(gates * sel_gl, axis=-1, keepdims=True)
        gc_row = gc_rows[:, r0:r0 + GROUP]

        kb = k * beta
        k16 = k.astype(BF16)
        gram = lax.dot_general(jnp.concatenate([kb, q], axis=0).astype(BF16), k16, _NT,
                               preferred_element_type=F32)
        m_blocks, a_blocks = [], []
        for d0 in range(0, GROUP, LANES):
            dd = slice(d0, d0 + LANES)
            decay = jnp.exp(jnp.where(incl, gc[dd] - gc_row[:, dd], NEG_BIG))
            m_blocks.append(jnp.where(strict, gram[dd, dd] * decay, 0.0))
            a_blocks.append(gram[GROUP + d0:GROUP + d0 + LANES, dd] * decay)
        zero = jnp.zeros((LANES, LANES), F32)
        n_pows[gi] = jnp.concatenate(
            [jnp.concatenate([m_blocks[i] if j == i else zero for j in range(len(m_blocks))], axis=1)
             for i in range(len(m_blocks))], axis=0)

        egc = jnp.exp(gc)
        xs[gi] = jnp.concatenate([v * beta, kb * egc], axis=1)
        qd_ref[r0:r0 + GROUP, :] = (q * egc).astype(BF16)
        kd_ref[r0:r0 + GROUP, :] = (k * jnp.exp(gl - gc)).astype(BF16)
        for c in range(GROUP // CHUNK):
            lo = c * CHUNK
            in_block = lo % LANES
            a_ref[r0 + lo:r0 + lo + CHUNK, :] = a_blocks[lo // LANES][
                in_block:in_block + CHUNK, in_block:in_block + CHUNK].astype(BF16)

    def level(gi, lvl):
        n16 = n_pows[gi].astype(BF16)
        step = jnp.dot(n16, xs[gi].astype(BF16), preferred_element_type=F32)
        xs[gi] = xs[gi] - step if lvl == 0 else xs[gi] + step
        if lvl + 1 < levels:
            n_pows[gi] = jnp.dot(n16, n16, preferred_element_type=F32)

    def finish(gi):
        r0 = gi * GROUP
        u_ref[r0:r0 + GROUP, :] = xs[gi][:, :DN_DIM]
        w_ref[r0:r0 + GROUP, :] = xs[gi][:, DN_DIM:].astype(BF16)

    for t in range(n_groups + levels + 1):
        for gi in range(n_groups):
            stage = t - gi
            if stage == 0:
                prepare(gi)
            elif 1 <= stage <= levels:
                level(gi, stage - 1)
            elif stage == levels + 1:
                finish(gi)


def _dn_intra(qkvz, conv_w, gcol, grow, w_out, tb):
    b, t, _ = qkvz.shape
    hb = tb // SUBLANES
    nh = DN_HEADS
    nj = t // tb
    wo_rows, wo_share = _slab_rows(w_out.shape[0], b * nh * nj)
    assert wo_share == 1, "every grid step converts its own slab of w_out"
    wo_spec = pl.BlockSpec((wo_rows, w_out.shape[1]), lambda i, h, j: ((i * nh + h) * nj + j, 0))

    def tok(off):
        return pl.BlockSpec((None, tb, DN_DIM), lambda i, h, j: (i, j, off + h))

    def halo(off):
        return pl.BlockSpec((None, SUBLANES, DN_DIM),
                            lambda i, h, j: (i, jnp.maximum(j * hb - 1, 0), off + h))

    def cw(off):
        return pl.BlockSpec((CONV_WIDTH, DN_DIM), lambda i, h, j: (0, off + h))

    out_tok = pl.BlockSpec((None, tb, DN_DIM), lambda i, h, j: (i, j, h))
    return pl.pallas_call(
        _dn_intra_kernel,
        grid=(b, nh, nj),
        in_specs=[tok(0), tok(nh), tok(2 * nh), halo(0), halo(nh), halo(2 * nh),
                  cw(0), cw(nh), cw(2 * nh),
                  pl.BlockSpec((None, tb, LANES), lambda i, h, j: (i, j, 0)),
                  pl.BlockSpec((None, 3 * nh, tb), lambda i, h, j: (i, 0, j)),
                  wo_spec],
        out_specs=[out_tok, out_tok, out_tok, out_tok,
                   pl.BlockSpec((None, None, tb, CHUNK), lambda i, h, j: (i, h, j, 0)),
                   wo_spec],
        out_shape=[jax.ShapeDtypeStruct((b, t, DN_WIDTH), F32),
                   jax.ShapeDtypeStruct((b, t, DN_WIDTH), BF16),
                   jax.ShapeDtypeStruct((b, t, DN_WIDTH), BF16),
                   jax.ShapeDtypeStruct((b, t, DN_WIDTH), BF16),
                   jax.ShapeDtypeStruct((b, nh, t, CHUNK), BF16),
                   jax.ShapeDtypeStruct(w_out.shape, BF16)],
        scratch_shapes=[pltpu.VMEM((3, tb + SUBLANES, DN_DIM), F32)],
        compiler_params=_params("parallel", "parallel", "arbitrary"),
        name="dn_intra",
    )(qkvz, qkvz, qkvz, qkvz, qkvz, qkvz, conv_w, conv_w, conv_w, gcol, grow, w_out)


def _dn_scan_kernel(u_ref, w_ref, qd_ref, kd_ref, a_ref, z_ref, gcol_ref, nw_ref, o_ref, s_ref):
    @pl.when(pl.program_id(1) == 0)
    def _():
        s_ref[...] = jnp.zeros_like(s_ref)

    bb, tb = u_ref.shape[0], u_ref.shape[1]
    nw = nw_ref[...]
    units = [(i, h) for i in range(bb) for h in range(DN_HEADS)]

    def chunk(c, carry):
        rows = pl.ds(pl.multiple_of(c * CHUNK, CHUNK), CHUNK)
        cols = [slice(h * DN_DIM, (h + 1) * DN_DIM) for h in range(DN_HEADS)]
        s = [s_ref[i, h] for i, h in units]
        proj = [jnp.dot(jnp.concatenate([w_ref[i, rows, cols[h]], qd_ref[i, rows, cols[h]]], axis=0),
                        s[n].astype(BF16), preferred_element_type=F32)
                for n, (i, h) in enumerate(units)]
        v16 = [(u_ref[i, rows, cols[h]] - proj[n][:CHUNK]).astype(BF16)
               for n, (i, h) in enumerate(units)]
        upd = [lax.dot_general(kd_ref[i, rows, cols[h]], v16[n], _TN, preferred_element_type=F32)
               for n, (i, h) in enumerate(units)]
        for n, (i, h) in enumerate(units):
            gl = gcol_ref[i, pl.ds(pl.multiple_of(c * CHUNK, CHUNK), 1),
                          2 * DN_HEADS + h:2 * DN_HEADS + h + 1]
            s_ref[i, h] = s[n] * jnp.exp(gl) + upd[n]
        o = [proj[n][CHUNK:] + jnp.dot(a_ref[i, h, rows, :], v16[n], preferred_element_type=F32)
             for n, (i, h) in enumerate(units)]
        for n, (i, h) in enumerate(units):
            z = z_ref[i, rows, cols[h]]
            o_ref[i, rows, cols[h]] = (_rmsnorm_rows(o[n], nw) * _silu(z)).astype(o_ref.dtype)
        return carry

    lax.fori_loop(0, tb // CHUNK, chunk, 0)


def _dn_scan(u, w, qd, kd, a, qkvz, gcol, dn_norm_w, bb, tb):
    b, t, _ = u.shape
    tok = pl.BlockSpec((bb, tb, DN_WIDTH), lambda i, j: (i, j, 0))
    z_block = 3 * DN_WIDTH // DN_WIDTH
    return pl.pallas_call(
        _dn_scan_kernel,
        grid=(b // bb, t // tb),
        in_specs=[tok, tok, tok, tok,
                  pl.BlockSpec((bb, DN_HEADS, tb, CHUNK), lambda i, j: (i, 0, j, 0)),
                  pl.BlockSpec((bb, tb, DN_WIDTH), lambda i, j: (i, j, z_block)),
                  pl.BlockSpec((bb, tb, LANES), lambda i, j: (i, j, 0)),
                  pl.BlockSpec((1, DN_DIM), lambda i, j: (0, 0))],
        out_specs=tok,
        out_shape=jax.ShapeDtypeStruct((b, t, DN_WIDTH), BF16),
        scratch_shapes=[pltpu.VMEM((bb, DN_HEADS, DN_DIM, DN_DIM), F32)],
        compiler_params=_params("parallel", "arbitrary"),
        name="dn_scan",
    )(u, w, qd, kd, a, qkvz, gcol, dn_norm_w)


def _attn_kernel(q_ref, k_ref, v_ref, lq1_ref, lk1_ref, lq2_ref, lk2_ref, nw_ref,
                 wg_ref, wu_ref, wd_ref, o_ref, wg16_ref, wu16_ref, wd16_ref,
                 m_ref, l_ref, acc_ref, sa_ref, sb_ref, *, lam_init, down_share):
    def convert_gate_up():
        wg16_ref[...] = wg_ref[...].astype(BF16)
        wu16_ref[...] = wu_ref[...].astype(BF16)

    grid_step = ((pl.program_id(0) * pl.num_programs(1) + pl.program_id(1)) * pl.num_programs(2)
                 + pl.program_id(2))

    @pl.when(grid_step % down_share == 0)
    def _():
        wd16_ref[...] = wd_ref[...].astype(BF16)

    head = pl.program_id(1)
    qi = pl.program_id(2)
    tq = q_ref.shape[0]
    tk = tq
    slope = jnp.float32(0.0)
    for hh in range(DA_HEADS):
        slope = jnp.where(head == hh, jnp.float32(2.0 ** (-8.0 * (hh + 1) / DA_HEADS)), slope)

    kpos = lax.broadcasted_iota(jnp.int32, (1, tk), 1)
    slope2 = slope * LOG2E

    maps = range(2)

    def scores(j, s_ref):
        k0 = pl.multiple_of(j * tk, tk)
        bias = slope2 * (kpos + (j - qi) * tk).astype(F32)
        for c in maps:
            cols = slice(c * DA_DIM, (c + 1) * DA_DIM)
            s_ref[c] = lax.dot_general(q_ref[:, cols], k_ref[pl.ds(k0, tk), cols], _NT,
                                       preferred_element_type=F32) + bias

    def accumulate(j, s_ref, masked, first=False):
        k0 = pl.multiple_of(j * tk, tk)
        v = v_ref[pl.ds(k0, tk), :]
        s = [s_ref[c] for c in maps]
        if masked:
            rr = lax.broadcasted_iota(jnp.int32, (tq, tk), 0)
            cc = lax.broadcasted_iota(jnp.int32, (tq, tk), 1)
            s = [jnp.where(cc <= rr, s[c], NEG_BIG) for c in maps]
        m_cur = [jnp.broadcast_to(jnp.max(s[c], axis=-1, keepdims=True), (tq, LANES)) for c in maps]
        if first:
            m_new = m_cur
        else:
            m_prev = [m_ref[c] for c in maps]
            m_new = [jnp.maximum(m_prev[c], m_cur[c]) for c in maps]
        p = [jnp.exp2(s[c] - _lane_tile(m_new[c], tk // LANES)) for c in maps]
        p_sum = [jnp.broadcast_to(jnp.sum(p[c], axis=-1, keepdims=True), (tq, LANES)) for c in maps]
        p16 = [p[c].astype(BF16) for c in maps]
        pv = [jnp.dot(p16[c], v, preferred_element_type=F32) for c in maps]
        for c in maps:
            if first:
                l_ref[c] = p_sum[c]
                acc_ref[c] = pv[c]
            else:
                alpha = jnp.exp2(m_prev[c] - m_new[c])
                l_ref[c] = alpha * l_ref[c] + p_sum[c]
                acc_ref[c] = _lane_tile(alpha, DA_VDIM // LANES) * acc_ref[c] + pv[c]
            m_ref[c] = m_new[c]

    @pl.when(qi == 0)
    def _():
        convert_gate_up()
        scores(0, sa_ref)
        accumulate(0, sa_ref, True, first=True)

    @pl.when(qi > 0)
    def _():
        convert_gate_up()
        scores(0, sa_ref)
        scores(1, sb_ref)
        accumulate(0, sa_ref, False, first=True)

        def pair(jj, carry):
            j = 2 * jj + 1
            scores(j + 1, sa_ref)
            accumulate(j, sb_ref, False)
            scores(j + 2, sb_ref)
            accumulate(j + 1, sa_ref, False)
            return carry

        lax.fori_loop(0, (qi - 1) // 2, pair, 0)

        @pl.when(qi % 2 == 0)
        def _():
            scores(qi, sa_ref)
            accumulate(qi - 1, sb_ref, False)
            accumulate(qi, sa_ref, True)

        @pl.when(qi % 2 == 1)
        def _():
            accumulate(qi, sb_ref, True)

    lam = (jnp.exp(jnp.sum(lq1_ref[...] * lk1_ref[...], axis=-1, keepdims=True))
           - jnp.exp(jnp.sum(lq2_ref[...] * lk2_ref[...], axis=-1, keepdims=True)) + lam_init)
    rep = DA_VDIM // LANES
    inv_l0 = 1.0 / l_ref[0]
    inv_l1 = lam / l_ref[1]
    o = acc_ref[0] * _lane_tile(inv_l0, rep) - acc_ref[1] * _lane_tile(inv_l1, rep)
    o_ref[...] = (_rmsnorm_rows(o, nw_ref[...]) * (1.0 - lam_init)).astype(o_ref.dtype)


def _slab_rows(n_rows, n_steps):
    for share in (1, 2, 4, 8, 16):
        if n_steps % share == 0 and n_rows % (n_steps // share) == 0:
            rows = n_rows // (n_steps // share)
            if rows % 16 == 0:
                return rows, share
    raise ValueError(f"cannot split {n_rows} weight rows over {n_steps} grid steps")


def _attention(da, lam_q1, lam_k1, lam_q2, lam_k2, da_norm_w, w_gate, w_up, w_down, lam_init, tq):
    b, t, _ = da.shape
    nh = DA_HEADS
    nq = t // tq
    d, dff = w_gate.shape
    steps = b * nh * nq
    rg, sg = _slab_rows(d, steps)
    rd, sd = _slab_rows(dff, steps)

    def step(i, h, j):
        return (i * nh + h) * nq + j

    vec = pl.BlockSpec((1, DA_DIM), lambda i, h, j: (0, 0))
    up_spec = pl.BlockSpec((rg, dff), lambda i, h, j: (step(i, h, j) // sg, 0))
    down_spec = pl.BlockSpec((rd, d), lambda i, h, j: (step(i, h, j) // sd, 0))
    return pl.pallas_call(
        functools.partial(_attn_kernel, lam_init=lam_init, down_share=sd),
        grid=(b, nh, nq),
        in_specs=[pl.BlockSpec((None, tq, DA_VDIM), lambda i, h, j: (i, j, h)),
                  pl.BlockSpec((None, t, DA_VDIM), lambda i, h, j: (i, 0, nh + h)),
                  pl.BlockSpec((None, t, DA_VDIM), lambda i, h, j: (i, 0, 2 * nh + h)),
                  vec, vec, vec, vec,
                  pl.BlockSpec((1, DA_VDIM), lambda i, h, j: (0, 0)),
                  up_spec, up_spec, down_spec],
        out_specs=[pl.BlockSpec((None, tq, DA_VDIM), lambda i, h, j: (i, j, h)),
                   up_spec, up_spec, down_spec],
        out_shape=[jax.ShapeDtypeStruct((b, t, DA_WIDTH), BF16),
                   jax.ShapeDtypeStruct((d, dff), BF16),
                   jax.ShapeDtypeStruct((d, dff), BF16),
                   jax.ShapeDtypeStruct((dff, d), BF16)],
        scratch_shapes=[pltpu.VMEM((2, tq, LANES), F32), pltpu.VMEM((2, tq, LANES), F32),
                        pltpu.VMEM((2, tq, DA_VDIM), F32),
                        pltpu.VMEM((2, tq, tq), F32), pltpu.VMEM((2, tq, tq), F32)],
        compiler_params=_params("arbitrary", "arbitrary", "arbitrary"),
        name="diff_attn",
    )(da, da, da, lam_q1, lam_k1, lam_q2, lam_k2, da_norm_w, w_gate, w_up, w_down)


def _out_proj_kernel(a_ref, b_ref, wa_ref, wb_ref, x_ref, o_ref):
    acc = jnp.dot(a_ref[...], wa_ref[...], preferred_element_type=F32)
    acc = acc + jnp.dot(b_ref[...], wb_ref[...], preferred_element_type=F32)
    o_ref[...] = x_ref[...] + acc


def _out_proj(o_dn, o_da, w16, x2d, tm, tn):
    m, d = x2d.shape
    ka, kb = o_dn.shape[1], o_da.shape[1]
    assert ka == kb and w16.shape[0] == ka + kb, "the two head groups are equal row halves of w_out"
    return pl.pallas_call(
        _out_proj_kernel,
        grid=(m // tm, d // tn),
        in_specs=[pl.BlockSpec((tm, ka), lambda i, j: (i, 0)),
                  pl.BlockSpec((tm, kb), lambda i, j: (i, 0)),
                  pl.BlockSpec((ka, tn), lambda i, j: (0, j)),
                  pl.BlockSpec((kb, tn), lambda i, j: (1, j)),
                  pl.BlockSpec((tm, tn), lambda i, j: (i, j))],
        out_specs=pl.BlockSpec((tm, tn), lambda i, j: (i, j)),
        out_shape=jax.ShapeDtypeStruct((m, d), F32),
        compiler_params=_params("parallel", "arbitrary"),
        name="out_proj",
    )(o_dn, o_da, w16, w16, x2d)


def _ffn_kernel(x_ref, nw_ref, wg_ref, wu_ref, wd_ref, fw_ref, o_ref, h_ref):
    f = pl.program_id(1)

    @pl.when(f == 0)
    def _():
        x = x_ref[...]
        h_ref[...] = _rmsnorm_rows(x, nw_ref[...]).astype(BF16)
        o_ref[...] = x

    h = h_ref[...]
    g = jnp.dot(h, wg_ref[...], preferred_element_type=F32)
    u = jnp.dot(h, wu_ref[...], preferred_element_type=F32)
    act = (_silu(g) * u).astype(BF16)
    o_ref[...] += jnp.dot(act, wd_ref[...], preferred_element_type=F32)

    @pl.when(f == pl.num_programs(1) - 1)
    def _():
        o_ref[...] = _rmsnorm_rows(o_ref[...], fw_ref[...])


def _ffn(x2d, ffn_norm_w, w_gate, w_up, w_down, final_norm_w, tm, tf):
    m, d = x2d.shape
    dff = w_gate.shape[1]
    return pl.pallas_call(
        _ffn_kernel,
        grid=(m // tm, dff // tf),
        in_specs=[pl.BlockSpec((tm, d), lambda i, f: (i, 0)),
                  pl.BlockSpec((1, d), lambda i, f: (0, 0)),
                  pl.BlockSpec((d, tf), lambda i, f: (0, f)),
                  pl.BlockSpec((d, tf), lambda i, f: (0, f)),
                  pl.BlockSpec((tf, d), lambda i, f: (f, 0)),
                  pl.BlockSpec((1, d), lambda i, f: (0, 0))],
        out_specs=pl.BlockSpec((tm, d), lambda i, f: (i, 0)),
        out_shape=jax.ShapeDtypeStruct((m, d), F32),
        scratch_shapes=[pltpu.VMEM((tm, d), BF16)],
        compiler_params=_params("parallel", "arbitrary"),
        name="ffn",
    )(x2d, ffn_norm_w, w_gate, w_up, w_down, final_norm_w)


def _tile(n, pref):
    return pref if n % pref == 0 else n


def _layer(x, lam_init, attn_norm_w, w_in, conv_w, a_log, dt_bias, dn_norm_w, lam_q1, lam_k1,
           lam_q2, lam_k2, da_norm_w, w_out, ffn_norm_w, w_gate, w_up, w_down, final_norm_w):
    b, t, d = x.shape
    m = b * t
    x2d = x.reshape(m, d)
    off_z = 3 * DN_WIDTH
    off_b = off_z + DN_WIDTH
    off_q = off_b + 2 * DN_HEADS
    nw = attn_norm_w.reshape(1, d)

    n_da = w_in.shape[1] - off_q
    da_scale = jnp.concatenate([jnp.full((1, DA_HEADS * 2 * DA_DIM), LOG2E * DA_DIM ** -0.5, F32),
                                jnp.ones((1, n_da - DA_HEADS * 2 * DA_DIM), F32)], axis=1)

    w_in_t = w_in.T
    grow, gcol, h, w_dn, w_da = _gates(x, nw, w_in_t, a_log.reshape(DN_HEADS, 1),
                                       dt_bias.reshape(DN_HEADS, 1), off_b, _tile(t, 512))
    h2d = h.reshape(m, d)
    tp = _tile(m, 2048)
    qkvz = _proj(h2d, w_dn, jnp.ones((1, off_b), F32), F32, tp, 1024, "in_proj_dn")
    da = _proj(h2d, w_da, da_scale, BF16, tp, 1024, "in_proj_da")
    tm = _tile(m, 1024)

    qkvz = qkvz.reshape(b, t, off_b)
    u, w, qd, kd, a, w_out16 = _dn_intra(qkvz, conv_w, gcol, grow, w_out, _tile(t, 4096))
    o_dn = _dn_scan(u, w, qd, kd, a, qkvz, gcol, dn_norm_w.reshape(1, DN_DIM),
                    4 if b % 4 == 0 else 1, _tile(t, 256))

    o_da, w_gate16, w_up16, w_down16 = _attention(
        da.reshape(b, t, n_da), lam_q1.reshape(1, DA_DIM), lam_k1.reshape(1, DA_DIM),
        lam_q2.reshape(1, DA_DIM), lam_k2.reshape(1, DA_DIM), da_norm_w.reshape(1, DA_VDIM),
        w_gate, w_up, w_down, lam_init, _tile(t, 512))

    x2d = _out_proj(o_dn.reshape(m, DN_WIDTH), o_da.reshape(m, DA_WIDTH), w_out16, x2d,
                    _tile(m, 512), d)
    dff = w_gate.shape[1]
    return _ffn(x2d, ffn_norm_w.reshape(1, d), w_gate16, w_up16, w_down16,
                final_norm_w.reshape(1, d), tm, _tile(dff, 512))


def kernel(x, attn_norm_w, w_in, conv_w, a_log, dt_bias, dn_norm_w, lam_q1, lam_k1, lam_q2, lam_k2,
           da_norm_w, w_out, ffn_norm_w, w_gate, w_up, w_down, final_norm_w):
    b, t, d = x.shape
    depth = w_in.shape[0]
    assert depth == 1, "the final rmsnorm is fused into the last layer's FFN kernel"
    lam_init = 0.8 - 0.6 * math.exp(-0.3 * 0)
    out = _layer(x, lam_init, attn_norm_w[0], w_in[0], conv_w[0], a_log[0], dt_bias[0],
                 dn_norm_w[0], lam_q1[0], lam_k1[0], lam_q2[0], lam_k2[0], da_norm_w[0],
                 w_out[0], ffn_norm_w[0], w_gate[0], w_up[0], w_down[0], final_norm_w)
    return out.reshape(b, t, d)
```

```python
import functools
import math

import jax
import jax.numpy as jnp
from jax import lax
from jax.experimental import pallas as pl
from jax.experimental.pallas import tpu as pltpu

F32 = jnp.float32
BF16 = jnp.bfloat16
EPS = 1e-6
NEG_BIG = -1e30
LOG2E = math.log2(math.e)

DN_HEADS = 8
DN_DIM = 128
DN_WIDTH = DN_HEADS * DN_DIM
CONV_WIDTH = 4
CHUNK = 64
DA_HEADS = 4
DA_DIM = 128
DA_VDIM = 2 * DA_DIM
DA_WIDTH = DA_HEADS * DA_VDIM
LANES = 128
SUBLANES = 8
GROUP = 128
IN_FLIGHT = 7
GATE_SLAB = 256
VMEM_LIMIT = 56 * 1024 * 1024

_NT = (((1,), (1,)), ((), ()))
_TN = (((0,), (0,)), ((), ()))


def _params(*sem):
    return pltpu.CompilerParams(dimension_semantics=sem, vmem_limit_bytes=VMEM_LIMIT)


def _sigmoid(x):
    return 1.0 / (1.0 + jnp.exp(-x))


def _silu(x):
    h = 0.5 * x
    return h + h * jnp.tanh(h)


def _lane_tile(x, n):
    return jnp.concatenate([x] * n, axis=1)


def _rmsnorm_rows(x, w):
    return x * lax.rsqrt(jnp.mean(x * x, axis=-1, keepdims=True) + EPS) * w


def _proj_kernel(h_ref, w_ref, cs_ref, o_ref):
    acc = lax.dot_general(h_ref[...], w_ref[...], _NT, preferred_element_type=F32)
    o_ref[...] = (acc * cs_ref[...]).astype(o_ref.dtype)


def _proj(h2d, w_t_bf16, col_scale, out_dtype, tm, tn, name):
    m, d = h2d.shape
    n = w_t_bf16.shape[0]
    return pl.pallas_call(
        _proj_kernel,
        grid=(m // tm, n // tn),
        in_specs=[
            pl.BlockSpec((tm, d), lambda i, j: (i, 0)),
            pl.BlockSpec((tn, d), lambda i, j: (j, 0)),
            pl.BlockSpec((1, tn), lambda i, j: (0, j)),
        ],
        out_specs=pl.BlockSpec((tm, tn), lambda i, j: (i, j)),
        out_shape=jax.ShapeDtypeStruct((m, n), out_dtype),
        compiler_params=_params("parallel", "arbitrary"),
        name=name,
    )(h2d, w_t_bf16, col_scale)


def _gates_kernel(x_ref, nw_ref, wt_ref, alog_ref, dtb_ref, wdn_ref, *rest):
    wda_refs, (row_ref, col_ref, h_ref, wdn16_ref, wda16_ref) = rest[:-5], rest[-5:]
    wdn16_ref[...] = wdn_ref[...].astype(BF16)
    wda16_ref[...] = jnp.concatenate([r[...] for r in wda_refs], axis=0).astype(BF16)

    h = _rmsnorm_rows(x_ref[...], nw_ref[...]).astype(BF16)
    h_ref[...] = h
    r = lax.dot_general(wt_ref[...].astype(BF16), h, _NT, preferred_element_type=F32)
    beta = _sigmoid(r[0:DN_HEADS])
    a = r[DN_HEADS:2 * DN_HEADS] + dtb_ref[...]
    softplus = jnp.maximum(a, 0.0) + jnp.log1p(jnp.exp(-jnp.abs(a)))
    g = -jnp.exp(alog_ref[...]) * softplus
    tm = g.shape[1]
    sw = min(tm, GATE_SLAB)
    ii = lax.broadcasted_iota(jnp.int32, (sw, sw), 0)
    jj = lax.broadcasted_iota(jnp.int32, (sw, sw), 1)
    same = (ii // CHUNK) == (jj // CHUNK)
    sums = jnp.concatenate([same & (ii <= jj), same], axis=1).astype(BF16)
    g1 = g.astype(BF16)
    r1 = g - g1.astype(F32)
    g2 = r1.astype(BF16)
    g3 = (r1 - g2.astype(F32)).astype(BF16)
    pieces = jnp.concatenate([g1, g2, g3, jnp.zeros_like(g1)], axis=0)
    nh = DN_HEADS
    gc, gl = [], []
    for s0 in range(0, tm, sw):
        acc = jnp.dot(pieces[:, s0:s0 + sw], sums, preferred_element_type=F32)
        acc = acc[0:nh] + acc[nh:2 * nh] + acc[2 * nh:3 * nh]
        gc.append(acc[:, :sw])
        gl.append(acc[:, sw:])
    gc, gl = jnp.concatenate(gc, axis=1), jnp.concatenate(gl, axis=1)
    rows = jnp.concatenate([beta, gc, gl], axis=0)
    row_ref[...] = rows
    padded = jnp.concatenate([rows, jnp.zeros((LANES - 3 * DN_HEADS, tm), F32)], axis=0)
    col_ref[...] = padded.T


def _gates(x3d, norm_w, w_in_t, a_log_col, dt_bias_col, off_gate, tm):
    b, t, d = x3d.shape
    nj = t // tm
    steps = b * nj
    piece = 2 * DN_HEADS
    off_da = off_gate + piece
    n_da = w_in_t.shape[0] - off_da
    rows_dn, rows_da = off_gate // steps, n_da // steps
    assert rows_dn * steps == off_gate and rows_dn % piece == 0
    assert rows_da * steps == n_da and rows_da % piece == 0 and off_da % piece == 0
    n_pieces = rows_da // piece

    def step(i, j):
        return i * nj + j

    def da_piece(k):
        return pl.BlockSpec((piece, d),
                            lambda i, j: (off_da // piece + step(i, j) * n_pieces + k, 0))

    return pl.pallas_call(
        _gates_kernel,
        grid=(b, nj),
        in_specs=[
            pl.BlockSpec((None, tm, d), lambda i, j: (i, j, 0)),
            pl.BlockSpec((1, d), lambda i, j: (0, 0)),
            pl.BlockSpec((piece, d), lambda i, j: (off_gate // piece, 0)),
            pl.BlockSpec((DN_HEADS, 1), lambda i, j: (0, 0)),
            pl.BlockSpec((DN_HEADS, 1), lambda i, j: (0, 0)),
            pl.BlockSpec((rows_dn, d), lambda i, j: (step(i, j), 0)),
        ] + [da_piece(k) for k in range(n_pieces)],
        out_specs=[
            pl.BlockSpec((None, 3 * DN_HEADS, tm), lambda i, j: (i, 0, j)),
            pl.BlockSpec((None, tm, LANES), lambda i, j: (i, j, 0)),
            pl.BlockSpec((None, tm, d), lambda i, j: (i, j, 0)),
            pl.BlockSpec((rows_dn, d), lambda i, j: (step(i, j), 0)),
            pl.BlockSpec((rows_da, d), lambda i, j: (step(i, j), 0)),
        ],
        out_shape=[
            jax.ShapeDtypeStruct((b, 3 * DN_HEADS, t), F32),
            jax.ShapeDtypeStruct((b, t, LANES), F32),
            jax.ShapeDtypeStruct((b, t, d), BF16),
            jax.ShapeDtypeStruct((off_gate, d), BF16),
            jax.ShapeDtypeStruct((n_da, d), BF16),
        ],
        compiler_params=_params("parallel", "arbitrary"),
        name="dn_gates",
    )(x3d, norm_w, w_in_t, a_log_col, dt_bias_col, w_in_t, *([w_in_t] * n_pieces))


def _dn_intra_kernel(q_ref, k_ref, v_ref, z_ref, qh_ref, kh_ref, vh_ref, cwq_ref, cwk_ref, cwv_ref,
                     gcol_ref, grow_ref, wo_ref, u_ref, w_ref, qd_ref, kd_ref, a_ref, z16_ref, wo16_ref,
                     xs_ref):
    wo16_ref[...] = wo_ref[...].astype(BF16)
    z16_ref[...] = z_ref[...].astype(BF16)
    head = pl.program_id(1)
    first = pl.program_id(2) == 0
    tb = q_ref.shape[0]
    halo = SUBLANES
    for idx, (x_ref, xh_ref) in enumerate(((q_ref, qh_ref), (k_ref, kh_ref), (v_ref, vh_ref))):
        xs_ref[idx, 0:halo, :] = jnp.where(first, 0.0, xh_ref[...])
        xs_ref[idx, halo:halo + tb, :] = x_ref[...]

    lane = lax.broadcasted_iota(jnp.int32, (GROUP, LANES), 1)
    sel_beta = (lane == head).astype(F32)
    sel_gc = (lane == head + DN_HEADS).astype(F32)
    sel_gl = (lane == head + 2 * DN_HEADS).astype(F32)
    ri = lax.broadcasted_iota(jnp.int32, (LANES, LANES), 0)
    ci = lax.broadcasted_iota(jnp.int32, (LANES, LANES), 1)
    same = (ri // CHUNK) == (ci // CHUNK)
    incl = same & (ri >= ci)
    strict = same & (ri > ci)

    def conv_silu(idx, cw_ref, r0, anchor):
        base = r0 + halo - (CONV_WIDTH - 1)
        taps = cw_ref[...] if anchor is None else cw_ref[...] + anchor
        acc = xs_ref[idx, base:base + GROUP, :] * taps[0:1, :]
        for i in range(1, CONV_WIDTH):
            acc = acc + xs_ref[idx, base + i:base + i + GROUP, :] * taps[i:i + 1, :]
        return _silu(acc)

    n_groups = tb // GROUP
    levels = int(math.log2(CHUNK))
    n_pows = [None] * n_groups
    xs = [None] * n_groups
    gc_rows = grow_ref[pl.ds(head + DN_HEADS, 1), :]

    def prepare(gi):
        r0 = gi * GROUP
        anchor = None if gi < IN_FLIGHT else xs[gi - IN_FLIGHT][0:1, :DN_DIM] * 0.0
        q = conv_silu(0, cwq_ref, r0, anchor)
        k = conv_silu(1, cwk_ref, r0, anchor)
        v = conv_silu(2, cwv_ref, r0, anchor)
        q = q * lax.rsqrt(jnp.sum(q * q, axis=-1, keepdims=True) + EPS) * (DN_DIM ** -0.5)
        k = k * lax.rsqrt(jnp.sum(k * k, axis=-1, keepdims=True) + EPS)

        gates = gcol_ref[r0:r0 + GROUP, :]
        beta = jnp.sum(gates * sel_beta, axis=-1, keepdims=True)
        gc = jnp.sum(gates * sel_gc, axis=-1, keepdims=True)
        gl = jnp.sum(gates * sel_gl, axis=-1, keepdims=True)
        gc_row = gc_rows[:, r0:r0 + GROUP]

        kb = k * beta
        k16 = k.astype(BF16)
        gram = lax.dot_general(jnp.concatenate([kb, q], axis=0).astype(BF16), k16, _NT,
                               preferred_element_type=F32)
        m_blocks, a_blocks = [], []
        for d0 in range(0, GROUP, LANES):
            dd = slice(d0, d0 + LANES)
            decay = jnp.exp(jnp.where(incl, gc[dd] - gc_row[:, dd], NEG_BIG))
            m_blocks.append(jnp.where(strict, gram[dd, dd] * decay, 0.0))
            a_blocks.append(gram[GROUP + d0:GROUP + d0 + LANES, dd] * decay)
        zero = jnp.zeros((LANES, LANES), F32)
        n_pows[gi] = jnp.concatenate(
            [jnp.concatenate([m_blocks[i] if j == i else zero for j in range(len(m_blocks))], axis=1)
             for i in range(len(m_blocks))], axis=0)

        egc = jnp.exp(gc)
        xs[gi] = jnp.concatenate([v * beta, kb * egc], axis=1)
        qd_ref[r0:r0 + GROUP, :] = (q * egc).astype(BF16)
        kd_ref[r0:r0 + GROUP, :] = (k * jnp.exp(gl - gc)).astype(BF16)
        for c in range(GROUP // CHUNK):
            lo = c * CHUNK
            in_block = lo % LANES
            a_ref[r0 + lo:r0 + lo + CHUNK, :] = a_blocks[lo // LANES][
                in_block:in_block + CHUNK, in_block:in_block + CHUNK].astype(BF16)

    def level(gi, lvl):
        n16 = n_pows[gi].astype(BF16)
        step = jnp.dot(n16, xs[gi].astype(BF16), preferred_element_type=F32)
        xs[gi] = xs[gi] - step if lvl == 0 else xs[gi] + step
        if lvl + 1 < levels:
            n_pows[gi] = jnp.dot(n16, n16, preferred_element_type=F32)

    def finish(gi):
        r0 = gi * GROUP
        u_ref[r0:r0 + GROUP, :] = xs[gi][:, :DN_DIM].astype(BF16)
        w_ref[r0:r0 + GROUP, :] = xs[gi][:, DN_DIM:].astype(BF16)

    for t in range(n_groups + levels + 1):
        for gi in range(n_groups):
            stage = t - gi
            if stage == 0:
                prepare(gi)
            elif 1 <= stage <= levels:
                level(gi, stage - 1)
            elif stage == levels + 1:
                finish(gi)


def _dn_intra(qkvz, conv_w, gcol, grow, w_out, tb):
    b, t, _ = qkvz.shape
    hb = tb // SUBLANES
    nh = DN_HEADS
    nj = t // tb
    wo_rows, wo_share = _slab_rows(w_out.shape[0], b * nh * nj)
    assert wo_share == 1, "every grid step converts its own slab of w_out"
    wo_spec = pl.BlockSpec((wo_rows, w_out.shape[1]), lambda i, h, j: ((i * nh + h) * nj + j, 0))

    def tok(off):
        return pl.BlockSpec((None, tb, DN_DIM), lambda i, h, j: (i, j, off + h))

    def halo(off):
        return pl.BlockSpec((None, SUBLANES, DN_DIM),
                            lambda i, h, j: (i, jnp.maximum(j * hb - 1, 0), off + h))

    def cw(off):
        return pl.BlockSpec((CONV_WIDTH, DN_DIM), lambda i, h, j: (0, off + h))

    out_tok = pl.BlockSpec((None, tb, DN_DIM), lambda i, h, j: (i, j, h))
    return pl.pallas_call(
        _dn_intra_kernel,
        grid=(b, nh, nj),
        in_specs=[tok(0), tok(nh), tok(2 * nh), tok(3 * nh), halo(0), halo(nh), halo(2 * nh),
                  cw(0), cw(nh), cw(2 * nh),
                  pl.BlockSpec((None, tb, LANES), lambda i, h, j: (i, j, 0)),
                  pl.BlockSpec((None, 3 * nh, tb), lambda i, h, j: (i, 0, j)),
                  wo_spec],
        out_specs=[out_tok, out_tok, out_tok, out_tok,
                   pl.BlockSpec((None, None, tb, CHUNK), lambda i, h, j: (i, h, j, 0)),
                   out_tok, wo_spec],
        out_shape=[jax.ShapeDtypeStruct((b, t, DN_WIDTH), BF16),
                   jax.ShapeDtypeStruct((b, t, DN_WIDTH), BF16),
                   jax.ShapeDtypeStruct((b, t, DN_WIDTH), BF16),
                   jax.ShapeDtypeStruct((b, t, DN_WIDTH), BF16),
                   jax.ShapeDtypeStruct((b, nh, t, CHUNK), BF16),
                   jax.ShapeDtypeStruct((b, t, DN_WIDTH), BF16),
                   jax.ShapeDtypeStruct(w_out.shape, BF16)],
        scratch_shapes=[pltpu.VMEM((3, tb + SUBLANES, DN_DIM), F32)],
        compiler_params=_params("parallel", "parallel", "arbitrary"),
        name="dn_intra",
    )(qkvz, qkvz, qkvz, qkvz, qkvz, qkvz, qkvz, conv_w, conv_w, conv_w, gcol, grow, w_out)


def _dn_scan_kernel(u_ref, w_ref, qd_ref, kd_ref, a_ref, z_ref, gcol_ref, nw_ref, o_ref, s_ref):
    @pl.when(pl.program_id(1) == 0)
    def _():
        s_ref[...] = jnp.zeros_like(s_ref)

    bb, tb = u_ref.shape[0], u_ref.shape[1]
    nw = nw_ref[...]
    units = [(i, h) for i in range(bb) for h in range(DN_HEADS)]

    def chunk(c, carry):
        rows = pl.ds(pl.multiple_of(c * CHUNK, CHUNK), CHUNK)
        cols = [slice(h * DN_DIM, (h + 1) * DN_DIM) for h in range(DN_HEADS)]
        s = [s_ref[i, h] for i, h in units]
        proj = [jnp.dot(jnp.concatenate([w_ref[i, rows, cols[h]], qd_ref[i, rows, cols[h]]], axis=0),
                        s[n].astype(BF16), preferred_element_type=F32)
                for n, (i, h) in enumerate(units)]
        v16 = [(u_ref[i, rows, cols[h]] - proj[n][:CHUNK]).astype(BF16)
               for n, (i, h) in enumerate(units)]
        upd = [lax.dot_general(kd_ref[i, rows, cols[h]], v16[n], _TN, preferred_element_type=F32)
               for n, (i, h) in enumerate(units)]
        for n, (i, h) in enumerate(units):
            gl = gcol_ref[i, pl.ds(pl.multiple_of(c * CHUNK, CHUNK), 1),
                          2 * DN_HEADS + h:2 * DN_HEADS + h + 1]
            s_ref[i, h] = s[n] * jnp.exp(gl) + upd[n]
        o = [proj[n][CHUNK:] + jnp.dot(a_ref[i, h, rows, :], v16[n], preferred_element_type=F32)
             for n, (i, h) in enumerate(units)]
        for n, (i, h) in enumerate(units):
            z = z_ref[i, rows, cols[h]].astype(F32)
            o_ref[i, rows, cols[h]] = (_rmsnorm_rows(o[n], nw) * _silu(z)).astype(o_ref.dtype)
        return carry

    lax.fori_loop(0, tb // CHUNK, chunk, 0)


def _dn_scan(u, w, qd, kd, a, z16, gcol, dn_norm_w, bb, tb):
    b, t, _ = u.shape
    tok = pl.BlockSpec((bb, tb, DN_WIDTH), lambda i, j: (i, j, 0))
    return pl.pallas_call(
        _dn_scan_kernel,
        grid=(b // bb, t // tb),
        in_specs=[tok, tok, tok, tok,
                  pl.BlockSpec((bb, DN_HEADS, tb, CHUNK), lambda i, j: (i, 0, j, 0)),
                  tok,
                  pl.BlockSpec((bb, tb, LANES), lambda i, j: (i, j, 0)),
                  pl.BlockSpec((1, DN_DIM), lambda i, j: (0, 0))],
        out_specs=tok,
        out_shape=jax.ShapeDtypeStruct((b, t, DN_WIDTH), BF16),
        scratch_shapes=[pltpu.VMEM((bb, DN_HEADS, DN_DIM, DN_DIM), F32)],
        compiler_params=_params("parallel", "arbitrary"),
        name="dn_scan",
    )(u, w, qd, kd, a, z16, gcol, dn_norm_w)


def _attn_kernel(q_ref, k_ref, v_ref, lq1_ref, lk1_ref, lq2_ref, lk2_ref, nw_ref,
                 wg_ref, wu_ref, wd_ref, o_ref, wg16_ref, wu16_ref, wd16_ref,
                 m_ref, l_ref, acc_ref, sa_ref, sb_ref, *, lam_init, down_share):
    def convert_gate_up():
        wg16_ref[...] = wg_ref[...].astype(BF16)
        wu16_ref[...] = wu_ref[...].astype(BF16)

    grid_step = ((pl.program_id(0) * pl.num_programs(1) + pl.program_id(1)) * pl.num_programs(2)
                 + pl.program_id(2))

    @pl.when(grid_step % down_share == 0)
    def _():
        wd16_ref[...] = wd_ref[...].astype(BF16)

    head = pl.program_id(1)
    qi = pl.program_id(2)
    tq = q_ref.shape[0]
    tk = tq
    slope = jnp.float32(0.0)
    for hh in range(DA_HEADS):
        slope = jnp.where(head == hh, jnp.float32(2.0 ** (-8.0 * (hh + 1) / DA_HEADS)), slope)

    kpos = lax.broadcasted_iota(jnp.int32, (1, tk), 1)
    slope2 = slope * LOG2E

    maps = range(2)

    def scores(j, s_ref):
        k0 = pl.multiple_of(j * tk, tk)
        bias = slope2 * (kpos + (j - qi) * tk).astype(F32)
        for c in maps:
            cols = slice(c * DA_DIM, (c + 1) * DA_DIM)
            s_ref[c] = lax.dot_general(q_ref[:, cols], k_ref[pl.ds(k0, tk), cols], _NT,
                                       preferred_element_type=F32) + bias

    def accumulate(j, s_ref, masked, first=False):
        k0 = pl.multiple_of(j * tk, tk)
        v = v_ref[pl.ds(k0, tk), :]
        s = [s_ref[c] for c in maps]
        if masked:
            rr = lax.broadcasted_iota(jnp.int32, (tq, tk), 0)
            cc = lax.broadcasted_iota(jnp.int32, (tq, tk), 1)
            s = [jnp.where(cc <= rr, s[c], NEG_BIG) for c in maps]
        m_cur = [jnp.broadcast_to(jnp.max(s[c], axis=-1, keepdims=True), (tq, LANES)) for c in maps]
        if first:
            m_new = m_cur
        else:
            m_prev = [m_ref[c] for c in maps]
            m_new = [jnp.maximum(m_prev[c], m_cur[c]) for c in maps]
        p = [jnp.exp2(s[c] - _lane_tile(m_new[c], tk // LANES)) for c in maps]
        p_sum = [jnp.broadcast_to(jnp.sum(p[c], axis=-1, keepdims=True), (tq, LANES)) for c in maps]
        p16 = [p[c].astype(BF16) for c in maps]
        pv = [jnp.dot(p16[c], v, preferred_element_type=F32) for c in maps]
        for c in maps:
            if first:
                l_ref[c] = p_sum[c]
                acc_ref[c] = pv[c]
            else:
                alpha = jnp.exp2(m_prev[c] - m_new[c])
                l_ref[c] = alpha * l_ref[c] + p_sum[c]
                acc_ref[c] = _lane_tile(alpha, DA_VDIM // LANES) * acc_ref[c] + pv[c]
            m_ref[c] = m_new[c]

    @pl.when(qi == 0)
    def _():
        convert_gate_up()
        scores(0, sa_ref)
        accumulate(0, sa_ref, True, first=True)

    @pl.when(qi > 0)
    def _():
        convert_gate_up()
        scores(0, sa_ref)
        scores(1, sb_ref)
        accumulate(0, sa_ref, False, first=True)

        def pair(jj, carry):
            j = 2 * jj + 1
            scores(j + 1, sa_ref)
            accumulate(j, sb_ref, False)
            scores(j + 2, sb_ref)
            accumulate(j + 1, sa_ref, False)
            return carry

        lax.fori_loop(0, (qi - 1) // 2, pair, 0)

        @pl.when(qi % 2 == 0)
        def _():
            scores(qi, sa_ref)
            accumulate(qi - 1, sb_ref, False)
            accumulate(qi, sa_ref, True)

        @pl.when(qi % 2 == 1)
        def _():
            accumulate(qi, sb_ref, True)

    lam = (jnp.exp(jnp.sum(lq1_ref[...] * lk1_ref[...], axis=-1, keepdims=True))
           - jnp.exp(jnp.sum(lq2_ref[...] * lk2_ref[...], axis=-1, keepdims=True)) + lam_init)
    rep = DA_VDIM // LANES
    inv_l0 = 1.0 / l_ref[0]
    inv_l1 = lam / l_ref[1]
    o = acc_ref[0] * _lane_tile(inv_l0, rep) - acc_ref[1] * _lane_tile(inv_l1, rep)
    o_ref[...] = (_rmsnorm_rows(o, nw_ref[...]) * (1.0 - lam_init)).astype(o_ref.dtype)


def _slab_rows(n_rows, n_steps):
    for share in (1, 2, 4, 8, 16):
        if n_steps % share == 0 and n_rows % (n_steps // share) == 0:
            rows = n_rows // (n_steps // share)
            if rows % 16 == 0:
                return rows, share
    raise ValueError(f"cannot split {n_rows} weight rows over {n_steps} grid steps")


def _attention(da, lam_q1, lam_k1, lam_q2, lam_k2, da_norm_w, w_gate, w_up, w_down, lam_init, tq):
    b, t, _ = da.shape
    nh = DA_HEADS
    nq = t // tq
    d, dff = w_gate.shape
    steps = b * nh * nq
    rg, sg = _slab_rows(d, steps)
    rd, sd = _slab_rows(dff, steps)

    def step(i, h, j):
        return (i * nh + h) * nq + j

    vec = pl.BlockSpec((1, DA_DIM), lambda i, h, j: (0, 0))
    up_spec = pl.BlockSpec((rg, dff), lambda i, h, j: (step(i, h, j) // sg, 0))
    down_spec = pl.BlockSpec((rd, d), lambda i, h, j: (step(i, h, j) // sd, 0))
    return pl.pallas_call(
        functools.partial(_attn_kernel, lam_init=lam_init, down_share=sd),
        grid=(b, nh, nq),
        in_specs=[pl.BlockSpec((None, tq, DA_VDIM), lambda i, h, j: (i, j, h)),
                  pl.BlockSpec((None, t, DA_VDIM), lambda i, h, j: (i, 0, nh + h)),
                  pl.BlockSpec((None, t, DA_VDIM), lambda i, h, j: (i, 0, 2 * nh + h)),
                  vec, vec, vec, vec,
                  pl.BlockSpec((1, DA_VDIM), lambda i, h, j: (0, 0)),
                  up_spec, up_spec, down_spec],
        out_specs=[pl.BlockSpec((None, tq, DA_VDIM), lambda i, h, j: (i, j, h)),
                   up_spec, up_spec, down_spec],
        out_shape=[jax.ShapeDtypeStruct((b, t, DA_WIDTH), BF16),
                   jax.ShapeDtypeStruct((d, dff), BF16),
                   jax.ShapeDtypeStruct((d, dff), BF16),
                   jax.ShapeDtypeStruct((dff, d), BF16)],
        scratch_shapes=[pltpu.VMEM((2, tq, LANES), F32), pltpu.VMEM((2, tq, LANES), F32),
                        pltpu.VMEM((2, tq, DA_VDIM), F32),
                        pltpu.VMEM((2, tq, tq), F32), pltpu.VMEM((2, tq, tq), F32)],
        compiler_params=_params("arbitrary", "arbitrary", "arbitrary"),
        name="diff_attn",
    )(da, da, da, lam_q1, lam_k1, lam_q2, lam_k2, da_norm_w, w_gate, w_up, w_down)


def _out_proj_kernel(a_ref, b_ref, wa_ref, wb_ref, x_ref, o_ref):
    acc = jnp.dot(a_ref[...], wa_ref[...], preferred_element_type=F32)
    acc = acc + jnp.dot(b_ref[...], wb_ref[...], preferred_element_type=F32)
    o_ref[...] = x_ref[...] + acc


def _out_proj(o_dn, o_da, w16, x2d, tm, tn):
    m, d = x2d.shape
    ka, kb = o_dn.shape[1], o_da.shape[1]
    assert ka == kb and w16.shape[0] == ka + kb, "the two head groups are equal row halves of w_out"
    return pl.pallas_call(
        _out_proj_kernel,
        grid=(m // tm, d // tn),
        in_specs=[pl.BlockSpec((tm, ka), lambda i, j: (i, 0)),
                  pl.BlockSpec((tm, kb), lambda i, j: (i, 0)),
                  pl.BlockSpec((ka, tn), lambda i, j: (0, j)),
                  pl.BlockSpec((kb, tn), lambda i, j: (1, j)),
                  pl.BlockSpec((tm, tn), lambda i, j: (i, j))],
        out_specs=pl.BlockSpec((tm, tn), lambda i, j: (i, j)),
        out_shape=jax.ShapeDtypeStruct((m, d), F32),
        compiler_params=_params("parallel", "arbitrary"),
        name="out_proj",
    )(o_dn, o_da, w16, w16, x2d)


def _ffn_kernel(x_ref, nw_ref, wg_ref, wu_ref, wd_ref, fw_ref, o_ref, h_ref):
    f = pl.program_id(1)

    @pl.when(f == 0)
    def _():
        x = x_ref[...]
        h_ref[...] = _rmsnorm_rows(x, nw_ref[...]).astype(BF16)
        o_ref[...] = x

    h = h_ref[...]
    g = jnp.dot(h, wg_ref[...], preferred_element_type=F32)
    u = jnp.dot(h, wu_ref[...], preferred_element_type=F32)
    act = (_silu(g) * u).astype(BF16)
    o_ref[...] += jnp.dot(act, wd_ref[...], preferred_element_type=F32)

    @pl.when(f == pl.num_programs(1) - 1)
    def _():
        o_ref[...] = _rmsnorm_rows(o_ref[...], fw_ref[...])


def _ffn(x2d, ffn_norm_w, w_gate, w_up, w_down, final_norm_w, tm, tf):
    m, d = x2d.shape
    dff = w_gate.shape[1]
    return pl.pallas_call(
        _ffn_kernel,
        grid=(m // tm, dff // tf),
        in_specs=[pl.BlockSpec((tm, d), lambda i, f: (i, 0)),
                  pl.BlockSpec((1, d), lambda i, f: (0, 0)),
                  pl.BlockSpec((d, tf), lambda i, f: (0, f)),
                  pl.BlockSpec((d, tf), lambda i, f: (0, f)),
                  pl.BlockSpec((tf, d), lambda i, f: (f, 0)),
                  pl.BlockSpec((1, d), lambda i, f: (0, 0))],
        out_specs=pl.BlockSpec((tm, d), lambda i, f: (i, 0)),
        out_shape=jax.ShapeDtypeStruct((m, d), F32),
        scratch_shapes=[pltpu.VMEM((tm, d), BF16)],
        compiler_params=_params("parallel", "arbitrary"),
        name="ffn",
    )(x2d, ffn_norm_w, w_gate, w_up, w_down, final_norm_w)


def _tile(n, pref):
    return pref if n % pref == 0 else n


def _layer(x, lam_init, attn_norm_w, w_in, conv_w, a_log, dt_bias, dn_norm_w, lam_q1, lam_k1,
           lam_q2, lam_k2, da_norm_w, w_out, ffn_norm_w, w_gate, w_up, w_down, final_norm_w):
    b, t, d = x.shape
    m = b * t
    x2d = x.reshape(m, d)
    off_z = 3 * DN_WIDTH
    off_b = off_z + DN_WIDTH
    off_q = off_b + 2 * DN_HEADS
    nw = attn_norm_w.reshape(1, d)

    n_da = w_in.shape[1] - off_q
    da_scale = jnp.concatenate([jnp.full((1, DA_HEADS * 2 * DA_DIM), LOG2E * DA_DIM ** -0.5, F32),
                                jnp.ones((1, n_da - DA_HEADS * 2 * DA_DIM), F32)], axis=1)

    w_in_t = w_in.T
    grow, gcol, h, w_dn, w_da = _gates(x, nw, w_in_t, a_log.reshape(DN_HEADS, 1),
                                       dt_bias.reshape(DN_HEADS, 1), off_b, _tile(t, 512))
    h2d = h.reshape(m, d)
    tp = _tile(m, 2048)
    qkvz = _proj(h2d, w_dn, jnp.ones((1, off_b), F32), F32, tp, 1024, "in_proj_dn")
    da = _proj(h2d, w_da, da_scale, BF16, tp, 1024, "in_proj_da")
    tm = _tile(m, 1024)

    qkvz = qkvz.reshape(b, t, off_b)
    u, w, qd, kd, a, z16, w_out16 = _dn_intra(qkvz, conv_w, gcol, grow, w_out, _tile(t, 4096))
    o_dn = _dn_scan(u, w, qd, kd, a, z16, gcol, dn_norm_w.reshape(1, DN_DIM),
                    4 if b % 4 == 0 else 1, _tile(t, 256))

    o_da, w_gate16, w_up16, w_down16 = _attention(
        da.reshape(b, t, n_da), lam_q1.reshape(1, DA_DIM), lam_k1.reshape(1, DA_DIM),
        lam_q2.reshape(1, DA_DIM), lam_k2.reshape(1, DA_DIM), da_norm_w.reshape(1, DA_VDIM),
        w_gate, w_up, w_down, lam_init, _tile(t, 512))

    x2d = _out_proj(o_dn.reshape(m, DN_WIDTH), o_da.reshape(m, DA_WIDTH), w_out16, x2d,
                    _tile(m, 512), d)
    dff = w_gate.shape[1]
    return _ffn(x2d, ffn_norm_w.reshape(1, d), w_gate16, w_up16, w_down16,
                final_norm_w.reshape(1, d), tm, _tile(dff, 512))


def kernel(x, attn_norm_w, w_in, conv_w, a_log, dt_bias, dn_norm_w, lam_q1, lam_k1, lam_q2, lam_k2,
           da_norm_w, w_out, ffn_norm_w, w_gate, w_up, w_down, final_norm_w):
    b, t, d = x.shape
    depth = w_in.shape[0]
    assert depth == 1, "the final rmsnorm is fused into the last layer's FFN kernel"
    lam_init = 0.8 - 0.6 * math.exp(-0.3 * 0)
    out = _layer(x, lam_init, attn_norm_w[0], w_in[0], conv_w[0], a_log[0], dt_bias[0],
                 dn_norm_w[0], lam_q1[0], lam_k1[0], lam_q2[0], lam_k2[0], da_norm_w[0],
                 w_out[0], ffn_norm_w[0], w_gate[0], w_up[0], w_down[0], final_norm_w)
    return out.reshape(b, t, d)
```

```python
import functools
import math

import jax
import jax.numpy as jnp
from jax import lax
from jax.experimental import pallas as pl
from jax.experimental.pallas import tpu as pltpu

F32 = jnp.float32
BF16 = jnp.bfloat16
EPS = 1e-6
NEG_BIG = -1e30
LOG2E = math.log2(math.e)

DN_HEADS = 8
DN_DIM = 128
DN_WIDTH = DN_HEADS * DN_DIM
CONV_WIDTH = 4
CHUNK = 64
DA_HEADS = 4
DA_DIM = 128
DA_VDIM = 2 * DA_DIM
DA_WIDTH = DA_HEADS * DA_VDIM
LANES = 128
SUBLANES = 8
GROUP = 128
IN_FLIGHT = 7
GATE_SLAB = 256
VMEM_LIMIT = 56 * 1024 * 1024

_NT = (((1,), (1,)), ((), ()))
_TN = (((0,), (0,)), ((), ()))


def _params(*sem):
    return pltpu.CompilerParams(dimension_semantics=sem, vmem_limit_bytes=VMEM_LIMIT)


def _sigmoid(x):
    return 1.0 / (1.0 + jnp.exp(-x))


def _silu(x):
    h = 0.5 * x
    return h + h * jnp.tanh(h)


def _lane_tile(x, n):
    return jnp.concatenate([x] * n, axis=1)


def _rmsnorm_rows(x, w):
    return x * lax.rsqrt(jnp.mean(x * x, axis=-1, keepdims=True) + EPS) * w


def _proj_kernel(h_ref, w_ref, cs_ref, o_ref):
    acc = lax.dot_general(h_ref[...], w_ref[...], _NT, preferred_element_type=F32)
    o_ref[...] = (acc * cs_ref[...]).astype(o_ref.dtype)


def _proj(h2d, w_t_bf16, col_scale, out_dtype, tm, tn, name):
    m, d = h2d.shape
    n = w_t_bf16.shape[0]
    return pl.pallas_call(
        _proj_kernel,
        grid=(m // tm, n // tn),
        in_specs=[
            pl.BlockSpec((tm, d), lambda i, j: (i, 0)),
            pl.BlockSpec((tn, d), lambda i, j: (j, 0)),
            pl.BlockSpec((1, tn), lambda i, j: (0, j)),
        ],
        out_specs=pl.BlockSpec((tm, tn), lambda i, j: (i, j)),
        out_shape=jax.ShapeDtypeStruct((m, n), out_dtype),
        compiler_params=_params("parallel", "arbitrary"),
        name=name,
    )(h2d, w_t_bf16, col_scale)


def _gates_kernel(x_ref, nw_ref, wt_ref, alog_ref, dtb_ref, wdn_ref, *rest):
    wda_refs, (row_ref, col_ref, h_ref, wdn16_ref, wda16_ref) = rest[:-5], rest[-5:]
    wdn16_ref[...] = wdn_ref[...].astype(BF16)
    wda16_ref[...] = jnp.concatenate([r[...] for r in wda_refs], axis=0).astype(BF16)

    h = _rmsnorm_rows(x_ref[...], nw_ref[...]).astype(BF16)
    h_ref[...] = h
    r = lax.dot_general(wt_ref[...].astype(BF16), h, _NT, preferred_element_type=F32)
    beta = _sigmoid(r[0:DN_HEADS])
    a = r[DN_HEADS:2 * DN_HEADS] + dtb_ref[...]
    softplus = jnp.maximum(a, 0.0) + jnp.log1p(jnp.exp(-jnp.abs(a)))
    g = -jnp.exp(alog_ref[...]) * softplus
    tm = g.shape[1]
    sw = min(tm, GATE_SLAB)
    ii = lax.broadcasted_iota(jnp.int32, (sw, sw), 0)
    jj = lax.broadcasted_iota(jnp.int32, (sw, sw), 1)
    same = (ii // CHUNK) == (jj // CHUNK)
    sums = jnp.concatenate([same & (ii <= jj), same], axis=1).astype(BF16)
    g1 = g.astype(BF16)
    r1 = g - g1.astype(F32)
    g2 = r1.astype(BF16)
    g3 = (r1 - g2.astype(F32)).astype(BF16)
    pieces = jnp.concatenate([g1, g2, g3, jnp.zeros_like(g1)], axis=0)
    nh = DN_HEADS
    gc, gl = [], []
    for s0 in range(0, tm, sw):
        acc = jnp.dot(pieces[:, s0:s0 + sw], sums, preferred_element_type=F32)
        acc = acc[0:nh] + acc[nh:2 * nh] + acc[2 * nh:3 * nh]
        gc.append(acc[:, :sw])
        gl.append(acc[:, sw:])
    gc, gl = jnp.concatenate(gc, axis=1), jnp.concatenate(gl, axis=1)
    rows = jnp.concatenate([beta, gc, gl], axis=0)
    row_ref[...] = rows
    padded = jnp.concatenate([rows, jnp.zeros((LANES - 3 * DN_HEADS, tm), F32)], axis=0)
    col_ref[...] = padded.T


def _gates(x3d, norm_w, w_in_t, a_log_col, dt_bias_col, off_gate, tm):
    b, t, d = x3d.shape
    nj = t // tm
    steps = b * nj
    piece = 2 * DN_HEADS
    off_da = off_gate + piece
    n_da = w_in_t.shape[0] - off_da
    rows_dn, rows_da = off_gate // steps, n_da // steps
    assert rows_dn * steps == off_gate and rows_dn % piece == 0
    assert rows_da * steps == n_da and rows_da % piece == 0 and off_da % piece == 0
    n_pieces = rows_da // piece

    def step(i, j):
        return i * nj + j

    def da_piece(k):
        return pl.BlockSpec((piece, d),
                            lambda i, j: (off_da // piece + step(i, j) * n_pieces + k, 0))

    return pl.pallas_call(
        _gates_kernel,
        grid=(b, nj),
        in_specs=[
            pl.BlockSpec((None, tm, d), lambda i, j: (i, j, 0)),
            pl.BlockSpec((1, d), lambda i, j: (0, 0)),
            pl.BlockSpec((piece, d), lambda i, j: (off_gate // piece, 0)),
            pl.BlockSpec((DN_HEADS, 1), lambda i, j: (0, 0)),
            pl.BlockSpec((DN_HEADS, 1), lambda i, j: (0, 0)),
            pl.BlockSpec((rows_dn, d), lambda i, j: (step(i, j), 0)),
        ] + [da_piece(k) for k in range(n_pieces)],
        out_specs=[
            pl.BlockSpec((None, 3 * DN_HEADS, tm), lambda i, j: (i, 0, j)),
            pl.BlockSpec((None, tm, LANES), lambda i, j: (i, j, 0)),
            pl.BlockSpec((None, tm, d), lambda i, j: (i, j, 0)),
            pl.BlockSpec((rows_dn, d), lambda i, j: (step(i, j), 0)),
            pl.BlockSpec((rows_da, d), lambda i, j: (step(i, j), 0)),
        ],
        out_shape=[
            jax.ShapeDtypeStruct((b, 3 * DN_HEADS, t), F32),
            jax.ShapeDtypeStruct((b, t, LANES), F32),
            jax.ShapeDtypeStruct((b, t, d), BF16),
            jax.ShapeDtypeStruct((off_gate, d), BF16),
            jax.ShapeDtypeStruct((n_da, d), BF16),
        ],
        compiler_params=_params("parallel", "arbitrary"),
        name="dn_gates",
    )(x3d, norm_w, w_in_t, a_log_col, dt_bias_col, w_in_t, *([w_in_t] * n_pieces))


def _dn_intra_kernel(q_ref, k_ref, v_ref, qh_ref, kh_ref, vh_ref, cwq_ref, cwk_ref, cwv_ref,
                     gcol_ref, grow_ref, wo_ref, u_ref, w_ref, qd_ref, kd_ref, a_ref, wo16_ref, xs_ref):
    wo16_ref[...] = wo_ref[...].astype(BF16)
    head = pl.program_id(1)
    first = pl.program_id(2) == 0
    tb = q_ref.shape[0]
    halo = SUBLANES
    for idx, (x_ref, xh_ref) in enumerate(((q_ref, qh_ref), (k_ref, kh_ref), (v_ref, vh_ref))):
        xs_ref[idx, 0:halo, :] = jnp.where(first, 0.0, xh_ref[...])
        xs_ref[idx, halo:halo + tb, :] = x_ref[...]

    lane = lax.broadcasted_iota(jnp.int32, (GROUP, LANES), 1)
    sel_beta = (lane == head).astype(F32)
    sel_gc = (lane == head + DN_HEADS).astype(F32)
    sel_gl = (lane == head + 2 * DN_HEADS).astype(F32)
    ri = lax.broadcasted_iota(jnp.int32, (LANES, LANES), 0)
    ci = lax.broadcasted_iota(jnp.int32, (LANES, LANES), 1)
    same = (ri // CHUNK) == (ci // CHUNK)
    incl = same & (ri >= ci)
    strict = same & (ri > ci)

    def conv_silu(idx, cw_ref, r0, anchor):
        base = r0 + halo - (CONV_WIDTH - 1)
        taps = cw_ref[...] if anchor is None else cw_ref[...] + anchor
        acc = xs_ref[idx, base:base + GROUP, :] * taps[0:1, :]
        for i in range(1, CONV_WIDTH):
            acc = acc + xs_ref[idx, base + i:base + i + GROUP, :] * taps[i:i + 1, :]
        return _silu(acc)

    n_groups = tb // GROUP
    levels = int(math.log2(CHUNK))
    n_pows = [None] * n_groups
    xs = [None] * n_groups
    gc_rows = grow_ref[pl.ds(head + DN_HEADS, 1), :]

    def prepare(gi):
        r0 = gi * GROUP
        anchor = None if gi < IN_FLIGHT else xs[gi - IN_FLIGHT][0:1, :DN_DIM] * 0.0
        q = conv_silu(0, cwq_ref, r0, anchor)
        k = conv_silu(1, cwk_ref, r0, anchor)
        v = conv_silu(2, cwv_ref, r0, anchor)
        q = q * lax.rsqrt(jnp.sum(q * q, axis=-1, keepdims=True) + EPS) * (DN_DIM ** -0.5)
        k = k * lax.rsqrt(jnp.sum(k * k, axis=-1, keepdims=True) + EPS)

        gates = gcol_ref[r0:r0 + GROUP, :]
        beta = jnp.sum(gates * sel_beta, axis=-1, keepdims=True)
        gc = jnp.sum(gates * sel_gc, axis=-1, keepdims=True)
        gl = jnp.sum(gates * sel_gl, axis=-1, keepdims=True)
        gc_row = gc_rows[:, r0:r0 + GROUP]

        kb = k * beta
        k16 = k.astype(BF16)
        gram = lax.dot_general(jnp.concatenate([kb, q], axis=0).astype(BF16), k16, _NT,
                               preferred_element_type=F32)
        m_blocks, a_blocks = [], []
        for d0 in range(0, GROUP, LANES):
            dd = slice(d0, d0 + LANES)
            decay = jnp.exp(jnp.where(incl, gc[dd] - gc_row[:, dd], NEG_BIG))
            m_blocks.append(jnp.where(strict, gram[dd, dd] * decay, 0.0))
            a_blocks.append(gram[GROUP + d0:GROUP + d0 + LANES, dd] * decay)
        zero = jnp.zeros((LANES, LANES), F32)
        n_pows[gi] = jnp.concatenate(
            [jnp.concatenate([m_blocks[i] if j == i else zero for j in range(len(m_blocks))], axis=1)
             for i in range(len(m_blocks))], axis=0)

        egc = jnp.exp(gc)
        xs[gi] = jnp.concatenate([v * beta, kb * egc], axis=1)
        qd_ref[r0:r0 + GROUP, :] = (q * egc).astype(BF16)
        kd_ref[r0:r0 + GROUP, :] = (k * jnp.exp(gl - gc)).astype(BF16)
        for c in range(GROUP // CHUNK):
            lo = c * CHUNK
            in_block = lo % LANES
            a_ref[r0 + lo:r0 + lo + CHUNK, :] = a_blocks[lo // LANES][
                in_block:in_block + CHUNK, in_block:in_block + CHUNK].astype(BF16)

    def level(gi, lvl):
        n16 = n_pows[gi].astype(BF16)
        step = jnp.dot(n16, xs[gi].astype(BF16), preferred_element_type=F32)
        xs[gi] = xs[gi] - step if lvl == 0 else xs[gi] + step
        if lvl + 1 < levels:
            n_pows[gi] = jnp.dot(n16, n16, preferred_element_type=F32)

    def finish(gi):
        r0 = gi * GROUP
        u_ref[r0:r0 + GROUP, :] = xs[gi][:, :DN_DIM]
        w_ref[r0:r0 + GROUP, :] = xs[gi][:, DN_DIM:].astype(BF16)

    for t in range(n_groups + levels + 1):
        for gi in range(n_groups):
            stage = t - gi
            if stage == 0:
                prepare(gi)
            elif 1 <= stage <= levels:
                level(gi, stage - 1)
            elif stage == levels + 1:
                finish(gi)


def _dn_intra(qkvz, conv_w, gcol, grow, w_out, tb):
    b, t, _ = qkvz.shape
    hb = tb // SUBLANES
    nh = DN_HEADS
    nj = t // tb
    wo_rows, wo_share = _slab_rows(w_out.shape[0], b * nh * nj)
    assert wo_share == 1, "every grid step converts its own slab of w_out"
    wo_spec = pl.BlockSpec((wo_rows, w_out.shape[1]), lambda i, h, j: ((i * nh + h) * nj + j, 0))

    def tok(off):
        return pl.BlockSpec((None, tb, DN_DIM), lambda i, h, j: (i, j, off + h))

    def halo(off):
        return pl.BlockSpec((None, SUBLANES, DN_DIM),
                            lambda i, h, j: (i, jnp.maximum(j * hb - 1, 0), off + h))

    def cw(off):
        return pl.BlockSpec((CONV_WIDTH, DN_DIM), lambda i, h, j: (0, off + h))

    out_tok = pl.BlockSpec((None, tb, DN_DIM), lambda i, h, j: (i, j, h))
    return pl.pallas_call(
        _dn_intra_kernel,
        grid=(b, nh, nj),
        in_specs=[tok(0), tok(nh), tok(2 * nh), halo(0), halo(nh), halo(2 * nh),
                  cw(0), cw(nh), cw(2 * nh),
                  pl.BlockSpec((None, tb, LANES), lambda i, h, j: (i, j, 0)),
                  pl.BlockSpec((None, 3 * nh, tb), lambda i, h, j: (i, 0, j)),
                  wo_spec],
        out_specs=[out_tok, out_tok, out_tok, out_tok,
                   pl.BlockSpec((None, None, tb, CHUNK), lambda i, h, j: (i, h, j, 0)),
                   wo_spec],
        out_shape=[jax.ShapeDtypeStruct((b, t, DN_WIDTH), F32),
                   jax.ShapeDtypeStruct((b, t, DN_WIDTH), BF16),
                   jax.ShapeDtypeStruct((b, t, DN_WIDTH), BF16),
                   jax.ShapeDtypeStruct((b, t, DN_WIDTH), BF16),
                   jax.ShapeDtypeStruct((b, nh, t, CHUNK), BF16),
                   jax.ShapeDtypeStruct(w_out.shape, BF16)],
        scratch_shapes=[pltpu.VMEM((3, tb + SUBLANES, DN_DIM), F32)],
        compiler_params=_params("parallel", "parallel", "arbitrary"),
        name="dn_intra",
    )(qkvz, qkvz, qkvz, qkvz, qkvz, qkvz, conv_w, conv_w, conv_w, gcol, grow, w_out)


def _dn_scan_kernel(u_ref, w_ref, qd_ref, kd_ref, a_ref, z_ref, gcol_ref, nw_ref, o_ref, s_ref):
    @pl.when(pl.program_id(1) == 0)
    def _():
        s_ref[...] = jnp.zeros_like(s_ref)

    bb, tb = u_ref.shape[0], u_ref.shape[1]
    nw = nw_ref[...]
    units = [(i, h) for i in range(bb) for h in range(DN_HEADS)]

    def chunk(c, carry):
        rows = pl.ds(pl.multiple_of(c * CHUNK, CHUNK), CHUNK)
        cols = [slice(h * DN_DIM, (h + 1) * DN_DIM) for h in range(DN_HEADS)]
        s = [s_ref[i, h] for i, h in units]
        proj = [jnp.dot(jnp.concatenate([w_ref[i, rows, cols[h]], qd_ref[i, rows, cols[h]]], axis=0),
                        s[n].astype(BF16), preferred_element_type=F32)
                for n, (i, h) in enumerate(units)]
        v16 = [(u_ref[i, rows, cols[h]] - proj[n][:CHUNK]).astype(BF16)
               for n, (i, h) in enumerate(units)]
        upd = [lax.dot_general(kd_ref[i, rows, cols[h]], v16[n], _TN, preferred_element_type=F32)
               for n, (i, h) in enumerate(units)]
        for n, (i, h) in enumerate(units):
            gl = gcol_ref[i, pl.ds(pl.multiple_of(c * CHUNK, CHUNK), 1),
                          2 * DN_HEADS + h:2 * DN_HEADS + h + 1]
            s_ref[i, h] = s[n] * jnp.exp(gl) + upd[n]
        o = [proj[n][CHUNK:] + jnp.dot(a_ref[i, h, rows, :], v16[n], preferred_element_type=F32)
             for n, (i, h) in enumerate(units)]
        for n, (i, h) in enumerate(units):
            z = z_ref[i, rows, cols[h]]
            o_ref[i, rows, cols[h]] = (_rmsnorm_rows(o[n], nw) * _silu(z)).astype(o_ref.dtype)
        return carry

    lax.fori_loop(0, tb // CHUNK, chunk, 0)


def _dn_scan(u, w, qd, kd, a, qkvz, gcol, dn_norm_w, bb, tb):
    b, t, _ = u.shape
    tok = pl.BlockSpec((bb, tb, DN_WIDTH), lambda i, j: (i, j, 0))
    z_block = 3 * DN_WIDTH // DN_WIDTH
    return pl.pallas_call(
        _dn_scan_kernel,
        grid=(b // bb, t // tb),
        in_specs=[tok, tok, tok, tok,
                  pl.BlockSpec((bb, DN_HEADS, tb, CHUNK), lambda i, j: (i, 0, j, 0)),
                  pl.BlockSpec((bb, tb, DN_WIDTH), lambda i, j: (i, j, z_block)),
                  pl.BlockSpec((bb, tb, LANES), lambda i, j: (i, j, 0)),
                  pl.BlockSpec((1, DN_DIM), lambda i, j: (0, 0))],
        out_specs=tok,
        out_shape=jax.ShapeDtypeStruct((b, t, DN_WIDTH), BF16),
        scratch_shapes=[pltpu.VMEM((bb, DN_HEADS, DN_DIM, DN_DIM), F32)],
        compiler_params=_params("parallel", "arbitrary"),
        name="dn_scan",
    )(u, w, qd, kd, a, qkvz, gcol, dn_norm_w)


def _attn_kernel(q_ref, k_ref, v_ref, lq1_ref, lk1_ref, lq2_ref, lk2_ref, nw_ref,
                 wg_ref, wu_ref, wd_ref, o_ref, wg16_ref, wu16_ref, wd16_ref,
                 m_ref, l_ref, acc_ref, sa_ref, sb_ref, *, lam_init, down_share):
    def convert_gate_up():
        wg16_ref[...] = wg_ref[...].astype(BF16)
        wu16_ref[...] = wu_ref[...].astype(BF16)

    grid_step = ((pl.program_id(0) * pl.num_programs(1) + pl.program_id(1)) * pl.num_programs(2)
                 + pl.program_id(2))

    @pl.when(grid_step % down_share == 0)
    def _():
        wd16_ref[...] = wd_ref[...].astype(BF16)

    head = pl.program_id(1)
    qi = pl.program_id(2)
    tq = q_ref.shape[0]
    tk = tq
    slope = jnp.float32(0.0)
    for hh in range(DA_HEADS):
        slope = jnp.where(head == hh, jnp.float32(2.0 ** (-8.0 * (hh + 1) / DA_HEADS)), slope)

    kpos = lax.broadcasted_iota(jnp.int32, (1, tk), 1)
    slope2 = slope * LOG2E

    maps = range(2)

    def scores(j, s_ref):
        k0 = pl.multiple_of(j * tk, tk)
        bias = slope2 * (kpos + (j - qi) * tk).astype(F32)
        for c in maps:
            cols = slice(c * DA_DIM, (c + 1) * DA_DIM)
            s_ref[c] = lax.dot_general(q_ref[:, cols], k_ref[pl.ds(k0, tk), cols], _NT,
                                       preferred_element_type=F32) + bias

    def accumulate(j, s_ref, masked, first=False):
        k0 = pl.multiple_of(j * tk, tk)
        v = v_ref[pl.ds(k0, tk), :]
        chunks = range(tk // LANES)

        def load(c, kc):
            sc = s_ref[c, :, kc * LANES:(kc + 1) * LANES]
            if masked:
                rr = lax.broadcasted_iota(jnp.int32, (tq, LANES), 0)
                cc = lax.broadcasted_iota(jnp.int32, (tq, LANES), 1) + kc * LANES
                sc = jnp.where(cc <= rr, sc, NEG_BIG)
            return sc

        m_cur = []
        for c in maps:
            mx = load(c, 0)
            for kc in chunks[1:]:
                mx = jnp.maximum(mx, load(c, kc))
            m_cur.append(jnp.broadcast_to(jnp.max(mx, axis=-1, keepdims=True), (tq, LANES)))
        if first:
            m_new = m_cur
        else:
            m_prev = [m_ref[c] for c in maps]
            m_new = [jnp.maximum(m_prev[c], m_cur[c]) for c in maps]
        p16, p_sum = [], []
        for c in maps:
            pk = [jnp.exp2(load(c, kc) - m_new[c]) for kc in chunks]
            tot = pk[0]
            for kc in chunks[1:]:
                tot = tot + pk[kc]
            p_sum.append(jnp.broadcast_to(jnp.sum(tot, axis=-1, keepdims=True), (tq, LANES)))
            p16.append(jnp.concatenate([x.astype(BF16) for x in pk], axis=1))
        pv = [jnp.dot(p16[c], v, preferred_element_type=F32) for c in maps]
        for c in maps:
            if first:
                l_ref[c] = p_sum[c]
                acc_ref[c] = pv[c]
            else:
                alpha = jnp.exp2(m_prev[c] - m_new[c])
                l_ref[c] = alpha * l_ref[c] + p_sum[c]
                acc_ref[c] = _lane_tile(alpha, DA_VDIM // LANES) * acc_ref[c] + pv[c]
            m_ref[c] = m_new[c]

    @pl.when(qi == 0)
    def _():
        convert_gate_up()
        scores(0, sa_ref)
        accumulate(0, sa_ref, True, first=True)

    @pl.when(qi > 0)
    def _():
        convert_gate_up()
        scores(0, sa_ref)
        scores(1, sb_ref)
        accumulate(0, sa_ref, False, first=True)

        def pair(jj, carry):
            j = 2 * jj + 1
            scores(j + 1, sa_ref)
            accumulate(j, sb_ref, False)
            scores(j + 2, sb_ref)
            accumulate(j + 1, sa_ref, False)
            return carry

        lax.fori_loop(0, (qi - 1) // 2, pair, 0)

        @pl.when(qi % 2 == 0)
        def _():
            scores(qi, sa_ref)
            accumulate(qi - 1, sb_ref, False)
            accumulate(qi, sa_ref, True)

        @pl.when(qi % 2 == 1)
        def _():
            accumulate(qi, sb_ref, True)

    lam = (jnp.exp(jnp.sum(lq1_ref[...] * lk1_ref[...], axis=-1, keepdims=True))
           - jnp.exp(jnp.sum(lq2_ref[...] * lk2_ref[...], axis=-1, keepdims=True)) + lam_init)
    rep = DA_VDIM // LANES
    inv_l0 = 1.0 / l_ref[0]
    inv_l1 = lam / l_ref[1]
    o = acc_ref[0] * _lane_tile(inv_l0, rep) - acc_ref[1] * _lane_tile(inv_l1, rep)
    o_ref[...] = (_rmsnorm_rows(o, nw_ref[...]) * (1.0 - lam_init)).astype(o_ref.dtype)


def _slab_rows(n_rows, n_steps):
    for share in (1, 2, 4, 8, 16):
        if n_steps % share == 0 and n_rows % (n_steps // share) == 0:
            rows = n_rows // (n_steps // share)
            if rows % 16 == 0:
                return rows, share
    raise ValueError(f"cannot split {n_rows} weight rows over {n_steps} grid steps")


def _attention(da, lam_q1, lam_k1, lam_q2, lam_k2, da_norm_w, w_gate, w_up, w_down, lam_init, tq):
    b, t, _ = da.shape
    nh = DA_HEADS
    nq = t // tq
    d, dff = w_gate.shape
    steps = b * nh * nq
    rg, sg = _slab_rows(d, steps)
    rd, sd = _slab_rows(dff, steps)

    def step(i, h, j):
        return (i * nh + h) * nq + j

    vec = pl.BlockSpec((1, DA_DIM), lambda i, h, j: (0, 0))
    up_spec = pl.BlockSpec((rg, dff), lambda i, h, j: (step(i, h, j) // sg, 0))
    down_spec = pl.BlockSpec((rd, d), lambda i, h, j: (step(i, h, j) // sd, 0))
    return pl.pallas_call(
        functools.partial(_attn_kernel, lam_init=lam_init, down_share=sd),
        grid=(b, nh, nq),
        in_specs=[pl.BlockSpec((None, tq, DA_VDIM), lambda i, h, j: (i, j, h)),
                  pl.BlockSpec((None, t, DA_VDIM), lambda i, h, j: (i, 0, nh + h)),
                  pl.BlockSpec((None, t, DA_VDIM), lambda i, h, j: (i, 0, 2 * nh + h)),
                  vec, vec, vec, vec,
                  pl.BlockSpec((1, DA_VDIM), lambda i, h, j: (0, 0)),
                  up_spec, up_spec, down_spec],
        out_specs=[pl.BlockSpec((None, tq, DA_VDIM), lambda i, h, j: (i, j, h)),
                   up_spec, up_spec, down_spec],
        out_shape=[jax.ShapeDtypeStruct((b, t, DA_WIDTH), BF16),
                   jax.ShapeDtypeStruct((d, dff), BF16),
                   jax.ShapeDtypeStruct((d, dff), BF16),
                   jax.ShapeDtypeStruct((dff, d), BF16)],
        scratch_shapes=[pltpu.VMEM((2, tq, LANES), F32), pltpu.VMEM((2, tq, LANES), F32),
                        pltpu.VMEM((2, tq, DA_VDIM), F32),
                        pltpu.VMEM((2, tq, tq), F32), pltpu.VMEM((2, tq, tq), F32)],
        compiler_params=_params("arbitrary", "arbitrary", "arbitrary"),
        name="diff_attn",
    )(da, da, da, lam_q1, lam_k1, lam_q2, lam_k2, da_norm_w, w_gate, w_up, w_down)


def _out_proj_kernel(a_ref, b_ref, wa_ref, wb_ref, x_ref, o_ref):
    acc = jnp.dot(a_ref[...], wa_ref[...], preferred_element_type=F32)
    acc = acc + jnp.dot(b_ref[...], wb_ref[...], preferred_element_type=F32)
    o_ref[...] = x_ref[...] + acc


def _out_proj(o_dn, o_da, w16, x2d, tm, tn):
    m, d = x2d.shape
    ka, kb = o_dn.shape[1], o_da.shape[1]
    assert ka == kb and w16.shape[0] == ka + kb, "the two head groups are equal row halves of w_out"
    return pl.pallas_call(
        _out_proj_kernel,
        grid=(m // tm, d // tn),
        in_specs=[pl.BlockSpec((tm, ka), lambda i, j: (i, 0)),
                  pl.BlockSpec((tm, kb), lambda i, j: (i, 0)),
                  pl.BlockSpec((ka, tn), lambda i, j: (0, j)),
                  pl.BlockSpec((kb, tn), lambda i, j: (1, j)),
                  pl.BlockSpec((tm, tn), lambda i, j: (i, j))],
        out_specs=pl.BlockSpec((tm, tn), lambda i, j: (i, j)),
        out_shape=jax.ShapeDtypeStruct((m, d), F32),
        compiler_params=_params("parallel", "arbitrary"),
        name="out_proj",
    )(o_dn, o_da, w16, w16, x2d)


def _ffn_kernel(x_ref, nw_ref, wg_ref, wu_ref, wd_ref, fw_ref, o_ref, h_ref):
    f = pl.program_id(1)

    @pl.when(f == 0)
    def _():
        x = x_ref[...]
        h_ref[...] = _rmsnorm_rows(x, nw_ref[...]).astype(BF16)
        o_ref[...] = x

    h = h_ref[...]
    g = jnp.dot(h, wg_ref[...], preferred_element_type=F32)
    u = jnp.dot(h, wu_ref[...], preferred_element_type=F32)
    act = (_silu(g) * u).astype(BF16)
    o_ref[...] += jnp.dot(act, wd_ref[...], preferred_element_type=F32)

    @pl.when(f == pl.num_programs(1) - 1)
    def _():
        o_ref[...] = _rmsnorm_rows(o_ref[...], fw_ref[...])


def _ffn(x2d, ffn_norm_w, w_gate, w_up, w_down, final_norm_w, tm, tf):
    m, d = x2d.shape
    dff = w_gate.shape[1]
    return pl.pallas_call(
        _ffn_kernel,
        grid=(m // tm, dff // tf),
        in_specs=[pl.BlockSpec((tm, d), lambda i, f: (i, 0)),
                  pl.BlockSpec((1, d), lambda i, f: (0, 0)),
                  pl.BlockSpec((d, tf), lambda i, f: (0, f)),
                  pl.BlockSpec((d, tf), lambda i, f: (0, f)),
                  pl.BlockSpec((tf, d), lambda i, f: (f, 0)),
                  pl.BlockSpec((1, d), lambda i, f: (0, 0))],
        out_specs=pl.BlockSpec((tm, d), lambda i, f: (i, 0)),
        out_shape=jax.ShapeDtypeStruct((m, d), F32),
        scratch_shapes=[pltpu.VMEM((tm, d), BF16)],
        compiler_params=_params("parallel", "arbitrary"),
        name="ffn",
    )(x2d, ffn_norm_w, w_gate, w_up, w_down, final_norm_w)


def _tile(n, pref):
    return pref if n % pref == 0 else n


def _layer(x, lam_init, attn_norm_w, w_in, conv_w, a_log, dt_bias, dn_norm_w, lam_q1, lam_k1,
           lam_q2, lam_k2, da_norm_w, w_out, ffn_norm_w, w_gate, w_up, w_down, final_norm_w):
    b, t, d = x.shape
    m = b * t
    x2d = x.reshape(m, d)
    off_z = 3 * DN_WIDTH
    off_b = off_z + DN_WIDTH
    off_q = off_b + 2 * DN_HEADS
    nw = attn_norm_w.reshape(1, d)

    n_da = w_in.shape[1] - off_q
    da_scale = jnp.concatenate([jnp.full((1, DA_HEADS * 2 * DA_DIM), LOG2E * DA_DIM ** -0.5, F32),
                                jnp.ones((1, n_da - DA_HEADS * 2 * DA_DIM), F32)], axis=1)

    w_in_t = w_in.T
    grow, gcol, h, w_dn, w_da = _gates(x, nw, w_in_t, a_log.reshape(DN_HEADS, 1),
                                       dt_bias.reshape(DN_HEADS, 1), off_b, _tile(t, 512))
    h2d = h.reshape(m, d)
    tp = _tile(m, 2048)
    qkvz = _proj(h2d, w_dn, jnp.ones((1, off_b), F32), F32, tp, 1024, "in_proj_dn")
    da = _proj(h2d, w_da, da_scale, BF16, tp, 1024, "in_proj_da")
    tm = _tile(m, 1024)

    qkvz = qkvz.reshape(b, t, off_b)
    u, w, qd, kd, a, w_out16 = _dn_intra(qkvz, conv_w, gcol, grow, w_out, _tile(t, 4096))
    o_dn = _dn_scan(u, w, qd, kd, a, qkvz, gcol, dn_norm_w.reshape(1, DN_DIM),
                    4 if b % 4 == 0 else 1, _tile(t, 256))

    o_da, w_gate16, w_up16, w_down16 = _attention(
        da.reshape(b, t, n_da), lam_q1.reshape(1, DA_DIM), lam_k1.reshape(1, DA_DIM),
        lam_q2.reshape(1, DA_DIM), lam_k2.reshape(1, DA_DIM), da_norm_w.reshape(1, DA_VDIM),
        w_gate, w_up, w_down, lam_init, _tile(t, 512))

    x2d = _out_proj(o_dn.reshape(m, DN_WIDTH), o_da.reshape(m, DA_WIDTH), w_out16, x2d,
                    _tile(m, 512), d)
    dff = w_gate.shape[1]
    return _ffn(x2d, ffn_norm_w.reshape(1, d), w_gate16, w_up16, w_down16,
                final_norm_w.reshape(1, d), tm, _tile(dff, 512))


def kernel(x, attn_norm_w, w_in, conv_w, a_log, dt_bias, dn_norm_w, lam_q1, lam_k1, lam_q2, lam_k2,
           da_norm_w, w_out, ffn_norm_w, w_gate, w_up, w_down, final_norm_w):
    b, t, d = x.shape
    depth = w_in.shape[0]
    assert depth == 1, "the final rmsnorm is fused into the last layer's FFN kernel"
    lam_init = 0.8 - 0.6 * math.exp(-0.3 * 0)
    out = _layer(x, lam_init, attn_norm_w[0], w_in[0], conv_w[0], a_log[0], dt_bias[0],
                 dn_norm_w[0], lam_q1[0], lam_k1[0], lam_q2[0], lam_k2[0], da_norm_w[0],
                 w_out[0], ffn_norm_w[0], w_gate[0], w_up[0], w_down[0], final_norm_w)
    return out.reshape(b, t, d)
```
